```python
import math
import jax, jax.numpy as jnp
from jax import lax
import numpy as np

D_MODEL = 2048
BATCH = 4
SEQ = 2048
DEPTH = 1
DEC_BATCH = 128
DEC_SEQ = 1
PAST_LEN = 16384
PAGE_SIZE = 128

N_MEM = 256
S5_WIDTH = D_MODEL // 2
S5_GROUP = 16
S5_GROUPS = S5_WIDTH // S5_GROUP
S5_STATE = 64
DT_MIN = 1e-3
DT_MAX = 1e-1
ML_WIDTH = D_MODEL // 2
ML_HEADS = 4
ML_HEAD_DIM = ML_WIDTH // ML_HEADS
ML_CHUNK = 64
XA_WIDTH = D_MODEL // 2
XA_HEADS = 4
XA_HEAD_DIM = XA_WIDTH // XA_HEADS
N_BRANCH = 3
FF_DIM = 5504
EPS = 1e-6

IN_SIZES = (S5_WIDTH, ML_WIDTH, ML_WIDTH, ML_WIDTH, ML_WIDTH, ML_HEADS, ML_HEADS, XA_WIDTH, N_BRANCH * D_MODEL)
D_IN = sum(IN_SIZES)
IN_SPLIT = tuple(int(c) for c in np.cumsum(IN_SIZES)[:-1])

kernel_name = "hybrid_s5_mlstm_memxattn_decode_step"


def rmsnorm(x, g):
    xf = x.astype(jnp.float32)
    r = lax.rsqrt(jnp.mean(xf * xf, axis=-1, keepdims=True) + EPS)
    return (xf * r).astype(x.dtype) * g


def swiglu(x, w_gate, w_up, w_down):
    return (jax.nn.silu(x @ w_gate) * (x @ w_up)) @ w_down


def _complex_affine_combine(e1, e2):
    a1r, a1i, b1r, b1i = e1
    a2r, a2i, b2r, b2i = e2
    return (a2r * a1r - a2i * a1i, a2r * a1i + a2i * a1r,
            a2r * b1r - a2i * b1i + b2r, a2r * b1i + a2i * b1r + b2i)


def s5_branch(u, s_re, s_im, lam_re, lam_im, log_step, b_re, b_im, c_re, c_im, d, w_glu):
    bn, L, _ = u.shape
    f32 = jnp.float32
    lam_re = lam_re.astype(f32)
    lam_im = lam_im.astype(f32)
    dt = jnp.exp(log_step.astype(f32))[:, None]
    mag = jnp.exp(lam_re * dt)
    ab_re = mag * jnp.cos(lam_im * dt)
    ab_im = mag * jnp.sin(lam_im * dt)
    den = lam_re * lam_re + lam_im * lam_im
    nr = ab_re - 1.0
    z_re = (nr * lam_re + ab_im * lam_im) / den
    z_im = (ab_im * lam_re - nr * lam_im) / den
    b_re = b_re.astype(f32)
    b_im = b_im.astype(f32)
    bb_re = z_re[..., None] * b_re - z_im[..., None] * b_im
    bb_im = z_re[..., None] * b_im + z_im[..., None] * b_re
    ug = u.astype(f32).reshape(bn, L, S5_GROUPS, S5_GROUP)
    bu_re = jnp.einsum('blgh,gph->blgp', ug, bb_re)
    bu_im = jnp.einsum('blgh,gph->blgp', ug, bb_im)
    s_re = s_re.astype(f32)
    s_im = s_im.astype(f32)
    bu_re = bu_re.at[:, 0].add(ab_re * s_re - ab_im * s_im)
    bu_im = bu_im.at[:, 0].add(ab_re * s_im + ab_im * s_re)
    a_re = jnp.broadcast_to(ab_re, bu_re.shape)
    a_im = jnp.broadcast_to(ab_im, bu_im.shape)
    _, _, x_re, x_im = lax.associative_scan(_complex_affine_combine, (a_re, a_im, bu_re, bu_im), axis=1)
    y = (jnp.einsum('blgp,ghp->blgh', x_re, c_re.astype(f32))
         - jnp.einsum('blgp,ghp->blgh', x_im, c_im.astype(f32))
         + d.astype(f32) * ug)
    y = jax.nn.gelu(y.reshape(bn, L, S5_WIDTH)).astype(u.dtype)
    y = y * jax.nn.sigmoid(y @ w_glu)
    return y, x_re[:, -1], x_im[:, -1]


def mlstm_branch(q, k, v, i_pre, f_pre, C0, n0, m0):
    bn, L, _ = q.shape
    f32 = jnp.float32
    cl = math.gcd(ML_CHUNK, L)
    nc = L // cl

    def blocks(t):
        return t.astype(f32).reshape(bn, nc, cl, ML_HEADS, ML_HEAD_DIM).transpose(1, 0, 3, 2, 4)

    def gblocks(t):
        return t.astype(f32).reshape(bn, nc, cl, ML_HEADS).transpose(1, 0, 3, 2)

    qb = blocks(q)
    kb = blocks(k) * (ML_HEAD_DIM ** -0.5)
    vb = blocks(v)
    ib = gblocks(i_pre)
    lfb = jax.nn.log_sigmoid(gblocks(f_pre))
    causal = jnp.tril(jnp.ones((cl, cl), dtype=bool))

    def chunk_step(carry, xs):
        C, n, m = carry
        qc, kc, vc, ic, lfc = xs
        bcum = jnp.cumsum(lfc, axis=-1)
        g_inter = bcum + m[..., None]
        dlog = bcum[..., :, None] - bcum[..., None, :] + ic[..., None, :]
        dlog = jnp.where(causal, dlog, -jnp.inf)
        m_t = jnp.maximum(g_inter, jnp.max(dlog, axis=-1))
        w_inter = jnp.exp(g_inter - m_t)
        w_intra = jnp.exp(dlog - m_t[..., None])
        s = jnp.einsum('bhtd,bhsd->bhts', qc, kc) * w_intra
        num = jnp.einsum('bhts,bhsv->bhtv', s, vc) + w_inter[..., None] * jnp.einsum('bhtd,bhdv->bhtv', qc, C)
        nq = jnp.sum(s, axis=-1) + w_inter * jnp.einsum('bhtd,bhd->bht', qc, n)
        h = num / jnp.maximum(jnp.abs(nq), jnp.exp(-m_t))[..., None]
        w_last = w_intra[..., -1, :]
        C_new = w_inter[..., -1, None, None] * C + jnp.einsum('bhs,bhsd,bhsv->bhdv', w_last, kc, vc)
        n_new = w_inter[..., -1, None] * n + jnp.einsum('bhs,bhsd->bhd', w_last, kc)
        return (C_new, n_new, m_t[..., -1]), h

    (C, n, m), hb = lax.scan(chunk_step, (C0.astype(f32), n0.astype(f32), m0.astype(f32)),
                             (qb, kb, vb, ib, lfb))
    h = hb.transpose(1, 0, 3, 2, 4).reshape(bn, L, ML_HEADS, ML_HEAD_DIM)
    return h, C, n, m


def memory_kv(mem, g_mem, w_mem_k, w_mem_v):
    bn, nm, _ = mem.shape
    mn = rmsnorm(mem, g_mem)
    k = (mn @ w_mem_k).reshape(bn, nm, XA_HEADS, XA_HEAD_DIM)
    v = (mn @ w_mem_v).reshape(bn, nm, XA_HEADS, XA_HEAD_DIM)
    return k, v


def cross_attention(q, mem_k, mem_v):
    s = jnp.einsum('blhd,bmhd->bhlm', q, mem_k).astype(jnp.float32) * (XA_HEAD_DIM ** -0.5)
    p = jax.nn.softmax(s, axis=-1).astype(mem_v.dtype)
    return jnp.einsum('bhlm,bmhd->blhd', p, mem_v)


def hybrid_layer(x, mem_k, mem_v, s5_re, s5_im, C, n, m, lw):
    bn, L, _ = x.shape
    x = x + 0.5 * swiglu(rmsnorm(x, lw['g_ffn1']), lw['w1_gate'], lw['w1_up'], lw['w1_down'])
    h = rmsnorm(x, lw['g_mix'])
    z = h @ lw['w_in']
    u, q, k, v, o, ig, fg, qx, gates = jnp.split(z, IN_SPLIT, axis=-1)
    s5_out, s5_re, s5_im = s5_branch(u, s5_re, s5_im, lw['s5_lambda_re'], lw['s5_lambda_im'], lw['s5_log_step'],
                                     lw['s5_b_re'], lw['s5_b_im'], lw['s5_c_re'], lw['s5_c_im'], lw['s5_d'],
                                     lw['w_s5_glu'])
    hm, C, n, m = mlstm_branch(q, k, v, ig + lw['b_igate'], fg + lw['b_fgate'], C, n, m)
    hm = rmsnorm(hm.astype(x.dtype), lw['g_mlstm_head'].reshape(ML_HEADS, ML_HEAD_DIM)).reshape(bn, L, ML_WIDTH)
    ml_out = hm * jax.nn.sigmoid(o)
    xa_out = cross_attention(qx.reshape(bn, L, XA_HEADS, XA_HEAD_DIM), mem_k, mem_v).reshape(bn, L, XA_WIDTH)
    gts = jax.nn.sigmoid(gates).reshape(bn, L, N_BRANCH, D_MODEL)
    merged = (gts[:, :, 0] * (s5_out @ lw['w_br_s5'])
              + gts[:, :, 1] * (ml_out @ lw['w_br_ml'])
              + gts[:, :, 2] * (xa_out @ lw['w_br_xa']))
    x = x + merged @ lw['w_out']
    x = x + 0.5 * swiglu(rmsnorm(x, lw['g_ffn2']), lw['w2_gate'], lw['w2_up'], lw['w2_down'])
    return x, (s5_re, s5_im, C, n, m)


def setup_inputs(seed: int = 0) -> dict:
    key = jax.random.key(seed)
    ks = iter(jax.random.split(key, 64))
    f32 = jnp.float32

    def normal(shape, scale):
        return jax.random.normal(next(ks), shape, f32) * scale

    def dense(shape, fan_in):
        return normal(shape, fan_in ** -0.5)

    def gain(shape):
        return 1.0 + normal(shape, 0.02)

    Ld = DEPTH
    G, P, Hg = S5_GROUPS, S5_STATE, S5_GROUP
    inp = {}
    inp['x_prompt'] = normal((BATCH, SEQ, D_MODEL), 1.0)
    inp['x_sample'] = normal((DEC_BATCH, DEC_SEQ, D_MODEL), 1.0)
    inp['mem_prompt'] = normal((BATCH, N_MEM, D_MODEL), 1.0)
    inp['cache_mem_k'] = normal((Ld, DEC_BATCH, N_MEM, XA_HEADS, XA_HEAD_DIM), 1.0)
    inp['cache_mem_v'] = normal((Ld, DEC_BATCH, N_MEM, XA_HEADS, XA_HEAD_DIM), 1.0)
    inp['state_s5_re'] = normal((Ld, DEC_BATCH, G, P), 0.5)
    inp['state_s5_im'] = normal((Ld, DEC_BATCH, G, P), 0.5)
    inp['state_mlstm_C'] = normal((Ld, DEC_BATCH, ML_HEADS, ML_HEAD_DIM, ML_HEAD_DIM), 0.05)
    inp['state_mlstm_n'] = normal((Ld, DEC_BATCH, ML_HEADS, ML_HEAD_DIM), 0.1)
    inp['state_mlstm_m'] = jax.random.uniform(next(ks), (Ld, DEC_BATCH, ML_HEADS), f32, 0.0, 2.0)
    inp['g_ffn1'] = gain((Ld, D_MODEL))
    inp['w1_gate'] = dense((Ld, D_MODEL, FF_DIM), D_MODEL)
    inp['w1_up'] = dense((Ld, D_MODEL, FF_DIM), D_MODEL)
    inp['w1_down'] = dense((Ld, FF_DIM, D_MODEL), FF_DIM)
    inp['g_mix'] = gain((Ld, D_MODEL))
    inp['w_in'] = dense((Ld, D_MODEL, D_IN), D_MODEL)
    inp['s5_lambda_re'] = -0.5 + normal((Ld, G, P), 0.01)
    inp['s5_lambda_im'] = math.pi * jnp.broadcast_to(jnp.arange(P, dtype=f32), (Ld, G, P)) + normal((Ld, G, P), 0.01)
    inp['s5_log_step'] = jax.random.uniform(next(ks), (Ld, G), f32, math.log(DT_MIN), math.log(DT_MAX))
    inp['s5_b_re'] = dense((Ld, G, P, Hg), 2 * Hg)
    inp['s5_b_im'] = dense((Ld, G, P, Hg), 2 * Hg)
    inp['s5_c_re'] = dense((Ld, G, Hg, P), 2 * P)
    inp['s5_c_im'] = dense((Ld, G, Hg, P), 2 * P)
    inp['s5_d'] = normal((Ld, G, Hg), 0.5)
    inp['w_s5_glu'] = dense((Ld, S5_WIDTH, S5_WIDTH), S5_WIDTH)
    inp['b_igate'] = normal((Ld, ML_HEADS), 0.1)
    inp['b_fgate'] = 3.0 + normal((Ld, ML_HEADS), 0.1)
    inp['g_mlstm_head'] = gain((Ld, ML_WIDTH))
    inp['g_mem'] = gain((Ld, D_MODEL))
    inp['w_mem_k'] = dense((Ld, D_MODEL, XA_WIDTH), D_MODEL)
    inp['w_mem_v'] = dense((Ld, D_MODEL, XA_WIDTH), D_MODEL)
    inp['w_br_s5'] = dense((Ld, S5_WIDTH, D_MODEL), S5_WIDTH)
    inp['w_br_ml'] = dense((Ld, ML_WIDTH, D_MODEL), ML_WIDTH)
    inp['w_br_xa'] = dense((Ld, XA_WIDTH, D_MODEL), XA_WIDTH)
    inp['w_out'] = dense((Ld, D_MODEL, D_MODEL), D_MODEL)
    inp['g_ffn2'] = gain((Ld, D_MODEL))
    inp['w2_gate'] = dense((Ld, D_MODEL, FF_DIM), D_MODEL)
    inp['w2_up'] = dense((Ld, D_MODEL, FF_DIM), D_MODEL)
    inp['w2_down'] = dense((Ld, FF_DIM, D_MODEL), FF_DIM)
    inp['g_final'] = gain((D_MODEL,))
    return inp


def reference(x_prompt, x_sample, mem_prompt, cache_mem_k, cache_mem_v,
              state_s5_re, state_s5_im, state_mlstm_C, state_mlstm_n, state_mlstm_m,
              g_ffn1, w1_gate, w1_up, w1_down, g_mix, w_in,
              s5_lambda_re, s5_lambda_im, s5_log_step, s5_b_re, s5_b_im, s5_c_re, s5_c_im, s5_d, w_s5_glu,
              b_igate, b_fgate, g_mlstm_head, g_mem, w_mem_k, w_mem_v,
              w_br_s5, w_br_ml, w_br_xa, w_out, g_ffn2, w2_gate, w2_up, w2_down, g_final):
    f32 = jnp.float32
    bp = x_prompt.shape[0]
    xp, xs = x_prompt, x_sample
    prompt_rows, sample_rows = [], []
    for l in range(DEPTH):
        lw = dict(g_ffn1=g_ffn1[l], w1_gate=w1_gate[l], w1_up=w1_up[l], w1_down=w1_down[l],
                  g_mix=g_mix[l], w_in=w_in[l],
                  s5_lambda_re=s5_lambda_re[l], s5_lambda_im=s5_lambda_im[l], s5_log_step=s5_log_step[l],
                  s5_b_re=s5_b_re[l], s5_b_im=s5_b_im[l], s5_c_re=s5_c_re[l], s5_c_im=s5_c_im[l],
                  s5_d=s5_d[l], w_s5_glu=w_s5_glu[l],
                  b_igate=b_igate[l], b_fgate=b_fgate[l], g_mlstm_head=g_mlstm_head[l],
                  w_br_s5=w_br_s5[l], w_br_ml=w_br_ml[l], w_br_xa=w_br_xa[l], w_out=w_out[l],
                  g_ffn2=g_ffn2[l], w2_gate=w2_gate[l], w2_up=w2_up[l], w2_down=w2_down[l])
        mk_p, mv_p = memory_kv(mem_prompt, g_mem[l], w_mem_k[l], w_mem_v[l])
        s0 = jnp.zeros((bp, S5_GROUPS, S5_STATE), f32)
        C0 = jnp.zeros((bp, ML_HEADS, ML_HEAD_DIM, ML_HEAD_DIM), f32)
        n0 = jnp.zeros((bp, ML_HEADS, ML_HEAD_DIM), f32)
        m0 = jnp.zeros((bp, ML_HEADS), f32)
        xp, st_p = hybrid_layer(xp, mk_p, mv_p, s0, s0, C0, n0, m0, lw)
        prompt_rows.append((mk_p, mv_p) + st_p)
        xs, st_s = hybrid_layer(xs, cache_mem_k[l], cache_mem_v[l], state_s5_re[l], state_s5_im[l],
                                state_mlstm_C[l], state_mlstm_n[l], state_mlstm_m[l], lw)
        sample_rows.append(st_s)
    mk_p, mv_p, s5r_p, s5i_p, C_p, n_p, m_p = [jnp.stack(a) for a in zip(*prompt_rows)]
    s5r_s, s5i_s, C_s, n_s, m_s = [jnp.stack(a) for a in zip(*sample_rows)]
    y_prompt = rmsnorm(xp, g_final)
    y_sample = rmsnorm(xs, g_final)
    return (y_prompt, y_sample, mk_p, mv_p, s5r_p, s5i_p, C_p, n_p, m_p, s5r_s, s5i_s, C_s, n_s, m_s)
```

```python
import functools

import numpy as np
import jax
import jax.numpy as jnp
from jax import lax
from jax.experimental import pallas as pl
from jax.experimental.pallas import tpu as pltpu

F32 = jnp.float32
BF16 = jnp.bfloat16

D = 2048
BATCH = 4
SEQ = 2048
NS = 128
NP = BATCH * SEQ
TM = 512
NPT = NP // TM
NT = NPT + 1
MROWS = NT * TM
N_MEM = 256
FF = 5504
FFP = 5632
TF = 512
G = 64
P = 64
HG = 16
TC = 16
GP = G // 2
NCH = SEQ // TC
W = 1024
NH = 4
DH = 256
CL = 256
NCL = SEQ // CL
EPS = 1e-6
VMEM_LIMIT = 56 * 1024 * 1024


def _cp(sem, vmem=VMEM_LIMIT):
    return pltpu.CompilerParams(dimension_semantics=sem, vmem_limit_bytes=vmem)


def _dot(a, b):
    return jnp.dot(a, b, preferred_element_type=F32)


def _dot_nt(a, b):
    return lax.dot_general(a, b, (((1,), (1,)), ((), ())), preferred_element_type=F32)


def _dot_tn(a, b):
    return lax.dot_general(a, b, (((0,), (0,)), ((), ())), preferred_element_type=F32)


def _hi_lo(x):
    hi = x.astype(BF16)
    lo = (x - hi.astype(F32)).astype(BF16)
    return hi, lo


def _split3(x):
    hi = x.astype(BF16)
    r1 = x - hi.astype(F32)
    mid = r1.astype(BF16)
    lo = (r1 - mid.astype(F32)).astype(BF16)
    return hi, mid, lo


def _dot3(a, b):
    ah, al = _hi_lo(a)
    bh, bl = _hi_lo(b)
    return _dot(ah, bh) + (_dot(ah, bl) + _dot(al, bh))


def _dot3_nt(a, b):
    ah, al = _hi_lo(a)
    bh, bl = _hi_lo(b)
    return _dot_nt(ah, bh) + (_dot_nt(ah, bl) + _dot_nt(al, bh))


def _rms(x, g):
    r = lax.rsqrt(jnp.mean(x * x, axis=-1, keepdims=True) + EPS)
    return (x * r) * g


def _sigmoid(x):
    return 1.0 / (1.0 + jnp.exp(-x))


def _log_sigmoid(x):
    return jnp.minimum(x, 0.0) - jnp.log1p(jnp.exp(-jnp.abs(x)))


def _gelu_tanh(x):
    return x * (0.5 * (1.0 + jnp.tanh(0.7978845608028654 * (x + 0.044715 * (x * x * x)))))


def _ffn_kernel(*refs, two_src, final_norm):
    refs = list(refs)
    if two_src:
        xp_ref, xs_ref = refs[:2]
        refs = refs[2:]
    else:
        x_ref = refs[0]
        refs = refs[1:]
    g_ref, wg_ref, wu_ref, wd_ref = refs[:4]
    refs = refs[4:]
    if final_norm:
        gf_ref, op_ref, os_ref, h_scr, acc_scr = refs
    else:
        o_ref, h_scr, acc_scr = refs
    i = pl.program_id(0)
    j = pl.program_id(1)

    def load_x():
        if two_src:
            return jnp.where(i < NPT, xp_ref[...], xs_ref[...])
        return x_ref[...]

    @pl.when(j == 0)
    def _():
        h_scr[...] = _rms(load_x(), g_ref[...]).astype(BF16)
        acc_scr[...] = jnp.zeros_like(acc_scr)

    h = h_scr[...]
    gt = _dot(h, wg_ref[...])
    up = _dot(h, wu_ref[...])
    hid = (gt * _sigmoid(gt)) * up
    acc_scr[...] += _dot(hid.astype(BF16), wd_ref[...])

    @pl.when(j == pl.num_programs(1) - 1)
    def _():
        y = load_x() + 0.5 * acc_scr[...]
        if final_norm:
            y = _rms(y, gf_ref[...])

            @pl.when(i < NPT)
            def _():
                op_ref[...] = y

            @pl.when(i == NPT)
            def _():
                os_ref[...] = y
        else:
            o_ref[...] = y


def _ffn(xs, g, wg, wu, wd, g_final=None):
    two_src = len(xs) == 2
    final_norm = g_final is not None
    row = pl.BlockSpec((TM, D), lambda i, j: (i, 0))
    prow = pl.BlockSpec((TM, D), lambda i, j: (jnp.minimum(i, NPT - 1), 0))
    srow = pl.BlockSpec((TM, D), lambda i, j: (0, 0))
    vec = pl.BlockSpec((1, D), lambda i, j: (0, 0))
    in_specs = ([prow, srow] if two_src else [row]) + [
        vec,
        pl.BlockSpec((D, TF), lambda i, j: (0, j)),
        pl.BlockSpec((D, TF), lambda i, j: (0, j)),
        pl.BlockSpec((TF, D), lambda i, j: (j, 0)),
    ]
    args = list(xs) + [g, wg, wu, wd]
    if final_norm:
        in_specs.append(vec)
        args.append(g_final)
        out_shape = (jax.ShapeDtypeStruct((NP, D), F32), jax.ShapeDtypeStruct((TM, D), F32))
        out_specs = (prow, srow)
    else:
        out_shape = jax.ShapeDtypeStruct((MROWS, D), F32)
        out_specs = row
    return pl.pallas_call(
        functools.partial(_ffn_kernel, two_src=two_src, final_norm=final_norm),
        grid=(NT, FFP // TF),
        in_specs=in_specs,
        out_specs=out_specs,
        out_shape=out_shape,
        scratch_shapes=[pltpu.VMEM((TM, D), BF16), pltpu.VMEM((TM, D), F32)],
        compiler_params=_cp(("arbitrary", "arbitrary")),
        name="ffn_final" if final_norm else "ffn",
    )(*args)


def _proj_kernel(x_ref, g_ref, w_ref, *rest, with_gates):
    if with_gates:
        wif_ref, wift_ref, o_ref, zif_ref, zt_ref, h_scr = rest
    else:
        o_ref, h_scr = rest

    @pl.when(pl.program_id(1) == 0)
    def _():
        h = _rms(x_ref[...], g_ref[...]).astype(BF16)
        h_scr[...] = h
        if with_gates:
            zif_ref[...] = _dot(h, wif_ref[...])
            zt_ref[...] = _dot_nt(wift_ref[...], h)

    o_ref[...] = _dot(h_scr[...], w_ref[...])


def _proj(x, g, w, tn, w_if=None, w_ift=None):
    m, n = x.shape[0], w.shape[1]
    with_gates = w_if is not None
    in_specs = [
        pl.BlockSpec((TM, D), lambda i, j: (i, 0)),
        pl.BlockSpec((1, D), lambda i, j: (0, 0)),
        pl.BlockSpec((D, tn), lambda i, j: (0, j)),
    ]
    args = [x, g, w]
    out_shape = [jax.ShapeDtypeStruct((m, n), F32)]
    out_specs = [pl.BlockSpec((TM, tn), lambda i, j: (i, j))]
    if with_gates:
        in_specs += [pl.BlockSpec((D, 128), lambda i, j: (0, 0)), pl.BlockSpec((8, D), lambda i, j: (0, 0))]
        args += [w_if, w_ift]
        out_shape += [jax.ShapeDtypeStruct((m, 128), F32), jax.ShapeDtypeStruct((8, m), F32)]
        out_specs += [pl.BlockSpec((TM, 128), lambda i, j: (i, 0)), pl.BlockSpec((8, TM), lambda i, j: (0, i))]
    return pl.pallas_call(
        functools.partial(_proj_kernel, with_gates=with_gates),
        grid=(m // TM, n // tn),
        in_specs=in_specs,
        out_specs=out_specs,
        out_shape=out_shape,
        scratch_shapes=[pltpu.VMEM((TM, D), BF16)],
        compiler_params=_cp(("arbitrary", "arbitrary")),
        name="in_proj" if with_gates else "mem_proj",
    )(*args)


GB = 8


def _s5_prep_kernel(lr_ref, li_ref, ls_ref, btr_ref, bti_ref, cr_ref, ci_ref, d_ref,
                    ar_ref, ai_ref, a16r_ref, a16i_ref, bbr_ref, bbi_ref,
                    er_ref, ei_ref, ftr_ref, ftin_ref, kern_ref):
    lr = lr_ref[...]
    li = li_ref[...]
    dt = jnp.exp(ls_ref[...])

    def power(k):
        mag = jnp.exp(lr * dt * float(k))
        ang = li * dt * float(k)
        return mag * jnp.cos(ang), mag * jnp.sin(ang)

    pw = [power(k) for k in range(TC + 1)]
    ar, ai = pw[1]
    den = lr * lr + li * li
    nr = ar - 1.0
    z_re = (nr * lr + ai * li) / den
    z_im = (ai * lr - nr * li) / den
    btr = btr_ref[...]
    bti = bti_ref[...]
    bbr = z_re * btr - z_im * bti
    bbi = z_re * bti + z_im * btr
    cr = cr_ref[...]
    ci = ci_ref[...]

    ar_ref[...] = ar
    ai_ref[...] = ai
    a16r_ref[...] = pw[TC][0]
    a16i_ref[...] = pw[TC][1]
    bbr_ref[...] = bbr
    bbi_ref[...] = bbi

    def cmul(xr, xi, k):
        pr, pi = pw[k]
        return xr * pr - xi * pi, xr * pi + xi * pr

    e = [cmul(bbr, bbi, TC - 1 - s) for s in range(TC)]
    er_ref[...] = jnp.concatenate([v[0] for v in e], axis=1)
    ei_ref[...] = jnp.concatenate([v[1] for v in e], axis=1)
    ft = [cmul(cr, ci, t + 1) for t in range(TC)]
    ftr_ref[...] = jnp.concatenate([v[0] for v in ft], axis=1)
    ftin_ref[...] = -jnp.concatenate([v[1] for v in ft], axis=1)
    rk = [cmul(cr, ci, k) for k in range(TC)]
    rr = jnp.concatenate([v[0] for v in rk], axis=1)
    ri = jnp.concatenate([v[1] for v in rk], axis=1)
    row = lax.broadcasted_iota(jnp.int32, (HG, TC * HG), 0)
    col = lax.broadcasted_iota(jnp.int32, (HG, TC * HG), 1)
    for gidx in range(GB):
        kern = _dot3_nt(bbr[gidx], rr[gidx]) - _dot3_nt(bbi[gidx], ri[gidx])
        kern_ref[gidx] = kern + jnp.where(row == col, d_ref[gidx], 0.0)


def _s5_prep(lam_re, lam_im, log_step, bt_re, bt_im, c_re, c_im, d):
    def spec(*shape):
        nd = len(shape)
        return pl.BlockSpec((GB,) + shape, lambda i: (i,) + (0,) * nd)

    def sds(*shape):
        return jax.ShapeDtypeStruct((G,) + shape, F32)

    return pl.pallas_call(
        _s5_prep_kernel,
        grid=(G // GB,),
        in_specs=[spec(1, P), spec(1, P), spec(1, 1), spec(HG, P), spec(HG, P), spec(HG, P), spec(HG, P),
                  spec(HG, 1)],
        out_specs=[spec(1, P)] * 4 + [spec(HG, P)] * 2 + [spec(TC * HG, P)] * 4 + [spec(HG, TC * HG)],
        out_shape=[sds(1, P)] * 4 + [sds(HG, P)] * 2 + [sds(TC * HG, P)] * 4 + [sds(HG, TC * HG)],
        compiler_params=_cp(("arbitrary",)),
        name="s5_prep",
    )(lam_re, lam_im, log_step, bt_re, bt_im, c_re, c_im, d)


def _s5_chunk_kernel(u_ref, m_ref, e_ref, f_ref, ar_ref, ai_ref, y_ref, fin_ref, s_scr, xs_scr):
    u = u_ref[0]
    uh, ul = _hi_lo(u)

    def mm3(bmat):
        bh, bl = _hi_lo(bmat)
        return _dot(uh, bh) + (_dot(uh, bl) + _dot(ul, bh))

    s_scr[...] = mm3(e_ref[0])
    ar = ar_ref[0]
    ai = ai_ref[0]
    xr = jnp.zeros((BATCH, 2 * P), F32)
    xi = jnp.zeros((BATCH, 2 * P), F32)
    for c in range(NCH):
        r0 = c * BATCH
        xs_scr[r0:r0 + BATCH, 0:2 * P] = xr
        xs_scr[r0:r0 + BATCH, 2 * P:4 * P] = xi
        sr = s_scr[r0:r0 + BATCH, 0:2 * P]
        si = s_scr[r0:r0 + BATCH, 2 * P:4 * P]
        xr, xi = ar * xr - ai * xi + sr, ar * xi + ai * xr + si
    fin_ref[0, :, 0:2 * P] = xr
    fin_ref[0, :, 2 * P:4 * P] = xi
    y_ref[0] = mm3(m_ref[0]) + _dot3(xs_scr[...], f_ref[0])


def _s5_chunk(u2, m2, e2, f2, a16r, a16i):
    rows = NCH * BATCH
    wide = 2 * TC * HG
    return pl.pallas_call(
        _s5_chunk_kernel,
        grid=(GP,),
        in_specs=[
            pl.BlockSpec((1, rows, wide), lambda i: (i, 0, 0)),
            pl.BlockSpec((1, wide, wide), lambda i: (i, 0, 0)),
            pl.BlockSpec((1, wide, 4 * P), lambda i: (i, 0, 0)),
            pl.BlockSpec((1, 4 * P, wide), lambda i: (i, 0, 0)),
            pl.BlockSpec((1, 1, 2 * P), lambda i: (i, 0, 0)),
            pl.BlockSpec((1, 1, 2 * P), lambda i: (i, 0, 0)),
        ],
        out_specs=[
            pl.BlockSpec((1, rows, wide), lambda i: (i, 0, 0)),
            pl.BlockSpec((1, BATCH, 4 * P), lambda i: (i, 0, 0)),
        ],
        out_shape=[jax.ShapeDtypeStruct((GP, rows, wide), F32), jax.ShapeDtypeStruct((GP, BATCH, 4 * P), F32)],
        scratch_shapes=[pltpu.VMEM((rows, 4 * P), F32), pltpu.VMEM((rows, 4 * P), F32)],
        compiler_params=_cp(("arbitrary",)),
        name="s5_chunk",
    )(u2, m2, e2, f2, a16r, a16i)


S5C = 8


def _s5_step_kernel(u_ref, sr_ref, si_ref, ar_ref, ai_ref, bbr_ref, bbi_ref, ctr_ref, cti_ref, d_ref,
                    y_ref, xr_ref, xi_ref):
    wu = S5C * HG
    ws = S5C * P
    r1 = lax.broadcasted_iota(jnp.int32, (wu, ws), 0) // HG
    c1 = lax.broadcasted_iota(jnp.int32, (wu, ws), 1) // P
    mask_in = r1 == c1
    r2 = lax.broadcasted_iota(jnp.int32, (ws, wu), 0) // P
    c2 = lax.broadcasted_iota(jnp.int32, (ws, wu), 1) // HG
    mask_out = r2 == c2
    for j in range(G // S5C):
        us = u_ref[:, j * wu:(j + 1) * wu]
        sl = slice(j * ws, (j + 1) * ws)
        bre = jnp.where(mask_in, bbr_ref[:, sl], 0.0)
        bim = jnp.where(mask_in, bbi_ref[:, sl], 0.0)
        ar = ar_ref[:, sl]
        ai = ai_ref[:, sl]
        s_re = sr_ref[:, sl]
        s_im = si_ref[:, sl]
        x_re = ar * s_re - ai * s_im + _dot3(us, bre)
        x_im = ar * s_im + ai * s_re + _dot3(us, bim)
        xr_ref[:, sl] = x_re
        xi_ref[:, sl] = x_im
        cre = jnp.where(mask_out, ctr_ref[sl, :], 0.0)
        cim = jnp.where(mask_out, cti_ref[sl, :], 0.0)
        y_ref[:, j * wu:(j + 1) * wu] = (_dot3(x_re, cre) - _dot3(x_im, cim)
                                         + d_ref[:, j * wu:(j + 1) * wu] * us)


def _s5_step(z_main, s_re, s_im, a_re, a_im, bbt_re, bbt_im, ct_re, ct_im, d_row):
    full = lambda *shape: pl.BlockSpec(shape, lambda i: (0,) * len(shape))
    return pl.pallas_call(
        _s5_step_kernel,
        grid=(1,),
        in_specs=[pl.BlockSpec((NS, W), lambda i: (NP // NS, 0)),
                  full(NS, G * P), full(NS, G * P), full(1, G * P), full(1, G * P),
                  full(S5C * HG, G * P), full(S5C * HG, G * P), full(G * P, S5C * HG), full(G * P, S5C * HG),
                  full(1, W)],
        out_specs=[full(NS, W), full(NS, G * P), full(NS, G * P)],
        out_shape=[jax.ShapeDtypeStruct((NS, W), F32), jax.ShapeDtypeStruct((NS, G * P), F32),
                   jax.ShapeDtypeStruct((NS, G * P), F32)],
        compiler_params=_cp(("arbitrary",)),
        name="s5_step",
    )(z_main, s_re, s_im, a_re, a_im, bbt_re, bbt_im, ct_re, ct_im, d_row)


def _mlstm_chunk_kernel(q_ref, k_ref, v_ref, o_ref, zif_ref, zt_ref, brow_ref, bcol_ref, gh_ref,
                        h_ref, c_ref, n_ref, m_ref):
    @pl.when(pl.program_id(1) == 0)
    def _():
        c_ref[...] = jnp.zeros_like(c_ref)
        n_ref[...] = jnp.zeros_like(n_ref)
        m_ref[...] = jnp.zeros_like(m_ref)

    zi = zif_ref[...] + brow_ref[...]
    zt = zt_ref[...] + bcol_ref[...]
    rr = lax.broadcasted_iota(jnp.int32, (CL, CL), 0)
    cc = lax.broadcasted_iota(jnp.int32, (CL, CL), 1)
    causal = cc <= rr
    tril = causal.astype(BF16)
    triu = (rr <= cc).astype(BF16)
    lfc = _split3(_log_sigmoid(zi))
    bcum_col = _dot(tril, lfc[0]) + (_dot(tril, lfc[1]) + _dot(tril, lfc[2]))
    lfr = _split3(_log_sigmoid(zt))
    bcum_row = _dot(lfr[0], triu) + (_dot(lfr[1], triu) + _dot(lfr[2], triu))

    for hh in range(NH):
        hs = slice(hh * DH, (hh + 1) * DH)
        bc = bcum_col[:, NH + hh:NH + hh + 1]
        ic = zi[:, hh:hh + 1]
        br = bcum_row[NH + hh:NH + hh + 1, :]
        ir = zt[hh:hh + 1, :]
        m_prev = m_ref[0, hh:hh + 1, 0:1]
        g_inter = bc + m_prev
        dlog = jnp.where(causal, (bc - br) + ir, -jnp.inf)
        m_t = jnp.maximum(g_inter, jnp.max(dlog, axis=-1, keepdims=True))
        w_inter = jnp.exp(g_inter - m_t)
        w_intra = jnp.exp(dlog - m_t)
        qf = q_ref[:, hs]
        kf = k_ref[:, hs]
        qb = qf.astype(BF16)
        kb = kf.astype(BF16)
        vb = v_ref[:, hs].astype(BF16)
        s = _dot_nt(qb, kb) * (w_intra * (DH ** -0.5))
        c_prev = c_ref[0, hh]
        n_prev = n_ref[0, hh:hh + 1, :]
        num = _dot(s.astype(BF16), vb) + w_inter * _dot(qb, c_prev.astype(BF16))
        nq = jnp.sum(s, axis=-1, keepdims=True) + w_inter * jnp.sum(qf * n_prev, axis=-1, keepdims=True)
        h = num / jnp.maximum(jnp.abs(nq), jnp.exp(-m_t))
        m_last = m_t[CL - 1:CL, :]
        w_last = jnp.exp((bc[CL - 1:CL, :] - bc) + ic - m_last) * (DH ** -0.5)
        wi_last = w_inter[CL - 1:CL, :]
        kw = kf * w_last
        c_ref[0, hh] = wi_last * c_prev + _dot_tn(kw.astype(BF16), vb)
        n_ref[0, hh:hh + 1, :] = wi_last * n_prev + jnp.sum(kw, axis=0, keepdims=True)
        m_ref[0, hh:hh + 1, :] = jnp.broadcast_to(m_last, (1, 128))
        hn = _rms(h, gh_ref[:, hs])
        h_ref[:, hs] = (hn * _sigmoid(o_ref[:, hs])).astype(BF16)


def _mlstm_chunk(z_main, z_if, z_t, brow, bcol, gh):
    blk = lambda col: pl.BlockSpec((CL, W), lambda b, c: (b * NCL + c, col))
    return pl.pallas_call(
        _mlstm_chunk_kernel,
        grid=(BATCH, NCL),
        in_specs=[blk(1), blk(2), blk(3), blk(4),
                  pl.BlockSpec((CL, 128), lambda b, c: (b * NCL + c, 0)),
                  pl.BlockSpec((8, CL), lambda b, c: (0, b * NCL + c)),
                  pl.BlockSpec((1, 128), lambda b, c: (0, 0)),
                  pl.BlockSpec((8, 1), lambda b, c: (0, 0)),
                  pl.BlockSpec((1, W), lambda b, c: (0, 0))],
        out_specs=[pl.BlockSpec((CL, W), lambda b, c: (b * NCL + c, 0)),
                   pl.BlockSpec((1, NH, DH, DH), lambda b, c: (b, 0, 0, 0)),
                   pl.BlockSpec((1, NH, DH), lambda b, c: (b, 0, 0)),
                   pl.BlockSpec((1, NH, 128), lambda b, c: (b, 0, 0))],
        out_shape=[jax.ShapeDtypeStruct((NP, W), BF16),
                   jax.ShapeDtypeStruct((BATCH, NH, DH, DH), F32),
                   jax.ShapeDtypeStruct((BATCH, NH, DH), F32),
                   jax.ShapeDtypeStruct((BATCH, NH, 128), F32)],
        compiler_params=_cp(("arbitrary", "arbitrary")),
        name="mlstm_chunk",
    )(z_main, z_main, z_main, z_main, z_if, z_t, brow, bcol, gh)


SB = 8


def _mlstm_step_kernel(q_ref, k_ref, v_ref, o_ref, zif_ref, brow_ref, gh_ref, c0_ref, n0_ref, m0_ref,
                       h_ref, c_ref, n_ref, m_ref):
    zi = zif_ref[...] + brow_ref[...]
    pad = jnp.zeros((128 - SB, DH), F32)
    for hh in range(NH):
        hs = slice(hh * DH, (hh + 1) * DH)
        q8 = q_ref[:, hs]
        k8 = k_ref[:, hs]
        qT = jnp.concatenate([q8, pad], axis=0).T
        kT = jnp.concatenate([k8, pad], axis=0).T
        for s in range(SB):
            q_row = q8[s:s + 1, :]
            k_row = k8[s:s + 1, :]
            v_row = v_ref[s:s + 1, hs]
            q_col = qT[:, s:s + 1]
            k_col = kT[:, s:s + 1]
            ig = zi[s:s + 1, hh:hh + 1]
            lf = _log_sigmoid(zi[s:s + 1, NH + hh:NH + hh + 1])
            g_inter = lf + m0_ref[s:s + 1, hh:hh + 1]
            m_t = jnp.maximum(g_inter, ig)
            w_inter = jnp.exp(g_inter - m_t)
            w_intra = jnp.exp(ig - m_t) * (DH ** -0.5)
            c_prev = c0_ref[s, hh]
            n_prev = n0_ref[s, hh:hh + 1, :]
            sv = jnp.sum(q_row * k_row, axis=-1, keepdims=True) * w_intra
            q_c = jnp.sum(q_col * c_prev, axis=0, keepdims=True)
            num = sv * v_row + w_inter * q_c
            nq = sv + w_inter * jnp.sum(q_row * n_prev, axis=-1, keepdims=True)
            h = num / jnp.maximum(jnp.abs(nq), jnp.exp(-m_t))
            c_ref[s, hh] = w_inter * c_prev + (k_col * w_intra) * v_row
            n_ref[s, hh:hh + 1, :] = w_inter * n_prev + w_intra * k_row
            m_ref[s:s + 1, hh:hh + 1] = m_t
            hn = _rms(h, gh_ref[:, hs])
            h_ref[s:s + 1, hs] = hn * _sigmoid(o_ref[s:s + 1, hs])


def _mlstm_step(z_main, z_if, brow, gh, c0, n0, m0):
    base = NP // SB
    blk = lambda col: pl.BlockSpec((SB, W), lambda i: (base + i, col))
    return pl.pallas_call(
        _mlstm_step_kernel,
        grid=(NS // SB,),
        in_specs=[blk(1), blk(2), blk(3), blk(4),
                  pl.BlockSpec((SB, 128), lambda i: (base + i, 0)),
                  pl.BlockSpec((1, 128), lambda i: (0, 0)),
                  pl.BlockSpec((1, W), lambda i: (0, 0)),
                  pl.BlockSpec((SB, NH, DH, DH), lambda i: (i, 0, 0, 0)),
                  pl.BlockSpec((SB, NH, DH), lambda i: (i, 0, 0)),
                  pl.BlockSpec((SB, NH), lambda i: (i, 0))],
        out_specs=[pl.BlockSpec((SB, W), lambda i: (i, 0)),
                   pl.BlockSpec((SB, NH, DH, DH), lambda i: (i, 0, 0, 0)),
                   pl.BlockSpec((SB, NH, DH), lambda i: (i, 0, 0)),
                   pl.BlockSpec((SB, NH), lambda i: (i, 0))],
        out_shape=[jax.ShapeDtypeStruct((NS, W), F32),
                   jax.ShapeDtypeStruct((NS, NH, DH, DH), F32),
                   jax.ShapeDtypeStruct((NS, NH, DH), F32),
                   jax.ShapeDtypeStruct((NS, NH), F32)],
        compiler_params=_cp(("arbitrary",)),
        name="mlstm_step",
    )(z_main, z_main, z_main, z_main, z_if, brow, gh, c0, n0, m0)


def _softmax_rows(s):
    e = jnp.exp(s - jnp.max(s, axis=-1, keepdims=True))
    return e / jnp.sum(e, axis=-1, keepdims=True)


def _xattn_prompt_kernel(q_ref, k_ref, v_ref, o_ref):
    for hh in range(NH):
        hs = slice(hh * DH, (hh + 1) * DH)
        s = _dot_nt(q_ref[:, hs].astype(BF16), k_ref[:, hs].astype(BF16)) * (DH ** -0.5)
        p = _softmax_rows(s)
        o_ref[:, hs] = _dot(p.astype(BF16), v_ref[:, hs].astype(BF16)).astype(BF16)


def _xattn_prompt(z_main, kv):
    nt = SEQ // TM
    return pl.pallas_call(
        _xattn_prompt_kernel,
        grid=(BATCH, nt),
        in_specs=[pl.BlockSpec((TM, W), lambda b, t: (b * nt + t, 5)),
                  pl.BlockSpec((N_MEM, W), lambda b, t: (b, 0)),
                  pl.BlockSpec((N_MEM, W), lambda b, t: (b, 1))],
        out_specs=pl.BlockSpec((TM, W), lambda b, t: (b * nt + t, 0)),
        out_shape=jax.ShapeDtypeStruct((NP, W), BF16),
        compiler_params=_cp(("arbitrary", "arbitrary")),
        name="xattn_prompt",
    )(z_main, kv, kv)


def _xattn_step_kernel(q_ref, k_ref, v_ref, o_ref):
    for hh in range(NH):
        hs = slice(hh * DH, (hh + 1) * DH)
        q8 = q_ref[:, hs].astype(BF16)
        for s in range(SB):
            sc = _dot_nt(q8, k_ref[s, :, hs].astype(BF16))[s:s + 1, :] * (DH ** -0.5)
            p = _softmax_rows(sc)
            p8 = jnp.broadcast_to(p, (SB, N_MEM)).astype(BF16)
            o_ref[s:s + 1, hs] = _dot(p8, v_ref[s, :, hs].astype(BF16))[0:1, :]


def _xattn_step(z_main, mem_k, mem_v):
    base = NP // SB
    return pl.pallas_call(
        _xattn_step_kernel,
        grid=(NS // SB,),
        in_specs=[pl.BlockSpec((SB, W), lambda i: (base + i, 5)),
                  pl.BlockSpec((SB, N_MEM, W), lambda i: (i, 0, 0)),
                  pl.BlockSpec((SB, N_MEM, W), lambda i: (i, 0, 0))],
        out_specs=pl.BlockSpec((SB, W), lambda i: (i, 0)),
        out_shape=jax.ShapeDtypeStruct((NS, W), F32),
        compiler_params=_cp(("arbitrary",)),
        name="xattn_step",
    )(z_main, mem_k, mem_v)


TMX = 512


def _mix_kernel(x_ref, g_ref, yp_ref, ys_ref, mp_ref, ms_ref, ap_ref, as_ref, wglu_ref,
                wg0_ref, wg1_ref, wg2_ref, wb0_ref, wb1_ref, wb2_ref, o_ref, h_scr, s5_scr):
    i = pl.program_id(0)
    prompt = i < NPT

    @pl.when(pl.program_id(1) == 0)
    def _():
        h_scr[...] = _rms(x_ref[...], g_ref[...]).astype(BF16)
        y = _gelu_tanh(jnp.where(prompt, yp_ref[...], ys_ref[...]).astype(F32))
        s5_scr[...] = (y * _sigmoid(_dot(y.astype(BF16), wglu_ref[...]))).astype(BF16)

    h = h_scr[...]
    ml = jnp.where(prompt, mp_ref[...], ms_ref[...])
    xa = jnp.where(prompt, ap_ref[...], as_ref[...])
    merged = (_sigmoid(_dot(h, wg0_ref[...])) * _dot(s5_scr[...], wb0_ref[...])
              + _sigmoid(_dot(h, wg1_ref[...])) * _dot(ml, wb1_ref[...])
              + _sigmoid(_dot(h, wg2_ref[...])) * _dot(xa, wb2_ref[...]))
    o_ref[...] = merged.astype(BF16)


def _mix(x1, g, yp, ys, mp, ms, ap, as_, wglu, wg, wb):
    prow = pl.BlockSpec((TM, W), lambda i, j: (jnp.minimum(i, NPT - 1), 0))
    srow = pl.BlockSpec((TM, W), lambda i, j: (0, 0))
    wgs = pl.BlockSpec((D, TMX), lambda i, j: (0, j))
    wbs = pl.BlockSpec((W, TMX), lambda i, j: (0, j))
    return pl.pallas_call(
        _mix_kernel,
        grid=(NT, D // TMX),
        in_specs=[pl.BlockSpec((TM, D), lambda i, j: (i, 0)), pl.BlockSpec((1, D), lambda i, j: (0, 0)),
                  prow, srow, prow, srow, prow, srow,
                  pl.BlockSpec((W, W), lambda i, j: (0, 0)),
                  wgs, wgs, wgs, wbs, wbs, wbs],
        out_specs=pl.BlockSpec((TM, TMX), lambda i, j: (i, j)),
        out_shape=jax.ShapeDtypeStruct((MROWS, D), BF16),
        scratch_shapes=[pltpu.VMEM((TM, D), BF16), pltpu.VMEM((TM, W), BF16)],
        compiler_params=_cp(("arbitrary", "arbitrary")),
        name="mix",
    )(x1, g, yp, ys, mp, ms, ap, as_, wglu, *wg, *wb)


def _outproj_kernel(x_ref, m_ref, w_ref, o_ref):
    o_ref[...] = x_ref[...] + _dot(m_ref[...], w_ref[...])


def _outproj(x1, merged, w_out):
    return pl.pallas_call(
        _outproj_kernel,
        grid=(NT,),
        in_specs=[pl.BlockSpec((TM, D), lambda i: (i, 0)), pl.BlockSpec((TM, D), lambda i: (i, 0)),
                  pl.BlockSpec((D, D), lambda i: (0, 0))],
        out_specs=pl.BlockSpec((TM, D), lambda i: (i, 0)),
        out_shape=jax.ShapeDtypeStruct((MROWS, D), F32),
        compiler_params=_cp(("arbitrary",)),
        name="out_proj",
    )(x1, merged, w_out)


def _blockdiag2(a):
    r, c = a.shape[1:]
    a = a.reshape(GP, 2, r, c)
    z = jnp.zeros((GP, r, c), a.dtype)
    top = jnp.concatenate([a[:, 0], z], axis=-1)
    bot = jnp.concatenate([z, a[:, 1]], axis=-1)
    return jnp.concatenate([top, bot], axis=-2)


def _pad_rows(a, rows):
    return jnp.pad(a, ((0, rows - a.shape[0]), (0, 0)))


def kernel(x_prompt, x_sample, mem_prompt, cache_mem_k, cache_mem_v, state_s5_re, state_s5_im, state_mlstm_C,
           state_mlstm_n, state_mlstm_m, g_ffn1, w1_gate, w1_up, w1_down, g_mix, w_in, s5_lambda_re,
           s5_lambda_im, s5_log_step, s5_b_re, s5_b_im, s5_c_re, s5_c_im, s5_d, w_s5_glu, b_igate, b_fgate,
           g_mlstm_head, g_mem, w_mem_k, w_mem_v, w_br_s5, w_br_ml, w_br_xa, w_out, g_ffn2, w2_gate, w2_up,
           w2_down, g_final):
    bf = lambda a: a.astype(BF16)
    padc = lambda a: jnp.pad(a, ((0, 0), (0, FFP - FF)))
    padr = lambda a: jnp.pad(a, ((0, FFP - FF), (0, 0)))

    wi = w_in[0]
    w_main = bf(jnp.concatenate([wi[:, :5 * W], wi[:, 5 * W + 2 * NH:6 * W + 2 * NH]], axis=1))
    w_gate_cols = wi[:, 5 * W:5 * W + 2 * NH]
    w_if = bf(jnp.pad(w_gate_cols, ((0, 0), (0, 128 - 2 * NH))))
    w_ift = bf(w_gate_cols.T)
    g0 = 6 * W + 2 * NH
    w_bg = [bf(wi[:, g0 + b * D:g0 + (b + 1) * D]) for b in range(3)]
    w_br = [bf(w_br_s5[0]), bf(w_br_ml[0]), bf(w_br_xa[0])]
    brow = jnp.pad(jnp.concatenate([b_igate[0], b_fgate[0]]), (0, 128 - 2 * NH)).reshape(1, 128)
    bcol = jnp.concatenate([b_igate[0], b_fgate[0]]).reshape(2 * NH, 1)
    gh = g_mlstm_head[0].reshape(1, W)

    xp = x_prompt.reshape(NP, D)
    xs = _pad_rows(x_sample.reshape(NS, D), TM)
    x1 = _ffn((xp, xs), g_ffn1[0].reshape(1, D), padc(bf(w1_gate[0])), padc(bf(w1_up[0])), padr(bf(w1_down[0])))

    z_main, z_if, z_t = _proj(x1, g_mix[0].reshape(1, D), w_main, W, w_if, w_ift)

    lam_re = s5_lambda_re[0].reshape(G, 1, P)
    lam_im = s5_lambda_im[0].reshape(G, 1, P)
    (a_re, a_im, a16_re, a16_im, bbt_re, bbt_im, e_re, e_im, ft_re, ft_imn, kern) = _s5_prep(
        lam_re, lam_im, s5_log_step[0].reshape(G, 1, 1),
        s5_b_re[0].transpose(0, 2, 1), s5_b_im[0].transpose(0, 2, 1), s5_c_re[0], s5_c_im[0],
        s5_d[0].reshape(G, HG, 1))
    sig = np.arange(TC)[:, None]
    tau = np.arange(TC)[None, :]
    lag = np.clip(tau - sig, 0, TC - 1)
    kern4 = kern.reshape(G, HG, TC, HG)
    m_g = jnp.where((tau >= sig)[None, None, :, :, None], kern4[:, :, lag, :], 0.0)
    m_g = m_g.transpose(0, 2, 1, 3, 4).reshape(G, TC * HG, TC * HG)
    m2 = _blockdiag2(m_g)
    e2 = jnp.concatenate([_blockdiag2(e_re), _blockdiag2(e_im)], axis=-1)
    f2 = jnp.concatenate([_blockdiag2(ft_re.transpose(0, 2, 1)), _blockdiag2(ft_imn.transpose(0, 2, 1))], axis=-2)
    u2 = (z_main[:NP, :W].reshape(BATCH, NCH, TC, GP, 2, HG).transpose(3, 1, 0, 4, 2, 5)
          .reshape(GP, NCH * BATCH, 2 * TC * HG))
    y2, fin = _s5_chunk(u2, m2, e2, f2, a16_re.reshape(GP, 1, 2 * P), a16_im.reshape(GP, 1, 2 * P))
    y_s5_p = bf(y2.reshape(GP, NCH, BATCH, 2, TC, HG).transpose(2, 1, 4, 0, 3, 5).reshape(NP, W))
    s5_re_p = fin[:, :, :2 * P].reshape(GP, BATCH, 2, P).transpose(1, 0, 2, 3).reshape(1, BATCH, G, P)
    s5_im_p = fin[:, :, 2 * P:].reshape(GP, BATCH, 2, P).transpose(1, 0, 2, 3).reshape(1, BATCH, G, P)

    tile_r = lambda a: jnp.tile(a.transpose(1, 0, 2).reshape(HG, G * P), (S5C, 1))
    tile_c = lambda a: jnp.tile(a.transpose(0, 2, 1).reshape(G * P, HG), (1, S5C))
    y_s5_s, s5_re_s, s5_im_s = _s5_step(
        z_main, state_s5_re[0].reshape(NS, G * P), state_s5_im[0].reshape(NS, G * P),
        a_re.reshape(1, G * P), a_im.reshape(1, G * P), tile_r(bbt_re), tile_r(bbt_im),
        tile_c(s5_c_re[0]), tile_c(s5_c_im[0]), s5_d[0].reshape(1, W))

    ml_p, c_p, n_p, m_p = _mlstm_chunk(z_main, z_if, z_t, brow, bcol, gh)
    ml_s, c_s, n_s, m_s = _mlstm_step(z_main, z_if, brow, gh, state_mlstm_C[0], state_mlstm_n[0], state_mlstm_m[0])

    w_kv = bf(jnp.concatenate([w_mem_k[0], w_mem_v[0]], axis=1))
    kv = _proj(mem_prompt.reshape(BATCH * N_MEM, D), g_mem[0].reshape(1, D), w_kv, W)[0]
    xa_p = _xattn_prompt(z_main, kv)
    xa_s = _xattn_step(z_main, cache_mem_k[0].reshape(NS, N_MEM, W), cache_mem_v[0].reshape(NS, N_MEM, W))

    pad_s = lambda a: _pad_rows(bf(a), TM)
    merged = _mix(x1, g_mix[0].reshape(1, D), y_s5_p, pad_s(y_s5_s), ml_p, pad_s(ml_s), xa_p, pad_s(xa_s),
                  bf(w_s5_glu[0]), w_bg, w_br)
    x2 = _outproj(x1, merged, bf(w_out[0]))
    y_p, y_s = _ffn((x2,), g_ffn2[0].reshape(1, D), padc(bf(w2_gate[0])), padc(bf(w2_up[0])), padr(bf(w2_down[0])),
                    g_final.reshape(1, D))

    return (y_p.reshape(BATCH, SEQ, D), y_s[:NS].reshape(NS, 1, D),
            kv[:, :W].reshape(1, BATCH, N_MEM, NH, DH), kv[:, W:].reshape(1, BATCH, N_MEM, NH, DH),
            s5_re_p, s5_im_p, c_p[None], n_p[None], m_p[:, :, 0][None],
            s5_re_s.reshape(1, NS, G, P), s5_im_s.reshape(1, NS, G, P), c_s[None], n_s[None], m_s[None])
```

```python
import functools

import jax
import jax.numpy as jnp
from jax import lax
from jax.experimental import pallas as pl
from jax.experimental.pallas import tpu as pltpu

F32 = jnp.float32
BF16 = jnp.bfloat16

D = 2048
BATCH = 4
SEQ = 2048
NS = 128
NP = BATCH * SEQ
TM = 512
NPT = NP // TM
NT = NPT + 1
MROWS = NT * TM
N_MEM = 256
FF = 5504
TF = 512
G = 64
P = 64
HG = 16
TC = 16
GB = 8
NGB = G // GB
WU = GB * HG
WS = GB * P
NCH = SEQ // TC
S5B = 2
W = 1024
NH = 4
DH = 256
CL = 256
NCL = SEQ // CL
SB = 8
EPS = 1e-6
VMEM_LIMIT = 56 * 1024 * 1024


def _cp(sem, vmem=VMEM_LIMIT):
    return pltpu.CompilerParams(dimension_semantics=sem, vmem_limit_bytes=vmem)


def _dot(a, b):
    return jnp.dot(a, b, preferred_element_type=F32)


def _dot_nt(a, b):
    return lax.dot_general(a, b, (((1,), (1,)), ((), ())), preferred_element_type=F32)


def _dot_tn(a, b):
    return lax.dot_general(a, b, (((0,), (0,)), ((), ())), preferred_element_type=F32)


def _hi_lo(x):
    hi = x.astype(BF16)
    lo = (x - hi.astype(F32)).astype(BF16)
    return hi, lo


def _split3(x):
    hi = x.astype(BF16)
    r1 = x - hi.astype(F32)
    mid = r1.astype(BF16)
    lo = (r1 - mid.astype(F32)).astype(BF16)
    return hi, mid, lo


def _dot3(a, b):
    ah, al = _hi_lo(a)
    bh, bl = _hi_lo(b)
    return _dot(ah, bh) + (_dot(ah, bl) + _dot(al, bh))


def _dot3_nt(a, b):
    ah, al = _hi_lo(a)
    bh, bl = _hi_lo(b)
    return _dot_nt(ah, bh) + (_dot_nt(ah, bl) + _dot_nt(al, bh))


def _rms(x, g):
    r = lax.rsqrt(jnp.mean(x * x, axis=-1, keepdims=True) + EPS)
    return (x * r) * g


def _sigmoid(x):
    return 1.0 / (1.0 + jnp.exp(-x))


def _log_sigmoid(x):
    return jnp.minimum(x, 0.0) - jnp.log1p(jnp.exp(-jnp.abs(x)))


def _gelu_tanh(x):
    return x * (0.5 * (1.0 + jnp.tanh(0.7978845608028654 * (x + 0.044715 * (x * x * x)))))


def _ffn_kernel(*refs, two_src, final_norm):
    refs = list(refs)
    if two_src:
        xp_ref, xs_ref = refs[:2]
        refs = refs[2:]
    else:
        x_ref = refs[0]
        refs = refs[1:]
    g_ref, wg_ref, wu_ref, wd_ref = refs[:4]
    refs = refs[4:]
    if final_norm:
        gf_ref, op_ref, os_ref, h_scr, acc_scr = refs
    else:
        o_ref, h_scr, acc_scr = refs
    i = pl.program_id(0)
    j = pl.program_id(1)

    def load_x():
        if two_src:
            return jnp.where(i < NPT, xp_ref[...], xs_ref[...])
        return x_ref[...]

    @pl.when(j == 0)
    def _():
        h_scr[...] = _rms(load_x(), g_ref[...]).astype(BF16)
        acc_scr[...] = jnp.zeros_like(acc_scr)

    h = h_scr[...]
    gt = _dot(h, wg_ref[...])
    up = _dot(h, wu_ref[...])
    valid = FF - j * TF
    hid = jnp.where(lax.broadcasted_iota(jnp.int32, (TM, TF), 1) < valid, (gt * _sigmoid(gt)) * up, 0.0)
    wd = wd_ref[...]
    wd = jnp.where(lax.broadcasted_iota(jnp.int32, (TF, D), 0) < valid, wd, jnp.zeros_like(wd))
    acc_scr[...] += _dot(hid.astype(BF16), wd)

    @pl.when(j == pl.num_programs(1) - 1)
    def _():
        y = load_x() + 0.5 * acc_scr[...]
        if final_norm:
            y = _rms(y, gf_ref[...])

            @pl.when(i < NPT)
            def _():
                op_ref[...] = y

            @pl.when(i == NPT)
            def _():
                os_ref[...] = y
        else:
            o_ref[...] = y


def _ffn(xs, g, wg, wu, wd, g_final=None):
    two_src = len(xs) == 2
    final_norm = g_final is not None
    row = pl.BlockSpec((TM, D), lambda i, j: (i, 0))
    prow = pl.BlockSpec((TM, D), lambda i, j: (jnp.minimum(i, NPT - 1), 0))
    srow = pl.BlockSpec((TM, D), lambda i, j: (0, 0))
    vec = pl.BlockSpec((1, D), lambda i, j: (0, 0))
    in_specs = ([prow, srow] if two_src else [row]) + [
        vec,
        pl.BlockSpec((D, TF), lambda i, j: (0, j)),
        pl.BlockSpec((D, TF), lambda i, j: (0, j)),
        pl.BlockSpec((TF, D), lambda i, j: (j, 0)),
    ]
    args = list(xs) + [g, wg, wu, wd]
    if final_norm:
        in_specs.append(vec)
        args.append(g_final)
        out_shape = (jax.ShapeDtypeStruct((NP, D), F32), jax.ShapeDtypeStruct((TM, D), F32))
        out_specs = (prow, srow)
    else:
        out_shape = jax.ShapeDtypeStruct((MROWS, D), F32)
        out_specs = row
    return pl.pallas_call(
        functools.partial(_ffn_kernel, two_src=two_src, final_norm=final_norm),
        grid=(NT, pl.cdiv(FF, TF)),
        in_specs=in_specs,
        out_specs=out_specs,
        out_shape=out_shape,
        scratch_shapes=[pltpu.VMEM((TM, D), BF16), pltpu.VMEM((TM, D), F32)],
        compiler_params=_cp(("arbitrary", "arbitrary")),
        name="ffn_final" if final_norm else "ffn",
    )(*args)


NF32 = 2


def _in_proj_kernel(x_ref, g_ref, w_ref, wif_ref, wift_ref, of_ref, ob_ref, zif_ref, zt_ref, h_scr):
    j = pl.program_id(1)

    @pl.when(j == 0)
    def _():
        h = _rms(x_ref[...], g_ref[...]).astype(BF16)
        h_scr[...] = h
        zif_ref[...] = _dot(h, wif_ref[...])
        zt_ref[...] = _dot_nt(wift_ref[...], h)

    res = _dot(h_scr[...], w_ref[...])

    @pl.when(j < NF32)
    def _():
        of_ref[...] = res

    @pl.when(j >= NF32)
    def _():
        ob_ref[...] = res.astype(BF16)


def _in_proj(x, g, w, w_if, w_ift):
    m, n = x.shape[0], w.shape[1]
    nb = n // W
    return pl.pallas_call(
        _in_proj_kernel,
        grid=(m // TM, nb),
        in_specs=[
            pl.BlockSpec((TM, D), lambda i, j: (i, 0)),
            pl.BlockSpec((1, D), lambda i, j: (0, 0)),
            pl.BlockSpec((D, W), lambda i, j: (0, j)),
            pl.BlockSpec((D, 128), lambda i, j: (0, 0)),
            pl.BlockSpec((8, D), lambda i, j: (0, 0)),
        ],
        out_specs=[
            pl.BlockSpec((TM, W), lambda i, j: (i, jnp.minimum(j, NF32 - 1))),
            pl.BlockSpec((TM, W), lambda i, j: (i, jnp.maximum(j - NF32, 0))),
            pl.BlockSpec((TM, 128), lambda i, j: (i, 0)),
            pl.BlockSpec((8, TM), lambda i, j: (0, i)),
        ],
        out_shape=[
            jax.ShapeDtypeStruct((m, NF32 * W), F32),
            jax.ShapeDtypeStruct((m, (nb - NF32) * W), BF16),
            jax.ShapeDtypeStruct((m, 128), F32),
            jax.ShapeDtypeStruct((8, m), F32),
        ],
        scratch_shapes=[pltpu.VMEM((TM, D), BF16)],
        compiler_params=_cp(("arbitrary", "arbitrary")),
        name="in_proj",
    )(x, g, w, w_if, w_ift)


def _mem_proj_kernel(x_ref, g_ref, w_ref, o_ref, h_scr):
    @pl.when(pl.program_id(1) == 0)
    def _():
        h_scr[...] = _rms(x_ref[...], g_ref[...]).astype(BF16)

    o_ref[...] = _dot(h_scr[...], w_ref[...])


def _mem_proj(x, g, w):
    m, n = x.shape[0], w.shape[1]
    return pl.pallas_call(
        _mem_proj_kernel,
        grid=(m // TM, n // W),
        in_specs=[pl.BlockSpec((TM, D), lambda i, j: (i, 0)), pl.BlockSpec((1, D), lambda i, j: (0, 0)),
                  pl.BlockSpec((D, W), lambda i, j: (0, j))],
        out_specs=pl.BlockSpec((TM, W), lambda i, j: (i, j)),
        out_shape=jax.ShapeDtypeStruct((m, n), F32),
        scratch_shapes=[pltpu.VMEM((TM, D), BF16)],
        compiler_params=_cp(("arbitrary", "arbitrary")),
        name="mem_proj",
    )(x, g, w)


def _s5_prep_kernel(lr_ref, li_ref, ls_ref, btr_ref, bti_ref, cr_ref, ci_ref, d_ref,
                    ar_ref, ai_ref, a16r_ref, a16i_ref, bbr_ref, bbi_ref, e_ref, ft_ref, bd_ref):
    lr = lr_ref[0]
    li = li_ref[0]
    dt = jnp.exp(ls_ref[0])

    def power(k):
        mag = jnp.exp(lr * dt * float(k))
        ang = li * dt * float(k)
        return mag * jnp.cos(ang), mag * jnp.sin(ang)

    pw = [power(k) for k in range(TC + 1)]
    ar, ai = pw[1]
    den = lr * lr + li * li
    nr = ar - 1.0
    z_re = (nr * lr + ai * li) / den
    z_im = (ai * lr - nr * li) / den
    mask = (lax.broadcasted_iota(jnp.int32, (WU, WS), 0) // HG
            == lax.broadcasted_iota(jnp.int32, (WU, WS), 1) // P)
    btr = btr_ref[0]
    bti = bti_ref[0]
    bbr = jnp.where(mask, z_re * btr - z_im * bti, 0.0)
    bbi = jnp.where(mask, z_re * bti + z_im * btr, 0.0)
    cr = jnp.where(mask, cr_ref[0], 0.0)
    ci = jnp.where(mask, ci_ref[0], 0.0)

    ar_ref[0] = ar
    ai_ref[0] = ai
    a16r_ref[0] = pw[TC][0]
    a16i_ref[0] = pw[TC][1]
    bbr_ref[0] = bbr
    bbi_ref[0] = bbi

    def cmul(xr, xi, k):
        pr, pi = pw[k]
        return xr * pr - xi * pi, xr * pi + xi * pr

    diag = (lax.broadcasted_iota(jnp.int32, (WU, WU), 0) == lax.broadcasted_iota(jnp.int32, (WU, WU), 1))
    for s in range(TC):
        er, ei = cmul(bbr, bbi, TC - 1 - s)
        e_ref[0, s * WU:(s + 1) * WU, 0:WS] = er.astype(BF16)
        e_ref[0, s * WU:(s + 1) * WU, WS:2 * WS] = ei.astype(BF16)
        fr, fi = cmul(cr, ci, s + 1)
        ft_ref[0, s * WU:(s + 1) * WU, 0:WS] = fr.astype(BF16)
        ft_ref[0, s * WU:(s + 1) * WU, WS:2 * WS] = (-fi).astype(BF16)
        rr, ri = cmul(cr, ci, s)
        kern = _dot3_nt(bbr, rr) - _dot3_nt(bbi, ri)
        if s == 0:
            kern = kern + jnp.where(diag, d_ref[0], 0.0)
        bd_ref[0, s] = kern


def _s5_prep(lam_re, lam_im, log_step, bt_re, bt_im, c_re, c_im, d):
    def spec(*shape):
        nd = len(shape)
        return pl.BlockSpec((1,) + shape, lambda i: (i,) + (0,) * nd)

    def sds(*shape, dtype=F32):
        return jax.ShapeDtypeStruct((NGB,) + shape, dtype)

    return pl.pallas_call(
        _s5_prep_kernel,
        grid=(NGB,),
        in_specs=[spec(1, WS)] * 3 + [spec(WU, WS)] * 4 + [spec(WU, 1)],
        out_specs=[spec(1, WS)] * 4 + [spec(WU, WS)] * 2 + [spec(TC * WU, 2 * WS)] * 2 + [spec(TC, WU, WU)],
        out_shape=[sds(1, WS)] * 4 + [sds(WU, WS)] * 2 + [sds(TC * WU, 2 * WS, dtype=BF16)] * 2
        + [sds(TC, WU, WU)],
        compiler_params=_cp(("arbitrary",)),
        name="s5_prep",
    )(lam_re, lam_im, log_step, bt_re, bt_im, c_re, c_im, d)


def _s5_chunk_kernel(u_ref, bd_ref, e_ref, ft_ref, ar_ref, ai_ref, y_ref, fin_ref,
                     w_scr, lhs_scr, s_scr, xs_scr):
    rows = S5B * NCH

    @pl.when(pl.program_id(1) == 0)
    def _():
        w_scr[...] = jnp.zeros_like(w_scr)
        bd = [bd_ref[0, k].astype(BF16) for k in range(TC)]
        for s in range(TC):
            for t in range(s, TC):
                w_scr[s * WU:(s + 1) * WU, t * WU:(t + 1) * WU] = bd[t - s]

    for s in range(TC):
        lhs_scr[:, s * WU:(s + 1) * WU] = u_ref[pl.ds(s, rows, stride=TC), :].astype(BF16)
    lhs = lhs_scr[...]
    s_loc = _dot(lhs, e_ref[0])
    nl = WS // 128
    for k in range(2 * nl):
        s_scr[k] = s_loc[:, k * 128:(k + 1) * 128]
    ar = [ar_ref[0, :, k * 128:(k + 1) * 128] for k in range(nl)]
    ai = [ai_ref[0, :, k * 128:(k + 1) * 128] for k in range(nl)]
    xr = [jnp.zeros((S5B, 128), F32)] * nl
    xi = [jnp.zeros((S5B, 128), F32)] * nl
    for c in range(NCH):
        chunk_rows = pl.ds(c, S5B, stride=NCH)
        for k in range(nl):
            xs_scr[k, chunk_rows, :] = xr[k]
            xs_scr[nl + k, chunk_rows, :] = xi[k]
            sr = s_scr[k, chunk_rows, :]
            si = s_scr[nl + k, chunk_rows, :]
            xr[k], xi[k] = ar[k] * xr[k] - ai[k] * xi[k] + sr, ar[k] * xi[k] + ai[k] * xr[k] + si
    for k in range(nl):
        fin_ref[0, 0, :, k * 128:(k + 1) * 128] = xr[k]
        fin_ref[0, 0, :, WS + k * 128:WS + (k + 1) * 128] = xi[k]
    xs = jnp.concatenate([xs_scr[k] for k in range(2 * nl)], axis=1)
    y = _dot(lhs, w_scr[...]) + _dot_nt(xs.astype(BF16), ft_ref[0])
    for t in range(TC):
        y_ref[pl.ds(t, rows, stride=TC), :] = y[:, t * WU:(t + 1) * WU]


def _s5_chunk(z_f, bd, e, ft, a16r, a16i):
    rows = S5B * NCH
    nh = BATCH // S5B
    return pl.pallas_call(
        _s5_chunk_kernel,
        grid=(NGB, nh),
        in_specs=[
            pl.BlockSpec((S5B * SEQ, WU), lambda j, b: (b, j)),
            pl.BlockSpec((1, TC, WU, WU), lambda j, b: (j, 0, 0, 0)),
            pl.BlockSpec((1, TC * WU, 2 * WS), lambda j, b: (j, 0, 0)),
            pl.BlockSpec((1, TC * WU, 2 * WS), lambda j, b: (j, 0, 0)),
            pl.BlockSpec((1, 1, WS), lambda j, b: (j, 0, 0)),
            pl.BlockSpec((1, 1, WS), lambda j, b: (j, 0, 0)),
        ],
        out_specs=[
            pl.BlockSpec((S5B * SEQ, WU), lambda j, b: (b, j)),
            pl.BlockSpec((1, 1, S5B, 2 * WS), lambda j, b: (j, b, 0, 0)),
        ],
        out_shape=[jax.ShapeDtypeStruct((NP, W), F32), jax.ShapeDtypeStruct((NGB, nh, S5B, 2 * WS), F32)],
        scratch_shapes=[pltpu.VMEM((TC * WU, TC * WU), BF16), pltpu.VMEM((rows, TC * WU), BF16),
                        pltpu.VMEM((2 * WS // 128, rows, 128), F32), pltpu.VMEM((2 * WS // 128, rows, 128), F32)],
        compiler_params=_cp(("arbitrary", "arbitrary")),
        name="s5_chunk",
    )(z_f, bd, e, ft, a16r, a16i)


def _s5_step_kernel(u_ref, sr_ref, si_ref, ar_ref, ai_ref, bbr_ref, bbi_ref, cr_ref, ci_ref, d_ref,
                    y_ref, xr_ref, xi_ref):
    mask = (lax.broadcasted_iota(jnp.int32, (WU, WS), 0) // HG
            == lax.broadcasted_iota(jnp.int32, (WU, WS), 1) // P)
    for j in range(NGB):
        ul = slice(j * WU, (j + 1) * WU)
        sl = slice(j * WS, (j + 1) * WS)
        us = u_ref[:, ul]
        ar = ar_ref[j]
        ai = ai_ref[j]
        s_re = sr_ref[:, sl]
        s_im = si_ref[:, sl]
        x_re = ar * s_re - ai * s_im + _dot3(us, bbr_ref[j])
        x_im = ar * s_im + ai * s_re + _dot3(us, bbi_ref[j])
        xr_ref[:, sl] = x_re
        xi_ref[:, sl] = x_im
        cre = jnp.where(mask, cr_ref[j], 0.0)
        cim = jnp.where(mask, ci_ref[j], 0.0)
        y_ref[:, ul] = _dot3_nt(x_re, cre) - _dot3_nt(x_im, cim) + d_ref[:, ul] * us


def _s5_step(z_f, s_re, s_im, a_re, a_im, bb_re, bb_im, c_re, c_im, d_row):
    full = lambda *shape: pl.BlockSpec(shape, lambda i: (0,) * len(shape))
    return pl.pallas_call(
        _s5_step_kernel,
        grid=(1,),
        in_specs=[pl.BlockSpec((NS, W), lambda i: (NP // NS, 0)),
                  full(NS, G * P), full(NS, G * P), full(NGB, 1, WS), full(NGB, 1, WS),
                  full(NGB, WU, WS), full(NGB, WU, WS), full(NGB, WU, WS), full(NGB, WU, WS),
                  full(1, W)],
        out_specs=[full(NS, W), full(NS, G * P), full(NS, G * P)],
        out_shape=[jax.ShapeDtypeStruct((NS, W), F32), jax.ShapeDtypeStruct((NS, G * P), F32),
                   jax.ShapeDtypeStruct((NS, G * P), F32)],
        compiler_params=_cp(("arbitrary",)),
        name="s5_step",
    )(z_f, s_re, s_im, a_re, a_im, bb_re, bb_im, c_re, c_im, d_row)


def _mlstm_chunk_kernel(q_ref, k_ref, v_ref, o_ref, zif_ref, zt_ref, brow_ref, bcol_ref, gh_ref,
                        h_ref, c_ref, n_ref, m_ref):
    @pl.when(pl.program_id(1) == 0)
    def _():
        c_ref[...] = jnp.zeros_like(c_ref)
        n_ref[...] = jnp.zeros_like(n_ref)
        m_ref[...] = jnp.zeros_like(m_ref)

    zi = zif_ref[...] + brow_ref[...]
    zt = zt_ref[...] + bcol_ref[...]
    rr = lax.broadcasted_iota(jnp.int32, (CL, CL), 0)
    cc = lax.broadcasted_iota(jnp.int32, (CL, CL), 1)
    causal = cc <= rr
    tril = jnp.where(causal, 1.0, 0.0).astype(BF16)
    triu = jnp.where(rr <= cc, 1.0, 0.0).astype(BF16)
    lfc = _split3(_log_sigmoid(zi))
    bcum_col = _dot(tril, lfc[0]) + (_dot(tril, lfc[1]) + _dot(tril, lfc[2]))
    lfr = _split3(_log_sigmoid(zt))
    bcum_row = _dot(lfr[0], triu) + (_dot(lfr[1], triu) + _dot(lfr[2], triu))

    for hh in range(NH):
        hs = slice(hh * DH, (hh + 1) * DH)
        bc = bcum_col[:, NH + hh:NH + hh + 1]
        ic = zi[:, hh:hh + 1]
        br = bcum_row[NH + hh:NH + hh + 1, :]
        ir = zt[hh:hh + 1, :]
        m_prev = m_ref[0, hh:hh + 1, 0:1]
        g_inter = bc + m_prev
        dlog = jnp.where(causal, (bc - br) + ir, -jnp.inf)
        m_t = jnp.maximum(g_inter, jnp.max(dlog, axis=-1, keepdims=True))
        w_inter = jnp.exp(g_inter - m_t)
        w_intra = jnp.exp(dlog - m_t)
        qb = q_ref[:, hs]
        kb = k_ref[:, hs]
        vb = v_ref[:, hs]
        qf = qb.astype(F32)
        kf = kb.astype(F32)
        s = _dot_nt(qb, kb) * (w_intra * (DH ** -0.5))
        c_prev = c_ref[0, hh]
        n_prev = n_ref[0, hh:hh + 1, :]
        num = _dot(s.astype(BF16), vb) + w_inter * _dot(qb, c_prev.astype(BF16))
        nq = jnp.sum(s, axis=-1, keepdims=True) + w_inter * jnp.sum(qf * n_prev, axis=-1, keepdims=True)
        h = num / jnp.maximum(jnp.abs(nq), jnp.exp(-m_t))
        m_last = m_t[CL - 1:CL, :]
        w_last = jnp.exp((bc[CL - 1:CL, :] - bc) + ic - m_last) * (DH ** -0.5)
        wi_last = w_inter[CL - 1:CL, :]
        kw = kf * w_last
        c_ref[0, hh] = wi_last * c_prev + _dot_tn(kw.astype(BF16), vb)
        n_ref[0, hh:hh + 1, :] = wi_last * n_prev + jnp.sum(kw, axis=0, keepdims=True)
        m_ref[0, hh:hh + 1, :] = jnp.broadcast_to(m_last, (1, 128))
        hn = _rms(h, gh_ref[:, hs])
        h_ref[:, hs] = (hn * _sigmoid(o_ref[:, hs])).astype(BF16)


def _mlstm_chunk(z_f, z_b, z_if, z_t, brow, bcol, gh):
    blk = lambda col: pl.BlockSpec((CL, W), lambda b, c: (b * NCL + c, col))
    return pl.pallas_call(
        _mlstm_chunk_kernel,
        grid=(BATCH, NCL),
        in_specs=[blk(0), blk(1), blk(2), blk(1),
                  pl.BlockSpec((CL, 128), lambda b, c: (b * NCL + c, 0)),
                  pl.BlockSpec((8, CL), lambda b, c: (0, b * NCL + c)),
                  pl.BlockSpec((1, 128), lambda b, c: (0, 0)),
                  pl.BlockSpec((8, 1), lambda b, c: (0, 0)),
                  pl.BlockSpec((1, W), lambda b, c: (0, 0))],
        out_specs=[pl.BlockSpec((CL, W), lambda b, c: (b * NCL + c, 0)),
                   pl.BlockSpec((1, NH, DH, DH), lambda b, c: (b, 0, 0, 0)),
                   pl.BlockSpec((1, NH, DH), lambda b, c: (b, 0, 0)),
                   pl.BlockSpec((1, NH, 128), lambda b, c: (b, 0, 0))],
        out_shape=[jax.ShapeDtypeStruct((NP, W), BF16),
                   jax.ShapeDtypeStruct((BATCH, NH, DH, DH), F32),
                   jax.ShapeDtypeStruct((BATCH, NH, DH), F32),
                   jax.ShapeDtypeStruct((BATCH, NH, 128), F32)],
        compiler_params=_cp(("arbitrary", "arbitrary")),
        name="mlstm_chunk",
    )(z_b, z_b, z_b, z_f, z_if, z_t, brow, bcol, gh)


def _mlstm_step_kernel(q_ref, k_ref, v_ref, o_ref, zif_ref, brow_ref, gh_ref, c0_ref, n0_ref, m0_ref,
                       h_ref, c_ref, n_ref, m_ref):
    zi = zif_ref[...] + brow_ref[...]
    pad = jnp.zeros((128 - SB, DH), F32)
    for hh in range(NH):
        hs = slice(hh * DH, (hh + 1) * DH)
        q8 = q_ref[:, hs]
        k8 = k_ref[:, hs]
        qT = jnp.concatenate([q8, pad], axis=0).T
        kT = jnp.concatenate([k8, pad], axis=0).T
        for s in range(SB):
            q_row = q8[s:s + 1, :]
            k_row = k8[s:s + 1, :]
            v_row = v_ref[s:s + 1, hs]
            q_col = qT[:, s:s + 1]
            k_col = kT[:, s:s + 1]
            ig = zi[s:s + 1, hh:hh + 1]
            lf = _log_sigmoid(zi[s:s + 1, NH + hh:NH + hh + 1])
            g_inter = lf + m0_ref[s:s + 1, hh:hh + 1]
            m_t = jnp.maximum(g_inter, ig)
            w_inter = jnp.exp(g_inter - m_t)
            w_intra = jnp.exp(ig - m_t) * (DH ** -0.5)
            c_prev = c0_ref[s, hh]
            n_prev = n0_ref[s, hh:hh + 1, :]
            sv = jnp.sum(q_row * k_row, axis=-1, keepdims=True) * w_intra
            q_c = jnp.sum(q_col * c_prev, axis=0, keepdims=True)
            num = sv * v_row + w_inter * q_c
            nq = sv + w_inter * jnp.sum(q_row * n_prev, axis=-1, keepdims=True)
            h = num / jnp.maximum(jnp.abs(nq), jnp.exp(-m_t))
            c_ref[s, hh] = w_inter * c_prev + (k_col * w_intra) * v_row
            n_ref[s, hh:hh + 1, :] = w_inter * n_prev + w_intra * k_row
            m_ref[s:s + 1, hh:hh + 1] = m_t
            hn = _rms(h, gh_ref[:, hs])
            h_ref[s:s + 1, hs] = hn * _sigmoid(o_ref[s:s + 1, hs])


def _mlstm_step(zs, z_f, z_if, brow, gh, c0, n0, m0):
    base = NP // SB
    blk = lambda col: pl.BlockSpec((SB, W), lambda i: (i, col))
    return pl.pallas_call(
        _mlstm_step_kernel,
        grid=(NS // SB,),
        in_specs=[blk(0), blk(1), blk(2),
                  pl.BlockSpec((SB, W), lambda i: (base + i, 1)),
                  pl.BlockSpec((SB, 128), lambda i: (base + i, 0)),
                  pl.BlockSpec((1, 128), lambda i: (0, 0)),
                  pl.BlockSpec((1, W), lambda i: (0, 0)),
                  pl.BlockSpec((SB, NH, DH, DH), lambda i: (i, 0, 0, 0)),
                  pl.BlockSpec((SB, NH, DH), lambda i: (i, 0, 0)),
                  pl.BlockSpec((SB, NH), lambda i: (i, 0))],
        out_specs=[pl.BlockSpec((SB, W), lambda i: (i, 0)),
                   pl.BlockSpec((SB, NH, DH, DH), lambda i: (i, 0, 0, 0)),
                   pl.BlockSpec((SB, NH, DH), lambda i: (i, 0, 0)),
                   pl.BlockSpec((SB, NH), lambda i: (i, 0))],
        out_shape=[jax.ShapeDtypeStruct((NS, W), F32),
                   jax.ShapeDtypeStruct((NS, NH, DH, DH), F32),
                   jax.ShapeDtypeStruct((NS, NH, DH), F32),
                   jax.ShapeDtypeStruct((NS, NH), F32)],
        compiler_params=_cp(("arbitrary",)),
        name="mlstm_step",
    )(zs, zs, zs, z_f, z_if, brow, gh, c0, n0, m0)


def _softmax_rows(s):
    e = jnp.exp(s - jnp.max(s, axis=-1, keepdims=True))
    return e / jnp.sum(e, axis=-1, keepdims=True)


def _xattn_prompt_kernel(q_ref, k_ref, v_ref, o_ref):
    for hh in range(NH):
        hs = slice(hh * DH, (hh + 1) * DH)
        s = _dot_nt(q_ref[:, hs], k_ref[:, hs].astype(BF16)) * (DH ** -0.5)
        p = _softmax_rows(s)
        o_ref[:, hs] = _dot(p.astype(BF16), v_ref[:, hs].astype(BF16)).astype(BF16)


def _xattn_prompt(z_b, kv):
    nt = SEQ // TM
    return pl.pallas_call(
        _xattn_prompt_kernel,
        grid=(BATCH, nt),
        in_specs=[pl.BlockSpec((TM, W), lambda b, t: (b * nt + t, 3)),
                  pl.BlockSpec((N_MEM, W), lambda b, t: (b, 0)),
                  pl.BlockSpec((N_MEM, W), lambda b, t: (b, 1))],
        out_specs=pl.BlockSpec((TM, W), lambda b, t: (b * nt + t, 0)),
        out_shape=jax.ShapeDtypeStruct((NP, W), BF16),
        compiler_params=_cp(("arbitrary", "arbitrary")),
        name="xattn_prompt",
    )(z_b, kv, kv)


XS = 4


def _xattn_step_kernel(q_ref, k_ref, v_ref, o_ref):
    def part(t):
        for hh in range(NH):
            hs = slice(hh * DH, (hh + 1) * DH)
            q8 = q_ref[:, hs].astype(BF16)
            for s in range(XS):
                r = t * XS + s
                sc = _dot_nt(q8, k_ref[s, :, hh, :].astype(BF16))[r:r + 1, :] * (DH ** -0.5)
                p = _softmax_rows(sc)
                p8 = jnp.broadcast_to(p, (SB, N_MEM)).astype(BF16)
                o_ref[r:r + 1, hs] = _dot(p8, v_ref[s, :, hh, :].astype(BF16))[0:1, :]

    for t in range(SB // XS):
        pl.when(pl.program_id(1) == t)(functools.partial(part, t))


def _xattn_step(zs, mem_k, mem_v):
    nt = SB // XS
    return pl.pallas_call(
        _xattn_step_kernel,
        grid=(NS // SB, nt),
        in_specs=[pl.BlockSpec((SB, W), lambda i, t: (i, 3)),
                  pl.BlockSpec((XS, N_MEM, NH, DH), lambda i, t: (i * nt + t, 0, 0, 0)),
                  pl.BlockSpec((XS, N_MEM, NH, DH), lambda i, t: (i * nt + t, 0, 0, 0))],
        out_specs=pl.BlockSpec((SB, W), lambda i, t: (i, 0)),
        out_shape=jax.ShapeDtypeStruct((NS, W), F32),
        compiler_params=_cp(("arbitrary", "arbitrary")),
        name="xattn_step",
    )(zs, mem_k, mem_v)


TMX = 512


def _mix_kernel(x_ref, g_ref, yp_ref, ys_ref, mp_ref, ms_ref, ap_ref, as_ref, wglu_ref,
                wg0_ref, wg1_ref, wg2_ref, wb0_ref, wb1_ref, wb2_ref, o_ref, h_scr, s5_scr):
    i = pl.program_id(0)
    prompt = i < NPT

    @pl.when(pl.program_id(1) == 0)
    def _():
        h_scr[...] = _rms(x_ref[...], g_ref[...]).astype(BF16)
        y = _gelu_tanh(jnp.where(prompt, yp_ref[...], ys_ref[...]))
        s5_scr[...] = (y * _sigmoid(_dot(y.astype(BF16), wglu_ref[...]))).astype(BF16)

    h = h_scr[...]
    ml = jnp.where(prompt, mp_ref[...], ms_ref[...])
    xa = jnp.where(prompt, ap_ref[...], as_ref[...])
    merged = (_sigmoid(_dot(h, wg0_ref[...])) * _dot(s5_scr[...], wb0_ref[...])
              + _sigmoid(_dot(h, wg1_ref[...])) * _dot(ml, wb1_ref[...])
              + _sigmoid(_dot(h, wg2_ref[...])) * _dot(xa, wb2_ref[...]))
    o_ref[...] = merged.astype(BF16)


def _mix(x1, g, yp, ys, mp, ms, ap, as_, wglu, wg, wb):
    prow = pl.BlockSpec((TM, W), lambda i, j: (jnp.minimum(i, NPT - 1), 0))
    srow = pl.BlockSpec((TM, W), lambda i, j: (0, 0))
    wgs = pl.BlockSpec((D, TMX), lambda i, j: (0, j))
    wbs = pl.BlockSpec((W, TMX), lambda i, j: (0, j))
    return pl.pallas_call(
        _mix_kernel,
        grid=(NT, D // TMX),
        in_specs=[pl.BlockSpec((TM, D), lambda i, j: (i, 0)), pl.BlockSpec((1, D), lambda i, j: (0, 0)),
                  prow, srow, prow, srow, prow, srow,
                  pl.BlockSpec((W, W), lambda i, j: (0, 0)),
                  wgs, wgs, wgs, wbs, wbs, wbs],
        out_specs=pl.BlockSpec((TM, TMX), lambda i, j: (i, j)),
        out_shape=jax.ShapeDtypeStruct((MROWS, D), BF16),
        scratch_shapes=[pltpu.VMEM((TM, D), BF16), pltpu.VMEM((TM, W), BF16)],
        compiler_params=_cp(("arbitrary", "arbitrary")),
        name="mix",
    )(x1, g, yp, ys, mp, ms, ap, as_, wglu, *wg, *wb)


def _outproj_kernel(x_ref, m_ref, w_ref, o_ref):
    o_ref[...] = x_ref[...] + _dot(m_ref[...], w_ref[...])


def _outproj(x1, merged, w_out):
    return pl.pallas_call(
        _outproj_kernel,
        grid=(NT,),
        in_specs=[pl.BlockSpec((TM, D), lambda i: (i, 0)), pl.BlockSpec((TM, D), lambda i: (i, 0)),
                  pl.BlockSpec((D, D), lambda i: (0, 0))],
        out_specs=pl.BlockSpec((TM, D), lambda i: (i, 0)),
        out_shape=jax.ShapeDtypeStruct((MROWS, D), F32),
        compiler_params=_cp(("arbitrary",)),
        name="out_proj",
    )(x1, merged, w_out)


def _pad_rows(a, rows):
    return jnp.pad(a, ((0, rows - a.shape[0]), (0, 0)))


def kernel(x_prompt, x_sample, mem_prompt, cache_mem_k, cache_mem_v, state_s5_re, state_s5_im, state_mlstm_C,
           state_mlstm_n, state_mlstm_m, g_ffn1, w1_gate, w1_up, w1_down, g_mix, w_in, s5_lambda_re,
           s5_lambda_im, s5_log_step, s5_b_re, s5_b_im, s5_c_re, s5_c_im, s5_d, w_s5_glu, b_igate, b_fgate,
           g_mlstm_head, g_mem, w_mem_k, w_mem_v, w_br_s5, w_br_ml, w_br_xa, w_out, g_ffn2, w2_gate, w2_up,
           w2_down, g_final):
    bf = lambda a: a.astype(BF16)

    wi = w_in[0]
    q0, o0, i0, x0, g0 = W, 4 * W, 5 * W, 5 * W + 2 * NH, 6 * W + 2 * NH
    w_main = bf(jnp.concatenate([wi[:, :W], wi[:, o0:i0], wi[:, q0:o0], wi[:, x0:g0]], axis=1))
    w_gate_cols = wi[:, i0:x0]
    w_if = bf(jnp.pad(w_gate_cols, ((0, 0), (0, 128 - 2 * NH))))
    w_ift = bf(w_gate_cols.T)
    w_bg = [bf(wi[:, g0 + b * D:g0 + (b + 1) * D]) for b in range(3)]
    w_br = [bf(w_br_s5[0]), bf(w_br_ml[0]), bf(w_br_xa[0])]
    brow = jnp.pad(jnp.concatenate([b_igate[0], b_fgate[0]]), (0, 128 - 2 * NH)).reshape(1, 128)
    bcol = jnp.concatenate([b_igate[0], b_fgate[0]]).reshape(2 * NH, 1)
    gh = g_mlstm_head[0].reshape(1, W)

    xp = x_prompt.reshape(NP, D)
    xs = _pad_rows(x_sample.reshape(NS, D), TM)
    x1 = _ffn((xp, xs), g_ffn1[0].reshape(1, D), bf(w1_gate[0]), bf(w1_up[0]), bf(w1_down[0]))

    z_f, z_b, z_if, z_t = _in_proj(x1, g_mix[0].reshape(1, D), w_main, w_if, w_ift)
    zs = z_b[NP:NP + NS].astype(F32)

    blk = lambda a: jnp.tile(a.reshape(NGB, WU, P), (1, 1, GB))
    row = lambda a: a.reshape(NGB, 1, WS)
    (a_re, a_im, a16_re, a16_im, bb_re, bb_im, e_op, ft_op, bd_op) = _s5_prep(
        row(s5_lambda_re[0]), row(s5_lambda_im[0]), row(jnp.repeat(s5_log_step[0], P)),
        blk(s5_b_re[0].transpose(0, 2, 1)), blk(s5_b_im[0].transpose(0, 2, 1)),
        blk(s5_c_re[0]), blk(s5_c_im[0]), s5_d[0].reshape(NGB, WU, 1))
    y_s5_p, fin = _s5_chunk(z_f, bd_op, e_op, ft_op, a16_re, a16_im)
    fin = fin.reshape(NGB, BATCH, 2, GB, P).transpose(2, 1, 0, 3, 4).reshape(2, 1, BATCH, G, P)
    y_s5_s, s5_re_s, s5_im_s = _s5_step(
        z_f, state_s5_re[0].reshape(NS, G * P), state_s5_im[0].reshape(NS, G * P),
        a_re, a_im, bb_re, bb_im, blk(s5_c_re[0]), blk(s5_c_im[0]), s5_d[0].reshape(1, W))

    ml_p, c_p, n_p, m_p = _mlstm_chunk(z_f, z_b, z_if, z_t, brow, bcol, gh)
    ml_s, c_s, n_s, m_s = _mlstm_step(zs, z_f, z_if, brow, gh, state_mlstm_C[0], state_mlstm_n[0], state_mlstm_m[0])

    w_kv = bf(jnp.concatenate([w_mem_k[0], w_mem_v[0]], axis=1))
    kv = _mem_proj(mem_prompt.reshape(BATCH * N_MEM, D), g_mem[0].reshape(1, D), w_kv)
    xa_p = _xattn_prompt(z_b, kv)
    xa_s = _xattn_step(zs, cache_mem_k[0], cache_mem_v[0])

    merged = _mix(x1, g_mix[0].reshape(1, D), y_s5_p, _pad_rows(y_s5_s, TM), ml_p, _pad_rows(bf(ml_s), TM),
                  xa_p, _pad_rows(bf(xa_s), TM), bf(w_s5_glu[0]), w_bg, w_br)
    x2 = _outproj(x1, merged, bf(w_out[0]))
    y_p, y_s = _ffn((x2,), g_ffn2[0].reshape(1, D), bf(w2_gate[0]), bf(w2_up[0]), bf(w2_down[0]),
                    g_final.reshape(1, D))

    return (y_p.reshape(BATCH, SEQ, D), y_s[:NS].reshape(NS, 1, D),
            kv[:, :W].reshape(1, BATCH, N_MEM, NH, DH), kv[:, W:].reshape(1, BATCH, N_MEM, NH, DH),
            fin[0], fin[1], c_p[None], n_p[None], m_p[:, :, 0][None],
            s5_re_s.reshape(1, NS, G, P), s5_im_s.reshape(1, NS, G, P), c_s[None], n_s[None], m_s[None])
```

```python
import functools

import jax
import jax.numpy as jnp
from jax import lax
from jax.experimental import pallas as pl
from jax.experimental.pallas import tpu as pltpu

F32 = jnp.float32
BF16 = jnp.bfloat16

D = 2048
BATCH = 4
SEQ = 2048
NS = 128
NP = BATCH * SEQ
TM = 512
NPT = NP // TM
NT = NPT + 1
MROWS = NT * TM
N_MEM = 256
FF = 5504
TF = 1024
FFN_VMEM_LIMIT = 60 * 1024 * 1024
TFC = 512
G = 64
P = 64
HG = 16
TC = 16
GB = 8
NGB = G // GB
WU = GB * HG
WS = GB * P
NCH = SEQ // TC
S5B = 2
W = 1024
NH = 4
DH = 256
CL = 256
NCL = SEQ // CL
SB = 8
EPS = 1e-6
VMEM_LIMIT = 56 * 1024 * 1024


def _cp(sem, vmem=VMEM_LIMIT):
    return pltpu.CompilerParams(dimension_semantics=sem, vmem_limit_bytes=vmem)


def _dot(a, b):
    return jnp.dot(a, b, preferred_element_type=F32)


def _dot_nt(a, b):
    return lax.dot_general(a, b, (((1,), (1,)), ((), ())), preferred_element_type=F32)


def _dot_tn(a, b):
    return lax.dot_general(a, b, (((0,), (0,)), ((), ())), preferred_element_type=F32)


def _hi_lo(x):
    hi = x.astype(BF16)
    lo = (x - hi.astype(F32)).astype(BF16)
    return hi, lo


def _split3(x):
    hi = x.astype(BF16)
    r1 = x - hi.astype(F32)
    mid = r1.astype(BF16)
    lo = (r1 - mid.astype(F32)).astype(BF16)
    return hi, mid, lo


def _dot3(a, b):
    ah, al = _hi_lo(a)
    bh, bl = _hi_lo(b)
    return _dot(ah, bh) + (_dot(ah, bl) + _dot(al, bh))


def _dot3_nt(a, b):
    ah, al = _hi_lo(a)
    bh, bl = _hi_lo(b)
    return _dot_nt(ah, bh) + (_dot_nt(ah, bl) + _dot_nt(al, bh))


def _rms(x, g):
    r = lax.rsqrt(jnp.mean(x * x, axis=-1, keepdims=True) + EPS)
    return (x * r) * g


def _sigmoid(x):
    return 1.0 / (1.0 + jnp.exp(-x))


def _log_sigmoid(x):
    return jnp.minimum(x, 0.0) - jnp.log1p(jnp.exp(-jnp.abs(x)))


def _gelu_tanh(x):
    return x * (0.5 * (1.0 + jnp.tanh(0.7978845608028654 * (x + 0.044715 * (x * x * x)))))


def _ffn_kernel(*refs, two_src, final_norm):
    refs = list(refs)
    if two_src:
        xp_ref, xs_ref = refs[:2]
        refs = refs[2:]
    else:
        x_ref = refs[0]
        refs = refs[1:]
    g_ref, wg_ref, wu_ref, wd_ref = refs[:4]
    refs = refs[4:]
    if final_norm:
        gf_ref, op_ref, os_ref, h_scr, acc_scr = refs
    else:
        o_ref, h_scr, acc_scr = refs
    i = pl.program_id(0)
    j = pl.program_id(1)

    def load_x():
        if two_src:
            return jnp.where(i < NPT, xp_ref[...], xs_ref[...])
        return x_ref[...]

    @pl.when(j == 0)
    def _():
        h_scr[...] = _rms(load_x(), g_ref[...]).astype(BF16)
        acc_scr[...] = jnp.zeros_like(acc_scr)

    h = h_scr[...]
    def chunk(c):
        cs = slice(c * TFC, (c + 1) * TFC)
        valid = FF - j * TF - c * TFC
        gt = _dot(h, wg_ref[:, cs])
        up = _dot(h, wu_ref[:, cs])
        hid = jnp.where(lax.broadcasted_iota(jnp.int32, (TM, TFC), 1) < valid, (gt * _sigmoid(gt)) * up, 0.0)
        wd = wd_ref[cs, :]
        wd = jnp.where(lax.broadcasted_iota(jnp.int32, (TFC, D), 0) < valid, wd, jnp.zeros_like(wd))
        acc_scr[...] += _dot(hid.astype(BF16), wd)

    for c in range(TF // TFC):
        pl.when(FF - j * TF - c * TFC > 0)(functools.partial(chunk, c))

    @pl.when(j == pl.num_programs(1) - 1)
    def _():
        y = load_x() + 0.5 * acc_scr[...]
        if final_norm:
            y = _rms(y, gf_ref[...])

            @pl.when(i < NPT)
            def _():
                op_ref[...] = y

            @pl.when(i == NPT)
            def _():
                os_ref[...] = y
        else:
            o_ref[...] = y


def _ffn(xs, g, wg, wu, wd, g_final=None):
    two_src = len(xs) == 2
    final_norm = g_final is not None
    row = pl.BlockSpec((TM, D), lambda i, j: (i, 0))
    prow = pl.BlockSpec((TM, D), lambda i, j: (jnp.minimum(i, NPT - 1), 0))
    srow = pl.BlockSpec((TM, D), lambda i, j: (0, 0))
    srow_in = pl.BlockSpec((TM, D), lambda i, j: (0, 0), pipeline_mode=pl.Buffered(1))
    vec = pl.BlockSpec((1, D), lambda i, j: (0, 0))
    in_specs = ([prow, srow_in] if two_src else [row]) + [
        vec,
        pl.BlockSpec((D, TF), lambda i, j: (0, j)),
        pl.BlockSpec((D, TF), lambda i, j: (0, j)),
        pl.BlockSpec((TF, D), lambda i, j: (j, 0)),
    ]
    args = list(xs) + [g, wg, wu, wd]
    if final_norm:
        in_specs.append(vec)
        args.append(g_final)
        out_shape = (jax.ShapeDtypeStruct((NP, D), F32), jax.ShapeDtypeStruct((TM, D), F32))
        out_specs = (prow, srow)
    else:
        out_shape = jax.ShapeDtypeStruct((MROWS, D), F32)
        out_specs = row
    return pl.pallas_call(
        functools.partial(_ffn_kernel, two_src=two_src, final_norm=final_norm),
        grid=(NT, pl.cdiv(FF, TF)),
        in_specs=in_specs,
        out_specs=out_specs,
        out_shape=out_shape,
        scratch_shapes=[pltpu.VMEM((TM, D), BF16), pltpu.VMEM((TM, D), F32)],
        compiler_params=_cp(("arbitrary", "arbitrary"), FFN_VMEM_LIMIT),
        name="ffn_final" if final_norm else "ffn",
    )(*args)


NF32 = 2
TMI = 256


def _in_proj_kernel(x_ref, g_ref, w_ref, wif_ref, wift_ref, of_ref, ob_ref, zif_ref, zt_ref):
    h = _rms(x_ref[...], g_ref[...]).astype(BF16)
    zif_ref[...] = _dot(h, wif_ref[...])
    zt_ref[...] = _dot_nt(wift_ref[...], h)
    nb = w_ref.shape[1] // W
    for j in range(nb):
        res = _dot(h, w_ref[:, j * W:(j + 1) * W])
        if j < NF32:
            of_ref[:, j * W:(j + 1) * W] = res
        else:
            ob_ref[:, (j - NF32) * W:(j - NF32 + 1) * W] = res.astype(BF16)


def _in_proj(x, g, w, w_if, w_ift):
    m, n = x.shape[0], w.shape[1]
    nb = n // W
    once = pl.Buffered(1)
    return pl.pallas_call(
        _in_proj_kernel,
        grid=(m // TMI,),
        in_specs=[
            pl.BlockSpec((TMI, D), lambda i: (i, 0)),
            pl.BlockSpec((1, D), lambda i: (0, 0)),
            pl.BlockSpec((D, n), lambda i: (0, 0), pipeline_mode=once),
            pl.BlockSpec((D, 128), lambda i: (0, 0), pipeline_mode=once),
            pl.BlockSpec((8, D), lambda i: (0, 0), pipeline_mode=once),
        ],
        out_specs=[
            pl.BlockSpec((TMI, NF32 * W), lambda i: (i, 0)),
            pl.BlockSpec((TMI, (nb - NF32) * W), lambda i: (i, 0)),
            pl.BlockSpec((TMI, 128), lambda i: (i, 0)),
            pl.BlockSpec((8, TMI), lambda i: (0, i)),
        ],
        out_shape=[
            jax.ShapeDtypeStruct((m, NF32 * W), F32),
            jax.ShapeDtypeStruct((m, (nb - NF32) * W), BF16),
            jax.ShapeDtypeStruct((m, 128), F32),
            jax.ShapeDtypeStruct((8, m), F32),
        ],
        compiler_params=_cp(("arbitrary",)),
        name="in_proj",
    )(x, g, w, w_if, w_ift)


def _mem_proj_kernel(x_ref, g_ref, w_ref, o_ref, h_scr):
    @pl.when(pl.program_id(1) == 0)
    def _():
        h_scr[...] = _rms(x_ref[...], g_ref[...]).astype(BF16)

    o_ref[...] = _dot(h_scr[...], w_ref[...])


def _mem_proj(x, g, w):
    m, n = x.shape[0], w.shape[1]
    return pl.pallas_call(
        _mem_proj_kernel,
        grid=(m // TM, n // W),
        in_specs=[pl.BlockSpec((TM, D), lambda i, j: (i, 0)), pl.BlockSpec((1, D), lambda i, j: (0, 0)),
                  pl.BlockSpec((D, W), lambda i, j: (0, j))],
        out_specs=pl.BlockSpec((TM, W), lambda i, j: (i, j)),
        out_shape=jax.ShapeDtypeStruct((m, n), F32),
        scratch_shapes=[pltpu.VMEM((TM, D), BF16)],
        compiler_params=_cp(("arbitrary", "arbitrary")),
        name="mem_proj",
    )(x, g, w)


def _s5_prep_kernel(lr_ref, li_ref, ls_ref, btr_ref, bti_ref, cr_ref, ci_ref, d_ref,
                    ar_ref, ai_ref, a16r_ref, a16i_ref, bbr_ref, bbi_ref, e_ref, ft_ref, bd_ref):
    lr = lr_ref[0]
    li = li_ref[0]
    dt = jnp.exp(ls_ref[0])

    def power(k):
        mag = jnp.exp(lr * dt * float(k))
        ang = li * dt * float(k)
        return mag * jnp.cos(ang), mag * jnp.sin(ang)

    pw = [power(k) for k in range(TC + 1)]
    ar, ai = pw[1]
    den = lr * lr + li * li
    nr = ar - 1.0
    z_re = (nr * lr + ai * li) / den
    z_im = (ai * lr - nr * li) / den
    mask = (lax.broadcasted_iota(jnp.int32, (WU, WS), 0) // HG
            == lax.broadcasted_iota(jnp.int32, (WU, WS), 1) // P)
    btr = btr_ref[0]
    bti = bti_ref[0]
    bbr = jnp.where(mask, z_re * btr - z_im * bti, 0.0)
    bbi = jnp.where(mask, z_re * bti + z_im * btr, 0.0)
    cr = jnp.where(mask, cr_ref[0], 0.0)
    ci = jnp.where(mask, ci_ref[0], 0.0)

    ar_ref[0] = ar
    ai_ref[0] = ai
    a16r_ref[0] = pw[TC][0]
    a16i_ref[0] = pw[TC][1]
    bbr_ref[0] = bbr
    bbi_ref[0] = bbi

    def cmul(xr, xi, k):
        pr, pi = pw[k]
        return xr * pr - xi * pi, xr * pi + xi * pr

    diag = (lax.broadcasted_iota(jnp.int32, (WU, WU), 0) == lax.broadcasted_iota(jnp.int32, (WU, WU), 1))
    for s in range(TC):
        er, ei = cmul(bbr, bbi, TC - 1 - s)
        e_ref[0, s * WU:(s + 1) * WU, 0:WS] = er.astype(BF16)
        e_ref[0, s * WU:(s + 1) * WU, WS:2 * WS] = ei.astype(BF16)
        fr, fi = cmul(cr, ci, s + 1)
        ft_ref[0, s * WU:(s + 1) * WU, 0:WS] = fr.astype(BF16)
        ft_ref[0, s * WU:(s + 1) * WU, WS:2 * WS] = (-fi).astype(BF16)
        rr, ri = cmul(cr, ci, s)
        kern = _dot3_nt(bbr, rr) - _dot3_nt(bbi, ri)
        if s == 0:
            kern = kern + jnp.where(diag, d_ref[0], 0.0)
        bd_ref[0, s] = kern


def _s5_prep(lam_re, lam_im, log_step, bt_re, bt_im, c_re, c_im, d):
    def spec(*shape):
        nd = len(shape)
        return pl.BlockSpec((1,) + shape, lambda i: (i,) + (0,) * nd)

    def sds(*shape, dtype=F32):
        return jax.ShapeDtypeStruct((NGB,) + shape, dtype)

    return pl.pallas_call(
        _s5_prep_kernel,
        grid=(NGB,),
        in_specs=[spec(1, WS)] * 3 + [spec(WU, WS)] * 4 + [spec(WU, 1)],
        out_specs=[spec(1, WS)] * 4 + [spec(WU, WS)] * 2 + [spec(TC * WU, 2 * WS)] * 2 + [spec(TC, WU, WU)],
        out_shape=[sds(1, WS)] * 4 + [sds(WU, WS)] * 2 + [sds(TC * WU, 2 * WS, dtype=BF16)] * 2
        + [sds(TC, WU, WU)],
        compiler_params=_cp(("arbitrary",)),
        name="s5_prep",
    )(lam_re, lam_im, log_step, bt_re, bt_im, c_re, c_im, d)


def _s5_chunk_kernel(u_ref, bd_ref, e_ref, ft_ref, ar_ref, ai_ref, y_ref, fin_ref,
                     w_scr, lhs_scr, s_scr, xs_scr):
    rows = S5B * NCH

    @pl.when(pl.program_id(1) == 0)
    def _():
        w_scr[...] = jnp.zeros_like(w_scr)
        bd = [bd_ref[0, k].astype(BF16) for k in range(TC)]
        for s in range(TC):
            for t in range(s, TC):
                w_scr[s * WU:(s + 1) * WU, t * WU:(t + 1) * WU] = bd[t - s]

    for s in range(TC):
        lhs_scr[:, s * WU:(s + 1) * WU] = u_ref[pl.ds(s, rows, stride=TC), :].astype(BF16)
    lhs = lhs_scr[...]
    s_loc = _dot(lhs, e_ref[0])
    nl = WS // 128
    for k in range(2 * nl):
        s_scr[k] = s_loc[:, k * 128:(k + 1) * 128]
    ar = [ar_ref[0, :, k * 128:(k + 1) * 128] for k in range(nl)]
    ai = [ai_ref[0, :, k * 128:(k + 1) * 128] for k in range(nl)]
    xr = [jnp.zeros((S5B, 128), F32)] * nl
    xi = [jnp.zeros((S5B, 128), F32)] * nl
    for c in range(NCH):
        chunk_rows = pl.ds(c, S5B, stride=NCH)
        for k in range(nl):
            xs_scr[k, chunk_rows, :] = xr[k]
            xs_scr[nl + k, chunk_rows, :] = xi[k]
            sr = s_scr[k, chunk_rows, :]
            si = s_scr[nl + k, chunk_rows, :]
            xr[k], xi[k] = ar[k] * xr[k] - ai[k] * xi[k] + sr, ar[k] * xi[k] + ai[k] * xr[k] + si
    for k in range(nl):
        fin_ref[0, 0, :, k * 128:(k + 1) * 128] = xr[k]
        fin_ref[0, 0, :, WS + k * 128:WS + (k + 1) * 128] = xi[k]
    xs = jnp.concatenate([xs_scr[k] for k in range(2 * nl)], axis=1)
    y = _dot(lhs, w_scr[...]) + _dot_nt(xs.astype(BF16), ft_ref[0])
    for t in range(TC):
        y_ref[pl.ds(t, rows, stride=TC), :] = y[:, t * WU:(t + 1) * WU]


def _s5_chunk(z_f, bd, e, ft, a16r, a16i):
    rows = S5B * NCH
    nh = BATCH // S5B
    return pl.pallas_call(
        _s5_chunk_kernel,
        grid=(NGB, nh),
        in_specs=[
            pl.BlockSpec((S5B * SEQ, WU), lambda j, b: (b, j)),
            pl.BlockSpec((1, TC, WU, WU), lambda j, b: (j, 0, 0, 0)),
            pl.BlockSpec((1, TC * WU, 2 * WS), lambda j, b: (j, 0, 0)),
            pl.BlockSpec((1, TC * WU, 2 * WS), lambda j, b: (j, 0, 0)),
            pl.BlockSpec((1, 1, WS), lambda j, b: (j, 0, 0)),
            pl.BlockSpec((1, 1, WS), lambda j, b: (j, 0, 0)),
        ],
        out_specs=[
            pl.BlockSpec((S5B * SEQ, WU), lambda j, b: (b, j)),
            pl.BlockSpec((1, 1, S5B, 2 * WS), lambda j, b: (j, b, 0, 0)),
        ],
        out_shape=[jax.ShapeDtypeStruct((NP, W), F32), jax.ShapeDtypeStruct((NGB, nh, S5B, 2 * WS), F32)],
        scratch_shapes=[pltpu.VMEM((TC * WU, TC * WU), BF16), pltpu.VMEM((rows, TC * WU), BF16),
                        pltpu.VMEM((2 * WS // 128, rows, 128), F32), pltpu.VMEM((2 * WS // 128, rows, 128), F32)],
        compiler_params=_cp(("arbitrary", "arbitrary")),
        name="s5_chunk",
    )(z_f, bd, e, ft, a16r, a16i)


def _s5_step_kernel(u_ref, sr_ref, si_ref, ar_ref, ai_ref, bbr_ref, bbi_ref, cr_ref, ci_ref, d_ref,
                    y_ref, xr_ref, xi_ref):
    mask = (lax.broadcasted_iota(jnp.int32, (WU, WS), 0) // HG
            == lax.broadcasted_iota(jnp.int32, (WU, WS), 1) // P)
    for j in range(NGB):
        ul = slice(j * WU, (j + 1) * WU)
        sl = slice(j * WS, (j + 1) * WS)
        us = u_ref[:, ul]
        ar = ar_ref[j]
        ai = ai_ref[j]
        s_re = sr_ref[:, sl]
        s_im = si_ref[:, sl]
        x_re = ar * s_re - ai * s_im + _dot3(us, bbr_ref[j])
        x_im = ar * s_im + ai * s_re + _dot3(us, bbi_ref[j])
        xr_ref[:, sl] = x_re
        xi_ref[:, sl] = x_im
        cre = jnp.where(mask, cr_ref[j], 0.0)
        cim = jnp.where(mask, ci_ref[j], 0.0)
        y_ref[:, ul] = _dot3_nt(x_re, cre) - _dot3_nt(x_im, cim) + d_ref[:, ul] * us


def _s5_step(z_f, s_re, s_im, a_re, a_im, bb_re, bb_im, c_re, c_im, d_row):
    full = lambda *shape: pl.BlockSpec(shape, lambda i: (0,) * len(shape))
    return pl.pallas_call(
        _s5_step_kernel,
        grid=(1,),
        in_specs=[pl.BlockSpec((NS, W), lambda i: (NP // NS, 0)),
                  full(NS, G * P), full(NS, G * P), full(NGB, 1, WS), full(NGB, 1, WS),
                  full(NGB, WU, WS), full(NGB, WU, WS), full(NGB, WU, WS), full(NGB, WU, WS),
                  full(1, W)],
        out_specs=[full(NS, W), full(NS, G * P), full(NS, G * P)],
        out_shape=[jax.ShapeDtypeStruct((NS, W), F32), jax.ShapeDtypeStruct((NS, G * P), F32),
                   jax.ShapeDtypeStruct((NS, G * P), F32)],
        compiler_params=_cp(("arbitrary",)),
        name="s5_step",
    )(z_f, s_re, s_im, a_re, a_im, bb_re, bb_im, c_re, c_im, d_row)


def _mlstm_chunk_kernel(q_ref, k_ref, v_ref, o_ref, zif_ref, zt_ref, brow_ref, bcol_ref, gh_ref,
                        h_ref, c_ref, n_ref, m_ref):
    @pl.when(pl.program_id(1) == 0)
    def _():
        c_ref[...] = jnp.zeros_like(c_ref)
        n_ref[...] = jnp.zeros_like(n_ref)
        m_ref[...] = jnp.zeros_like(m_ref)

    zi = zif_ref[...] + brow_ref[...]
    zt = zt_ref[...] + bcol_ref[...]
    rr = lax.broadcasted_iota(jnp.int32, (CL, CL), 0)
    cc = lax.broadcasted_iota(jnp.int32, (CL, CL), 1)
    causal = cc <= rr
    tril = jnp.where(causal, 1.0, 0.0).astype(BF16)
    triu = jnp.where(rr <= cc, 1.0, 0.0).astype(BF16)
    lfc = _split3(_log_sigmoid(zi))
    bcum_col = _dot(tril, lfc[0]) + (_dot(tril, lfc[1]) + _dot(tril, lfc[2]))
    lfr = _split3(_log_sigmoid(zt))
    bcum_row = _dot(lfr[0], triu) + (_dot(lfr[1], triu) + _dot(lfr[2], triu))

    for hh in range(NH):
        hs = slice(hh * DH, (hh + 1) * DH)
        bc = bcum_col[:, NH + hh:NH + hh + 1]
        ic = zi[:, hh:hh + 1]
        br = bcum_row[NH + hh:NH + hh + 1, :]
        ir = zt[hh:hh + 1, :]
        m_prev = m_ref[0, hh:hh + 1, 0:1]
        g_inter = bc + m_prev
        dlog = jnp.where(causal, (bc - br) + ir, -jnp.inf)
        m_t = jnp.maximum(g_inter, jnp.max(dlog, axis=-1, keepdims=True))
        w_inter = jnp.exp(g_inter - m_t)
        w_intra = jnp.exp(dlog - m_t)
        qb = q_ref[:, hs]
        kb = k_ref[:, hs]
        vb = v_ref[:, hs]
        qf = qb.astype(F32)
        kf = kb.astype(F32)
        s = _dot_nt(qb, kb) * (w_intra * (DH ** -0.5))
        c_prev = c_ref[0, hh]
        n_prev = n_ref[0, hh:hh + 1, :]
        num = _dot(s.astype(BF16), vb) + w_inter * _dot(qb, c_prev.astype(BF16))
        nq = jnp.sum(s, axis=-1, keepdims=True) + w_inter * jnp.sum(qf * n_prev, axis=-1, keepdims=True)
        h = num / jnp.maximum(jnp.abs(nq), jnp.exp(-m_t))
        m_last = m_t[CL - 1:CL, :]
        w_last = jnp.exp((bc[CL - 1:CL, :] - bc) + ic - m_last) * (DH ** -0.5)
        wi_last = w_inter[CL - 1:CL, :]
        kw = kf * w_last
        c_ref[0, hh] = wi_last * c_prev + _dot_tn(kw.astype(BF16), vb)
        n_ref[0, hh:hh + 1, :] = wi_last * n_prev + jnp.sum(kw, axis=0, keepdims=True)
        m_ref[0, hh:hh + 1, :] = jnp.broadcast_to(m_last, (1, 128))
        hn = _rms(h, gh_ref[:, hs])
        h_ref[:, hs] = (hn * _sigmoid(o_ref[:, hs])).astype(BF16)


def _mlstm_chunk(z_f, z_b, z_if, z_t, brow, bcol, gh):
    blk = lambda col: pl.BlockSpec((CL, W), lambda b, c: (b * NCL + c, col))
    return pl.pallas_call(
        _mlstm_chunk_kernel,
        grid=(BATCH, NCL),
        in_specs=[blk(0), blk(1), blk(2), blk(1),
                  pl.BlockSpec((CL, 128), lambda b, c: (b * NCL + c, 0)),
                  pl.BlockSpec((8, CL), lambda b, c: (0, b * NCL + c)),
                  pl.BlockSpec((1, 128), lambda b, c: (0, 0)),
                  pl.BlockSpec((8, 1), lambda b, c: (0, 0)),
                  pl.BlockSpec((1, W), lambda b, c: (0, 0))],
        out_specs=[pl.BlockSpec((CL, W), lambda b, c: (b * NCL + c, 0)),
                   pl.BlockSpec((1, NH, DH, DH), lambda b, c: (b, 0, 0, 0)),
                   pl.BlockSpec((1, NH, DH), lambda b, c: (b, 0, 0)),
                   pl.BlockSpec((1, NH, 128), lambda b, c: (b, 0, 0))],
        out_shape=[jax.ShapeDtypeStruct((NP, W), BF16),
                   jax.ShapeDtypeStruct((BATCH, NH, DH, DH), F32),
                   jax.ShapeDtypeStruct((BATCH, NH, DH), F32),
                   jax.ShapeDtypeStruct((BATCH, NH, 128), F32)],
        compiler_params=_cp(("arbitrary", "arbitrary")),
        name="mlstm_chunk",
    )(z_b, z_b, z_b, z_f, z_if, z_t, brow, bcol, gh)


MSR = 64


def _mlstm_step_kernel(q_ref, k_ref, v_ref, o_ref, zif_ref, brow_ref, gh_ref, c0_ref, n0_ref, m0_ref,
                       h_ref, c_ref, n_ref, m_ref):
    zi = zif_ref[...] + brow_ref[...]
    ig = zi[:, 0:NH]
    g_inter = _log_sigmoid(zi[:, NH:2 * NH]) + m0_ref[...]
    m_t = jnp.maximum(g_inter, ig)
    w_inter = jnp.exp(g_inter - m_t)
    w_intra = jnp.exp(ig - m_t) * (DH ** -0.5)
    floor = jnp.exp(-m_t)
    m_ref[...] = m_t

    def heads_on_rows(x):
        return jnp.pad(x, ((0, 128 - SB), (0, 128 - NH))).T

    w_inter_t = heads_on_rows(w_inter)
    w_intra_t = heads_on_rows(w_intra)
    floor_t = heads_on_rows(floor)
    pad = jnp.zeros((128 - SB, DH), F32)
    q_t = [jnp.concatenate([q_ref[:, hh * DH:(hh + 1) * DH], pad], axis=0).T for hh in range(NH)]
    k_t = [jnp.concatenate([k_ref[:, hh * DH:(hh + 1) * DH], pad], axis=0).T for hh in range(NH)]

    def per_head_rows(ref, s):
        return jnp.concatenate([ref[s:s + 1, hh * DH:(hh + 1) * DH] for hh in range(NH)], axis=0)

    for s in range(SB):
        q4 = per_head_rows(q_ref, s)
        k4 = per_head_rows(k_ref, s)
        v4 = per_head_rows(v_ref, s)
        qc_rows = []
        for hh in range(NH):
            wi = w_inter[s:s + 1, hh:hh + 1]
            vw = v4[hh:hh + 1, :] * w_intra[s:s + 1, hh:hh + 1]
            acc = jnp.zeros((MSR, DH), F32)
            for r0 in range(0, DH, MSR):
                c_blk = c0_ref[s, hh, r0:r0 + MSR, :]
                acc = acc + q_t[hh][r0:r0 + MSR, s:s + 1] * c_blk
                c_ref[s, hh, r0:r0 + MSR, :] = wi * c_blk + k_t[hh][r0:r0 + MSR, s:s + 1] * vw
            qc_rows.append(jnp.sum(acc, axis=0, keepdims=True))
        q_c = jnp.concatenate(qc_rows, axis=0)
        wi_c = w_inter_t[0:NH, s:s + 1]
        wa_c = w_intra_t[0:NH, s:s + 1]
        n_prev = n0_ref[s]
        sv = jnp.sum(q4 * k4, axis=-1, keepdims=True) * wa_c
        num = sv * v4 + wi_c * q_c
        nq = sv + wi_c * jnp.sum(q4 * n_prev, axis=-1, keepdims=True)
        h = num / jnp.maximum(jnp.abs(nq), floor_t[0:NH, s:s + 1])
        n_ref[s] = wi_c * n_prev + wa_c * k4
        out = _rms(h, gh_ref[...]) * _sigmoid(per_head_rows(o_ref, s))
        for hh in range(NH):
            h_ref[s:s + 1, hh * DH:(hh + 1) * DH] = out[hh:hh + 1, :]


def _mlstm_step(zs, z_f, z_if, brow, gh, c0, n0, m0):
    base = NP // SB
    blk = lambda col: pl.BlockSpec((SB, W), lambda i: (i, col))
    return pl.pallas_call(
        _mlstm_step_kernel,
        grid=(NS // SB,),
        in_specs=[blk(0), blk(1), blk(2),
                  pl.BlockSpec((SB, W), lambda i: (base + i, 1)),
                  pl.BlockSpec((SB, 128), lambda i: (base + i, 0)),
                  pl.BlockSpec((1, 128), lambda i: (0, 0)),
                  pl.BlockSpec((NH, DH), lambda i: (0, 0)),
                  pl.BlockSpec((SB, NH, DH, DH), lambda i: (i, 0, 0, 0)),
                  pl.BlockSpec((SB, NH, DH), lambda i: (i, 0, 0)),
                  pl.BlockSpec((SB, NH), lambda i: (i, 0))],
        out_specs=[pl.BlockSpec((SB, W), lambda i: (i, 0)),
                   pl.BlockSpec((SB, NH, DH, DH), lambda i: (i, 0, 0, 0)),
                   pl.BlockSpec((SB, NH, DH), lambda i: (i, 0, 0)),
                   pl.BlockSpec((SB, NH), lambda i: (i, 0))],
        out_shape=[jax.ShapeDtypeStruct((NS, W), F32),
                   jax.ShapeDtypeStruct((NS, NH, DH, DH), F32),
                   jax.ShapeDtypeStruct((NS, NH, DH), F32),
                   jax.ShapeDtypeStruct((NS, NH), F32)],
        compiler_params=_cp(("arbitrary",)),
        name="mlstm_step",
    )(zs, zs, zs, z_f, z_if, brow, gh, c0, n0, m0)


def _softmax_rows(s):
    e = jnp.exp(s - jnp.max(s, axis=-1, keepdims=True))
    return e / jnp.sum(e, axis=-1, keepdims=True)


def _xattn_prompt_kernel(q_ref, k_ref, v_ref, o_ref):
    for hh in range(NH):
        hs = slice(hh * DH, (hh + 1) * DH)
        s = _dot_nt(q_ref[:, hs], k_ref[:, hs].astype(BF16)) * (DH ** -0.5)
        p = _softmax_rows(s)
        o_ref[:, hs] = _dot(p.astype(BF16), v_ref[:, hs].astype(BF16)).astype(BF16)


def _xattn_prompt(z_b, kv):
    nt = SEQ // TM
    return pl.pallas_call(
        _xattn_prompt_kernel,
        grid=(BATCH, nt),
        in_specs=[pl.BlockSpec((TM, W), lambda b, t: (b * nt + t, 3)),
                  pl.BlockSpec((N_MEM, W), lambda b, t: (b, 0)),
                  pl.BlockSpec((N_MEM, W), lambda b, t: (b, 1))],
        out_specs=pl.BlockSpec((TM, W), lambda b, t: (b * nt + t, 0)),
        out_shape=jax.ShapeDtypeStruct((NP, W), BF16),
        compiler_params=_cp(("arbitrary", "arbitrary")),
        name="xattn_prompt",
    )(z_b, kv, kv)


XS = 4


XMC = 64


def _xattn_step_kernel(q_ref, k_ref, v_ref, o_ref):
    def part(t):
        for s in range(XS):
            r = t * XS + s
            q4 = jnp.concatenate([q_ref[r:r + 1, hh * DH:(hh + 1) * DH] for hh in range(NH)], axis=0)
            sc = jnp.concatenate(
                [jnp.sum(k_ref[s, m0:m0 + XMC] * q4[None], axis=-1, keepdims=True) for m0 in range(0, N_MEM, XMC)],
                axis=0) * (DH ** -0.5)
            e = jnp.exp(sc - jnp.max(sc, axis=0, keepdims=True))
            p = e / jnp.sum(e, axis=0, keepdims=True)
            acc = jnp.zeros((NH, DH), F32)
            for m0 in range(0, N_MEM, XMC):
                acc = acc + jnp.sum(p[m0:m0 + XMC] * v_ref[s, m0:m0 + XMC], axis=0)
            for hh in range(NH):
                o_ref[r:r + 1, hh * DH:(hh + 1) * DH] = acc[hh:hh + 1, :]

    for t in range(SB // XS):
        pl.when(pl.program_id(1) == t)(functools.partial(part, t))


def _xattn_step(zs, mem_k, mem_v):
    nt = SB // XS
    return pl.pallas_call(
        _xattn_step_kernel,
        grid=(NS // SB, nt),
        in_specs=[pl.BlockSpec((SB, W), lambda i, t: (i, 3)),
                  pl.BlockSpec((XS, N_MEM, NH, DH), lambda i, t: (i * nt + t, 0, 0, 0)),
                  pl.BlockSpec((XS, N_MEM, NH, DH), lambda i, t: (i * nt + t, 0, 0, 0))],
        out_specs=pl.BlockSpec((SB, W), lambda i, t: (i, 0)),
        out_shape=jax.ShapeDtypeStruct((NS, W), F32),
        compiler_params=_cp(("arbitrary", "arbitrary")),
        name="xattn_step",
    )(zs, mem_k, mem_v)


TMX = 512


def _mix_kernel(x_ref, g_ref, yp_ref, ys_ref, mp_ref, ms_ref, ap_ref, as_ref, wglu_ref,
                wg0_ref, wg1_ref, wg2_ref, wb0_ref, wb1_ref, wb2_ref, o_ref, h_scr, s5_scr):
    i = pl.program_id(0)
    prompt = i < NPT

    @pl.when(pl.program_id(1) == 0)
    def _():
        h_scr[...] = _rms(x_ref[...], g_ref[...]).astype(BF16)
        y = _gelu_tanh(jnp.where(prompt, yp_ref[...], ys_ref[...]))
        s5_scr[...] = (y * _sigmoid(_dot(y.astype(BF16), wglu_ref[...]))).astype(BF16)

    h = h_scr[...]
    ml = jnp.where(prompt, mp_ref[...], ms_ref[...])
    xa = jnp.where(prompt, ap_ref[...], as_ref[...])
    merged = (_sigmoid(_dot(h, wg0_ref[...])) * _dot(s5_scr[...], wb0_ref[...])
              + _sigmoid(_dot(h, wg1_ref[...])) * _dot(ml, wb1_ref[...])
              + _sigmoid(_dot(h, wg2_ref[...])) * _dot(xa, wb2_ref[...]))
    o_ref[...] = merged.astype(BF16)


def _mix(x1, g, yp, ys, mp, ms, ap, as_, wglu, wg, wb):
    prow = pl.BlockSpec((TM, W), lambda i, j: (jnp.minimum(i, NPT - 1), 0))
    srow = pl.BlockSpec((TM, W), lambda i, j: (0, 0))
    wgs = pl.BlockSpec((D, TMX), lambda i, j: (0, j))
    wbs = pl.BlockSpec((W, TMX), lambda i, j: (0, j))
    return pl.pallas_call(
        _mix_kernel,
        grid=(NT, D // TMX),
        in_specs=[pl.BlockSpec((TM, D), lambda i, j: (i, 0)), pl.BlockSpec((1, D), lambda i, j: (0, 0)),
                  prow, srow, prow, srow, prow, srow,
                  pl.BlockSpec((W, W), lambda i, j: (0, 0)),
                  wgs, wgs, wgs, wbs, wbs, wbs],
        out_specs=pl.BlockSpec((TM, TMX), lambda i, j: (i, j)),
        out_shape=jax.ShapeDtypeStruct((MROWS, D), BF16),
        scratch_shapes=[pltpu.VMEM((TM, D), BF16), pltpu.VMEM((TM, W), BF16)],
        compiler_params=_cp(("arbitrary", "arbitrary")),
        name="mix",
    )(x1, g, yp, ys, mp, ms, ap, as_, wglu, *wg, *wb)


def _outproj_kernel(x_ref, m_ref, w_ref, o_ref):
    o_ref[...] = x_ref[...] + _dot(m_ref[...], w_ref[...])


def _outproj(x1, merged, w_out):
    return pl.pallas_call(
        _outproj_kernel,
        grid=(NT,),
        in_specs=[pl.BlockSpec((TM, D), lambda i: (i, 0)), pl.BlockSpec((TM, D), lambda i: (i, 0)),
                  pl.BlockSpec((D, D), lambda i: (0, 0))],
        out_specs=pl.BlockSpec((TM, D), lambda i: (i, 0)),
        out_shape=jax.ShapeDtypeStruct((MROWS, D), F32),
        compiler_params=_cp(("arbitrary",)),
        name="out_proj",
    )(x1, merged, w_out)


def _pad_rows(a, rows):
    return jnp.pad(a, ((0, rows - a.shape[0]), (0, 0)))


def kernel(x_prompt, x_sample, mem_prompt, cache_mem_k, cache_mem_v, state_s5_re, state_s5_im, state_mlstm_C,
           state_mlstm_n, state_mlstm_m, g_ffn1, w1_gate, w1_up, w1_down, g_mix, w_in, s5_lambda_re,
           s5_lambda_im, s5_log_step, s5_b_re, s5_b_im, s5_c_re, s5_c_im, s5_d, w_s5_glu, b_igate, b_fgate,
           g_mlstm_head, g_mem, w_mem_k, w_mem_v, w_br_s5, w_br_ml, w_br_xa, w_out, g_ffn2, w2_gate, w2_up,
           w2_down, g_final):
    bf = lambda a: a.astype(BF16)

    wi = w_in[0]
    q0, o0, i0, x0, g0 = W, 4 * W, 5 * W, 5 * W + 2 * NH, 6 * W + 2 * NH
    w_main = bf(jnp.concatenate([wi[:, :W], wi[:, o0:i0], wi[:, q0:o0], wi[:, x0:g0]], axis=1))
    w_gate_cols = wi[:, i0:x0]
    w_if = bf(jnp.pad(w_gate_cols, ((0, 0), (0, 128 - 2 * NH))))
    w_ift = bf(w_gate_cols.T)
    w_bg = [bf(wi[:, g0 + b * D:g0 + (b + 1) * D]) for b in range(3)]
    w_br = [bf(w_br_s5[0]), bf(w_br_ml[0]), bf(w_br_xa[0])]
    brow = jnp.pad(jnp.concatenate([b_igate[0], b_fgate[0]]), (0, 128 - 2 * NH)).reshape(1, 128)
    bcol = jnp.concatenate([b_igate[0], b_fgate[0]]).reshape(2 * NH, 1)
    gh = g_mlstm_head[0].reshape(1, W)

    xp = x_prompt.reshape(NP, D)
    xs = _pad_rows(x_sample.reshape(NS, D), TM)
    x1 = _ffn((xp, xs), g_ffn1[0].reshape(1, D), bf(w1_gate[0]), bf(w1_up[0]), bf(w1_down[0]))

    z_f, z_b, z_if, z_t = _in_proj(x1, g_mix[0].reshape(1, D), w_main, w_if, w_ift)
    zs = z_b[NP:NP + NS].astype(F32)

    blk = lambda a: jnp.tile(a.reshape(NGB, WU, P), (1, 1, GB))
    row = lambda a: a.reshape(NGB, 1, WS)
    (a_re, a_im, a16_re, a16_im, bb_re, bb_im, e_op, ft_op, bd_op) = _s5_prep(
        row(s5_lambda_re[0]), row(s5_lambda_im[0]), row(jnp.repeat(s5_log_step[0], P)),
        blk(s5_b_re[0].transpose(0, 2, 1)), blk(s5_b_im[0].transpose(0, 2, 1)),
        blk(s5_c_re[0]), blk(s5_c_im[0]), s5_d[0].reshape(NGB, WU, 1))
    y_s5_p, fin = _s5_chunk(z_f, bd_op, e_op, ft_op, a16_re, a16_im)
    fin = fin.reshape(NGB, BATCH, 2, GB, P).transpose(2, 1, 0, 3, 4).reshape(2, 1, BATCH, G, P)
    y_s5_s, s5_re_s, s5_im_s = _s5_step(
        z_f, state_s5_re[0].reshape(NS, G * P), state_s5_im[0].reshape(NS, G * P),
        a_re, a_im, bb_re, bb_im, blk(s5_c_re[0]), blk(s5_c_im[0]), s5_d[0].reshape(1, W))

    ml_p, c_p, n_p, m_p = _mlstm_chunk(z_f, z_b, z_if, z_t, brow, bcol, gh)
    ml_s, c_s, n_s, m_s = _mlstm_step(zs, z_f, z_if, brow, gh.reshape(NH, DH), state_mlstm_C[0], state_mlstm_n[0], state_mlstm_m[0])

    w_kv = bf(jnp.concatenate([w_mem_k[0], w_mem_v[0]], axis=1))
    kv = _mem_proj(mem_prompt.reshape(BATCH * N_MEM, D), g_mem[0].reshape(1, D), w_kv)
    xa_p = _xattn_prompt(z_b, kv)
    xa_s = _xattn_step(zs, cache_mem_k[0], cache_mem_v[0])

    merged = _mix(x1, g_mix[0].reshape(1, D), y_s5_p, _pad_rows(y_s5_s, TM), ml_p, _pad_rows(bf(ml_s), TM),
                  xa_p, _pad_rows(bf(xa_s), TM), bf(w_s5_glu[0]), w_bg, w_br)
    x2 = _outproj(x1, merged, bf(w_out[0]))
    y_p, y_s = _ffn((x2,), g_ffn2[0].reshape(1, D), bf(w2_gate[0]), bf(w2_up[0]), bf(w2_down[0]),
                    g_final.reshape(1, D))

    return (y_p.reshape(BATCH, SEQ, D), y_s[:NS].reshape(NS, 1, D),
            kv[:, :W].reshape(1, BATCH, N_MEM, NH, DH), kv[:, W:].reshape(1, BATCH, N_MEM, NH, DH),
            fin[0], fin[1], c_p[None], n_p[None], m_p[:, :, 0][None],
            s5_re_s.reshape(1, NS, G, P), s5_im_s.reshape(1, NS, G, P), c_s[None], n_s[None], m_s[None])
```

```python
import functools

import jax
import jax.numpy as jnp
from jax import lax
from jax.experimental import pallas as pl
from jax.experimental.pallas import tpu as pltpu

F32 = jnp.float32
BF16 = jnp.bfloat16

D = 2048
BATCH = 4
SEQ = 2048
NS = 128
NP = BATCH * SEQ
TM = 512
NPT = NP // TM
NT = NPT + 1
MROWS = NT * TM
N_MEM = 256
FF = 5504
TF = 512
G = 64
P = 64
HG = 16
TC = 16
GB = 8
NGB = G // GB
WU = GB * HG
WS = GB * P
NCH = SEQ // TC
S5B = 2
W = 1024
NH = 4
DH = 256
CL = 256
NCL = SEQ // CL
SB = 8
EPS = 1e-6
VMEM_LIMIT = 56 * 1024 * 1024


def _cp(sem, vmem=VMEM_LIMIT):
    return pltpu.CompilerParams(dimension_semantics=sem, vmem_limit_bytes=vmem)


def _dot(a, b):
    return jnp.dot(a, b, preferred_element_type=F32)


def _dot_nt(a, b):
    return lax.dot_general(a, b, (((1,), (1,)), ((), ())), preferred_element_type=F32)


def _dot_tn(a, b):
    return lax.dot_general(a, b, (((0,), (0,)), ((), ())), preferred_element_type=F32)


def _hi_lo(x):
    hi = x.astype(BF16)
    lo = (x - hi.astype(F32)).astype(BF16)
    return hi, lo


def _split3(x):
    hi = x.astype(BF16)
    r1 = x - hi.astype(F32)
    mid = r1.astype(BF16)
    lo = (r1 - mid.astype(F32)).astype(BF16)
    return hi, mid, lo


def _dot3(a, b):
    ah, al = _hi_lo(a)
    bh, bl = _hi_lo(b)
    return _dot(ah, bh) + (_dot(ah, bl) + _dot(al, bh))


def _dot3_nt(a, b):
    ah, al = _hi_lo(a)
    bh, bl = _hi_lo(b)
    return _dot_nt(ah, bh) + (_dot_nt(ah, bl) + _dot_nt(al, bh))


def _rms(x, g):
    r = lax.rsqrt(jnp.mean(x * x, axis=-1, keepdims=True) + EPS)
    return (x * r) * g


def _sigmoid(x):
    return 1.0 / (1.0 + jnp.exp(-x))


def _log_sigmoid(x):
    return jnp.minimum(x, 0.0) - jnp.log1p(jnp.exp(-jnp.abs(x)))


def _gelu_tanh(x):
    return x * (0.5 * (1.0 + jnp.tanh(0.7978845608028654 * (x + 0.044715 * (x * x * x)))))


def _ffn_kernel(*refs, two_src, final_norm):
    refs = list(refs)
    if two_src:
        xp_ref, xs_ref = refs[:2]
        refs = refs[2:]
    else:
        xp_ref = xs_ref = refs[0]
        refs = refs[1:]
    g_ref, wg_ref, wu_ref, wd_ref = refs[:4]
    refs = refs[4:]
    if final_norm:
        gf_ref, op_ref, os_ref, h_scr, acc_scr = refs
    else:
        o_ref, h_scr, acc_scr = refs
    j = pl.program_id(1)

    def tile(rows, x_ref):
        rs = slice(0, rows)

        @pl.when(j == 0)
        def _():
            h_scr[rs, :] = _rms(x_ref[rs, :], g_ref[...]).astype(BF16)
            acc_scr[rs, :] = jnp.zeros((rows, D), F32)

        h = h_scr[rs, :]
        gt = _dot(h, wg_ref[...])
        up = _dot(h, wu_ref[...])
        valid = FF - j * TF
        hid = jnp.where(lax.broadcasted_iota(jnp.int32, (rows, TF), 1) < valid, (gt * _sigmoid(gt)) * up, 0.0)
        wd = wd_ref[...]
        wd = jnp.where(lax.broadcasted_iota(jnp.int32, (TF, D), 0) < valid, wd, jnp.zeros_like(wd))
        acc_scr[rs, :] += _dot(hid.astype(BF16), wd)

        @pl.when(j == pl.num_programs(1) - 1)
        def _():
            y = x_ref[rs, :] + 0.5 * acc_scr[rs, :]
            if final_norm:
                y = _rms(y, gf_ref[...])
                (op_ref if rows == TM else os_ref)[...] = y
            else:
                o_ref[rs, :] = y
                if rows < TM:
                    o_ref[rows:, :] = jnp.zeros((TM - rows, D), F32)

    pl.when(pl.program_id(0) < NPT)(functools.partial(tile, TM, xp_ref))
    pl.when(pl.program_id(0) == NPT)(functools.partial(tile, NS, xs_ref))


def _ffn(xs, g, wg, wu, wd, g_final=None):
    two_src = len(xs) == 2
    final_norm = g_final is not None
    row = pl.BlockSpec((TM, D), lambda i, j: (i, 0))
    prow = pl.BlockSpec((TM, D), lambda i, j: (jnp.minimum(i, NPT - 1), 0))
    srow = pl.BlockSpec((NS, D), lambda i, j: (0, 0))
    vec = pl.BlockSpec((1, D), lambda i, j: (0, 0))
    in_specs = ([prow, srow] if two_src else [row]) + [
        vec,
        pl.BlockSpec((D, TF), lambda i, j: (0, j)),
        pl.BlockSpec((D, TF), lambda i, j: (0, j)),
        pl.BlockSpec((TF, D), lambda i, j: (j, 0)),
    ]
    args = list(xs) + [g, wg, wu, wd]
    if final_norm:
        in_specs.append(vec)
        args.append(g_final)
        out_shape = (jax.ShapeDtypeStruct((NP, D), F32), jax.ShapeDtypeStruct((NS, D), F32))
        out_specs = (prow, srow)
    else:
        out_shape = jax.ShapeDtypeStruct((MROWS, D), F32)
        out_specs = row
    return pl.pallas_call(
        functools.partial(_ffn_kernel, two_src=two_src, final_norm=final_norm),
        grid=(NT, pl.cdiv(FF, TF)),
        in_specs=in_specs,
        out_specs=out_specs,
        out_shape=out_shape,
        scratch_shapes=[pltpu.VMEM((TM, D), BF16), pltpu.VMEM((TM, D), F32)],
        compiler_params=_cp(("arbitrary", "arbitrary")),
        name="ffn_final" if final_norm else "ffn",
    )(*args)


NF32 = 2
TMI = 256
NPI = NP // TMI


def _in_proj_kernel(x_ref, g_ref, w_ref, wif_ref, wift_ref,
                    ofp_ref, obp_ref, zifp_ref, ztp_ref, ofs_ref, obs_ref, zifs_ref):
    def emit(of_ref, ob_ref, zif_ref, zt_ref):
        h = _rms(x_ref[...], g_ref[...]).astype(BF16)
        zif_ref[...] = _dot(h, wif_ref[...])
        if zt_ref is not None:
            zt_ref[...] = _dot_nt(wift_ref[...], h)
        for j in range(w_ref.shape[1] // W):
            res = _dot(h, w_ref[:, j * W:(j + 1) * W])
            if j < NF32:
                of_ref[:, j * W:(j + 1) * W] = res
            else:
                ob_ref[:, (j - NF32) * W:(j - NF32 + 1) * W] = res.astype(BF16)

    pl.when(pl.program_id(0) < NPI)(functools.partial(emit, ofp_ref, obp_ref, zifp_ref, ztp_ref))
    pl.when(pl.program_id(0) == NPI)(functools.partial(emit, ofs_ref, obs_ref, zifs_ref, None))


def _in_proj(x, g, w, w_if, w_ift):
    n = w.shape[1]
    nb = n // W
    once = pl.Buffered(1)
    pblk = lambda cols: pl.BlockSpec((TMI, cols), lambda i: (jnp.minimum(i, NPI - 1), 0))
    sblk = lambda cols: pl.BlockSpec((TMI, cols), lambda i: (0, 0))
    widths = (NF32 * W, (nb - NF32) * W, 128)
    dtypes = (F32, BF16, F32)
    return pl.pallas_call(
        _in_proj_kernel,
        grid=(NPI + 1,),
        in_specs=[
            pl.BlockSpec((TMI, D), lambda i: (i, 0)),
            pl.BlockSpec((1, D), lambda i: (0, 0)),
            pl.BlockSpec((D, n), lambda i: (0, 0), pipeline_mode=once),
            pl.BlockSpec((D, 128), lambda i: (0, 0), pipeline_mode=once),
            pl.BlockSpec((8, D), lambda i: (0, 0), pipeline_mode=once),
        ],
        out_specs=[pblk(c) for c in widths]
        + [pl.BlockSpec((8, TMI), lambda i: (0, jnp.minimum(i, NPI - 1)))]
        + [sblk(c) for c in widths],
        out_shape=[jax.ShapeDtypeStruct((NP, c), t) for c, t in zip(widths, dtypes)]
        + [jax.ShapeDtypeStruct((8, NP), F32)]
        + [jax.ShapeDtypeStruct((TMI, c), t) for c, t in zip(widths, dtypes)],
        compiler_params=_cp(("arbitrary",)),
        name="in_proj",
    )(x, g, w, w_if, w_ift)


def _mem_proj_kernel(x_ref, g_ref, w_ref, o_ref, h_scr):
    @pl.when(pl.program_id(1) == 0)
    def _():
        h_scr[...] = _rms(x_ref[...], g_ref[...]).astype(BF16)

    o_ref[...] = _dot(h_scr[...], w_ref[...])


def _mem_proj(x, g, w):
    m, n = x.shape[0], w.shape[1]
    return pl.pallas_call(
        _mem_proj_kernel,
        grid=(m // TM, n // W),
        in_specs=[pl.BlockSpec((TM, D), lambda i, j: (i, 0)), pl.BlockSpec((1, D), lambda i, j: (0, 0)),
                  pl.BlockSpec((D, W), lambda i, j: (0, j))],
        out_specs=pl.BlockSpec((TM, W), lambda i, j: (i, j)),
        out_shape=jax.ShapeDtypeStruct((m, n), F32),
        scratch_shapes=[pltpu.VMEM((TM, D), BF16)],
        compiler_params=_cp(("arbitrary", "arbitrary")),
        name="mem_proj",
    )(x, g, w)


def _s5_prep_kernel(lr_ref, li_ref, ls_ref, btr_ref, bti_ref, cr_ref, ci_ref, d_ref,
                    ar_ref, ai_ref, a16r_ref, a16i_ref, bbr_ref, bbi_ref, e_ref, ft_ref, bd_ref):
    lr = lr_ref[0]
    li = li_ref[0]
    dt = jnp.exp(ls_ref[0])

    def power(k):
        mag = jnp.exp(lr * dt * float(k))
        ang = li * dt * float(k)
        return mag * jnp.cos(ang), mag * jnp.sin(ang)

    pw = [power(k) for k in range(TC + 1)]
    ar, ai = pw[1]
    den = lr * lr + li * li
    nr = ar - 1.0
    z_re = (nr * lr + ai * li) / den
    z_im = (ai * lr - nr * li) / den
    mask = (lax.broadcasted_iota(jnp.int32, (WU, WS), 0) // HG
            == lax.broadcasted_iota(jnp.int32, (WU, WS), 1) // P)
    btr = btr_ref[0]
    bti = bti_ref[0]
    bbr = jnp.where(mask, z_re * btr - z_im * bti, 0.0)
    bbi = jnp.where(mask, z_re * bti + z_im * btr, 0.0)
    cr = jnp.where(mask, cr_ref[0], 0.0)
    ci = jnp.where(mask, ci_ref[0], 0.0)

    ar_ref[0] = ar
    ai_ref[0] = ai
    a16r_ref[0] = pw[TC][0]
    a16i_ref[0] = pw[TC][1]
    bbr_ref[0] = bbr
    bbi_ref[0] = bbi

    def cmul(xr, xi, k):
        pr, pi = pw[k]
        return xr * pr - xi * pi, xr * pi + xi * pr

    diag = (lax.broadcasted_iota(jnp.int32, (WU, WU), 0) == lax.broadcasted_iota(jnp.int32, (WU, WU), 1))
    for s in range(TC):
        er, ei = cmul(bbr, bbi, TC - 1 - s)
        e_ref[0, s * WU:(s + 1) * WU, 0:WS] = er.astype(BF16)
        e_ref[0, s * WU:(s + 1) * WU, WS:2 * WS] = ei.astype(BF16)
        fr, fi = cmul(cr, ci, s + 1)
        ft_ref[0, s * WU:(s + 1) * WU, 0:WS] = fr.astype(BF16)
        ft_ref[0, s * WU:(s + 1) * WU, WS:2 * WS] = (-fi).astype(BF16)
        rr, ri = cmul(cr, ci, s)
        kern = _dot3_nt(bbr, rr) - _dot3_nt(bbi, ri)
        if s == 0:
            kern = kern + jnp.where(diag, d_ref[0], 0.0)
        bd_ref[0, s] = kern


def _s5_prep(lam_re, lam_im, log_step, bt_re, bt_im, c_re, c_im, d):
    def spec(*shape):
        nd = len(shape)
        return pl.BlockSpec((1,) + shape, lambda i: (i,) + (0,) * nd)

    def sds(*shape, dtype=F32):
        return jax.ShapeDtypeStruct((NGB,) + shape, dtype)

    return pl.pallas_call(
        _s5_prep_kernel,
        grid=(NGB,),
        in_specs=[spec(1, WS)] * 3 + [spec(WU, WS)] * 4 + [spec(WU, 1)],
        out_specs=[spec(1, WS)] * 4 + [spec(WU, WS)] * 2 + [spec(TC * WU, 2 * WS)] * 2 + [spec(TC, WU, WU)],
        out_shape=[sds(1, WS)] * 4 + [sds(WU, WS)] * 2 + [sds(TC * WU, 2 * WS, dtype=BF16)] * 2
        + [sds(TC, WU, WU)],
        compiler_params=_cp(("arbitrary",)),
        name="s5_prep",
    )(lam_re, lam_im, log_step, bt_re, bt_im, c_re, c_im, d)


def _s5_chunk_kernel(u_ref, bd_ref, e_ref, ft_ref, ar_ref, ai_ref, y_ref, fin_ref,
                     w_scr, lhs_scr, s_scr, xs_scr):
    rows = S5B * NCH

    @pl.when(pl.program_id(1) == 0)
    def _():
        w_scr[...] = jnp.zeros_like(w_scr)
        bd = [bd_ref[0, k].astype(BF16) for k in range(TC)]
        for s in range(TC):
            for t in range(s, TC):
                w_scr[s * WU:(s + 1) * WU, t * WU:(t + 1) * WU] = bd[t - s]

    for s in range(TC):
        lhs_scr[:, s * WU:(s + 1) * WU] = u_ref[pl.ds(s, rows, stride=TC), :].astype(BF16)
    lhs = lhs_scr[...]
    s_loc = _dot(lhs, e_ref[0])
    nl = WS // 128
    for k in range(2 * nl):
        s_scr[k] = s_loc[:, k * 128:(k + 1) * 128]
    ar = [ar_ref[0, :, k * 128:(k + 1) * 128] for k in range(nl)]
    ai = [ai_ref[0, :, k * 128:(k + 1) * 128] for k in range(nl)]
    xr = [jnp.zeros((S5B, 128), F32)] * nl
    xi = [jnp.zeros((S5B, 128), F32)] * nl
    for c in range(NCH):
        chunk_rows = pl.ds(c, S5B, stride=NCH)
        for k in range(nl):
            xs_scr[k, chunk_rows, :] = xr[k]
            xs_scr[nl + k, chunk_rows, :] = xi[k]
            sr = s_scr[k, chunk_rows, :]
            si = s_scr[nl + k, chunk_rows, :]
            xr[k], xi[k] = ar[k] * xr[k] - ai[k] * xi[k] + sr, ar[k] * xi[k] + ai[k] * xr[k] + si
    for k in range(nl):
        fin_ref[0, 0, :, k * 128:(k + 1) * 128] = xr[k]
        fin_ref[0, 0, :, WS + k * 128:WS + (k + 1) * 128] = xi[k]
    xs = jnp.concatenate([xs_scr[k] for k in range(2 * nl)], axis=1)
    y = _dot(lhs, w_scr[...]) + _dot_nt(xs.astype(BF16), ft_ref[0])
    for t in range(TC):
        y_ref[pl.ds(t, rows, stride=TC), :] = y[:, t * WU:(t + 1) * WU]


def _s5_chunk(z_f, bd, e, ft, a16r, a16i):
    rows = S5B * NCH
    nh = BATCH // S5B
    return pl.pallas_call(
        _s5_chunk_kernel,
        grid=(NGB, nh),
        in_specs=[
            pl.BlockSpec((S5B * SEQ, WU), lambda j, b: (b, j)),
            pl.BlockSpec((1, TC, WU, WU), lambda j, b: (j, 0, 0, 0)),
            pl.BlockSpec((1, TC * WU, 2 * WS), lambda j, b: (j, 0, 0)),
            pl.BlockSpec((1, TC * WU, 2 * WS), lambda j, b: (j, 0, 0)),
            pl.BlockSpec((1, 1, WS), lambda j, b: (j, 0, 0)),
            pl.BlockSpec((1, 1, WS), lambda j, b: (j, 0, 0)),
        ],
        out_specs=[
            pl.BlockSpec((S5B * SEQ, WU), lambda j, b: (b, j)),
            pl.BlockSpec((1, 1, S5B, 2 * WS), lambda j, b: (j, b, 0, 0)),
        ],
        out_shape=[jax.ShapeDtypeStruct((NP, W), F32), jax.ShapeDtypeStruct((NGB, nh, S5B, 2 * WS), F32)],
        scratch_shapes=[pltpu.VMEM((TC * WU, TC * WU), BF16), pltpu.VMEM((rows, TC * WU), BF16),
                        pltpu.VMEM((2 * WS // 128, rows, 128), F32), pltpu.VMEM((2 * WS // 128, rows, 128), F32)],
        compiler_params=_cp(("arbitrary", "arbitrary")),
        name="s5_chunk",
    )(z_f, bd, e, ft, a16r, a16i)


def _s5_step_kernel(u_ref, sr_ref, si_ref, ar_ref, ai_ref, bbr_ref, bbi_ref, cr_ref, ci_ref, d_ref,
                    y_ref, xr_ref, xi_ref):
    mask = (lax.broadcasted_iota(jnp.int32, (WU, WS), 0) // HG
            == lax.broadcasted_iota(jnp.int32, (WU, WS), 1) // P)
    for j in range(NGB):
        ul = slice(j * WU, (j + 1) * WU)
        sl = slice(j * WS, (j + 1) * WS)
        us = u_ref[:, ul]
        ar = ar_ref[j]
        ai = ai_ref[j]
        s_re = sr_ref[:, sl]
        s_im = si_ref[:, sl]
        x_re = ar * s_re - ai * s_im + _dot3(us, bbr_ref[j])
        x_im = ar * s_im + ai * s_re + _dot3(us, bbi_ref[j])
        xr_ref[:, sl] = x_re
        xi_ref[:, sl] = x_im
        cre = jnp.where(mask, cr_ref[j], 0.0)
        cim = jnp.where(mask, ci_ref[j], 0.0)
        y_ref[:, ul] = _dot3_nt(x_re, cre) - _dot3_nt(x_im, cim) + d_ref[:, ul] * us


def _s5_step(z_f, s_re, s_im, a_re, a_im, bb_re, bb_im, c_re, c_im, d_row):
    full = lambda *shape: pl.BlockSpec(shape, lambda i: (0,) * len(shape))
    return pl.pallas_call(
        _s5_step_kernel,
        grid=(1,),
        in_specs=[pl.BlockSpec((NS, W), lambda i: (0, 0)),
                  full(NS, G * P), full(NS, G * P), full(NGB, 1, WS), full(NGB, 1, WS),
                  full(NGB, WU, WS), full(NGB, WU, WS), full(NGB, WU, WS), full(NGB, WU, WS),
                  full(1, W)],
        out_specs=[full(NS, W), full(NS, G * P), full(NS, G * P)],
        out_shape=[jax.ShapeDtypeStruct((NS, W), F32), jax.ShapeDtypeStruct((NS, G * P), F32),
                   jax.ShapeDtypeStruct((NS, G * P), F32)],
        compiler_params=_cp(("arbitrary",)),
        name="s5_step",
    )(z_f, s_re, s_im, a_re, a_im, bb_re, bb_im, c_re, c_im, d_row)


def _mlstm_chunk_kernel(q_ref, k_ref, v_ref, o_ref, zif_ref, zt0_ref, zt1_ref, zt2_ref, zt3_ref,
                        brow_ref, bcol_ref, gh_ref, h_ref, c_ref, n_ref, m_ref):
    @pl.when(pl.program_id(0) == 0)
    def _():
        c_ref[...] = jnp.zeros_like(c_ref)
        n_ref[...] = jnp.zeros_like(n_ref)
        m_ref[...] = jnp.zeros_like(m_ref)

    rr = lax.broadcasted_iota(jnp.int32, (CL, CL), 0)
    cc = lax.broadcasted_iota(jnp.int32, (CL, CL), 1)
    causal = cc <= rr
    tril = jnp.where(causal, 1.0, 0.0).astype(BF16)
    triu = jnp.where(rr <= cc, 1.0, 0.0).astype(BF16)

    for b, zt_ref in enumerate((zt0_ref, zt1_ref, zt2_ref, zt3_ref)):
        zi = zif_ref[b] + brow_ref[...]
        zt = zt_ref[...] + bcol_ref[...]
        lfc = _split3(_log_sigmoid(zi))
        bcum_col = _dot(tril, lfc[0]) + (_dot(tril, lfc[1]) + _dot(tril, lfc[2]))
        lfr = _split3(_log_sigmoid(zt))
        bcum_row = _dot(lfr[0], triu) + (_dot(lfr[1], triu) + _dot(lfr[2], triu))

        for hh in range(NH):
            hs = slice(hh * DH, (hh + 1) * DH)
            bc = bcum_col[:, NH + hh:NH + hh + 1]
            ic = zi[:, hh:hh + 1]
            br = bcum_row[NH + hh:NH + hh + 1, :]
            ir = zt[hh:hh + 1, :]
            m_prev = m_ref[b, hh:hh + 1, 0:1]
            g_inter = bc + m_prev
            dlog = jnp.where(causal, (bc - br) + ir, -jnp.inf)
            m_t = jnp.maximum(g_inter, jnp.max(dlog, axis=-1, keepdims=True))
            w_inter = jnp.exp(g_inter - m_t)
            w_intra = jnp.exp(dlog - m_t)
            qb = q_ref[b, :, hs]
            kb = k_ref[b, :, hs]
            vb = v_ref[b, :, hs]
            qf = qb.astype(F32)
            kf = kb.astype(F32)
            s = _dot_nt(qb, kb) * (w_intra * (DH ** -0.5))
            c_prev = c_ref[b, hh]
            n_prev = n_ref[b, hh:hh + 1, :]
            num = _dot(s.astype(BF16), vb) + w_inter * _dot(qb, c_prev.astype(BF16))
            nq = jnp.sum(s, axis=-1, keepdims=True) + w_inter * jnp.sum(qf * n_prev, axis=-1, keepdims=True)
            h = num / jnp.maximum(jnp.abs(nq), jnp.exp(-m_t))
            m_last = m_t[CL - 1:CL, :]
            w_last = jnp.exp((bc[CL - 1:CL, :] - bc) + ic - m_last) * (DH ** -0.5)
            wi_last = w_inter[CL - 1:CL, :]
            kw = kf * w_last
            c_ref[b, hh] = wi_last * c_prev + _dot_tn(kw.astype(BF16), vb)
            n_ref[b, hh:hh + 1, :] = wi_last * n_prev + jnp.sum(kw, axis=0, keepdims=True)
            m_ref[b, hh:hh + 1, :] = jnp.broadcast_to(m_last, (1, 128))
            hn = _rms(h, gh_ref[:, hs])
            h_ref[b, :, hs] = (hn * _sigmoid(o_ref[b, :, hs])).astype(BF16)


def _mlstm_chunk(z_f, z_b, z_if, z_t, brow, bcol, gh):
    blk = lambda col: pl.BlockSpec((BATCH, CL, W), lambda c: (0, c, col))
    const = lambda *shape: pl.BlockSpec(shape, lambda c: (0,) * len(shape))
    zt_specs = [pl.BlockSpec((8, CL), functools.partial(lambda c, b: (0, b * NCL + c), b=b)) for b in range(BATCH)]
    return pl.pallas_call(
        _mlstm_chunk_kernel,
        grid=(NCL,),
        in_specs=[blk(0), blk(1), blk(2), blk(1), pl.BlockSpec((BATCH, CL, 128), lambda c: (0, c, 0))]
        + zt_specs + [const(1, 128), const(8, 1), const(1, W)],
        out_specs=[pl.BlockSpec((BATCH, CL, W), lambda c: (0, c, 0)),
                   const(BATCH, NH, DH, DH), const(BATCH, NH, DH), const(BATCH, NH, 128)],
        out_shape=[jax.ShapeDtypeStruct((BATCH, SEQ, W), BF16),
                   jax.ShapeDtypeStruct((BATCH, NH, DH, DH), F32),
                   jax.ShapeDtypeStruct((BATCH, NH, DH), F32),
                   jax.ShapeDtypeStruct((BATCH, NH, 128), F32)],
        compiler_params=_cp(("arbitrary",)),
        name="mlstm_chunk",
    )(z_b, z_b, z_b, z_f, z_if, z_t, z_t, z_t, z_t, brow, bcol, gh)


MSR = 64


def _mlstm_step_kernel(q_ref, k_ref, v_ref, o_ref, zif_ref, brow_ref, gh_ref, c0_ref, n0_ref, m0_ref,
                       h_ref, c_ref, n_ref, m_ref):
    zi = zif_ref[...] + brow_ref[...]
    ig = zi[:, 0:NH]
    g_inter = _log_sigmoid(zi[:, NH:2 * NH]) + m0_ref[...]
    m_t = jnp.maximum(g_inter, ig)
    w_inter = jnp.exp(g_inter - m_t)
    w_intra = jnp.exp(ig - m_t) * (DH ** -0.5)
    floor = jnp.exp(-m_t)
    m_ref[...] = m_t

    def heads_on_rows(x):
        return jnp.pad(x, ((0, 128 - SB), (0, 128 - NH))).T

    w_inter_t = heads_on_rows(w_inter)
    w_intra_t = heads_on_rows(w_intra)
    floor_t = heads_on_rows(floor)
    pad = jnp.zeros((128 - SB, DH), F32)
    q_t = [jnp.concatenate([q_ref[:, hh * DH:(hh + 1) * DH], pad], axis=0).T for hh in range(NH)]
    k_t = [jnp.concatenate([k_ref[:, hh * DH:(hh + 1) * DH], pad], axis=0).T for hh in range(NH)]

    def per_head_rows(ref, s):
        return jnp.concatenate([ref[s:s + 1, hh * DH:(hh + 1) * DH] for hh in range(NH)], axis=0)

    for s in range(SB):
        q4 = per_head_rows(q_ref, s)
        k4 = per_head_rows(k_ref, s)
        v4 = per_head_rows(v_ref, s)
        qc_rows = []
        for hh in range(NH):
            wi = w_inter[s:s + 1, hh:hh + 1]
            vw = v4[hh:hh + 1, :] * w_intra[s:s + 1, hh:hh + 1]
            acc = jnp.zeros((MSR, DH), F32)
            for r0 in range(0, DH, MSR):
                c_blk = c0_ref[s, hh, r0:r0 + MSR, :]
                acc = acc + q_t[hh][r0:r0 + MSR, s:s + 1] * c_blk
                c_ref[s, hh, r0:r0 + MSR, :] = wi * c_blk + k_t[hh][r0:r0 + MSR, s:s + 1] * vw
            qc_rows.append(jnp.sum(acc, axis=0, keepdims=True))
        q_c = jnp.concatenate(qc_rows, axis=0)
        wi_c = w_inter_t[0:NH, s:s + 1]
        wa_c = w_intra_t[0:NH, s:s + 1]
        n_prev = n0_ref[s]
        sv = jnp.sum(q4 * k4, axis=-1, keepdims=True) * wa_c
        num = sv * v4 + wi_c * q_c
        nq = sv + wi_c * jnp.sum(q4 * n_prev, axis=-1, keepdims=True)
        h = num / jnp.maximum(jnp.abs(nq), floor_t[0:NH, s:s + 1])
        n_ref[s] = wi_c * n_prev + wa_c * k4
        out = _rms(h, gh_ref[...]) * _sigmoid(per_head_rows(o_ref, s))
        for hh in range(NH):
            h_ref[s:s + 1, hh * DH:(hh + 1) * DH] = out[hh:hh + 1, :]


def _mlstm_step(zs, z_f, z_if, brow, gh, c0, n0, m0):
    blk = lambda col: pl.BlockSpec((SB, W), lambda i: (i, col))
    return pl.pallas_call(
        _mlstm_step_kernel,
        grid=(NS // SB,),
        in_specs=[blk(0), blk(1), blk(2), blk(1),
                  pl.BlockSpec((SB, 128), lambda i: (i, 0)),
                  pl.BlockSpec((1, 128), lambda i: (0, 0)),
                  pl.BlockSpec((NH, DH), lambda i: (0, 0)),
                  pl.BlockSpec((SB, NH, DH, DH), lambda i: (i, 0, 0, 0)),
                  pl.BlockSpec((SB, NH, DH), lambda i: (i, 0, 0)),
                  pl.BlockSpec((SB, NH), lambda i: (i, 0))],
        out_specs=[pl.BlockSpec((SB, W), lambda i: (i, 0)),
                   pl.BlockSpec((SB, NH, DH, DH), lambda i: (i, 0, 0, 0)),
                   pl.BlockSpec((SB, NH, DH), lambda i: (i, 0, 0)),
                   pl.BlockSpec((SB, NH), lambda i: (i, 0))],
        out_shape=[jax.ShapeDtypeStruct((NS, W), F32),
                   jax.ShapeDtypeStruct((NS, NH, DH, DH), F32),
                   jax.ShapeDtypeStruct((NS, NH, DH), F32),
                   jax.ShapeDtypeStruct((NS, NH), F32)],
        compiler_params=_cp(("arbitrary",)),
        name="mlstm_step",
    )(zs, zs, zs, z_f, z_if, brow, gh, c0, n0, m0)


def _softmax_rows(s):
    e = jnp.exp(s - jnp.max(s, axis=-1, keepdims=True))
    return e / jnp.sum(e, axis=-1, keepdims=True)


def _xattn_prompt_kernel(q_ref, k_ref, v_ref, o_ref):
    for hh in range(NH):
        hs = slice(hh * DH, (hh + 1) * DH)
        s = _dot_nt(q_ref[:, hs], k_ref[:, hs].astype(BF16)) * (DH ** -0.5)
        p = _softmax_rows(s)
        o_ref[:, hs] = _dot(p.astype(BF16), v_ref[:, hs].astype(BF16)).astype(BF16)


def _xattn_prompt(z_b, kv):
    nt = SEQ // TM
    return pl.pallas_call(
        _xattn_prompt_kernel,
        grid=(BATCH, nt),
        in_specs=[pl.BlockSpec((TM, W), lambda b, t: (b * nt + t, 3)),
                  pl.BlockSpec((N_MEM, W), lambda b, t: (b, 0)),
                  pl.BlockSpec((N_MEM, W), lambda b, t: (b, 1))],
        out_specs=pl.BlockSpec((TM, W), lambda b, t: (b * nt + t, 0)),
        out_shape=jax.ShapeDtypeStruct((NP, W), BF16),
        compiler_params=_cp(("arbitrary", "arbitrary")),
        name="xattn_prompt",
    )(z_b, kv, kv)


XS = 4


XMC = 64


def _xattn_step_kernel(q_ref, k_ref, v_ref, o_ref):
    def part(t):
        for s in range(XS):
            r = t * XS + s
            q4 = jnp.concatenate([q_ref[r:r + 1, hh * DH:(hh + 1) * DH] for hh in range(NH)], axis=0)
            sc = jnp.concatenate(
                [jnp.sum(k_ref[s, m0:m0 + XMC] * q4[None], axis=-1, keepdims=True) for m0 in range(0, N_MEM, XMC)],
                axis=0) * (DH ** -0.5)
            e = jnp.exp(sc - jnp.max(sc, axis=0, keepdims=True))
            p = e / jnp.sum(e, axis=0, keepdims=True)
            acc = jnp.zeros((NH, DH), F32)
            for m0 in range(0, N_MEM, XMC):
                acc = acc + jnp.sum(p[m0:m0 + XMC] * v_ref[s, m0:m0 + XMC], axis=0)
            for hh in range(NH):
                o_ref[r:r + 1, hh * DH:(hh + 1) * DH] = acc[hh:hh + 1, :]

    for t in range(SB // XS):
        pl.when(pl.program_id(1) == t)(functools.partial(part, t))


def _xattn_step(zs, mem_k, mem_v):
    nt = SB // XS
    return pl.pallas_call(
        _xattn_step_kernel,
        grid=(NS // SB, nt),
        in_specs=[pl.BlockSpec((SB, W), lambda i, t: (i, 3)),
                  pl.BlockSpec((XS, N_MEM, NH, DH), lambda i, t: (i * nt + t, 0, 0, 0)),
                  pl.BlockSpec((XS, N_MEM, NH, DH), lambda i, t: (i * nt + t, 0, 0, 0))],
        out_specs=pl.BlockSpec((SB, W), lambda i, t: (i, 0)),
        out_shape=jax.ShapeDtypeStruct((NS, W), F32),
        compiler_params=_cp(("arbitrary", "arbitrary")),
        name="xattn_step",
    )(zs, mem_k, mem_v)


TMX = 512


def _mix_kernel(x_ref, g_ref, yp_ref, ys_ref, mp_ref, ms_ref, ap_ref, as_ref, wglu_ref,
                wg0_ref, wg1_ref, wg2_ref, wb0_ref, wb1_ref, wb2_ref, o_ref, h_scr, s5_scr):
    def tile(rows, y_ref, ml_ref, xa_ref):
        rs = slice(0, rows)

        @pl.when(pl.program_id(1) == 0)
        def _():
            h_scr[rs, :] = _rms(x_ref[rs, :], g_ref[...]).astype(BF16)
            y = _gelu_tanh(y_ref[...])
            s5_scr[rs, :] = (y * _sigmoid(_dot(y.astype(BF16), wglu_ref[...]))).astype(BF16)

        h = h_scr[rs, :]
        merged = (_sigmoid(_dot(h, wg0_ref[...])) * _dot(s5_scr[rs, :], wb0_ref[...])
                  + _sigmoid(_dot(h, wg1_ref[...])) * _dot(ml_ref[...].astype(BF16), wb1_ref[...])
                  + _sigmoid(_dot(h, wg2_ref[...])) * _dot(xa_ref[...].astype(BF16), wb2_ref[...]))
        o_ref[rs, :] = merged.astype(BF16)
        if rows < TM:
            o_ref[rows:, :] = jnp.zeros((TM - rows, TMX), BF16)

    pl.when(pl.program_id(0) < NPT)(functools.partial(tile, TM, yp_ref, mp_ref, ap_ref))
    pl.when(pl.program_id(0) == NPT)(functools.partial(tile, NS, ys_ref, ms_ref, as_ref))


def _mix(x1, g, yp, ys, mp, ms, ap, as_, wglu, wg, wb):
    prow = pl.BlockSpec((TM, W), lambda i, j: (jnp.minimum(i, NPT - 1), 0))
    srow = pl.BlockSpec((NS, W), lambda i, j: (0, 0))
    wgs = pl.BlockSpec((D, TMX), lambda i, j: (0, j))
    wbs = pl.BlockSpec((W, TMX), lambda i, j: (0, j))
    return pl.pallas_call(
        _mix_kernel,
        grid=(NT, D // TMX),
        in_specs=[pl.BlockSpec((TM, D), lambda i, j: (i, 0)), pl.BlockSpec((1, D), lambda i, j: (0, 0)),
                  prow, srow, prow, srow, prow, srow,
                  pl.BlockSpec((W, W), lambda i, j: (0, 0)),
                  wgs, wgs, wgs, wbs, wbs, wbs],
        out_specs=pl.BlockSpec((TM, TMX), lambda i, j: (i, j)),
        out_shape=jax.ShapeDtypeStruct((MROWS, D), BF16),
        scratch_shapes=[pltpu.VMEM((TM, D), BF16), pltpu.VMEM((TM, W), BF16)],
        compiler_params=_cp(("arbitrary", "arbitrary")),
        name="mix",
    )(x1, g, yp, ys, mp, ms, ap, as_, wglu, *wg, *wb)


def _outproj_kernel(x_ref, m_ref, w_ref, o_ref):
    def tile(rows):
        rs = slice(0, rows)
        o_ref[rs, :] = x_ref[rs, :] + _dot(m_ref[rs, :], w_ref[...])
        if rows < TM:
            o_ref[rows:, :] = jnp.zeros((TM - rows, D), F32)

    pl.when(pl.program_id(0) < NPT)(functools.partial(tile, TM))
    pl.when(pl.program_id(0) == NPT)(functools.partial(tile, NS))


def _outproj(x1, merged, w_out):
    return pl.pallas_call(
        _outproj_kernel,
        grid=(NT,),
        in_specs=[pl.BlockSpec((TM, D), lambda i: (i, 0)), pl.BlockSpec((TM, D), lambda i: (i, 0)),
                  pl.BlockSpec((D, D), lambda i: (0, 0))],
        out_specs=pl.BlockSpec((TM, D), lambda i: (i, 0)),
        out_shape=jax.ShapeDtypeStruct((MROWS, D), F32),
        compiler_params=_cp(("arbitrary",)),
        name="out_proj",
    )(x1, merged, w_out)


def kernel(x_prompt, x_sample, mem_prompt, cache_mem_k, cache_mem_v, state_s5_re, state_s5_im, state_mlstm_C,
           state_mlstm_n, state_mlstm_m, g_ffn1, w1_gate, w1_up, w1_down, g_mix, w_in, s5_lambda_re,
           s5_lambda_im, s5_log_step, s5_b_re, s5_b_im, s5_c_re, s5_c_im, s5_d, w_s5_glu, b_igate, b_fgate,
           g_mlstm_head, g_mem, w_mem_k, w_mem_v, w_br_s5, w_br_ml, w_br_xa, w_out, g_ffn2, w2_gate, w2_up,
           w2_down, g_final):
    bf = lambda a: a.astype(BF16)

    wi = w_in[0]
    q0, o0, i0, x0, g0 = W, 4 * W, 5 * W, 5 * W + 2 * NH, 6 * W + 2 * NH
    w_main = bf(jnp.concatenate([wi[:, :W], wi[:, o0:i0], wi[:, q0:o0], wi[:, x0:g0]], axis=1))
    w_gate_cols = wi[:, i0:x0]
    w_if = bf(jnp.pad(w_gate_cols, ((0, 0), (0, 128 - 2 * NH))))
    w_ift = bf(w_gate_cols.T)
    w_bg = [bf(wi[:, g0 + b * D:g0 + (b + 1) * D]) for b in range(3)]
    w_br = [bf(w_br_s5[0]), bf(w_br_ml[0]), bf(w_br_xa[0])]
    brow = jnp.pad(jnp.concatenate([b_igate[0], b_fgate[0]]), (0, 128 - 2 * NH)).reshape(1, 128)
    bcol = jnp.concatenate([b_igate[0], b_fgate[0]]).reshape(2 * NH, 1)
    gh = g_mlstm_head[0].reshape(1, W)

    x1 = _ffn((x_prompt.reshape(NP, D), x_sample.reshape(NS, D)), g_ffn1[0].reshape(1, D),
              bf(w1_gate[0]), bf(w1_up[0]), bf(w1_down[0]))

    z_f, z_b, z_if, z_t, zs_f, zs_b, zs_if = _in_proj(x1, g_mix[0].reshape(1, D), w_main, w_if, w_ift)
    zs = zs_b[:NS].astype(F32)
    per_seq = lambda a: a.reshape(BATCH, SEQ, a.shape[1])

    blk = lambda a: jnp.tile(a.reshape(NGB, WU, P), (1, 1, GB))
    row = lambda a: a.reshape(NGB, 1, WS)
    (a_re, a_im, a16_re, a16_im, bb_re, bb_im, e_op, ft_op, bd_op) = _s5_prep(
        row(s5_lambda_re[0]), row(s5_lambda_im[0]), row(jnp.repeat(s5_log_step[0], P)),
        blk(s5_b_re[0].transpose(0, 2, 1)), blk(s5_b_im[0].transpose(0, 2, 1)),
        blk(s5_c_re[0]), blk(s5_c_im[0]), s5_d[0].reshape(NGB, WU, 1))
    y_s5_p, fin = _s5_chunk(z_f, bd_op, e_op, ft_op, a16_re, a16_im)
    fin = fin.reshape(NGB, BATCH, 2, GB, P).transpose(2, 1, 0, 3, 4).reshape(2, 1, BATCH, G, P)
    y_s5_s, s5_re_s, s5_im_s = _s5_step(
        zs_f, state_s5_re[0].reshape(NS, G * P), state_s5_im[0].reshape(NS, G * P),
        a_re, a_im, bb_re, bb_im, blk(s5_c_re[0]), blk(s5_c_im[0]), s5_d[0].reshape(1, W))

    ml_p, c_p, n_p, m_p = _mlstm_chunk(per_seq(z_f), per_seq(z_b), per_seq(z_if), z_t, brow, bcol, gh)
    ml_s, c_s, n_s, m_s = _mlstm_step(zs, zs_f, zs_if, brow, gh.reshape(NH, DH), state_mlstm_C[0],
                                      state_mlstm_n[0], state_mlstm_m[0])

    w_kv = bf(jnp.concatenate([w_mem_k[0], w_mem_v[0]], axis=1))
    kv = _mem_proj(mem_prompt.reshape(BATCH * N_MEM, D), g_mem[0].reshape(1, D), w_kv)
    xa_p = _xattn_prompt(z_b, kv)
    xa_s = _xattn_step(zs, cache_mem_k[0], cache_mem_v[0])

    merged = _mix(x1, g_mix[0].reshape(1, D), y_s5_p, y_s5_s, ml_p.reshape(NP, W), ml_s, xa_p, xa_s,
                  bf(w_s5_glu[0]), w_bg, w_br)
    x2 = _outproj(x1, merged, bf(w_out[0]))
    y_p, y_s = _ffn((x2,), g_ffn2[0].reshape(1, D), bf(w2_gate[0]), bf(w2_up[0]), bf(w2_down[0]),
                    g_final.reshape(1, D))

    return (y_p.reshape(BATCH, SEQ, D), y_s.reshape(NS, 1, D),
            kv[:, :W].reshape(1, BATCH, N_MEM, NH, DH), kv[:, W:].reshape(1, BATCH, N_MEM, NH, DH),
            fin[0], fin[1], c_p[None], n_p[None], m_p[:, :, 0][None],
            s5_re_s.reshape(1, NS, G, P), s5_im_s.reshape(1, NS, G, P), c_s[None], n_s[None], m_s[None])
```

```python
import functools

import jax
import jax.numpy as jnp
from jax import lax
from jax.experimental import pallas as pl
from jax.experimental.pallas import tpu as pltpu

F32 = jnp.float32
BF16 = jnp.bfloat16

D = 2048
BATCH = 4
SEQ = 2048
NS = 128
NP = BATCH * SEQ
TM = 512
NPT = NP // TM
NT = NPT + 1
MROWS = NT * TM
N_MEM = 256
FF = 5504
TF = 512
G = 64
P = 64
HG = 16
TC = 16
GB = 8
NGB = G // GB
WU = GB * HG
WS = GB * P
NCH = SEQ // TC
S5B = 2
W = 1024
NH = 4
DH = 256
CL = 256
NCL = SEQ // CL
SB = 8
EPS = 1e-6
VMEM_LIMIT = 56 * 1024 * 1024


def _cp(sem, vmem=VMEM_LIMIT):
    return pltpu.CompilerParams(dimension_semantics=sem, vmem_limit_bytes=vmem)


def _dot(a, b):
    return jnp.dot(a, b, preferred_element_type=F32)


def _dot_nt(a, b):
    return lax.dot_general(a, b, (((1,), (1,)), ((), ())), preferred_element_type=F32)


def _dot_tn(a, b):
    return lax.dot_general(a, b, (((0,), (0,)), ((), ())), preferred_element_type=F32)


def _hi_lo(x):
    hi = x.astype(BF16)
    lo = (x - hi.astype(F32)).astype(BF16)
    return hi, lo


def _split3(x):
    hi = x.astype(BF16)
    r1 = x - hi.astype(F32)
    mid = r1.astype(BF16)
    lo = (r1 - mid.astype(F32)).astype(BF16)
    return hi, mid, lo


def _dot3(a, b):
    ah, al = _hi_lo(a)
    bh, bl = _hi_lo(b)
    return _dot(ah, bh) + (_dot(ah, bl) + _dot(al, bh))


def _dot3_nt(a, b):
    ah, al = _hi_lo(a)
    bh, bl = _hi_lo(b)
    return _dot_nt(ah, bh) + (_dot_nt(ah, bl) + _dot_nt(al, bh))


def _rms(x, g):
    r = lax.rsqrt(jnp.mean(x * x, axis=-1, keepdims=True) + EPS)
    return (x * r) * g


def _sigmoid(x):
    return 1.0 / (1.0 + jnp.exp(-x))


def _log_sigmoid(x):
    return jnp.minimum(x, 0.0) - jnp.log1p(jnp.exp(-jnp.abs(x)))


def _gelu_tanh(x):
    return x * (0.5 * (1.0 + jnp.tanh(0.7978845608028654 * (x + 0.044715 * (x * x * x)))))


NJ = pl.cdiv(FF, TF)


def _ffn_kernel(*refs, two_src, final_norm, cast_steps):
    refs = list(refs)
    if two_src:
        xp_ref, xs_ref = refs[:2]
        refs = refs[2:]
    else:
        xp_ref = xs_ref = refs[0]
        refs = refs[1:]
    g_ref, wg_ref, wu_ref, wd_ref = refs[:4]
    refs = refs[4:]
    nc = len(cast_steps)
    if final_norm:
        gf_ref = refs[0]
        refs = refs[1:]
    cast_in = refs[:nc]
    refs = refs[nc:]
    if final_norm:
        op_ref, os_ref = refs[:2]
        refs = refs[2:]
    else:
        o_ref = refs[0]
        refs = refs[1:]
    cast_out = refs[:nc]
    h_scr, acc_scr = refs[nc:]
    j = pl.program_id(1)

    step = pl.program_id(0) * NJ + j
    for (first, count), src_ref, dst_ref in zip(cast_steps, cast_in, cast_out):
        @pl.when(jnp.logical_and(step >= first, step < first + count))
        def _(src_ref=src_ref, dst_ref=dst_ref):
            dst_ref[...] = src_ref[...].astype(BF16)

    def tile(rows, x_ref):
        rs = slice(0, rows)

        @pl.when(j == 0)
        def _():
            h_scr[rs, :] = _rms(x_ref[rs, :], g_ref[...]).astype(BF16)
            acc_scr[rs, :] = jnp.zeros((rows, D), F32)

        def accumulate(ragged):
            h = h_scr[rs, :]
            gt = _dot(h, wg_ref[...])
            hid = (gt * _sigmoid(gt)) * _dot(h, wu_ref[...])
            wd = wd_ref[...]
            if ragged:
                valid = FF - (NJ - 1) * TF
                hid = jnp.where(lax.broadcasted_iota(jnp.int32, (rows, TF), 1) < valid, hid, 0.0)
                wd = jnp.where(lax.broadcasted_iota(jnp.int32, (TF, D), 0) < valid, wd, jnp.zeros_like(wd))
            acc_scr[rs, :] += _dot(hid.astype(BF16), wd)

        pl.when(j < NJ - 1)(functools.partial(accumulate, False))

        @pl.when(j == NJ - 1)
        def _():
            accumulate(FF % TF != 0)
            y = x_ref[rs, :] + 0.5 * acc_scr[rs, :]
            if final_norm:
                y = _rms(y, gf_ref[...])
                (op_ref if rows == TM else os_ref)[...] = y
            else:
                o_ref[rs, :] = y
                if rows < TM:
                    o_ref[rows:, :] = jnp.zeros((TM - rows, D), F32)

    pl.when(pl.program_id(0) < NPT)(functools.partial(tile, TM, xp_ref))
    pl.when(pl.program_id(0) == NPT)(functools.partial(tile, NS, xs_ref))


def _ffn(xs, g, wg, wu, wd, g_final=None, casts=()):
    two_src = len(xs) == 2
    final_norm = g_final is not None
    row = pl.BlockSpec((TM, D), lambda i, j: (i, 0))
    prow = pl.BlockSpec((TM, D), lambda i, j: (jnp.minimum(i, NPT - 1), 0))
    srow = pl.BlockSpec((NS, D), lambda i, j: (0, 0))
    vec = pl.BlockSpec((1, D), lambda i, j: (0, 0))
    in_specs = ([prow, srow] if two_src else [row]) + [
        vec,
        pl.BlockSpec((D, TF), lambda i, j: (0, j)),
        pl.BlockSpec((D, TF), lambda i, j: (0, j)),
        pl.BlockSpec((TF, D), lambda i, j: (j, 0)),
    ]
    args = list(xs) + [g, wg, wu, wd]
    if final_norm:
        in_specs.append(vec)
        args.append(g_final)
        out_shape = [jax.ShapeDtypeStruct((NP, D), F32), jax.ShapeDtypeStruct((NS, D), F32)]
        out_specs = [prow, srow]
    else:
        out_shape = [jax.ShapeDtypeStruct((MROWS, D), F32)]
        out_specs = [row]
    cast_steps = []
    first = 0
    for a, rb in casts:
        count = a.shape[0] // rb
        spec = pl.BlockSpec((rb, a.shape[1]),
                            functools.partial(lambda i, j, f, n: (jnp.clip(i * NJ + j - f, 0, n - 1), 0), f=first, n=count))
        in_specs.append(spec)
        args.append(a)
        out_specs.append(spec)
        out_shape.append(jax.ShapeDtypeStruct(a.shape, BF16))
        cast_steps.append((first, count))
        first += count
    assert first <= NPT * NJ, "side casts must fit under the full-size row tiles"
    return pl.pallas_call(
        functools.partial(_ffn_kernel, two_src=two_src, final_norm=final_norm, cast_steps=tuple(cast_steps)),
        grid=(NT, NJ),
        in_specs=in_specs,
        out_specs=out_specs,
        out_shape=out_shape,
        scratch_shapes=[pltpu.VMEM((TM, D), BF16), pltpu.VMEM((TM, D), F32)],
        compiler_params=_cp(("arbitrary", "arbitrary")),
        name="ffn_final" if final_norm else "ffn",
    )(*args)


NF32 = 2
TMI = 256
NPI = NP // TMI


def _in_proj_kernel(x_ref, g_ref, w_ref, wif_ref, wift_ref,
                    ofp_ref, obp_ref, zifp_ref, ztp_ref, ofs_ref, obs_ref, zifs_ref):
    def emit(of_ref, ob_ref, zif_ref, zt_ref):
        h = _rms(x_ref[...], g_ref[...]).astype(BF16)
        zif_ref[...] = _dot(h, wif_ref[...])
        if zt_ref is not None:
            zt_ref[...] = _dot_nt(wift_ref[...], h)
        for j in range(w_ref.shape[1] // W):
            res = _dot(h, w_ref[:, j * W:(j + 1) * W])
            if j < NF32:
                of_ref[:, j * W:(j + 1) * W] = res
            else:
                ob_ref[:, (j - NF32) * W:(j - NF32 + 1) * W] = res.astype(BF16)

    pl.when(pl.program_id(0) < NPI)(functools.partial(emit, ofp_ref, obp_ref, zifp_ref, ztp_ref))
    pl.when(pl.program_id(0) == NPI)(functools.partial(emit, ofs_ref, obs_ref, zifs_ref, None))


def _in_proj(x, g, w, w_if, w_ift):
    n = w.shape[1]
    nb = n // W
    once = pl.Buffered(1)
    pblk = lambda cols: pl.BlockSpec((TMI, cols), lambda i: (jnp.minimum(i, NPI - 1), 0))
    sblk = lambda cols: pl.BlockSpec((TMI, cols), lambda i: (0, 0))
    widths = (NF32 * W, (nb - NF32) * W, 128)
    dtypes = (F32, BF16, F32)
    return pl.pallas_call(
        _in_proj_kernel,
        grid=(NPI + 1,),
        in_specs=[
            pl.BlockSpec((TMI, D), lambda i: (i, 0)),
            pl.BlockSpec((1, D), lambda i: (0, 0)),
            pl.BlockSpec((D, n), lambda i: (0, 0), pipeline_mode=once),
            pl.BlockSpec((D, 128), lambda i: (0, 0), pipeline_mode=once),
            pl.BlockSpec((8, D), lambda i: (0, 0), pipeline_mode=once),
        ],
        out_specs=[pblk(c) for c in widths]
        + [pl.BlockSpec((8, TMI), lambda i: (0, jnp.minimum(i, NPI - 1)))]
        + [sblk(c) for c in widths],
        out_shape=[jax.ShapeDtypeStruct((NP, c), t) for c, t in zip(widths, dtypes)]
        + [jax.ShapeDtypeStruct((8, NP), F32)]
        + [jax.ShapeDtypeStruct((TMI, c), t) for c, t in zip(widths, dtypes)],
        compiler_params=_cp(("arbitrary",)),
        name="in_proj",
    )(x, g, w, w_if, w_ift)


def _mem_proj_kernel(x_ref, g_ref, w_ref, o_ref, h_scr):
    @pl.when(pl.program_id(1) == 0)
    def _():
        h_scr[...] = _rms(x_ref[...], g_ref[...]).astype(BF16)

    o_ref[...] = _dot(h_scr[...], w_ref[...])


def _mem_proj(x, g, w):
    m, n = x.shape[0], w.shape[1]
    return pl.pallas_call(
        _mem_proj_kernel,
        grid=(m // TM, n // W),
        in_specs=[pl.BlockSpec((TM, D), lambda i, j: (i, 0)), pl.BlockSpec((1, D), lambda i, j: (0, 0)),
                  pl.BlockSpec((D, W), lambda i, j: (0, j))],
        out_specs=pl.BlockSpec((TM, W), lambda i, j: (i, j)),
        out_shape=jax.ShapeDtypeStruct((m, n), F32),
        scratch_shapes=[pltpu.VMEM((TM, D), BF16)],
        compiler_params=_cp(("arbitrary", "arbitrary")),
        name="mem_proj",
    )(x, g, w)


def _s5_prep_kernel(lr_ref, li_ref, ls_ref, btr_ref, bti_ref, cr_ref, ci_ref, d_ref,
                    ar_ref, ai_ref, a16r_ref, a16i_ref, bbr_ref, bbi_ref, e_ref, ft_ref, bd_ref):
    lr = lr_ref[0]
    li = li_ref[0]
    dt = jnp.exp(ls_ref[0])

    def power(k):
        mag = jnp.exp(lr * dt * float(k))
        ang = li * dt * float(k)
        return mag * jnp.cos(ang), mag * jnp.sin(ang)

    pw = [power(k) for k in range(TC + 1)]
    ar, ai = pw[1]
    den = lr * lr + li * li
    nr = ar - 1.0
    z_re = (nr * lr + ai * li) / den
    z_im = (ai * lr - nr * li) / den
    mask = (lax.broadcasted_iota(jnp.int32, (WU, WS), 0) // HG
            == lax.broadcasted_iota(jnp.int32, (WU, WS), 1) // P)
    btr = btr_ref[0]
    bti = bti_ref[0]
    bbr = jnp.where(mask, z_re * btr - z_im * bti, 0.0)
    bbi = jnp.where(mask, z_re * bti + z_im * btr, 0.0)
    cr = jnp.where(mask, cr_ref[0], 0.0)
    ci = jnp.where(mask, ci_ref[0], 0.0)

    ar_ref[0] = ar
    ai_ref[0] = ai
    a16r_ref[0] = pw[TC][0]
    a16i_ref[0] = pw[TC][1]
    bbr_ref[0] = bbr
    bbi_ref[0] = bbi

    def cmul(xr, xi, k):
        pr, pi = pw[k]
        return xr * pr - xi * pi, xr * pi + xi * pr

    diag = (lax.broadcasted_iota(jnp.int32, (WU, WU), 0) == lax.broadcasted_iota(jnp.int32, (WU, WU), 1))
    for s in range(TC):
        er, ei = cmul(bbr, bbi, TC - 1 - s)
        e_ref[0, s * WU:(s + 1) * WU, 0:WS] = er.astype(BF16)
        e_ref[0, s * WU:(s + 1) * WU, WS:2 * WS] = ei.astype(BF16)
        fr, fi = cmul(cr, ci, s + 1)
        ft_ref[0, s * WU:(s + 1) * WU, 0:WS] = fr.astype(BF16)
        ft_ref[0, s * WU:(s + 1) * WU, WS:2 * WS] = (-fi).astype(BF16)
        rr, ri = cmul(cr, ci, s)
        kern = _dot3_nt(bbr, rr) - _dot3_nt(bbi, ri)
        if s == 0:
            kern = kern + jnp.where(diag, d_ref[0], 0.0)
        bd_ref[0, s] = kern


def _s5_prep(lam_re, lam_im, log_step, bt_re, bt_im, c_re, c_im, d):
    def spec(*shape):
        nd = len(shape)
        return pl.BlockSpec((1,) + shape, lambda i: (i,) + (0,) * nd)

    def sds(*shape, dtype=F32):
        return jax.ShapeDtypeStruct((NGB,) + shape, dtype)

    return pl.pallas_call(
        _s5_prep_kernel,
        grid=(NGB,),
        in_specs=[spec(1, WS)] * 3 + [spec(WU, WS)] * 4 + [spec(WU, 1)],
        out_specs=[spec(1, WS)] * 4 + [spec(WU, WS)] * 2 + [spec(TC * WU, 2 * WS)] * 2 + [spec(TC, WU, WU)],
        out_shape=[sds(1, WS)] * 4 + [sds(WU, WS)] * 2 + [sds(TC * WU, 2 * WS, dtype=BF16)] * 2
        + [sds(TC, WU, WU)],
        compiler_params=_cp(("arbitrary",)),
        name="s5_prep",
    )(lam_re, lam_im, log_step, bt_re, bt_im, c_re, c_im, d)


def _s5_chunk_kernel(u_ref, bd_ref, e_ref, ft_ref, ar_ref, ai_ref, y_ref, fin_ref,
                     w_scr, lhs_scr, s_scr, xs_scr):
    rows = S5B * NCH

    @pl.when(pl.program_id(1) == 0)
    def _():
        w_scr[...] = jnp.zeros_like(w_scr)
        bd = [bd_ref[0, k].astype(BF16) for k in range(TC)]
        for s in range(TC):
            for t in range(s, TC):
                w_scr[s * WU:(s + 1) * WU, t * WU:(t + 1) * WU] = bd[t - s]

    for s in range(TC):
        lhs_scr[:, s * WU:(s + 1) * WU] = u_ref[pl.ds(s, rows, stride=TC), :].astype(BF16)
    lhs = lhs_scr[...]
    s_loc = _dot(lhs, e_ref[0])
    nl = WS // 128
    for k in range(2 * nl):
        s_scr[k] = s_loc[:, k * 128:(k + 1) * 128]
    ar = [ar_ref[0, :, k * 128:(k + 1) * 128] for k in range(nl)]
    ai = [ai_ref[0, :, k * 128:(k + 1) * 128] for k in range(nl)]
    xr = [jnp.zeros((S5B, 128), F32)] * nl
    xi = [jnp.zeros((S5B, 128), F32)] * nl
    for c in range(NCH):
        chunk_rows = pl.ds(c, S5B, stride=NCH)
        for k in range(nl):
            xs_scr[k, chunk_rows, :] = xr[k]
            xs_scr[nl + k, chunk_rows, :] = xi[k]
            sr = s_scr[k, chunk_rows, :]
            si = s_scr[nl + k, chunk_rows, :]
            xr[k], xi[k] = ar[k] * xr[k] - ai[k] * xi[k] + sr, ar[k] * xi[k] + ai[k] * xr[k] + si
    for k in range(nl):
        fin_ref[0, 0, :, k * 128:(k + 1) * 128] = xr[k]
        fin_ref[0, 0, :, WS + k * 128:WS + (k + 1) * 128] = xi[k]
    xs = jnp.concatenate([xs_scr[k] for k in range(2 * nl)], axis=1)
    y = _dot(lhs, w_scr[...]) + _dot_nt(xs.astype(BF16), ft_ref[0])
    for t in range(TC):
        y_ref[pl.ds(t, rows, stride=TC), :] = y[:, t * WU:(t + 1) * WU]


def _s5_chunk(z_f, bd, e, ft, a16r, a16i):
    rows = S5B * NCH
    nh = BATCH // S5B
    return pl.pallas_call(
        _s5_chunk_kernel,
        grid=(NGB, nh),
        in_specs=[
            pl.BlockSpec((S5B * SEQ, WU), lambda j, b: (b, j)),
            pl.BlockSpec((1, TC, WU, WU), lambda j, b: (j, 0, 0, 0)),
            pl.BlockSpec((1, TC * WU, 2 * WS), lambda j, b: (j, 0, 0)),
            pl.BlockSpec((1, TC * WU, 2 * WS), lambda j, b: (j, 0, 0)),
            pl.BlockSpec((1, 1, WS), lambda j, b: (j, 0, 0)),
            pl.BlockSpec((1, 1, WS), lambda j, b: (j, 0, 0)),
        ],
        out_specs=[
            pl.BlockSpec((S5B * SEQ, WU), lambda j, b: (b, j)),
            pl.BlockSpec((1, 1, S5B, 2 * WS), lambda j, b: (j, b, 0, 0)),
        ],
        out_shape=[jax.ShapeDtypeStruct((NP, W), F32), jax.ShapeDtypeStruct((NGB, nh, S5B, 2 * WS), F32)],
        scratch_shapes=[pltpu.VMEM((TC * WU, TC * WU), BF16), pltpu.VMEM((rows, TC * WU), BF16),
                        pltpu.VMEM((2 * WS // 128, rows, 128), F32), pltpu.VMEM((2 * WS // 128, rows, 128), F32)],
        compiler_params=_cp(("arbitrary", "arbitrary")),
        name="s5_chunk",
    )(z_f, bd, e, ft, a16r, a16i)


def _s5_step_kernel(u_ref, sr_ref, si_ref, ar_ref, ai_ref, bbr_ref, bbi_ref, cr_ref, ci_ref, d_ref,
                    y_ref, xr_ref, xi_ref):
    mask = (lax.broadcasted_iota(jnp.int32, (WU, WS), 0) // HG
            == lax.broadcasted_iota(jnp.int32, (WU, WS), 1) // P)
    for j in range(NGB):
        ul = slice(j * WU, (j + 1) * WU)
        sl = slice(j * WS, (j + 1) * WS)
        us = u_ref[:, ul]
        ar = ar_ref[j]
        ai = ai_ref[j]
        s_re = sr_ref[:, sl]
        s_im = si_ref[:, sl]
        x_re = ar * s_re - ai * s_im + _dot3(us, bbr_ref[j])
        x_im = ar * s_im + ai * s_re + _dot3(us, bbi_ref[j])
        xr_ref[:, sl] = x_re
        xi_ref[:, sl] = x_im
        cre = jnp.where(mask, cr_ref[j], 0.0)
        cim = jnp.where(mask, ci_ref[j], 0.0)
        y_ref[:, ul] = _dot3_nt(x_re, cre) - _dot3_nt(x_im, cim) + d_ref[:, ul] * us


def _s5_step(z_f, s_re, s_im, a_re, a_im, bb_re, bb_im, c_re, c_im, d_row):
    full = lambda *shape: pl.BlockSpec(shape, lambda i: (0,) * len(shape))
    return pl.pallas_call(
        _s5_step_kernel,
        grid=(1,),
        in_specs=[pl.BlockSpec((NS, W), lambda i: (0, 0)),
                  full(NS, G * P), full(NS, G * P), full(NGB, 1, WS), full(NGB, 1, WS),
                  full(NGB, WU, WS), full(NGB, WU, WS), full(NGB, WU, WS), full(NGB, WU, WS),
                  full(1, W)],
        out_specs=[full(NS, W), full(NS, G * P), full(NS, G * P)],
        out_shape=[jax.ShapeDtypeStruct((NS, W), F32), jax.ShapeDtypeStruct((NS, G * P), F32),
                   jax.ShapeDtypeStruct((NS, G * P), F32)],
        compiler_params=_cp(("arbitrary",)),
        name="s5_step",
    )(z_f, s_re, s_im, a_re, a_im, bb_re, bb_im, c_re, c_im, d_row)


def _mlstm_chunk_kernel(q_ref, k_ref, v_ref, o_ref, zif_ref, zt0_ref, zt1_ref, zt2_ref, zt3_ref,
                        brow_ref, bcol_ref, gh_ref, h_ref, c_ref, n_ref, m_ref):
    @pl.when(pl.program_id(0) == 0)
    def _():
        c_ref[...] = jnp.zeros_like(c_ref)
        n_ref[...] = jnp.zeros_like(n_ref)
        m_ref[...] = jnp.zeros_like(m_ref)

    rr = lax.broadcasted_iota(jnp.int32, (CL, CL), 0)
    cc = lax.broadcasted_iota(jnp.int32, (CL, CL), 1)
    causal = cc <= rr
    tril = jnp.where(causal, 1.0, 0.0).astype(BF16)
    triu = jnp.where(rr <= cc, 1.0, 0.0).astype(BF16)

    for b, zt_ref in enumerate((zt0_ref, zt1_ref, zt2_ref, zt3_ref)):
        zi = zif_ref[b] + brow_ref[...]
        zt = zt_ref[...] + bcol_ref[...]
        lfc = _split3(_log_sigmoid(zi))
        bcum_col = _dot(tril, lfc[0]) + (_dot(tril, lfc[1]) + _dot(tril, lfc[2]))
        lfr = _split3(_log_sigmoid(zt))
        bcum_row = _dot(lfr[0], triu) + (_dot(lfr[1], triu) + _dot(lfr[2], triu))

        for hh in range(NH):
            hs = slice(hh * DH, (hh + 1) * DH)
            bc = bcum_col[:, NH + hh:NH + hh + 1]
            ic = zi[:, hh:hh + 1]
            br = bcum_row[NH + hh:NH + hh + 1, :]
            ir = zt[hh:hh + 1, :]
            m_prev = m_ref[b, hh:hh + 1, 0:1]
            g_inter = bc + m_prev
            dlog = jnp.where(causal, (bc - br) + ir, -jnp.inf)
            m_t = jnp.maximum(g_inter, jnp.max(dlog, axis=-1, keepdims=True))
            w_inter = jnp.exp(g_inter - m_t)
            w_intra = jnp.exp(dlog - m_t)
            qb = q_ref[b, :, hs]
            kb = k_ref[b, :, hs]
            vb = v_ref[b, :, hs]
            qf = qb.astype(F32)
            kf = kb.astype(F32)
            s = _dot_nt(qb, kb) * (w_intra * (DH ** -0.5))
            c_prev = c_ref[b, hh]
            n_prev = n_ref[b, hh:hh + 1, :]
            num = _dot(s.astype(BF16), vb) + w_inter * _dot(qb, c_prev.astype(BF16))
            nq = jnp.sum(s, axis=-1, keepdims=True) + w_inter * jnp.sum(qf * n_prev, axis=-1, keepdims=True)
            h = num / jnp.maximum(jnp.abs(nq), jnp.exp(-m_t))
            m_last = m_t[CL - 1:CL, :]
            w_last = jnp.exp((bc[CL - 1:CL, :] - bc) + ic - m_last) * (DH ** -0.5)
            wi_last = w_inter[CL - 1:CL, :]
            kw = kf * w_last
            c_ref[b, hh] = wi_last * c_prev + _dot_tn(kw.astype(BF16), vb)
            n_ref[b, hh:hh + 1, :] = wi_last * n_prev + jnp.sum(kw, axis=0, keepdims=True)
            m_ref[b, hh:hh + 1, :] = jnp.broadcast_to(m_last, (1, 128))
            hn = _rms(h, gh_ref[:, hs])
            h_ref[b, :, hs] = (hn * _sigmoid(o_ref[b, :, hs])).astype(BF16)


def _mlstm_chunk(z_f, z_b, z_if, z_t, brow, bcol, gh):
    blk = lambda col: pl.BlockSpec((BATCH, CL, W), lambda c: (0, c, col))
    const = lambda *shape: pl.BlockSpec(shape, lambda c: (0,) * len(shape))
    zt_specs = [pl.BlockSpec((8, CL), functools.partial(lambda c, b: (0, b * NCL + c), b=b)) for b in range(BATCH)]
    return pl.pallas_call(
        _mlstm_chunk_kernel,
        grid=(NCL,),
        in_specs=[blk(0), blk(1), blk(2), blk(1), pl.BlockSpec((BATCH, CL, 128), lambda c: (0, c, 0))]
        + zt_specs + [const(1, 128), const(8, 1), const(1, W)],
        out_specs=[pl.BlockSpec((BATCH, CL, W), lambda c: (0, c, 0)),
                   const(BATCH, NH, DH, DH), const(BATCH, NH, DH), const(BATCH, NH, 128)],
        out_shape=[jax.ShapeDtypeStruct((BATCH, SEQ, W), BF16),
                   jax.ShapeDtypeStruct((BATCH, NH, DH, DH), F32),
                   jax.ShapeDtypeStruct((BATCH, NH, DH), F32),
                   jax.ShapeDtypeStruct((BATCH, NH, 128), F32)],
        compiler_params=_cp(("arbitrary",)),
        name="mlstm_chunk",
    )(z_b, z_b, z_b, z_f, z_if, z_t, z_t, z_t, z_t, brow, bcol, gh)


MSR = 64


def _mlstm_step_kernel(q_ref, k_ref, v_ref, o_ref, zif_ref, brow_ref, gh_ref, c0_ref, n0_ref, m0_ref,
                       h_ref, c_ref, n_ref, m_ref):
    zi = zif_ref[...] + brow_ref[...]
    ig = zi[:, 0:NH]
    g_inter = _log_sigmoid(zi[:, NH:2 * NH]) + m0_ref[...]
    m_t = jnp.maximum(g_inter, ig)
    w_inter = jnp.exp(g_inter - m_t)
    w_intra = jnp.exp(ig - m_t) * (DH ** -0.5)
    floor = jnp.exp(-m_t)
    m_ref[...] = m_t

    def heads_on_rows(x):
        return jnp.pad(x, ((0, 128 - SB), (0, 128 - NH))).T

    w_inter_t = heads_on_rows(w_inter)
    w_intra_t = heads_on_rows(w_intra)
    floor_t = heads_on_rows(floor)
    pad = jnp.zeros((128 - SB, DH), F32)
    q_t = [jnp.concatenate([q_ref[:, hh * DH:(hh + 1) * DH], pad], axis=0).T for hh in range(NH)]
    k_t = [jnp.concatenate([k_ref[:, hh * DH:(hh + 1) * DH], pad], axis=0).T for hh in range(NH)]

    def per_head_rows(ref, s):
        return jnp.concatenate([ref[s:s + 1, hh * DH:(hh + 1) * DH] for hh in range(NH)], axis=0)

    for s in range(SB):
        q4 = per_head_rows(q_ref, s)
        k4 = per_head_rows(k_ref, s)
        v4 = per_head_rows(v_ref, s)
        qc_rows = []
        for hh in range(NH):
            wi = w_inter[s:s + 1, hh:hh + 1]
            vw = v4[hh:hh + 1, :] * w_intra[s:s + 1, hh:hh + 1]
            acc = jnp.zeros((MSR, DH), F32)
            for r0 in range(0, DH, MSR):
                c_blk = c0_ref[s, hh, r0:r0 + MSR, :]
                acc = acc + q_t[hh][r0:r0 + MSR, s:s + 1] * c_blk
                c_ref[s, hh, r0:r0 + MSR, :] = wi * c_blk + k_t[hh][r0:r0 + MSR, s:s + 1] * vw
            qc_rows.append(jnp.sum(acc, axis=0, keepdims=True))
        q_c = jnp.concatenate(qc_rows, axis=0)
        wi_c = w_inter_t[0:NH, s:s + 1]
        wa_c = w_intra_t[0:NH, s:s + 1]
        n_prev = n0_ref[s]
        sv = jnp.sum(q4 * k4, axis=-1, keepdims=True) * wa_c
        num = sv * v4 + wi_c * q_c
        nq = sv + wi_c * jnp.sum(q4 * n_prev, axis=-1, keepdims=True)
        h = num / jnp.maximum(jnp.abs(nq), floor_t[0:NH, s:s + 1])
        n_ref[s] = wi_c * n_prev + wa_c * k4
        out = _rms(h, gh_ref[...]) * _sigmoid(per_head_rows(o_ref, s))
        for hh in range(NH):
            h_ref[s:s + 1, hh * DH:(hh + 1) * DH] = out[hh:hh + 1, :]


def _mlstm_step(zs, z_f, z_if, brow, gh, c0, n0, m0):
    blk = lambda col: pl.BlockSpec((SB, W), lambda i: (i, col))
    return pl.pallas_call(
        _mlstm_step_kernel,
        grid=(NS // SB,),
        in_specs=[blk(0), blk(1), blk(2), blk(1),
                  pl.BlockSpec((SB, 128), lambda i: (i, 0)),
                  pl.BlockSpec((1, 128), lambda i: (0, 0)),
                  pl.BlockSpec((NH, DH), lambda i: (0, 0)),
                  pl.BlockSpec((SB, NH, DH, DH), lambda i: (i, 0, 0, 0)),
                  pl.BlockSpec((SB, NH, DH), lambda i: (i, 0, 0)),
                  pl.BlockSpec((SB, NH), lambda i: (i, 0))],
        out_specs=[pl.BlockSpec((SB, W), lambda i: (i, 0)),
                   pl.BlockSpec((SB, NH, DH, DH), lambda i: (i, 0, 0, 0)),
                   pl.BlockSpec((SB, NH, DH), lambda i: (i, 0, 0)),
                   pl.BlockSpec((SB, NH), lambda i: (i, 0))],
        out_shape=[jax.ShapeDtypeStruct((NS, W), F32),
                   jax.ShapeDtypeStruct((NS, NH, DH, DH), F32),
                   jax.ShapeDtypeStruct((NS, NH, DH), F32),
                   jax.ShapeDtypeStruct((NS, NH), F32)],
        compiler_params=_cp(("arbitrary",)),
        name="mlstm_step",
    )(zs, zs, zs, z_f, z_if, brow, gh, c0, n0, m0)


def _softmax_rows(s):
    e = jnp.exp(s - jnp.max(s, axis=-1, keepdims=True))
    return e / jnp.sum(e, axis=-1, keepdims=True)


def _xattn_prompt_kernel(q_ref, k_ref, v_ref, o_ref):
    for hh in range(NH):
        hs = slice(hh * DH, (hh + 1) * DH)
        s = _dot_nt(q_ref[:, hs], k_ref[:, hs].astype(BF16)) * (DH ** -0.5)
        p = _softmax_rows(s)
        o_ref[:, hs] = _dot(p.astype(BF16), v_ref[:, hs].astype(BF16)).astype(BF16)


def _xattn_prompt(z_b, kv):
    nt = SEQ // TM
    return pl.pallas_call(
        _xattn_prompt_kernel,
        grid=(BATCH, nt),
        in_specs=[pl.BlockSpec((TM, W), lambda b, t: (b * nt + t, 3)),
                  pl.BlockSpec((N_MEM, W), lambda b, t: (b, 0)),
                  pl.BlockSpec((N_MEM, W), lambda b, t: (b, 1))],
        out_specs=pl.BlockSpec((TM, W), lambda b, t: (b * nt + t, 0)),
        out_shape=jax.ShapeDtypeStruct((NP, W), BF16),
        compiler_params=_cp(("arbitrary", "arbitrary")),
        name="xattn_prompt",
    )(z_b, kv, kv)


XS = 4


XMC = 64


def _xattn_step_kernel(q_ref, k_ref, v_ref, o_ref):
    def part(t):
        for s in range(XS):
            r = t * XS + s
            q4 = jnp.concatenate([q_ref[r:r + 1, hh * DH:(hh + 1) * DH] for hh in range(NH)], axis=0)
            sc = jnp.concatenate(
                [jnp.sum(k_ref[s, m0:m0 + XMC] * q4[None], axis=-1, keepdims=True) for m0 in range(0, N_MEM, XMC)],
                axis=0) * (DH ** -0.5)
            e = jnp.exp(sc - jnp.max(sc, axis=0, keepdims=True))
            p = e / jnp.sum(e, axis=0, keepdims=True)
            acc = jnp.zeros((NH, DH), F32)
            for m0 in range(0, N_MEM, XMC):
                acc = acc + jnp.sum(p[m0:m0 + XMC] * v_ref[s, m0:m0 + XMC], axis=0)
            for hh in range(NH):
                o_ref[r:r + 1, hh * DH:(hh + 1) * DH] = acc[hh:hh + 1, :]

    for t in range(SB // XS):
        pl.when(pl.program_id(1) == t)(functools.partial(part, t))


def _xattn_step(zs, mem_k, mem_v):
    nt = SB // XS
    return pl.pallas_call(
        _xattn_step_kernel,
        grid=(NS // SB, nt),
        in_specs=[pl.BlockSpec((SB, W), lambda i, t: (i, 3)),
                  pl.BlockSpec((XS, N_MEM, NH, DH), lambda i, t: (i * nt + t, 0, 0, 0)),
                  pl.BlockSpec((XS, N_MEM, NH, DH), lambda i, t: (i * nt + t, 0, 0, 0))],
        out_specs=pl.BlockSpec((SB, W), lambda i, t: (i, 0)),
        out_shape=jax.ShapeDtypeStruct((NS, W), F32),
        compiler_params=_cp(("arbitrary", "arbitrary")),
        name="xattn_step",
    )(zs, mem_k, mem_v)


TMX = 512


def _mix_kernel(x_ref, g_ref, yp_ref, ys_ref, mp_ref, ms_ref, ap_ref, as_ref, wglu_ref,
                wg0_ref, wg1_ref, wg2_ref, wb0_ref, wb1_ref, wb2_ref, o_ref, h_scr, s5_scr):
    def tile(rows, y_ref, ml_ref, xa_ref):
        rs = slice(0, rows)

        @pl.when(pl.program_id(1) == 0)
        def _():
            h_scr[rs, :] = _rms(x_ref[rs, :], g_ref[...]).astype(BF16)
            y = _gelu_tanh(y_ref[...])
            s5_scr[rs, :] = (y * _sigmoid(_dot(y.astype(BF16), wglu_ref[...]))).astype(BF16)

        h = h_scr[rs, :]
        merged = (_sigmoid(_dot(h, wg0_ref[...])) * _dot(s5_scr[rs, :], wb0_ref[...])
                  + _sigmoid(_dot(h, wg1_ref[...])) * _dot(ml_ref[...].astype(BF16), wb1_ref[...])
                  + _sigmoid(_dot(h, wg2_ref[...])) * _dot(xa_ref[...].astype(BF16), wb2_ref[...]))
        o_ref[rs, :] = merged.astype(BF16)
        if rows < TM:
            o_ref[rows:, :] = jnp.zeros((TM - rows, TMX), BF16)

    pl.when(pl.program_id(0) < NPT)(functools.partial(tile, TM, yp_ref, mp_ref, ap_ref))
    pl.when(pl.program_id(0) == NPT)(functools.partial(tile, NS, ys_ref, ms_ref, as_ref))


def _mix(x1, g, yp, ys, mp, ms, ap, as_, wglu, wg, wb):
    prow = pl.BlockSpec((TM, W), lambda i, j: (jnp.minimum(i, NPT - 1), 0))
    srow = pl.BlockSpec((NS, W), lambda i, j: (0, 0))
    wgs = pl.BlockSpec((D, TMX), lambda i, j: (0, j))
    wbs = pl.BlockSpec((W, TMX), lambda i, j: (0, j))
    return pl.pallas_call(
        _mix_kernel,
        grid=(NT, D // TMX),
        in_specs=[pl.BlockSpec((TM, D), lambda i, j: (i, 0)), pl.BlockSpec((1, D), lambda i, j: (0, 0)),
                  prow, srow, prow, srow, prow, srow,
                  pl.BlockSpec((W, W), lambda i, j: (0, 0)),
                  wgs, wgs, wgs, wbs, wbs, wbs],
        out_specs=pl.BlockSpec((TM, TMX), lambda i, j: (i, j)),
        out_shape=jax.ShapeDtypeStruct((MROWS, D), BF16),
        scratch_shapes=[pltpu.VMEM((TM, D), BF16), pltpu.VMEM((TM, W), BF16)],
        compiler_params=_cp(("arbitrary", "arbitrary")),
        name="mix",
    )(x1, g, yp, ys, mp, ms, ap, as_, wglu, *wg, *wb)


def _outproj_kernel(x_ref, m_ref, w_ref, o_ref):
    def tile(rows):
        rs = slice(0, rows)
        o_ref[rs, :] = x_ref[rs, :] + _dot(m_ref[rs, :], w_ref[...])
        if rows < TM:
            o_ref[rows:, :] = jnp.zeros((TM - rows, D), F32)

    pl.when(pl.program_id(0) < NPT)(functools.partial(tile, TM))
    pl.when(pl.program_id(0) == NPT)(functools.partial(tile, NS))


def _outproj(x1, merged, w_out):
    return pl.pallas_call(
        _outproj_kernel,
        grid=(NT,),
        in_specs=[pl.BlockSpec((TM, D), lambda i: (i, 0)), pl.BlockSpec((TM, D), lambda i: (i, 0)),
                  pl.BlockSpec((D, D), lambda i: (0, 0))],
        out_specs=pl.BlockSpec((TM, D), lambda i: (i, 0)),
        out_shape=jax.ShapeDtypeStruct((MROWS, D), F32),
        compiler_params=_cp(("arbitrary",)),
        name="out_proj",
    )(x1, merged, w_out)


def kernel(x_prompt, x_sample, mem_prompt, cache_mem_k, cache_mem_v, state_s5_re, state_s5_im, state_mlstm_C,
           state_mlstm_n, state_mlstm_m, g_ffn1, w1_gate, w1_up, w1_down, g_mix, w_in, s5_lambda_re,
           s5_lambda_im, s5_log_step, s5_b_re, s5_b_im, s5_c_re, s5_c_im, s5_d, w_s5_glu, b_igate, b_fgate,
           g_mlstm_head, g_mem, w_mem_k, w_mem_v, w_br_s5, w_br_ml, w_br_xa, w_out, g_ffn2, w2_gate, w2_up,
           w2_down, g_final):
    bf = lambda a: a.astype(BF16)

    x1, w2g, w2u, w2d, wi = _ffn(
        (x_prompt.reshape(NP, D), x_sample.reshape(NS, D)), g_ffn1[0].reshape(1, D),
        bf(w1_gate[0]), bf(w1_up[0]), bf(w1_down[0]),
        casts=((w2_gate[0], 64), (w2_up[0], 64), (w2_down[0], 128), (w_in[0], 32)))

    q0, o0, i0, x0, g0 = W, 4 * W, 5 * W, 5 * W + 2 * NH, 6 * W + 2 * NH
    w_main = jnp.concatenate([wi[:, :W], wi[:, o0:i0], wi[:, q0:o0], wi[:, x0:g0]], axis=1)
    w_gate_cols = wi[:, i0:x0]
    w_if = jnp.pad(w_gate_cols, ((0, 0), (0, 128 - 2 * NH)))
    w_ift = w_gate_cols.T
    w_bg = [wi[:, g0 + b * D:g0 + (b + 1) * D] for b in range(3)]
    w_br = [bf(w_br_s5[0]), bf(w_br_ml[0]), bf(w_br_xa[0])]
    brow = jnp.pad(jnp.concatenate([b_igate[0], b_fgate[0]]), (0, 128 - 2 * NH)).reshape(1, 128)
    bcol = jnp.concatenate([b_igate[0], b_fgate[0]]).reshape(2 * NH, 1)
    gh = g_mlstm_head[0].reshape(1, W)

    z_f, z_b, z_if, z_t, zs_f, zs_b, zs_if = _in_proj(x1, g_mix[0].reshape(1, D), w_main, w_if, w_ift)
    zs = zs_b[:NS].astype(F32)
    per_seq = lambda a: a.reshape(BATCH, SEQ, a.shape[1])

    blk = lambda a: jnp.tile(a.reshape(NGB, WU, P), (1, 1, GB))
    row = lambda a: a.reshape(NGB, 1, WS)
    (a_re, a_im, a16_re, a16_im, bb_re, bb_im, e_op, ft_op, bd_op) = _s5_prep(
        row(s5_lambda_re[0]), row(s5_lambda_im[0]), row(jnp.repeat(s5_log_step[0], P)),
        blk(s5_b_re[0].transpose(0, 2, 1)), blk(s5_b_im[0].transpose(0, 2, 1)),
        blk(s5_c_re[0]), blk(s5_c_im[0]), s5_d[0].reshape(NGB, WU, 1))
    y_s5_p, fin = _s5_chunk(z_f, bd_op, e_op, ft_op, a16_re, a16_im)
    fin = fin.reshape(NGB, BATCH, 2, GB, P).transpose(2, 1, 0, 3, 4).reshape(2, 1, BATCH, G, P)
    y_s5_s, s5_re_s, s5_im_s = _s5_step(
        zs_f, state_s5_re[0].reshape(NS, G * P), state_s5_im[0].reshape(NS, G * P),
        a_re, a_im, bb_re, bb_im, blk(s5_c_re[0]), blk(s5_c_im[0]), s5_d[0].reshape(1, W))

    ml_p, c_p, n_p, m_p = _mlstm_chunk(per_seq(z_f), per_seq(z_b), per_seq(z_if), z_t, brow, bcol, gh)
    ml_s, c_s, n_s, m_s = _mlstm_step(zs, zs_f, zs_if, brow, gh.reshape(NH, DH), state_mlstm_C[0],
                                      state_mlstm_n[0], state_mlstm_m[0])

    w_kv = bf(jnp.concatenate([w_mem_k[0], w_mem_v[0]], axis=1))
    kv = _mem_proj(mem_prompt.reshape(BATCH * N_MEM, D), g_mem[0].reshape(1, D), w_kv)
    xa_p = _xattn_prompt(z_b, kv)
    xa_s = _xattn_step(zs, cache_mem_k[0], cache_mem_v[0])

    merged = _mix(x1, g_mix[0].reshape(1, D), y_s5_p, y_s5_s, ml_p.reshape(NP, W), ml_s, xa_p, xa_s,
                  bf(w_s5_glu[0]), w_bg, w_br)
    x2 = _outproj(x1, merged, bf(w_out[0]))
    y_p, y_s = _ffn((x2,), g_ffn2[0].reshape(1, D), w2g, w2u, w2d, g_final.reshape(1, D))

    return (y_p.reshape(BATCH, SEQ, D), y_s.reshape(NS, 1, D),
            kv[:, :W].reshape(1, BATCH, N_MEM, NH, DH), kv[:, W:].reshape(1, BATCH, N_MEM, NH, DH),
            fin[0], fin[1], c_p[None], n_p[None], m_p[:, :, 0][None],
            s5_re_s.reshape(1, NS, G, P), s5_im_s.reshape(1, NS, G, P), c_s[None], n_s[None], m_s[None])
```

```python
import functools

import jax
import jax.numpy as jnp
from jax import lax
from jax.experimental import pallas as pl
from jax.experimental.pallas import tpu as pltpu

F32 = jnp.float32
BF16 = jnp.bfloat16

D = 2048
BATCH = 4
SEQ = 2048
NS = 128
NP = BATCH * SEQ
TM = 512
NPT = NP // TM
NT = NPT + 1
MROWS = NT * TM
N_MEM = 256
FF = 5504
TF = 512
G = 64
P = 64
HG = 16
TC = 16
GB = 8
NGB = G // GB
WU = GB * HG
WS = GB * P
NCH = SEQ // TC
S5B = 2
W = 1024
NH = 4
DH = 256
CL = 256
NCL = SEQ // CL
SB = 8
EPS = 1e-6
VMEM_LIMIT = 56 * 1024 * 1024


def _cp(sem, vmem=VMEM_LIMIT):
    return pltpu.CompilerParams(dimension_semantics=sem, vmem_limit_bytes=vmem)


def _dot(a, b):
    return jnp.dot(a, b, preferred_element_type=F32)


def _dot_nt(a, b):
    return lax.dot_general(a, b, (((1,), (1,)), ((), ())), preferred_element_type=F32)


def _dot_tn(a, b):
    return lax.dot_general(a, b, (((0,), (0,)), ((), ())), preferred_element_type=F32)


def _hi_lo(x):
    hi = x.astype(BF16)
    lo = (x - hi.astype(F32)).astype(BF16)
    return hi, lo


def _split3(x):
    hi = x.astype(BF16)
    r1 = x - hi.astype(F32)
    mid = r1.astype(BF16)
    lo = (r1 - mid.astype(F32)).astype(BF16)
    return hi, mid, lo


def _dot3(a, b):
    ah, al = _hi_lo(a)
    bh, bl = _hi_lo(b)
    return _dot(ah, bh) + (_dot(ah, bl) + _dot(al, bh))


def _dot3_nt(a, b):
    ah, al = _hi_lo(a)
    bh, bl = _hi_lo(b)
    return _dot_nt(ah, bh) + (_dot_nt(ah, bl) + _dot_nt(al, bh))


def _rms(x, g):
    r = lax.rsqrt(jnp.mean(x * x, axis=-1, keepdims=True) + EPS)
    return (x * r) * g


def _sigmoid(x):
    return 1.0 / (1.0 + jnp.exp(-x))


def _log_sigmoid(x):
    return jnp.minimum(x, 0.0) - jnp.log1p(jnp.exp(-jnp.abs(x)))


def _gelu_tanh(x):
    return x * (0.5 * (1.0 + jnp.tanh(0.7978845608028654 * (x + 0.044715 * (x * x * x)))))


NJ = pl.cdiv(FF, TF)


def _ffn_kernel(*refs, two_src, final_norm, cast_steps):
    refs = list(refs)
    if two_src:
        xp_ref, xs_ref = refs[:2]
        refs = refs[2:]
    else:
        xp_ref = xs_ref = refs[0]
        refs = refs[1:]
    g_ref, wg_ref, wu_ref, wd_ref = refs[:4]
    refs = refs[4:]
    nc = len(cast_steps)
    if final_norm:
        gf_ref = refs[0]
        refs = refs[1:]
    cast_in = refs[:nc]
    refs = refs[nc:]
    if final_norm:
        op_ref, os_ref = refs[:2]
        refs = refs[2:]
    else:
        o_ref = refs[0]
        refs = refs[1:]
    n_out = sum(len(pieces) for _, _, pieces in cast_steps)
    cast_out = refs[:n_out]
    h_scr, acc_scr = refs[n_out:]
    j = pl.program_id(1)

    step = pl.program_id(0) * NJ + j
    k = 0
    for (first, count, pieces), src_ref in zip(cast_steps, cast_in):
        dst_refs = cast_out[k:k + len(pieces)]
        k += len(pieces)

        @pl.when(jnp.logical_and(step >= first, step < first + count))
        def _(src_ref=src_ref, dst_refs=dst_refs, pieces=pieces):
            for (c0, width), dst_ref in zip(pieces, dst_refs):
                dst_ref[...] = src_ref[:, c0:c0 + width].astype(BF16)

    def tile(rows, x_ref):
        rs = slice(0, rows)

        @pl.when(j == 0)
        def _():
            h_scr[rs, :] = _rms(x_ref[rs, :], g_ref[...]).astype(BF16)
            acc_scr[rs, :] = jnp.zeros((rows, D), F32)

        def accumulate(ragged):
            h = h_scr[rs, :]
            gt = _dot(h, wg_ref[...])
            hid = (gt * _sigmoid(gt)) * _dot(h, wu_ref[...])
            wd = wd_ref[...]
            if ragged:
                valid = FF - (NJ - 1) * TF
                hid = jnp.where(lax.broadcasted_iota(jnp.int32, (rows, TF), 1) < valid, hid, 0.0)
                wd = jnp.where(lax.broadcasted_iota(jnp.int32, (TF, D), 0) < valid, wd, jnp.zeros_like(wd))
            acc_scr[rs, :] += _dot(hid.astype(BF16), wd)

        pl.when(j < NJ - 1)(functools.partial(accumulate, False))

        @pl.when(j == NJ - 1)
        def _():
            accumulate(FF % TF != 0)
            y = x_ref[rs, :] + 0.5 * acc_scr[rs, :]
            if final_norm:
                y = _rms(y, gf_ref[...])
                (op_ref if rows == TM else os_ref)[...] = y
            else:
                o_ref[rs, :] = y
                if rows < TM:
                    o_ref[rows:, :] = jnp.zeros((TM - rows, D), F32)

    pl.when(pl.program_id(0) < NPT)(functools.partial(tile, TM, xp_ref))
    pl.when(pl.program_id(0) == NPT)(functools.partial(tile, NS, xs_ref))


def _ffn(xs, g, wg, wu, wd, g_final=None, casts=()):
    two_src = len(xs) == 2
    final_norm = g_final is not None
    row = pl.BlockSpec((TM, D), lambda i, j: (i, 0))
    prow = pl.BlockSpec((TM, D), lambda i, j: (jnp.minimum(i, NPT - 1), 0))
    srow = pl.BlockSpec((NS, D), lambda i, j: (0, 0))
    vec = pl.BlockSpec((1, D), lambda i, j: (0, 0))
    in_specs = ([prow, srow] if two_src else [row]) + [
        vec,
        pl.BlockSpec((D, TF), lambda i, j: (0, j)),
        pl.BlockSpec((D, TF), lambda i, j: (0, j)),
        pl.BlockSpec((TF, D), lambda i, j: (j, 0)),
    ]
    args = list(xs) + [g, wg, wu, wd]
    if final_norm:
        in_specs.append(vec)
        args.append(g_final)
        out_shape = [jax.ShapeDtypeStruct((NP, D), F32), jax.ShapeDtypeStruct((NS, D), F32)]
        out_specs = [prow, srow]
    else:
        out_shape = [jax.ShapeDtypeStruct((MROWS, D), F32)]
        out_specs = [row]
    cast_steps = []
    first = 0
    for a, rb, pieces in casts:
        count = a.shape[0] // rb
        pieces = pieces or ((0, a.shape[1]),)
        imap = functools.partial(lambda i, j, f, n: (jnp.clip(i * NJ + j - f, 0, n - 1), 0), f=first, n=count)
        in_specs.append(pl.BlockSpec((rb, a.shape[1]), imap))
        args.append(a)
        for _, width in pieces:
            out_specs.append(pl.BlockSpec((rb, width), imap))
            out_shape.append(jax.ShapeDtypeStruct((a.shape[0], width), BF16))
        cast_steps.append((first, count, tuple(pieces)))
        first += count
    assert first <= NPT * NJ, "side casts must fit under the full-size row tiles"
    return pl.pallas_call(
        functools.partial(_ffn_kernel, two_src=two_src, final_norm=final_norm, cast_steps=tuple(cast_steps)),
        grid=(NT, NJ),
        in_specs=in_specs,
        out_specs=out_specs,
        out_shape=out_shape,
        scratch_shapes=[pltpu.VMEM((TM, D), BF16), pltpu.VMEM((TM, D), F32)],
        compiler_params=_cp(("arbitrary", "arbitrary")),
        name="ffn_final" if final_norm else "ffn",
    )(*args)


NF32 = 2
TMI = 256
NPI = NP // TMI
GATE0 = 5 * W
HEAD = GATE0 + W + 128
MIX0 = GATE0 + 2 * NH + W


def _in_proj_kernel(x_ref, g_ref, w_ref, wift_ref,
                    ofp_ref, obp_ref, zifp_ref, ztp_ref, ofs_ref, obs_ref, zifs_ref):
    def emit(of_ref, ob_ref, zif_ref, zt_ref):
        h = _rms(x_ref[...], g_ref[...]).astype(BF16)
        if zt_ref is not None:
            zt_ref[...] = _dot_nt(wift_ref[...], h)
        of_ref[:, 0:W] = _dot(h, w_ref[:, 0:W])
        for j in range(1, 4):
            ob_ref[:, (j - 1) * W:j * W] = _dot(h, w_ref[:, j * W:(j + 1) * W]).astype(BF16)
        of_ref[:, W:2 * W] = _dot(h, w_ref[:, 4 * W:5 * W])
        tail = _dot(h, w_ref[:, 5 * W:HEAD])
        zif_ref[...] = tail[:, 0:128]
        ob_ref[:, 3 * W:4 * W] = tail[:, 2 * NH:2 * NH + W].astype(BF16)

    pl.when(pl.program_id(0) < NPI)(functools.partial(emit, ofp_ref, obp_ref, zifp_ref, ztp_ref))
    pl.when(pl.program_id(0) == NPI)(functools.partial(emit, ofs_ref, obs_ref, zifs_ref, None))


def _in_proj(x, g, w, w_ift):
    once = pl.Buffered(1)
    pblk = lambda cols: pl.BlockSpec((TMI, cols), lambda i: (jnp.minimum(i, NPI - 1), 0))
    sblk = lambda cols: pl.BlockSpec((TMI, cols), lambda i: (0, 0))
    widths = (NF32 * W, 4 * W, 128)
    dtypes = (F32, BF16, F32)
    return pl.pallas_call(
        _in_proj_kernel,
        grid=(NPI + 1,),
        in_specs=[
            pl.BlockSpec((TMI, D), lambda i: (i, 0)),
            pl.BlockSpec((1, D), lambda i: (0, 0)),
            pl.BlockSpec((D, HEAD), lambda i: (0, 0), pipeline_mode=once),
            pl.BlockSpec((8, D), lambda i: (0, 0), pipeline_mode=once),
        ],
        out_specs=[pblk(c) for c in widths]
        + [pl.BlockSpec((8, TMI), lambda i: (0, jnp.minimum(i, NPI - 1)))]
        + [sblk(c) for c in widths],
        out_shape=[jax.ShapeDtypeStruct((NP, c), t) for c, t in zip(widths, dtypes)]
        + [jax.ShapeDtypeStruct((8, NP), F32)]
        + [jax.ShapeDtypeStruct((TMI, c), t) for c, t in zip(widths, dtypes)],
        compiler_params=_cp(("arbitrary",)),
        name="in_proj",
    )(x, g, w, w_ift)


def _mem_proj_kernel(x_ref, g_ref, w_ref, o_ref, h_scr):
    @pl.when(pl.program_id(1) == 0)
    def _():
        h_scr[...] = _rms(x_ref[...], g_ref[...]).astype(BF16)

    o_ref[...] = _dot(h_scr[...], w_ref[...])


def _mem_proj(x, g, w):
    m, n = x.shape[0], w.shape[1]
    return pl.pallas_call(
        _mem_proj_kernel,
        grid=(m // TM, n // W),
        in_specs=[pl.BlockSpec((TM, D), lambda i, j: (i, 0)), pl.BlockSpec((1, D), lambda i, j: (0, 0)),
                  pl.BlockSpec((D, W), lambda i, j: (0, j))],
        out_specs=pl.BlockSpec((TM, W), lambda i, j: (i, j)),
        out_shape=jax.ShapeDtypeStruct((m, n), F32),
        scratch_shapes=[pltpu.VMEM((TM, D), BF16)],
        compiler_params=_cp(("arbitrary", "arbitrary")),
        name="mem_proj",
    )(x, g, w)


def _s5_prep_kernel(lr_ref, li_ref, ls_ref, btr_ref, bti_ref, cr_ref, ci_ref, d_ref,
                    ar_ref, ai_ref, a16r_ref, a16i_ref, bbr_ref, bbi_ref, e_ref, ft_ref, bd_ref):
    lr = lr_ref[0]
    li = li_ref[0]
    dt = jnp.exp(ls_ref[0])

    def power(k):
        mag = jnp.exp(lr * dt * float(k))
        ang = li * dt * float(k)
        return mag * jnp.cos(ang), mag * jnp.sin(ang)

    pw = [power(k) for k in range(TC + 1)]
    ar, ai = pw[1]
    den = lr * lr + li * li
    nr = ar - 1.0
    z_re = (nr * lr + ai * li) / den
    z_im = (ai * lr - nr * li) / den
    mask = (lax.broadcasted_iota(jnp.int32, (WU, WS), 0) // HG
            == lax.broadcasted_iota(jnp.int32, (WU, WS), 1) // P)
    btr = btr_ref[0]
    bti = bti_ref[0]
    bbr = jnp.where(mask, z_re * btr - z_im * bti, 0.0)
    bbi = jnp.where(mask, z_re * bti + z_im * btr, 0.0)
    cr = jnp.where(mask, cr_ref[0], 0.0)
    ci = jnp.where(mask, ci_ref[0], 0.0)

    ar_ref[0] = ar
    ai_ref[0] = ai
    a16r_ref[0] = pw[TC][0]
    a16i_ref[0] = pw[TC][1]
    bbr_ref[0] = bbr
    bbi_ref[0] = bbi

    def cmul(xr, xi, k):
        pr, pi = pw[k]
        return xr * pr - xi * pi, xr * pi + xi * pr

    diag = (lax.broadcasted_iota(jnp.int32, (WU, WU), 0) == lax.broadcasted_iota(jnp.int32, (WU, WU), 1))
    for s in range(TC):
        er, ei = cmul(bbr, bbi, TC - 1 - s)
        e_ref[0, s * WU:(s + 1) * WU, 0:WS] = er.astype(BF16)
        e_ref[0, s * WU:(s + 1) * WU, WS:2 * WS] = ei.astype(BF16)
        fr, fi = cmul(cr, ci, s + 1)
        ft_ref[0, s * WU:(s + 1) * WU, 0:WS] = fr.astype(BF16)
        ft_ref[0, s * WU:(s + 1) * WU, WS:2 * WS] = (-fi).astype(BF16)
        rr, ri = cmul(cr, ci, s)
        kern = _dot3_nt(bbr, rr) - _dot3_nt(bbi, ri)
        if s == 0:
            kern = kern + jnp.where(diag, d_ref[0], 0.0)
        bd_ref[0, s] = kern


def _s5_prep(lam_re, lam_im, log_step, bt_re, bt_im, c_re, c_im, d):
    def spec(*shape):
        nd = len(shape)
        return pl.BlockSpec((1,) + shape, lambda i: (i,) + (0,) * nd)

    def sds(*shape, dtype=F32):
        return jax.ShapeDtypeStruct((NGB,) + shape, dtype)

    return pl.pallas_call(
        _s5_prep_kernel,
        grid=(NGB,),
        in_specs=[spec(1, WS)] * 3 + [spec(WU, WS)] * 4 + [spec(WU, 1)],
        out_specs=[spec(1, WS)] * 4 + [spec(WU, WS)] * 2 + [spec(TC * WU, 2 * WS)] * 2 + [spec(TC, WU, WU)],
        out_shape=[sds(1, WS)] * 4 + [sds(WU, WS)] * 2 + [sds(TC * WU, 2 * WS, dtype=BF16)] * 2
        + [sds(TC, WU, WU)],
        compiler_params=_cp(("arbitrary",)),
        name="s5_prep",
    )(lam_re, lam_im, log_step, bt_re, bt_im, c_re, c_im, d)


def _s5_chunk_kernel(u_ref, bd_ref, e_ref, ft_ref, ar_ref, ai_ref, y_ref, fin_ref,
                     w_scr, lhs_scr, s_scr, xs_scr):
    rows = S5B * NCH

    @pl.when(pl.program_id(1) == 0)
    def _():
        w_scr[...] = jnp.zeros_like(w_scr)
        bd = [bd_ref[0, k].astype(BF16) for k in range(TC)]
        for s in range(TC):
            for t in range(s, TC):
                w_scr[s * WU:(s + 1) * WU, t * WU:(t + 1) * WU] = bd[t - s]

    for s in range(TC):
        lhs_scr[:, s * WU:(s + 1) * WU] = u_ref[pl.ds(s, rows, stride=TC), :].astype(BF16)
    lhs = lhs_scr[...]
    s_loc = _dot(lhs, e_ref[0])
    nl = WS // 128
    for k in range(2 * nl):
        s_scr[k] = s_loc[:, k * 128:(k + 1) * 128]
    ar = [ar_ref[0, :, k * 128:(k + 1) * 128] for k in range(nl)]
    ai = [ai_ref[0, :, k * 128:(k + 1) * 128] for k in range(nl)]
    xr = [jnp.zeros((S5B, 128), F32)] * nl
    xi = [jnp.zeros((S5B, 128), F32)] * nl
    for c in range(NCH):
        chunk_rows = pl.ds(c, S5B, stride=NCH)
        for k in range(nl):
            xs_scr[k, chunk_rows, :] = xr[k]
            xs_scr[nl + k, chunk_rows, :] = xi[k]
            sr = s_scr[k, chunk_rows, :]
            si = s_scr[nl + k, chunk_rows, :]
            xr[k], xi[k] = ar[k] * xr[k] - ai[k] * xi[k] + sr, ar[k] * xi[k] + ai[k] * xr[k] + si
    for k in range(nl):
        fin_ref[0, 0, :, k * 128:(k + 1) * 128] = xr[k]
        fin_ref[0, 0, :, WS + k * 128:WS + (k + 1) * 128] = xi[k]
    xs = jnp.concatenate([xs_scr[k] for k in range(2 * nl)], axis=1)
    y = _dot(lhs, w_scr[...]) + _dot_nt(xs.astype(BF16), ft_ref[0])
    for t in range(TC):
        y_ref[pl.ds(t, rows, stride=TC), :] = y[:, t * WU:(t + 1) * WU]


def _s5_chunk(z_f, bd, e, ft, a16r, a16i):
    rows = S5B * NCH
    nh = BATCH // S5B
    return pl.pallas_call(
        _s5_chunk_kernel,
        grid=(NGB, nh),
        in_specs=[
            pl.BlockSpec((S5B * SEQ, WU), lambda j, b: (b, j)),
            pl.BlockSpec((1, TC, WU, WU), lambda j, b: (j, 0, 0, 0)),
            pl.BlockSpec((1, TC * WU, 2 * WS), lambda j, b: (j, 0, 0)),
            pl.BlockSpec((1, TC * WU, 2 * WS), lambda j, b: (j, 0, 0)),
            pl.BlockSpec((1, 1, WS), lambda j, b: (j, 0, 0)),
            pl.BlockSpec((1, 1, WS), lambda j, b: (j, 0, 0)),
        ],
        out_specs=[
            pl.BlockSpec((S5B * SEQ, WU), lambda j, b: (b, j)),
            pl.BlockSpec((1, 1, S5B, 2 * WS), lambda j, b: (j, b, 0, 0)),
        ],
        out_shape=[jax.ShapeDtypeStruct((NP, W), F32), jax.ShapeDtypeStruct((NGB, nh, S5B, 2 * WS), F32)],
        scratch_shapes=[pltpu.VMEM((TC * WU, TC * WU), BF16), pltpu.VMEM((rows, TC * WU), BF16),
                        pltpu.VMEM((2 * WS // 128, rows, 128), F32), pltpu.VMEM((2 * WS // 128, rows, 128), F32)],
        compiler_params=_cp(("arbitrary", "arbitrary")),
        name="s5_chunk",
    )(z_f, bd, e, ft, a16r, a16i)


def _s5_step_kernel(u_ref, sr_ref, si_ref, ar_ref, ai_ref, bbr_ref, bbi_ref, cr_ref, ci_ref, d_ref,
                    y_ref, xr_ref, xi_ref):
    mask = (lax.broadcasted_iota(jnp.int32, (WU, WS), 0) // HG
            == lax.broadcasted_iota(jnp.int32, (WU, WS), 1) // P)
    for j in range(NGB):
        ul = slice(j * WU, (j + 1) * WU)
        sl = slice(j * WS, (j + 1) * WS)
        us = u_ref[:, ul]
        ar = ar_ref[j]
        ai = ai_ref[j]
        s_re = sr_ref[:, sl]
        s_im = si_ref[:, sl]
        x_re = ar * s_re - ai * s_im + _dot3(us, bbr_ref[j])
        x_im = ar * s_im + ai * s_re + _dot3(us, bbi_ref[j])
        xr_ref[:, sl] = x_re
        xi_ref[:, sl] = x_im
        cre = jnp.where(mask, cr_ref[j], 0.0)
        cim = jnp.where(mask, ci_ref[j], 0.0)
        y_ref[:, ul] = _dot3_nt(x_re, cre) - _dot3_nt(x_im, cim) + d_ref[:, ul] * us


def _s5_step(z_f, s_re, s_im, a_re, a_im, bb_re, bb_im, c_re, c_im, d_row):
    full = lambda *shape: pl.BlockSpec(shape, lambda i: (0,) * len(shape))
    return pl.pallas_call(
        _s5_step_kernel,
        grid=(1,),
        in_specs=[pl.BlockSpec((NS, W), lambda i: (0, 0)),
                  full(NS, G * P), full(NS, G * P), full(NGB, 1, WS), full(NGB, 1, WS),
                  full(NGB, WU, WS), full(NGB, WU, WS), full(NGB, WU, WS), full(NGB, WU, WS),
                  full(1, W)],
        out_specs=[full(NS, W), full(NS, G * P), full(NS, G * P)],
        out_shape=[jax.ShapeDtypeStruct((NS, W), F32), jax.ShapeDtypeStruct((NS, G * P), F32),
                   jax.ShapeDtypeStruct((NS, G * P), F32)],
        compiler_params=_cp(("arbitrary",)),
        name="s5_step",
    )(z_f, s_re, s_im, a_re, a_im, bb_re, bb_im, c_re, c_im, d_row)


def _mlstm_chunk_kernel(q_ref, k_ref, v_ref, o_ref, zif_ref, zt0_ref, zt1_ref, zt2_ref, zt3_ref,
                        brow_ref, bcol_ref, gh_ref, h_ref, c_ref, n_ref, m_ref):
    @pl.when(pl.program_id(0) == 0)
    def _():
        c_ref[...] = jnp.zeros_like(c_ref)
        n_ref[...] = jnp.zeros_like(n_ref)
        m_ref[...] = jnp.zeros_like(m_ref)

    rr = lax.broadcasted_iota(jnp.int32, (CL, CL), 0)
    cc = lax.broadcasted_iota(jnp.int32, (CL, CL), 1)
    causal = cc <= rr
    tril = jnp.where(causal, 1.0, 0.0).astype(BF16)
    triu = jnp.where(rr <= cc, 1.0, 0.0).astype(BF16)

    for b, zt_ref in enumerate((zt0_ref, zt1_ref, zt2_ref, zt3_ref)):
        zi = zif_ref[b] + brow_ref[...]
        zt = zt_ref[...] + bcol_ref[...]
        lfc = _split3(_log_sigmoid(zi))
        bcum_col = _dot(tril, lfc[0]) + (_dot(tril, lfc[1]) + _dot(tril, lfc[2]))
        lfr = _split3(_log_sigmoid(zt))
        bcum_row = _dot(lfr[0], triu) + (_dot(lfr[1], triu) + _dot(lfr[2], triu))

        for hh in range(NH):
            hs = slice(hh * DH, (hh + 1) * DH)
            bc = bcum_col[:, NH + hh:NH + hh + 1]
            ic = zi[:, hh:hh + 1]
            br = bcum_row[NH + hh:NH + hh + 1, :]
            ir = zt[hh:hh + 1, :]
            m_prev = m_ref[b, hh:hh + 1, 0:1]
            g_inter = bc + m_prev
            dlog = jnp.where(causal, (bc - br) + ir, -jnp.inf)
            m_t = jnp.maximum(g_inter, jnp.max(dlog, axis=-1, keepdims=True))
            w_inter = jnp.exp(g_inter - m_t)
            w_intra = jnp.exp(dlog - m_t)
            qb = q_ref[b, :, hs]
            kb = k_ref[b, :, hs]
            vb = v_ref[b, :, hs]
            qf = qb.astype(F32)
            kf = kb.astype(F32)
            s = _dot_nt(qb, kb) * (w_intra * (DH ** -0.5))
            c_prev = c_ref[b, hh]
            n_prev = n_ref[b, hh:hh + 1, :]
            num = _dot(s.astype(BF16), vb) + w_inter * _dot(qb, c_prev.astype(BF16))
            nq = jnp.sum(s, axis=-1, keepdims=True) + w_inter * jnp.sum(qf * n_prev, axis=-1, keepdims=True)
            h = num / jnp.maximum(jnp.abs(nq), jnp.exp(-m_t))
            m_last = m_t[CL - 1:CL, :]
            w_last = jnp.exp((bc[CL - 1:CL, :] - bc) + ic - m_last) * (DH ** -0.5)
            wi_last = w_inter[CL - 1:CL, :]
            kw = kf * w_last
            c_ref[b, hh] = wi_last * c_prev + _dot_tn(kw.astype(BF16), vb)
            n_ref[b, hh:hh + 1, :] = wi_last * n_prev + jnp.sum(kw, axis=0, keepdims=True)
            m_ref[b, hh:hh + 1, :] = jnp.broadcast_to(m_last, (1, 128))
            hn = _rms(h, gh_ref[:, hs])
            h_ref[b, :, hs] = (hn * _sigmoid(o_ref[b, :, hs])).astype(BF16)


def _mlstm_chunk(z_f, z_b, z_if, z_t, brow, bcol, gh):
    blk = lambda col: pl.BlockSpec((BATCH, CL, W), lambda c: (0, c, col))
    const = lambda *shape: pl.BlockSpec(shape, lambda c: (0,) * len(shape))
    zt_specs = [pl.BlockSpec((8, CL), functools.partial(lambda c, b: (0, b * NCL + c), b=b)) for b in range(BATCH)]
    return pl.pallas_call(
        _mlstm_chunk_kernel,
        grid=(NCL,),
        in_specs=[blk(0), blk(1), blk(2), blk(1), pl.BlockSpec((BATCH, CL, 128), lambda c: (0, c, 0))]
        + zt_specs + [const(1, 128), const(8, 1), const(1, W)],
        out_specs=[pl.BlockSpec((BATCH, CL, W), lambda c: (0, c, 0)),
                   const(BATCH, NH, DH, DH), const(BATCH, NH, DH), const(BATCH, NH, 128)],
        out_shape=[jax.ShapeDtypeStruct((BATCH, SEQ, W), BF16),
                   jax.ShapeDtypeStruct((BATCH, NH, DH, DH), F32),
                   jax.ShapeDtypeStruct((BATCH, NH, DH), F32),
                   jax.ShapeDtypeStruct((BATCH, NH, 128), F32)],
        compiler_params=_cp(("arbitrary",)),
        name="mlstm_chunk",
    )(z_b, z_b, z_b, z_f, z_if, z_t, z_t, z_t, z_t, brow, bcol, gh)


MSR = 64


def _mlstm_step_kernel(q_ref, k_ref, v_ref, o_ref, zif_ref, brow_ref, gh_ref, c0_ref, n0_ref, m0_ref,
                       h_ref, c_ref, n_ref, m_ref):
    zi = zif_ref[...] + brow_ref[...]
    ig = zi[:, 0:NH]
    g_inter = _log_sigmoid(zi[:, NH:2 * NH]) + m0_ref[...]
    m_t = jnp.maximum(g_inter, ig)
    w_inter = jnp.exp(g_inter - m_t)
    w_intra = jnp.exp(ig - m_t) * (DH ** -0.5)
    floor = jnp.exp(-m_t)
    m_ref[...] = m_t

    def heads_on_rows(x):
        return jnp.pad(x, ((0, 128 - SB), (0, 128 - NH))).T

    w_inter_t = heads_on_rows(w_inter)
    w_intra_t = heads_on_rows(w_intra)
    floor_t = heads_on_rows(floor)
    pad = jnp.zeros((128 - SB, DH), F32)
    q_t = [jnp.concatenate([q_ref[:, hh * DH:(hh + 1) * DH], pad], axis=0).T for hh in range(NH)]
    k_t = [jnp.concatenate([k_ref[:, hh * DH:(hh + 1) * DH], pad], axis=0).T for hh in range(NH)]

    def per_head_rows(ref, s):
        return jnp.concatenate([ref[s:s + 1, hh * DH:(hh + 1) * DH] for hh in range(NH)], axis=0)

    for s in range(SB):
        q4 = per_head_rows(q_ref, s)
        k4 = per_head_rows(k_ref, s)
        v4 = per_head_rows(v_ref, s)
        qc_rows = []
        for hh in range(NH):
            wi = w_inter[s:s + 1, hh:hh + 1]
            vw = v4[hh:hh + 1, :] * w_intra[s:s + 1, hh:hh + 1]
            acc = jnp.zeros((MSR, DH), F32)
            for r0 in range(0, DH, MSR):
                c_blk = c0_ref[s, hh, r0:r0 + MSR, :]
                acc = acc + q_t[hh][r0:r0 + MSR, s:s + 1] * c_blk
                c_ref[s, hh, r0:r0 + MSR, :] = wi * c_blk + k_t[hh][r0:r0 + MSR, s:s + 1] * vw
            qc_rows.append(jnp.sum(acc, axis=0, keepdims=True))
        q_c = jnp.concatenate(qc_rows, axis=0)
        wi_c = w_inter_t[0:NH, s:s + 1]
        wa_c = w_intra_t[0:NH, s:s + 1]
        n_prev = n0_ref[s]
        sv = jnp.sum(q4 * k4, axis=-1, keepdims=True) * wa_c
        num = sv * v4 + wi_c * q_c
        nq = sv + wi_c * jnp.sum(q4 * n_prev, axis=-1, keepdims=True)
        h = num / jnp.maximum(jnp.abs(nq), floor_t[0:NH, s:s + 1])
        n_ref[s] = wi_c * n_prev + wa_c * k4
        out = _rms(h, gh_ref[...]) * _sigmoid(per_head_rows(o_ref, s))
        for hh in range(NH):
            h_ref[s:s + 1, hh * DH:(hh + 1) * DH] = out[hh:hh + 1, :]


def _mlstm_step(zs, z_f, z_if, brow, gh, c0, n0, m0):
    blk = lambda col: pl.BlockSpec((SB, W), lambda i: (i, col))
    return pl.pallas_call(
        _mlstm_step_kernel,
        grid=(NS // SB,),
        in_specs=[blk(0), blk(1), blk(2), blk(1),
                  pl.BlockSpec((SB, 128), lambda i: (i, 0)),
                  pl.BlockSpec((1, 128), lambda i: (0, 0)),
                  pl.BlockSpec((NH, DH), lambda i: (0, 0)),
                  pl.BlockSpec((SB, NH, DH, DH), lambda i: (i, 0, 0, 0)),
                  pl.BlockSpec((SB, NH, DH), lambda i: (i, 0, 0)),
                  pl.BlockSpec((SB, NH), lambda i: (i, 0))],
        out_specs=[pl.BlockSpec((SB, W), lambda i: (i, 0)),
                   pl.BlockSpec((SB, NH, DH, DH), lambda i: (i, 0, 0, 0)),
                   pl.BlockSpec((SB, NH, DH), lambda i: (i, 0, 0)),
                   pl.BlockSpec((SB, NH), lambda i: (i, 0))],
        out_shape=[jax.ShapeDtypeStruct((NS, W), F32),
                   jax.ShapeDtypeStruct((NS, NH, DH, DH), F32),
                   jax.ShapeDtypeStruct((NS, NH, DH), F32),
                   jax.ShapeDtypeStruct((NS, NH), F32)],
        compiler_params=_cp(("arbitrary",)),
        name="mlstm_step",
    )(zs, zs, zs, z_f, z_if, brow, gh, c0, n0, m0)


def _softmax_rows(s):
    e = jnp.exp(s - jnp.max(s, axis=-1, keepdims=True))
    return e / jnp.sum(e, axis=-1, keepdims=True)


def _xattn_prompt_kernel(q_ref, k_ref, v_ref, o_ref):
    for hh in range(NH):
        hs = slice(hh * DH, (hh + 1) * DH)
        s = _dot_nt(q_ref[:, hs], k_ref[:, hs].astype(BF16)) * (DH ** -0.5)
        p = _softmax_rows(s)
        o_ref[:, hs] = _dot(p.astype(BF16), v_ref[:, hs].astype(BF16)).astype(BF16)


def _xattn_prompt(z_b, kv):
    nt = SEQ // TM
    return pl.pallas_call(
        _xattn_prompt_kernel,
        grid=(BATCH, nt),
        in_specs=[pl.BlockSpec((TM, W), lambda b, t: (b * nt + t, 3)),
                  pl.BlockSpec((N_MEM, W), lambda b, t: (b, 0)),
                  pl.BlockSpec((N_MEM, W), lambda b, t: (b, 1))],
        out_specs=pl.BlockSpec((TM, W), lambda b, t: (b * nt + t, 0)),
        out_shape=jax.ShapeDtypeStruct((NP, W), BF16),
        compiler_params=_cp(("arbitrary", "arbitrary")),
        name="xattn_prompt",
    )(z_b, kv, kv)


XS = 4


XMC = 64


def _xattn_step_kernel(q_ref, k_ref, v_ref, o_ref):
    def part(t):
        for s in range(XS):
            r = t * XS + s
            q4 = jnp.concatenate([q_ref[r:r + 1, hh * DH:(hh + 1) * DH] for hh in range(NH)], axis=0)
            sc = jnp.concatenate(
                [jnp.sum(k_ref[s, m0:m0 + XMC] * q4[None], axis=-1, keepdims=True) for m0 in range(0, N_MEM, XMC)],
                axis=0) * (DH ** -0.5)
            e = jnp.exp(sc - jnp.max(sc, axis=0, keepdims=True))
            p = e / jnp.sum(e, axis=0, keepdims=True)
            acc = jnp.zeros((NH, DH), F32)
            for m0 in range(0, N_MEM, XMC):
                acc = acc + jnp.sum(p[m0:m0 + XMC] * v_ref[s, m0:m0 + XMC], axis=0)
            for hh in range(NH):
                o_ref[r:r + 1, hh * DH:(hh + 1) * DH] = acc[hh:hh + 1, :]

    for t in range(SB // XS):
        pl.when(pl.program_id(1) == t)(functools.partial(part, t))


def _xattn_step(zs, mem_k, mem_v):
    nt = SB // XS
    return pl.pallas_call(
        _xattn_step_kernel,
        grid=(NS // SB, nt),
        in_specs=[pl.BlockSpec((SB, W), lambda i, t: (i, 3)),
                  pl.BlockSpec((XS, N_MEM, NH, DH), lambda i, t: (i * nt + t, 0, 0, 0)),
                  pl.BlockSpec((XS, N_MEM, NH, DH), lambda i, t: (i * nt + t, 0, 0, 0))],
        out_specs=pl.BlockSpec((SB, W), lambda i, t: (i, 0)),
        out_shape=jax.ShapeDtypeStruct((NS, W), F32),
        compiler_params=_cp(("arbitrary", "arbitrary")),
        name="xattn_step",
    )(zs, mem_k, mem_v)


TMX = 512


def _mix_kernel(x_ref, g_ref, yp_ref, ys_ref, mp_ref, ms_ref, ap_ref, as_ref, wglu_ref,
                wg0_ref, wg1_ref, wg2_ref, wb0_ref, wb1_ref, wb2_ref, o_ref, h_scr, s5_scr):
    def tile(rows, y_ref, ml_ref, xa_ref):
        rs = slice(0, rows)

        @pl.when(pl.program_id(1) == 0)
        def _():
            h_scr[rs, :] = _rms(x_ref[rs, :], g_ref[...]).astype(BF16)
            y = _gelu_tanh(y_ref[...])
            s5_scr[rs, :] = (y * _sigmoid(_dot(y.astype(BF16), wglu_ref[...]))).astype(BF16)

        h = h_scr[rs, :]
        merged = (_sigmoid(_dot(h, wg0_ref[...])) * _dot(s5_scr[rs, :], wb0_ref[...])
                  + _sigmoid(_dot(h, wg1_ref[...])) * _dot(ml_ref[...].astype(BF16), wb1_ref[...])
                  + _sigmoid(_dot(h, wg2_ref[...])) * _dot(xa_ref[...].astype(BF16), wb2_ref[...]))
        o_ref[rs, :] = merged.astype(BF16)
        if rows < TM:
            o_ref[rows:, :] = jnp.zeros((TM - rows, TMX), BF16)

    pl.when(pl.program_id(0) < NPT)(functools.partial(tile, TM, yp_ref, mp_ref, ap_ref))
    pl.when(pl.program_id(0) == NPT)(functools.partial(tile, NS, ys_ref, ms_ref, as_ref))


def _mix(x1, g, yp, ys, mp, ms, ap, as_, wglu, wg, wb):
    prow = pl.BlockSpec((TM, W), lambda i, j: (jnp.minimum(i, NPT - 1), 0))
    srow = pl.BlockSpec((NS, W), lambda i, j: (0, 0))
    nx = D // TMX
    wgs = [pl.BlockSpec((D, TMX), functools.partial(lambda i, j, b: (0, b * nx + j), b=b)) for b in range(3)]
    wbs = pl.BlockSpec((W, TMX), lambda i, j: (0, j))
    return pl.pallas_call(
        _mix_kernel,
        grid=(NT, D // TMX),
        in_specs=[pl.BlockSpec((TM, D), lambda i, j: (i, 0)), pl.BlockSpec((1, D), lambda i, j: (0, 0)),
                  prow, srow, prow, srow, prow, srow,
                  pl.BlockSpec((W, W), lambda i, j: (0, 0)),
                  *wgs, wbs, wbs, wbs],
        out_specs=pl.BlockSpec((TM, TMX), lambda i, j: (i, j)),
        out_shape=jax.ShapeDtypeStruct((MROWS, D), BF16),
        scratch_shapes=[pltpu.VMEM((TM, D), BF16), pltpu.VMEM((TM, W), BF16)],
        compiler_params=_cp(("arbitrary", "arbitrary")),
        name="mix",
    )(x1, g, yp, ys, mp, ms, ap, as_, wglu, wg, wg, wg, *wb)


def _outproj_kernel(x_ref, m_ref, w_ref, o_ref):
    def tile(rows):
        rs = slice(0, rows)
        o_ref[rs, :] = x_ref[rs, :] + _dot(m_ref[rs, :], w_ref[...])
        if rows < TM:
            o_ref[rows:, :] = jnp.zeros((TM - rows, D), F32)

    pl.when(pl.program_id(0) < NPT)(functools.partial(tile, TM))
    pl.when(pl.program_id(0) == NPT)(functools.partial(tile, NS))


def _outproj(x1, merged, w_out):
    return pl.pallas_call(
        _outproj_kernel,
        grid=(NT,),
        in_specs=[pl.BlockSpec((TM, D), lambda i: (i, 0)), pl.BlockSpec((TM, D), lambda i: (i, 0)),
                  pl.BlockSpec((D, D), lambda i: (0, 0))],
        out_specs=pl.BlockSpec((TM, D), lambda i: (i, 0)),
        out_shape=jax.ShapeDtypeStruct((MROWS, D), F32),
        compiler_params=_cp(("arbitrary",)),
        name="out_proj",
    )(x1, merged, w_out)


def kernel(x_prompt, x_sample, mem_prompt, cache_mem_k, cache_mem_v, state_s5_re, state_s5_im, state_mlstm_C,
           state_mlstm_n, state_mlstm_m, g_ffn1, w1_gate, w1_up, w1_down, g_mix, w_in, s5_lambda_re,
           s5_lambda_im, s5_log_step, s5_b_re, s5_b_im, s5_c_re, s5_c_im, s5_d, w_s5_glu, b_igate, b_fgate,
           g_mlstm_head, g_mem, w_mem_k, w_mem_v, w_br_s5, w_br_ml, w_br_xa, w_out, g_ffn2, w2_gate, w2_up,
           w2_down, g_final):
    bf = lambda a: a.astype(BF16)

    x1, w2g, w2u, w2d, w_head, w_bg = _ffn(
        (x_prompt.reshape(NP, D), x_sample.reshape(NS, D)), g_ffn1[0].reshape(1, D),
        bf(w1_gate[0]), bf(w1_up[0]), bf(w1_down[0]),
        casts=((w2_gate[0], 64, None), (w2_up[0], 64, None), (w2_down[0], 128, None),
               (w_in[0], 32, ((0, HEAD), (MIX0, 3 * D)))))

    w_ift = bf(w_in[0][:, GATE0:GATE0 + 2 * NH].T)
    w_br = [bf(w_br_s5[0]), bf(w_br_ml[0]), bf(w_br_xa[0])]
    brow = jnp.pad(jnp.concatenate([b_igate[0], b_fgate[0]]), (0, 128 - 2 * NH)).reshape(1, 128)
    bcol = jnp.concatenate([b_igate[0], b_fgate[0]]).reshape(2 * NH, 1)
    gh = g_mlstm_head[0].reshape(1, W)

    z_f, z_b, z_if, z_t, zs_f, zs_b, zs_if = _in_proj(x1, g_mix[0].reshape(1, D), w_head, w_ift)
    zs = zs_b[:NS].astype(F32)
    per_seq = lambda a: a.reshape(BATCH, SEQ, a.shape[1])

    blk = lambda a: jnp.tile(a.reshape(NGB, WU, P), (1, 1, GB))
    row = lambda a: a.reshape(NGB, 1, WS)
    (a_re, a_im, a16_re, a16_im, bb_re, bb_im, e_op, ft_op, bd_op) = _s5_prep(
        row(s5_lambda_re[0]), row(s5_lambda_im[0]), row(jnp.repeat(s5_log_step[0], P)),
        blk(s5_b_re[0].transpose(0, 2, 1)), blk(s5_b_im[0].transpose(0, 2, 1)),
        blk(s5_c_re[0]), blk(s5_c_im[0]), s5_d[0].reshape(NGB, WU, 1))
    y_s5_p, fin = _s5_chunk(z_f, bd_op, e_op, ft_op, a16_re, a16_im)
    fin = fin.reshape(NGB, BATCH, 2, GB, P).transpose(2, 1, 0, 3, 4).reshape(2, 1, BATCH, G, P)
    y_s5_s, s5_re_s, s5_im_s = _s5_step(
        zs_f, state_s5_re[0].reshape(NS, G * P), state_s5_im[0].reshape(NS, G * P),
        a_re, a_im, bb_re, bb_im, blk(s5_c_re[0]), blk(s5_c_im[0]), s5_d[0].reshape(1, W))

    ml_p, c_p, n_p, m_p = _mlstm_chunk(per_seq(z_f), per_seq(z_b), per_seq(z_if), z_t, brow, bcol, gh)
    ml_s, c_s, n_s, m_s = _mlstm_step(zs, zs_f, zs_if, brow, gh.reshape(NH, DH), state_mlstm_C[0],
                                      state_mlstm_n[0], state_mlstm_m[0])

    w_kv = bf(jnp.concatenate([w_mem_k[0], w_mem_v[0]], axis=1))
    kv = _mem_proj(mem_prompt.reshape(BATCH * N_MEM, D), g_mem[0].reshape(1, D), w_kv)
    xa_p = _xattn_prompt(z_b, kv)
    xa_s = _xattn_step(zs, cache_mem_k[0], cache_mem_v[0])

    merged = _mix(x1, g_mix[0].reshape(1, D), y_s5_p, y_s5_s, ml_p.reshape(NP, W), ml_s, xa_p, xa_s,
                  bf(w_s5_glu[0]), w_bg, w_br)
    x2 = _outproj(x1, merged, bf(w_out[0]))
    y_p, y_s = _ffn((x2,), g_ffn2[0].reshape(1, D), w2g, w2u, w2d, g_final.reshape(1, D))

    return (y_p.reshape(BATCH, SEQ, D), y_s.reshape(NS, 1, D),
            kv[:, :W].reshape(1, BATCH, N_MEM, NH, DH), kv[:, W:].reshape(1, BATCH, N_MEM, NH, DH),
            fin[0], fin[1], c_p[None], n_p[None], m_p[:, :, 0][None],
            s5_re_s.reshape(1, NS, G, P), s5_im_s.reshape(1, NS, G, P), c_s[None], n_s[None], m_s[None])
```

```python
import functools

import jax
import jax.numpy as jnp
from jax import lax
from jax.experimental import pallas as pl
from jax.experimental.pallas import tpu as pltpu

F32 = jnp.float32
BF16 = jnp.bfloat16

D = 2048
BATCH = 4
SEQ = 2048
NS = 128
NP = BATCH * SEQ
TM = 512
NPT = NP // TM
NT = NPT + 1
MROWS = NT * TM
N_MEM = 256
FF = 5504
TF = 512
G = 64
P = 64
HG = 16
TC = 16
GB = 8
NGB = G // GB
WU = GB * HG
WS = GB * P
NCH = SEQ // TC
S5B = 2
W = 1024
NH = 4
DH = 256
CL = 256
NCL = SEQ // CL
SB = 8
EPS = 1e-6
VMEM_LIMIT = 56 * 1024 * 1024


def _cp(sem, vmem=VMEM_LIMIT):
    return pltpu.CompilerParams(dimension_semantics=sem, vmem_limit_bytes=vmem)


def _dot(a, b):
    return jnp.dot(a, b, preferred_element_type=F32)


def _dot_nt(a, b):
    return lax.dot_general(a, b, (((1,), (1,)), ((), ())), preferred_element_type=F32)


def _dot_tn(a, b):
    return lax.dot_general(a, b, (((0,), (0,)), ((), ())), preferred_element_type=F32)


def _hi_lo(x):
    hi = x.astype(BF16)
    lo = (x - hi.astype(F32)).astype(BF16)
    return hi, lo


def _split3(x):
    hi = x.astype(BF16)
    r1 = x - hi.astype(F32)
    mid = r1.astype(BF16)
    lo = (r1 - mid.astype(F32)).astype(BF16)
    return hi, mid, lo


def _dot3(a, b):
    ah, al = _hi_lo(a)
    bh, bl = _hi_lo(b)
    return _dot(ah, bh) + (_dot(ah, bl) + _dot(al, bh))


def _dot3_nt(a, b):
    ah, al = _hi_lo(a)
    bh, bl = _hi_lo(b)
    return _dot_nt(ah, bh) + (_dot_nt(ah, bl) + _dot_nt(al, bh))


def _rms(x, g):
    r = lax.rsqrt(jnp.mean(x * x, axis=-1, keepdims=True) + EPS)
    return (x * r) * g


def _sigmoid(x):
    return 1.0 / (1.0 + jnp.exp(-x))


def _log_sigmoid(x):
    return jnp.minimum(x, 0.0) - jnp.log1p(jnp.exp(-jnp.abs(x)))


def _gelu_tanh(x):
    return x * (0.5 * (1.0 + jnp.tanh(0.7978845608028654 * (x + 0.044715 * (x * x * x)))))


NJ = pl.cdiv(FF, TF)


def _ffn_kernel(*refs, two_src, final_norm, cast_steps):
    refs = list(refs)
    if two_src:
        xp_ref, xs_ref = refs[:2]
        refs = refs[2:]
    else:
        xp_ref = xs_ref = refs[0]
        refs = refs[1:]
    g_ref, wg_ref, wu_ref, wd_ref = refs[:4]
    refs = refs[4:]
    nc = len(cast_steps)
    if final_norm:
        gf_ref = refs[0]
        refs = refs[1:]
    cast_in = refs[:nc]
    refs = refs[nc:]
    if final_norm:
        op_ref, os_ref = refs[:2]
        refs = refs[2:]
    else:
        o_ref = refs[0]
        refs = refs[1:]
    n_out = sum(len(pieces) for pieces in cast_steps)
    cast_out = refs[:n_out]
    h_scr, acc_scr = refs[n_out:]
    j = pl.program_id(1)

    def side_casts():
        k = 0
        for pieces, src_ref in zip(cast_steps, cast_in):
            for c0, width in pieces:
                cast_out[k][...] = src_ref[:, c0:c0 + width].astype(BF16)
                k += 1

    def tile(rows, x_ref):
        rs = slice(0, rows)

        @pl.when(j == 0)
        def _():
            h_scr[rs, :] = _rms(x_ref[rs, :], g_ref[...]).astype(BF16)
            acc_scr[rs, :] = jnp.zeros((rows, D), F32)

        def accumulate(ragged):
            side_casts()
            h = h_scr[rs, :]
            gt = _dot(h, wg_ref[...])
            hid = (gt * _sigmoid(gt)) * _dot(h, wu_ref[...])
            wd = wd_ref[...]
            if ragged:
                valid = FF - (NJ - 1) * TF
                hid = jnp.where(lax.broadcasted_iota(jnp.int32, (rows, TF), 1) < valid, hid, 0.0)
                wd = jnp.where(lax.broadcasted_iota(jnp.int32, (TF, D), 0) < valid, wd, jnp.zeros_like(wd))
            acc_scr[rs, :] += _dot(hid.astype(BF16), wd)

        pl.when(j < NJ - 1)(functools.partial(accumulate, False))

        @pl.when(j == NJ - 1)
        def _():
            accumulate(FF % TF != 0)
            y = x_ref[rs, :] + 0.5 * acc_scr[rs, :]
            if final_norm:
                y = _rms(y, gf_ref[...])
                (op_ref if rows == TM else os_ref)[...] = y
            else:
                o_ref[rs, :] = y
                if rows < TM:
                    o_ref[rows:, :] = jnp.zeros((TM - rows, D), F32)

    pl.when(pl.program_id(0) < NPT)(functools.partial(tile, TM, xp_ref))
    pl.when(pl.program_id(0) == NPT)(functools.partial(tile, NS, xs_ref))


def _ffn(xs, g, wg, wu, wd, g_final=None, casts=()):
    two_src = len(xs) == 2
    final_norm = g_final is not None
    row = pl.BlockSpec((TM, D), lambda i, j: (i, 0))
    prow = pl.BlockSpec((TM, D), lambda i, j: (jnp.minimum(i, NPT - 1), 0))
    srow = pl.BlockSpec((NS, D), lambda i, j: (0, 0))
    vec = pl.BlockSpec((1, D), lambda i, j: (0, 0))
    in_specs = ([prow, srow] if two_src else [row]) + [
        vec,
        pl.BlockSpec((D, TF), lambda i, j: (0, j)),
        pl.BlockSpec((D, TF), lambda i, j: (0, j)),
        pl.BlockSpec((TF, D), lambda i, j: (j, 0)),
    ]
    args = list(xs) + [g, wg, wu, wd]
    if final_norm:
        in_specs.append(vec)
        args.append(g_final)
        out_shape = [jax.ShapeDtypeStruct((NP, D), F32), jax.ShapeDtypeStruct((NS, D), F32)]
        out_specs = [prow, srow]
    else:
        out_shape = [jax.ShapeDtypeStruct((MROWS, D), F32)]
        out_specs = [row]
    cast_steps = []
    for a, rb, pieces in casts:
        rows_a, cols_a = a.shape[-2:]
        count = rows_a // rb
        assert count <= NPT * NJ, "side casts must fit under the full-size row tiles"
        pieces = pieces or ((0, cols_a),)
        blk = functools.partial(lambda i, j, n: jnp.minimum(i * NJ + j, n - 1), n=count)
        if a.ndim == 3:
            in_specs.append(pl.BlockSpec((None, rb, cols_a), functools.partial(lambda i, j, b: (0, b(i, j), 0), b=blk)))
        else:
            in_specs.append(pl.BlockSpec((rb, cols_a), functools.partial(lambda i, j, b: (b(i, j), 0), b=blk)))
        args.append(a)
        for _, width in pieces:
            out_specs.append(pl.BlockSpec((rb, width), functools.partial(lambda i, j, b: (b(i, j), 0), b=blk)))
            out_shape.append(jax.ShapeDtypeStruct((rows_a, width), BF16))
        cast_steps.append(tuple(pieces))
    return pl.pallas_call(
        functools.partial(_ffn_kernel, two_src=two_src, final_norm=final_norm, cast_steps=tuple(cast_steps)),
        grid=(NT, NJ),
        in_specs=in_specs,
        out_specs=out_specs,
        out_shape=out_shape,
        scratch_shapes=[pltpu.VMEM((TM, D), BF16), pltpu.VMEM((TM, D), F32)],
        compiler_params=_cp(("arbitrary", "arbitrary")),
        name="ffn_final" if final_norm else "ffn",
    )(*args)


NF32 = 2
TMI = 256
NPI = NP // TMI
GATE0 = 5 * W
HEAD = GATE0 + W + 128
MIX0 = GATE0 + 2 * NH + W


def _in_proj_kernel(x_ref, g_ref, w_ref, wift_ref,
                    ofp_ref, obp_ref, zifp_ref, ztp_ref, ofs_ref, obs_ref, zifs_ref):
    def emit(of_ref, ob_ref, zif_ref, zt_ref):
        h = _rms(x_ref[...], g_ref[...]).astype(BF16)
        if zt_ref is not None:
            zt_ref[...] = _dot_nt(wift_ref[...], h)
        of_ref[:, 0:W] = _dot(h, w_ref[:, 0:W])
        for j in range(1, 4):
            ob_ref[:, (j - 1) * W:j * W] = _dot(h, w_ref[:, j * W:(j + 1) * W]).astype(BF16)
        of_ref[:, W:2 * W] = _dot(h, w_ref[:, 4 * W:5 * W])
        tail = _dot(h, w_ref[:, 5 * W:HEAD])
        zif_ref[...] = tail[:, 0:128]
        ob_ref[:, 3 * W:4 * W] = tail[:, 2 * NH:2 * NH + W].astype(BF16)

    pl.when(pl.program_id(0) < NPI)(functools.partial(emit, ofp_ref, obp_ref, zifp_ref, ztp_ref))
    pl.when(pl.program_id(0) == NPI)(functools.partial(emit, ofs_ref, obs_ref, zifs_ref, None))


def _in_proj(x, g, w, w_ift):
    once = pl.Buffered(1)
    pblk = lambda cols: pl.BlockSpec((TMI, cols), lambda i: (jnp.minimum(i, NPI - 1), 0))
    sblk = lambda cols: pl.BlockSpec((TMI, cols), lambda i: (0, 0))
    widths = (NF32 * W, 4 * W, 128)
    dtypes = (F32, BF16, F32)
    return pl.pallas_call(
        _in_proj_kernel,
        grid=(NPI + 1,),
        in_specs=[
            pl.BlockSpec((TMI, D), lambda i: (i, 0)),
            pl.BlockSpec((1, D), lambda i: (0, 0)),
            pl.BlockSpec((D, HEAD), lambda i: (0, 0), pipeline_mode=once),
            pl.BlockSpec((8, D), lambda i: (0, 0), pipeline_mode=once),
        ],
        out_specs=[pblk(c) for c in widths]
        + [pl.BlockSpec((8, TMI), lambda i: (0, jnp.minimum(i, NPI - 1)))]
        + [sblk(c) for c in widths],
        out_shape=[jax.ShapeDtypeStruct((NP, c), t) for c, t in zip(widths, dtypes)]
        + [jax.ShapeDtypeStruct((8, NP), F32)]
        + [jax.ShapeDtypeStruct((TMI, c), t) for c, t in zip(widths, dtypes)],
        compiler_params=_cp(("arbitrary",)),
        name="in_proj",
    )(x, g, w, w_ift)


def _mem_proj_kernel(x_ref, g_ref, w_ref, o_ref, h_scr):
    @pl.when(pl.program_id(1) == 0)
    def _():
        h_scr[...] = _rms(x_ref[...], g_ref[...]).astype(BF16)

    o_ref[...] = _dot(h_scr[...], w_ref[...])


def _mem_proj(x, g, w):
    m, n = x.shape[0], w.shape[1]
    return pl.pallas_call(
        _mem_proj_kernel,
        grid=(m // TM, n // W),
        in_specs=[pl.BlockSpec((TM, D), lambda i, j: (i, 0)), pl.BlockSpec((1, D), lambda i, j: (0, 0)),
                  pl.BlockSpec((D, W), lambda i, j: (0, j))],
        out_specs=pl.BlockSpec((TM, W), lambda i, j: (i, j)),
        out_shape=jax.ShapeDtypeStruct((m, n), F32),
        scratch_shapes=[pltpu.VMEM((TM, D), BF16)],
        compiler_params=_cp(("arbitrary", "arbitrary")),
        name="mem_proj",
    )(x, g, w)


def _s5_prep_kernel(lr_ref, li_ref, ls_ref, btr_ref, bti_ref, cr_ref, ci_ref, d_ref,
                    ar_ref, ai_ref, a16r_ref, a16i_ref, bbr_ref, bbi_ref, e_ref, ft_ref, bd_ref):
    lr = lr_ref[0]
    li = li_ref[0]
    dt = jnp.exp(ls_ref[0])

    def power(k):
        mag = jnp.exp(lr * dt * float(k))
        ang = li * dt * float(k)
        return mag * jnp.cos(ang), mag * jnp.sin(ang)

    pw = [power(k) for k in range(TC + 1)]
    ar, ai = pw[1]
    den = lr * lr + li * li
    nr = ar - 1.0
    z_re = (nr * lr + ai * li) / den
    z_im = (ai * lr - nr * li) / den
    mask = (lax.broadcasted_iota(jnp.int32, (WU, WS), 0) // HG
            == lax.broadcasted_iota(jnp.int32, (WU, WS), 1) // P)
    btr = btr_ref[0]
    bti = bti_ref[0]
    bbr = jnp.where(mask, z_re * btr - z_im * bti, 0.0)
    bbi = jnp.where(mask, z_re * bti + z_im * btr, 0.0)
    cr = jnp.where(mask, cr_ref[0], 0.0)
    ci = jnp.where(mask, ci_ref[0], 0.0)

    ar_ref[0] = ar
    ai_ref[0] = ai
    a16r_ref[0] = pw[TC][0]
    a16i_ref[0] = pw[TC][1]
    bbr_ref[0] = bbr
    bbi_ref[0] = bbi

    def cmul(xr, xi, k):
        pr, pi = pw[k]
        return xr * pr - xi * pi, xr * pi + xi * pr

    diag = (lax.broadcasted_iota(jnp.int32, (WU, WU), 0) == lax.broadcasted_iota(jnp.int32, (WU, WU), 1))
    for s in range(TC):
        er, ei = cmul(bbr, bbi, TC - 1 - s)
        e_ref[0, s * WU:(s + 1) * WU, 0:WS] = er.astype(BF16)
        e_ref[0, s * WU:(s + 1) * WU, WS:2 * WS] = ei.astype(BF16)
        fr, fi = cmul(cr, ci, s + 1)
        ft_ref[0, s * WU:(s + 1) * WU, 0:WS] = fr.astype(BF16)
        ft_ref[0, s * WU:(s + 1) * WU, WS:2 * WS] = (-fi).astype(BF16)
        rr, ri = cmul(cr, ci, s)
        kern = _dot3_nt(bbr, rr) - _dot3_nt(bbi, ri)
        if s == 0:
            kern = kern + jnp.where(diag, d_ref[0], 0.0)
        bd_ref[0, s] = kern


def _s5_prep(lam_re, lam_im, log_step, bt_re, bt_im, c_re, c_im, d):
    def spec(*shape):
        nd = len(shape)
        return pl.BlockSpec((1,) + shape, lambda i: (i,) + (0,) * nd)

    def sds(*shape, dtype=F32):
        return jax.ShapeDtypeStruct((NGB,) + shape, dtype)

    return pl.pallas_call(
        _s5_prep_kernel,
        grid=(NGB,),
        in_specs=[spec(1, WS)] * 3 + [spec(WU, WS)] * 4 + [spec(WU, 1)],
        out_specs=[spec(1, WS)] * 4 + [spec(WU, WS)] * 2 + [spec(TC * WU, 2 * WS)] * 2 + [spec(TC, WU, WU)],
        out_shape=[sds(1, WS)] * 4 + [sds(WU, WS)] * 2 + [sds(TC * WU, 2 * WS, dtype=BF16)] * 2
        + [sds(TC, WU, WU)],
        compiler_params=_cp(("arbitrary",)),
        name="s5_prep",
    )(lam_re, lam_im, log_step, bt_re, bt_im, c_re, c_im, d)


def _s5_chunk_kernel(u_ref, bd_ref, e_ref, ft_ref, ar_ref, ai_ref, y_ref, fin_ref,
                     w_scr, lhs_scr, s_scr, xs_scr):
    rows = S5B * NCH

    @pl.when(pl.program_id(1) == 0)
    def _():
        w_scr[...] = jnp.zeros_like(w_scr)
        bd = [bd_ref[0, k].astype(BF16) for k in range(TC)]
        for s in range(TC):
            for t in range(s, TC):
                w_scr[s * WU:(s + 1) * WU, t * WU:(t + 1) * WU] = bd[t - s]

    for s in range(TC):
        lhs_scr[:, s * WU:(s + 1) * WU] = u_ref[pl.ds(s, rows, stride=TC), :].astype(BF16)
    lhs = lhs_scr[...]
    s_loc = _dot(lhs, e_ref[0])
    nl = WS // 128
    for k in range(2 * nl):
        s_scr[k] = s_loc[:, k * 128:(k + 1) * 128]
    ar = [ar_ref[0, :, k * 128:(k + 1) * 128] for k in range(nl)]
    ai = [ai_ref[0, :, k * 128:(k + 1) * 128] for k in range(nl)]
    xr = [jnp.zeros((S5B, 128), F32)] * nl
    xi = [jnp.zeros((S5B, 128), F32)] * nl
    for c in range(NCH):
        chunk_rows = pl.ds(c, S5B, stride=NCH)
        for k in range(nl):
            xs_scr[k, chunk_rows, :] = xr[k]
            xs_scr[nl + k, chunk_rows, :] = xi[k]
            sr = s_scr[k, chunk_rows, :]
            si = s_scr[nl + k, chunk_rows, :]
            xr[k], xi[k] = ar[k] * xr[k] - ai[k] * xi[k] + sr, ar[k] * xi[k] + ai[k] * xr[k] + si
    for k in range(nl):
        fin_ref[0, 0, :, k * 128:(k + 1) * 128] = xr[k]
        fin_ref[0, 0, :, WS + k * 128:WS + (k + 1) * 128] = xi[k]
    xs = jnp.concatenate([xs_scr[k] for k in range(2 * nl)], axis=1)
    y = _dot(lhs, w_scr[...]) + _dot_nt(xs.astype(BF16), ft_ref[0])
    for t in range(TC):
        y_ref[pl.ds(t, rows, stride=TC), :] = y[:, t * WU:(t + 1) * WU]


def _s5_chunk(z_f, bd, e, ft, a16r, a16i):
    rows = S5B * NCH
    nh = BATCH // S5B
    return pl.pallas_call(
        _s5_chunk_kernel,
        grid=(NGB, nh),
        in_specs=[
            pl.BlockSpec((S5B * SEQ, WU), lambda j, b: (b, j)),
            pl.BlockSpec((1, TC, WU, WU), lambda j, b: (j, 0, 0, 0)),
            pl.BlockSpec((1, TC * WU, 2 * WS), lambda j, b: (j, 0, 0)),
            pl.BlockSpec((1, TC * WU, 2 * WS), lambda j, b: (j, 0, 0)),
            pl.BlockSpec((1, 1, WS), lambda j, b: (j, 0, 0)),
            pl.BlockSpec((1, 1, WS), lambda j, b: (j, 0, 0)),
        ],
        out_specs=[
            pl.BlockSpec((S5B * SEQ, WU), lambda j, b: (b, j)),
            pl.BlockSpec((1, 1, S5B, 2 * WS), lambda j, b: (j, b, 0, 0)),
        ],
        out_shape=[jax.ShapeDtypeStruct((NP, W), F32), jax.ShapeDtypeStruct((NGB, nh, S5B, 2 * WS), F32)],
        scratch_shapes=[pltpu.VMEM((TC * WU, TC * WU), BF16), pltpu.VMEM((rows, TC * WU), BF16),
                        pltpu.VMEM((2 * WS // 128, rows, 128), F32), pltpu.VMEM((2 * WS // 128, rows, 128), F32)],
        compiler_params=_cp(("arbitrary", "arbitrary")),
        name="s5_chunk",
    )(z_f, bd, e, ft, a16r, a16i)


def _s5_step_kernel(u_ref, sr_ref, si_ref, ar_ref, ai_ref, bbr_ref, bbi_ref, cr_ref, ci_ref, d_ref,
                    y_ref, xr_ref, xi_ref):
    mask = (lax.broadcasted_iota(jnp.int32, (WU, WS), 0) // HG
            == lax.broadcasted_iota(jnp.int32, (WU, WS), 1) // P)
    for j in range(NGB):
        ul = slice(j * WU, (j + 1) * WU)
        sl = slice(j * WS, (j + 1) * WS)
        us = u_ref[:, ul]
        ar = ar_ref[j]
        ai = ai_ref[j]
        s_re = sr_ref[:, sl]
        s_im = si_ref[:, sl]
        x_re = ar * s_re - ai * s_im + _dot3(us, bbr_ref[j])
        x_im = ar * s_im + ai * s_re + _dot3(us, bbi_ref[j])
        xr_ref[:, sl] = x_re
        xi_ref[:, sl] = x_im
        cre = jnp.where(mask, cr_ref[j], 0.0)
        cim = jnp.where(mask, ci_ref[j], 0.0)
        y_ref[:, ul] = _dot3_nt(x_re, cre) - _dot3_nt(x_im, cim) + d_ref[:, ul] * us


def _s5_step(z_f, s_re, s_im, a_re, a_im, bb_re, bb_im, c_re, c_im, d_row):
    full = lambda *shape: pl.BlockSpec(shape, lambda i: (0,) * len(shape))
    return pl.pallas_call(
        _s5_step_kernel,
        grid=(1,),
        in_specs=[pl.BlockSpec((NS, W), lambda i: (0, 0)),
                  full(NS, G * P), full(NS, G * P), full(NGB, 1, WS), full(NGB, 1, WS),
                  full(NGB, WU, WS), full(NGB, WU, WS), full(NGB, WU, WS), full(NGB, WU, WS),
                  full(1, W)],
        out_specs=[full(NS, W), full(NS, G * P), full(NS, G * P)],
        out_shape=[jax.ShapeDtypeStruct((NS, W), F32), jax.ShapeDtypeStruct((NS, G * P), F32),
                   jax.ShapeDtypeStruct((NS, G * P), F32)],
        compiler_params=_cp(("arbitrary",)),
        name="s5_step",
    )(z_f, s_re, s_im, a_re, a_im, bb_re, bb_im, c_re, c_im, d_row)


def _mlstm_chunk_kernel(q_ref, k_ref, v_ref, o_ref, zif_ref, zt0_ref, zt1_ref, zt2_ref, zt3_ref,
                        brow_ref, bcol_ref, gh_ref, h_ref, c_ref, n_ref, m_ref):
    @pl.when(pl.program_id(0) == 0)
    def _():
        c_ref[...] = jnp.zeros_like(c_ref)
        n_ref[...] = jnp.zeros_like(n_ref)
        m_ref[...] = jnp.zeros_like(m_ref)

    rr = lax.broadcasted_iota(jnp.int32, (CL, CL), 0)
    cc = lax.broadcasted_iota(jnp.int32, (CL, CL), 1)
    causal = cc <= rr
    tril = jnp.where(causal, 1.0, 0.0).astype(BF16)
    triu = jnp.where(rr <= cc, 1.0, 0.0).astype(BF16)

    for b, zt_ref in enumerate((zt0_ref, zt1_ref, zt2_ref, zt3_ref)):
        zi = zif_ref[b] + brow_ref[...]
        zt = zt_ref[...] + bcol_ref[...]
        lfc = _split3(_log_sigmoid(zi))
        bcum_col = _dot(tril, lfc[0]) + (_dot(tril, lfc[1]) + _dot(tril, lfc[2]))
        lfr = _split3(_log_sigmoid(zt))
        bcum_row = _dot(lfr[0], triu) + (_dot(lfr[1], triu) + _dot(lfr[2], triu))

        for hh in range(NH):
            hs = slice(hh * DH, (hh + 1) * DH)
            bc = bcum_col[:, NH + hh:NH + hh + 1]
            ic = zi[:, hh:hh + 1]
            br = bcum_row[NH + hh:NH + hh + 1, :]
            ir = zt[hh:hh + 1, :]
            m_prev = m_ref[b, hh:hh + 1, 0:1]
            g_inter = bc + m_prev
            dlog = jnp.where(causal, (bc - br) + ir, -jnp.inf)
            m_t = jnp.maximum(g_inter, jnp.max(dlog, axis=-1, keepdims=True))
            w_inter = jnp.exp(g_inter - m_t)
            w_intra = jnp.exp(dlog - m_t)
            qb = q_ref[b, :, hs]
            kb = k_ref[b, :, hs]
            vb = v_ref[b, :, hs]
            qf = qb.astype(F32)
            kf = kb.astype(F32)
            s = _dot_nt(qb, kb) * (w_intra * (DH ** -0.5))
            c_prev = c_ref[b, hh]
            n_prev = n_ref[b, hh:hh + 1, :]
            num = _dot(s.astype(BF16), vb) + w_inter * _dot(qb, c_prev.astype(BF16))
            nq = jnp.sum(s, axis=-1, keepdims=True) + w_inter * jnp.sum(qf * n_prev, axis=-1, keepdims=True)
            h = num / jnp.maximum(jnp.abs(nq), jnp.exp(-m_t))
            m_last = m_t[CL - 1:CL, :]
            w_last = jnp.exp((bc[CL - 1:CL, :] - bc) + ic - m_last) * (DH ** -0.5)
            wi_last = w_inter[CL - 1:CL, :]
            kw = kf * w_last
            c_ref[b, hh] = wi_last * c_prev + _dot_tn(kw.astype(BF16), vb)
            n_ref[b, hh:hh + 1, :] = wi_last * n_prev + jnp.sum(kw, axis=0, keepdims=True)
            m_ref[b, hh:hh + 1, :] = jnp.broadcast_to(m_last, (1, 128))
            hn = _rms(h, gh_ref[:, hs])
            h_ref[b, :, hs] = (hn * _sigmoid(o_ref[b, :, hs])).astype(BF16)


def _mlstm_chunk(z_f, z_b, z_if, z_t, brow, bcol, gh):
    blk = lambda col: pl.BlockSpec((BATCH, CL, W), lambda c: (0, c, col))
    const = lambda *shape: pl.BlockSpec(shape, lambda c: (0,) * len(shape))
    zt_specs = [pl.BlockSpec((8, CL), functools.partial(lambda c, b: (0, b * NCL + c), b=b)) for b in range(BATCH)]
    return pl.pallas_call(
        _mlstm_chunk_kernel,
        grid=(NCL,),
        in_specs=[blk(0), blk(1), blk(2), blk(1), pl.BlockSpec((BATCH, CL, 128), lambda c: (0, c, 0))]
        + zt_specs + [const(1, 128), const(8, 1), const(1, W)],
        out_specs=[pl.BlockSpec((BATCH, CL, W), lambda c: (0, c, 0)),
                   const(BATCH, NH, DH, DH), const(BATCH, NH, DH), const(BATCH, NH, 128)],
        out_shape=[jax.ShapeDtypeStruct((BATCH, SEQ, W), BF16),
                   jax.ShapeDtypeStruct((BATCH, NH, DH, DH), F32),
                   jax.ShapeDtypeStruct((BATCH, NH, DH), F32),
                   jax.ShapeDtypeStruct((BATCH, NH, 128), F32)],
        compiler_params=_cp(("arbitrary",)),
        name="mlstm_chunk",
    )(z_b, z_b, z_b, z_f, z_if, z_t, z_t, z_t, z_t, brow, bcol, gh)


MSR = 64


def _mlstm_step_kernel(q_ref, k_ref, v_ref, o_ref, zif_ref, brow_ref, gh_ref, c0_ref, n0_ref, m0_ref,
                       h_ref, c_ref, n_ref, m_ref):
    zi = zif_ref[...] + brow_ref[...]
    ig = zi[:, 0:NH]
    g_inter = _log_sigmoid(zi[:, NH:2 * NH]) + m0_ref[...]
    m_t = jnp.maximum(g_inter, ig)
    w_inter = jnp.exp(g_inter - m_t)
    w_intra = jnp.exp(ig - m_t) * (DH ** -0.5)
    floor = jnp.exp(-m_t)
    m_ref[...] = m_t

    def heads_on_rows(x):
        return jnp.pad(x, ((0, 128 - SB), (0, 128 - NH))).T

    w_inter_t = heads_on_rows(w_inter)
    w_intra_t = heads_on_rows(w_intra)
    floor_t = heads_on_rows(floor)
    pad = jnp.zeros((128 - SB, DH), F32)
    q_t = [jnp.concatenate([q_ref[:, hh * DH:(hh + 1) * DH], pad], axis=0).T for hh in range(NH)]
    k_t = [jnp.concatenate([k_ref[:, hh * DH:(hh + 1) * DH], pad], axis=0).T for hh in range(NH)]

    def per_head_rows(ref, s):
        return jnp.concatenate([ref[s:s + 1, hh * DH:(hh + 1) * DH] for hh in range(NH)], axis=0)

    for s in range(SB):
        q4 = per_head_rows(q_ref, s)
        k4 = per_head_rows(k_ref, s)
        v4 = per_head_rows(v_ref, s)
        qc_rows = []
        for hh in range(NH):
            wi = w_inter[s:s + 1, hh:hh + 1]
            vw = v4[hh:hh + 1, :] * w_intra[s:s + 1, hh:hh + 1]
            acc = jnp.zeros((MSR, DH), F32)
            for r0 in range(0, DH, MSR):
                c_blk = c0_ref[s, hh, r0:r0 + MSR, :]
                acc = acc + q_t[hh][r0:r0 + MSR, s:s + 1] * c_blk
                c_ref[s, hh, r0:r0 + MSR, :] = wi * c_blk + k_t[hh][r0:r0 + MSR, s:s + 1] * vw
            qc_rows.append(jnp.sum(acc, axis=0, keepdims=True))
        q_c = jnp.concatenate(qc_rows, axis=0)
        wi_c = w_inter_t[0:NH, s:s + 1]
        wa_c = w_intra_t[0:NH, s:s + 1]
        n_prev = n0_ref[s]
        sv = jnp.sum(q4 * k4, axis=-1, keepdims=True) * wa_c
        num = sv * v4 + wi_c * q_c
        nq = sv + wi_c * jnp.sum(q4 * n_prev, axis=-1, keepdims=True)
        h = num / jnp.maximum(jnp.abs(nq), floor_t[0:NH, s:s + 1])
        n_ref[s] = wi_c * n_prev + wa_c * k4
        out = _rms(h, gh_ref[...]) * _sigmoid(per_head_rows(o_ref, s))
        for hh in range(NH):
            h_ref[s:s + 1, hh * DH:(hh + 1) * DH] = out[hh:hh + 1, :]


def _mlstm_step(zs, z_f, z_if, brow, gh, c0, n0, m0):
    blk = lambda col: pl.BlockSpec((SB, W), lambda i: (i, col))
    return pl.pallas_call(
        _mlstm_step_kernel,
        grid=(NS // SB,),
        in_specs=[blk(0), blk(1), blk(2), blk(1),
                  pl.BlockSpec((SB, 128), lambda i: (i, 0)),
                  pl.BlockSpec((1, 128), lambda i: (0, 0)),
                  pl.BlockSpec((NH, DH), lambda i: (0, 0)),
                  pl.BlockSpec((SB, NH, DH, DH), lambda i: (i, 0, 0, 0)),
                  pl.BlockSpec((SB, NH, DH), lambda i: (i, 0, 0)),
                  pl.BlockSpec((SB, NH), lambda i: (i, 0))],
        out_specs=[pl.BlockSpec((SB, W), lambda i: (i, 0)),
                   pl.BlockSpec((SB, NH, DH, DH), lambda i: (i, 0, 0, 0)),
                   pl.BlockSpec((SB, NH, DH), lambda i: (i, 0, 0)),
                   pl.BlockSpec((SB, NH), lambda i: (i, 0))],
        out_shape=[jax.ShapeDtypeStruct((NS, W), F32),
                   jax.ShapeDtypeStruct((NS, NH, DH, DH), F32),
                   jax.ShapeDtypeStruct((NS, NH, DH), F32),
                   jax.ShapeDtypeStruct((NS, NH), F32)],
        compiler_params=_cp(("arbitrary",)),
        name="mlstm_step",
    )(zs, zs, zs, z_f, z_if, brow, gh, c0, n0, m0)


def _softmax_rows(s):
    e = jnp.exp(s - jnp.max(s, axis=-1, keepdims=True))
    return e / jnp.sum(e, axis=-1, keepdims=True)


def _xattn_prompt_kernel(q_ref, k_ref, v_ref, o_ref):
    for hh in range(NH):
        hs = slice(hh * DH, (hh + 1) * DH)
        s = _dot_nt(q_ref[:, hs], k_ref[:, hs].astype(BF16)) * (DH ** -0.5)
        p = _softmax_rows(s)
        o_ref[:, hs] = _dot(p.astype(BF16), v_ref[:, hs].astype(BF16)).astype(BF16)


def _xattn_prompt(z_b, kv):
    nt = SEQ // TM
    return pl.pallas_call(
        _xattn_prompt_kernel,
        grid=(BATCH, nt),
        in_specs=[pl.BlockSpec((TM, W), lambda b, t: (b * nt + t, 3)),
                  pl.BlockSpec((N_MEM, W), lambda b, t: (b, 0)),
                  pl.BlockSpec((N_MEM, W), lambda b, t: (b, 1))],
        out_specs=pl.BlockSpec((TM, W), lambda b, t: (b * nt + t, 0)),
        out_shape=jax.ShapeDtypeStruct((NP, W), BF16),
        compiler_params=_cp(("arbitrary", "arbitrary")),
        name="xattn_prompt",
    )(z_b, kv, kv)


XS = 4


XMC = 64


def _xattn_step_kernel(q_ref, k_ref, v_ref, o_ref):
    def part(t):
        for s in range(XS):
            r = t * XS + s
            q4 = jnp.concatenate([q_ref[r:r + 1, hh * DH:(hh + 1) * DH] for hh in range(NH)], axis=0)
            sc = jnp.concatenate(
                [jnp.sum(k_ref[s, m0:m0 + XMC] * q4[None], axis=-1, keepdims=True) for m0 in range(0, N_MEM, XMC)],
                axis=0) * (DH ** -0.5)
            e = jnp.exp(sc - jnp.max(sc, axis=0, keepdims=True))
            p = e / jnp.sum(e, axis=0, keepdims=True)
            acc = jnp.zeros((NH, DH), F32)
            for m0 in range(0, N_MEM, XMC):
                acc = acc + jnp.sum(p[m0:m0 + XMC] * v_ref[s, m0:m0 + XMC], axis=0)
            for hh in range(NH):
                o_ref[r:r + 1, hh * DH:(hh + 1) * DH] = acc[hh:hh + 1, :]

    for t in range(SB // XS):
        pl.when(pl.program_id(1) == t)(functools.partial(part, t))


def _xattn_step(zs, mem_k, mem_v):
    nt = SB // XS
    return pl.pallas_call(
        _xattn_step_kernel,
        grid=(NS // SB, nt),
        in_specs=[pl.BlockSpec((SB, W), lambda i, t: (i, 3)),
                  pl.BlockSpec((XS, N_MEM, NH, DH), lambda i, t: (i * nt + t, 0, 0, 0)),
                  pl.BlockSpec((XS, N_MEM, NH, DH), lambda i, t: (i * nt + t, 0, 0, 0))],
        out_specs=pl.BlockSpec((SB, W), lambda i, t: (i, 0)),
        out_shape=jax.ShapeDtypeStruct((NS, W), F32),
        compiler_params=_cp(("arbitrary", "arbitrary")),
        name="xattn_step",
    )(zs, mem_k, mem_v)


TMX = 512


def _mix_kernel(x_ref, g_ref, yp_ref, ys_ref, mp_ref, ms_ref, ap_ref, as_ref, wglu_ref,
                wg0_ref, wg1_ref, wg2_ref, wb0_ref, wb1_ref, wb2_ref, o_ref, h_scr, s5_scr):
    def tile(rows, y_ref, ml_ref, xa_ref):
        rs = slice(0, rows)

        @pl.when(pl.program_id(1) == 0)
        def _():
            h_scr[rs, :] = _rms(x_ref[rs, :], g_ref[...]).astype(BF16)
            y = _gelu_tanh(y_ref[...])
            s5_scr[rs, :] = (y * _sigmoid(_dot(y.astype(BF16), wglu_ref[...]))).astype(BF16)

        h = h_scr[rs, :]
        merged = (_sigmoid(_dot(h, wg0_ref[...])) * _dot(s5_scr[rs, :], wb0_ref[...])
                  + _sigmoid(_dot(h, wg1_ref[...])) * _dot(ml_ref[...].astype(BF16), wb1_ref[...])
                  + _sigmoid(_dot(h, wg2_ref[...])) * _dot(xa_ref[...].astype(BF16), wb2_ref[...]))
        o_ref[rs, :] = merged.astype(BF16)
        if rows < TM:
            o_ref[rows:, :] = jnp.zeros((TM - rows, TMX), BF16)

    pl.when(pl.program_id(0) < NPT)(functools.partial(tile, TM, yp_ref, mp_ref, ap_ref))
    pl.when(pl.program_id(0) == NPT)(functools.partial(tile, NS, ys_ref, ms_ref, as_ref))


def _mix(x1, g, yp, ys, mp, ms, ap, as_, wglu, wg, wb):
    prow = pl.BlockSpec((TM, W), lambda i, j: (jnp.minimum(i, NPT - 1), 0))
    srow = pl.BlockSpec((NS, W), lambda i, j: (0, 0))
    nx = D // TMX
    wgs = [pl.BlockSpec((D, TMX), functools.partial(lambda i, j, b: (0, b * nx + j), b=b)) for b in range(3)]
    wbs = pl.BlockSpec((W, TMX), lambda i, j: (0, j))
    return pl.pallas_call(
        _mix_kernel,
        grid=(NT, D // TMX),
        in_specs=[pl.BlockSpec((TM, D), lambda i, j: (i, 0)), pl.BlockSpec((1, D), lambda i, j: (0, 0)),
                  prow, srow, prow, srow, prow, srow,
                  pl.BlockSpec((W, W), lambda i, j: (0, 0)),
                  *wgs, wbs, wbs, wbs],
        out_specs=pl.BlockSpec((TM, TMX), lambda i, j: (i, j)),
        out_shape=jax.ShapeDtypeStruct((MROWS, D), BF16),
        scratch_shapes=[pltpu.VMEM((TM, D), BF16), pltpu.VMEM((TM, W), BF16)],
        compiler_params=_cp(("arbitrary", "arbitrary")),
        name="mix",
    )(x1, g, yp, ys, mp, ms, ap, as_, wglu, wg, wg, wg, *wb)


def _outproj_kernel(x_ref, m_ref, w_ref, o_ref):
    def tile(rows):
        rs = slice(0, rows)
        o_ref[rs, :] = x_ref[rs, :] + _dot(m_ref[rs, :], w_ref[...])
        if rows < TM:
            o_ref[rows:, :] = jnp.zeros((TM - rows, D), F32)

    pl.when(pl.program_id(0) < NPT)(functools.partial(tile, TM))
    pl.when(pl.program_id(0) == NPT)(functools.partial(tile, NS))


def _outproj(x1, merged, w_out):
    return pl.pallas_call(
        _outproj_kernel,
        grid=(NT,),
        in_specs=[pl.BlockSpec((TM, D), lambda i: (i, 0)), pl.BlockSpec((TM, D), lambda i: (i, 0)),
                  pl.BlockSpec((D, D), lambda i: (0, 0))],
        out_specs=pl.BlockSpec((TM, D), lambda i: (i, 0)),
        out_shape=jax.ShapeDtypeStruct((MROWS, D), F32),
        compiler_params=_cp(("arbitrary",)),
        name="out_proj",
    )(x1, merged, w_out)


def kernel(x_prompt, x_sample, mem_prompt, cache_mem_k, cache_mem_v, state_s5_re, state_s5_im, state_mlstm_C,
           state_mlstm_n, state_mlstm_m, g_ffn1, w1_gate, w1_up, w1_down, g_mix, w_in, s5_lambda_re,
           s5_lambda_im, s5_log_step, s5_b_re, s5_b_im, s5_c_re, s5_c_im, s5_d, w_s5_glu, b_igate, b_fgate,
           g_mlstm_head, g_mem, w_mem_k, w_mem_v, w_br_s5, w_br_ml, w_br_xa, w_out, g_ffn2, w2_gate, w2_up,
           w2_down, g_final):
    bf = lambda a: a.astype(BF16)

    x1, w2g, w2u, w2d, w_head, w_bg = _ffn(
        (x_prompt.reshape(NP, D), x_sample.reshape(NS, D)), g_ffn1[0].reshape(1, D),
        bf(w1_gate[0]), bf(w1_up[0]), bf(w1_down[0]),
        casts=((w2_gate, 16, None), (w2_up, 16, None), (w2_down, 32, None),
               (w_in, 16, ((0, HEAD), (MIX0, 3 * D)))))

    w_ift = bf(w_in[0, :, GATE0:GATE0 + 2 * NH].T)
    w_br = [bf(w_br_s5[0]), bf(w_br_ml[0]), bf(w_br_xa[0])]
    brow = jnp.pad(jnp.concatenate([b_igate[0], b_fgate[0]]), (0, 128 - 2 * NH)).reshape(1, 128)
    bcol = jnp.concatenate([b_igate[0], b_fgate[0]]).reshape(2 * NH, 1)
    gh = g_mlstm_head[0].reshape(1, W)

    z_f, z_b, z_if, z_t, zs_f, zs_b, zs_if = _in_proj(x1, g_mix[0].reshape(1, D), w_head, w_ift)
    zs = zs_b[:NS].astype(F32)
    per_seq = lambda a: a.reshape(BATCH, SEQ, a.shape[1])

    blk = lambda a: jnp.tile(a.reshape(NGB, WU, P), (1, 1, GB))
    row = lambda a: a.reshape(NGB, 1, WS)
    (a_re, a_im, a16_re, a16_im, bb_re, bb_im, e_op, ft_op, bd_op) = _s5_prep(
        row(s5_lambda_re[0]), row(s5_lambda_im[0]), row(jnp.repeat(s5_log_step[0], P)),
        blk(s5_b_re[0].transpose(0, 2, 1)), blk(s5_b_im[0].transpose(0, 2, 1)),
        blk(s5_c_re[0]), blk(s5_c_im[0]), s5_d[0].reshape(NGB, WU, 1))
    y_s5_p, fin = _s5_chunk(z_f, bd_op, e_op, ft_op, a16_re, a16_im)
    fin = fin.reshape(NGB, BATCH, 2, GB, P).transpose(2, 1, 0, 3, 4).reshape(2, 1, BATCH, G, P)
    y_s5_s, s5_re_s, s5_im_s = _s5_step(
        zs_f, state_s5_re[0].reshape(NS, G * P), state_s5_im[0].reshape(NS, G * P),
        a_re, a_im, bb_re, bb_im, blk(s5_c_re[0]), blk(s5_c_im[0]), s5_d[0].reshape(1, W))

    ml_p, c_p, n_p, m_p = _mlstm_chunk(per_seq(z_f), per_seq(z_b), per_seq(z_if), z_t, brow, bcol, gh)
    ml_s, c_s, n_s, m_s = _mlstm_step(zs, zs_f, zs_if, brow, gh.reshape(NH, DH), state_mlstm_C[0],
                                      state_mlstm_n[0], state_mlstm_m[0])

    w_kv = bf(jnp.concatenate([w_mem_k[0], w_mem_v[0]], axis=1))
    kv = _mem_proj(mem_prompt.reshape(BATCH * N_MEM, D), g_mem[0].reshape(1, D), w_kv)
    xa_p = _xattn_prompt(z_b, kv)
    xa_s = _xattn_step(zs, cache_mem_k[0], cache_mem_v[0])

    merged = _mix(x1, g_mix[0].reshape(1, D), y_s5_p, y_s5_s, ml_p.reshape(NP, W), ml_s, xa_p, xa_s,
                  bf(w_s5_glu[0]), w_bg, w_br)
    x2 = _outproj(x1, merged, bf(w_out[0]))
    y_p, y_s = _ffn((x2,), g_ffn2[0].reshape(1, D), w2g, w2u, w2d, g_final.reshape(1, D))

    return (y_p.reshape(BATCH, SEQ, D), y_s.reshape(NS, 1, D),
            kv[:, :W].reshape(1, BATCH, N_MEM, NH, DH), kv[:, W:].reshape(1, BATCH, N_MEM, NH, DH),
            fin[0], fin[1], c_p[None], n_p[None], m_p[:, :, 0][None],
            s5_re_s.reshape(1, NS, G, P), s5_im_s.reshape(1, NS, G, P), c_s[None], n_s[None], m_s[None])
```

```python
import functools
from typing import Callable, NamedTuple

import jax
import jax.numpy as jnp
from jax import lax
from jax.experimental import pallas as pl
from jax.experimental.pallas import tpu as pltpu

F32 = jnp.float32
BF16 = jnp.bfloat16

D = 2048
BATCH = 4
SEQ = 2048
NS = 128
NP = BATCH * SEQ
TM = 512
NPT = NP // TM
NT = NPT + 1
MROWS = NT * TM
N_MEM = 256
FF = 5504
TF = 512
G = 64
P = 64
HG = 16
TC = 16
GB = 8
NGB = G // GB
WU = GB * HG
WS = GB * P
NCH = SEQ // TC
S5B = 2
W = 1024
NH = 4
DH = 256
CL = 256
NCL = SEQ // CL
SB = 8
EPS = 1e-6
VMEM_LIMIT = 56 * 1024 * 1024


def _cp(sem, vmem=VMEM_LIMIT):
    return pltpu.CompilerParams(dimension_semantics=sem, vmem_limit_bytes=vmem)


def _dot(a, b):
    return jnp.dot(a, b, preferred_element_type=F32)


def _dot_nt(a, b):
    return lax.dot_general(a, b, (((1,), (1,)), ((), ())), preferred_element_type=F32)


def _dot_tn(a, b):
    return lax.dot_general(a, b, (((0,), (0,)), ((), ())), preferred_element_type=F32)


def _hi_lo(x):
    hi = x.astype(BF16)
    lo = (x - hi.astype(F32)).astype(BF16)
    return hi, lo


def _split3(x):
    hi = x.astype(BF16)
    r1 = x - hi.astype(F32)
    mid = r1.astype(BF16)
    lo = (r1 - mid.astype(F32)).astype(BF16)
    return hi, mid, lo


def _dot3(a, b):
    ah, al = _hi_lo(a)
    bh, bl = _hi_lo(b)
    return _dot(ah, bh) + (_dot(ah, bl) + _dot(al, bh))


def _dot3_nt(a, b):
    ah, al = _hi_lo(a)
    bh, bl = _hi_lo(b)
    return _dot_nt(ah, bh) + (_dot_nt(ah, bl) + _dot_nt(al, bh))


def _rms(x, g):
    r = lax.rsqrt(jnp.mean(x * x, axis=-1, keepdims=True) + EPS)
    return (x * r) * g


def _sigmoid(x):
    return 1.0 / (1.0 + jnp.exp(-x))


def _log_sigmoid(x):
    return jnp.minimum(x, 0.0) - jnp.log1p(jnp.exp(-jnp.abs(x)))


def _gelu_tanh(x):
    return x * (0.5 * (1.0 + jnp.tanh(0.7978845608028654 * (x + 0.044715 * (x * x * x)))))


NJ = pl.cdiv(FF, TF)
TF_LAST = FF - (NJ - 1) * TF
assert TF_LAST % 128 == 0


def _ffn_kernel(*refs, two_src, final_norm, side_jobs):
    refs = list(refs)
    if two_src:
        xp_ref, xs_ref = refs[:2]
        refs = refs[2:]
    else:
        xp_ref = xs_ref = refs[0]
        refs = refs[1:]
    g_ref, wg_ref, wu_ref, wd_ref = refs[:4]
    refs = refs[4:]
    n_in = sum(n for n, _, _ in side_jobs)
    n_out = sum(n for _, n, _ in side_jobs)
    if final_norm:
        gf_ref = refs[0]
        refs = refs[1:]
    side_in = refs[:n_in]
    refs = refs[n_in:]
    if final_norm:
        op_ref, os_ref = refs[:2]
        refs = refs[2:]
    else:
        o_ref = refs[0]
        refs = refs[1:]
    side_out = refs[:n_out]
    h_scr, acc_scr = refs[n_out:]
    j = pl.program_id(1)

    def side_work():
        a = b = 0
        for ni, no, fn in side_jobs:
            fn(side_in[a:a + ni], side_out[b:b + no])
            a += ni
            b += no

    def tile(rows, x_ref):
        rs = slice(0, rows)

        @pl.when(j == 0)
        def _():
            h_scr[rs, :] = _rms(x_ref[rs, :], g_ref[...]).astype(BF16)
            acc_scr[rs, :] = jnp.zeros((rows, D), F32)

        def accumulate(width):
            side_work()
            h = h_scr[rs, :]
            gt = _dot(h, wg_ref[:, 0:width])
            hid = (gt * _sigmoid(gt)) * _dot(h, wu_ref[:, 0:width])
            acc_scr[rs, :] += _dot(hid.astype(BF16), wd_ref[0:width, :])

        pl.when(j < NJ - 1)(functools.partial(accumulate, TF))

        @pl.when(j == NJ - 1)
        def _():
            accumulate(TF_LAST)
            y = x_ref[rs, :] + 0.5 * acc_scr[rs, :]
            if final_norm:
                y = _rms(y, gf_ref[...])
                (op_ref if rows == TM else os_ref)[...] = y
            else:
                o_ref[rs, :] = y
                if rows < TM:
                    o_ref[rows:, :] = jnp.zeros((TM - rows, D), F32)

    pl.when(pl.program_id(0) < NPT)(functools.partial(tile, TM, xp_ref))
    pl.when(pl.program_id(0) == NPT)(functools.partial(tile, NS, xs_ref))


class _SideJob(NamedTuple):
    blocks: int
    args: list
    in_specs: list
    out_shape: list
    out_specs: list
    fn: Callable


def _step_block(blocks, first=0):
    return lambda i, j: first + jnp.minimum(i * NJ + j, blocks - 1)


def _cast_job(a, rb):
    rows_a, cols_a = a.shape[-2:]
    blk = _step_block(rows_a // rb)

    def fn(ins, outs):
        outs[0][...] = ins[0][...].astype(BF16)

    return _SideJob(rows_a // rb, [a], [pl.BlockSpec((None, rb, cols_a), lambda i, j: (0, blk(i, j), 0))],
                    [jax.ShapeDtypeStruct((rows_a, cols_a), BF16)], [pl.BlockSpec((rb, cols_a), lambda i, j: (blk(i, j), 0))],
                    fn)


def _transpose_job(wt, row0, cols):
    k = wt.shape[1]
    shift = row0 % 128
    assert shift in (0, 8) and cols % 128 == 0
    nblk = cols // 128
    blk = _step_block(nblk, row0 // 128)
    last = pl.cdiv(wt.shape[0], 128) - 1
    specs = [pl.BlockSpec((128, k), lambda i, j: (blk(i, j), 0))]
    if shift:
        specs.append(pl.BlockSpec((128, k), lambda i, j: (jnp.minimum(blk(i, j) + 1, last), 0)))

    def fn(ins, outs):
        win = ins[0][...]
        if shift:
            win = jnp.concatenate([win[shift:, :], ins[1][0:shift, :]], axis=0)
        outs[0][...] = win.T.astype(BF16)

    out_blk = _step_block(nblk)
    return _SideJob(nblk, [wt] * len(specs), specs, [jax.ShapeDtypeStruct((k, cols), BF16)],
                    [pl.BlockSpec((k, 128), lambda i, j: (0, out_blk(i, j)))], fn)


def _ffn(xs, g, wg, wu, wd, g_final=None, side_jobs=()):
    two_src = len(xs) == 2
    final_norm = g_final is not None
    row = pl.BlockSpec((TM, D), lambda i, j: (i, 0))
    prow = pl.BlockSpec((TM, D), lambda i, j: (jnp.minimum(i, NPT - 1), 0))
    srow = pl.BlockSpec((NS, D), lambda i, j: (0, 0))
    vec = pl.BlockSpec((1, D), lambda i, j: (0, 0))
    in_specs = ([prow, srow] if two_src else [row]) + [
        vec,
        pl.BlockSpec((D, TF), lambda i, j: (0, j)),
        pl.BlockSpec((D, TF), lambda i, j: (0, j)),
        pl.BlockSpec((TF, D), lambda i, j: (j, 0)),
    ]
    args = list(xs) + [g, wg, wu, wd]
    if final_norm:
        in_specs.append(vec)
        args.append(g_final)
        out_shape = [jax.ShapeDtypeStruct((NP, D), F32), jax.ShapeDtypeStruct((NS, D), F32)]
        out_specs = [prow, srow]
    else:
        out_shape = [jax.ShapeDtypeStruct((MROWS, D), F32)]
        out_specs = [row]
    jobs = []
    for job in side_jobs:
        assert job.blocks <= NPT * NJ, "side jobs must fit under the full-size row tiles"
        in_specs += job.in_specs
        args += job.args
        out_specs += job.out_specs
        out_shape += job.out_shape
        jobs.append((len(job.args), len(job.out_shape), job.fn))
    return pl.pallas_call(
        functools.partial(_ffn_kernel, two_src=two_src, final_norm=final_norm, side_jobs=tuple(jobs)),
        grid=(NT, NJ),
        in_specs=in_specs,
        out_specs=out_specs,
        out_shape=out_shape,
        scratch_shapes=[pltpu.VMEM((TM, D), BF16), pltpu.VMEM((TM, D), F32)],
        compiler_params=_cp(("arbitrary", "arbitrary")),
        name="ffn_final" if final_norm else "ffn",
    )(*args)


NF32 = 2
TMI = 256
NPI = NP // TMI
GATE0 = 5 * W
HEAD = GATE0 + W + 128
MIX0 = GATE0 + 2 * NH + W


def _in_proj_kernel(x_ref, g_ref, w_ref, wift_ref,
                    ofp_ref, obp_ref, zifp_ref, ztp_ref, ofs_ref, obs_ref, zifs_ref):
    def emit(of_ref, ob_ref, zif_ref, zt_ref):
        h = _rms(x_ref[...], g_ref[...]).astype(BF16)
        if zt_ref is not None:
            zt_ref[...] = _dot_nt(wift_ref[...], h)
        of_ref[:, 0:W] = _dot(h, w_ref[:, 0:W])
        for j in range(1, 4):
            ob_ref[:, (j - 1) * W:j * W] = _dot(h, w_ref[:, j * W:(j + 1) * W]).astype(BF16)
        of_ref[:, W:2 * W] = _dot(h, w_ref[:, 4 * W:5 * W])
        tail = _dot(h, w_ref[:, 5 * W:HEAD])
        zif_ref[...] = tail[:, 0:128]
        ob_ref[:, 3 * W:4 * W] = tail[:, 2 * NH:2 * NH + W].astype(BF16)

    pl.when(pl.program_id(0) < NPI)(functools.partial(emit, ofp_ref, obp_ref, zifp_ref, ztp_ref))
    pl.when(pl.program_id(0) == NPI)(functools.partial(emit, ofs_ref, obs_ref, zifs_ref, None))


def _in_proj(x, g, w, w_ift):
    once = pl.Buffered(1)
    pblk = lambda cols: pl.BlockSpec((TMI, cols), lambda i: (jnp.minimum(i, NPI - 1), 0))
    sblk = lambda cols: pl.BlockSpec((TMI, cols), lambda i: (0, 0))
    widths = (NF32 * W, 4 * W, 128)
    dtypes = (F32, BF16, F32)
    return pl.pallas_call(
        _in_proj_kernel,
        grid=(NPI + 1,),
        in_specs=[
            pl.BlockSpec((TMI, D), lambda i: (i, 0)),
            pl.BlockSpec((1, D), lambda i: (0, 0)),
            pl.BlockSpec((D, HEAD), lambda i: (0, 0), pipeline_mode=once),
            pl.BlockSpec((8, D), lambda i: (0, 0), pipeline_mode=once),
        ],
        out_specs=[pblk(c) for c in widths]
        + [pl.BlockSpec((8, TMI), lambda i: (0, jnp.minimum(i, NPI - 1)))]
        + [sblk(c) for c in widths],
        out_shape=[jax.ShapeDtypeStruct((NP, c), t) for c, t in zip(widths, dtypes)]
        + [jax.ShapeDtypeStruct((8, NP), F32)]
        + [jax.ShapeDtypeStruct((TMI, c), t) for c, t in zip(widths, dtypes)],
        compiler_params=_cp(("arbitrary",)),
        name="in_proj",
    )(x, g, w, w_ift)


def _mem_proj_kernel(x_ref, g_ref, w_ref, o_ref, h_scr):
    @pl.when(pl.program_id(1) == 0)
    def _():
        h_scr[...] = _rms(x_ref[...], g_ref[...]).astype(BF16)

    o_ref[...] = _dot(h_scr[...], w_ref[...])


def _mem_proj(x, g, w):
    m, n = x.shape[0], w.shape[1]
    return pl.pallas_call(
        _mem_proj_kernel,
        grid=(m // TM, n // W),
        in_specs=[pl.BlockSpec((TM, D), lambda i, j: (i, 0)), pl.BlockSpec((1, D), lambda i, j: (0, 0)),
                  pl.BlockSpec((D, W), lambda i, j: (0, j))],
        out_specs=pl.BlockSpec((TM, W), lambda i, j: (i, j)),
        out_shape=jax.ShapeDtypeStruct((m, n), F32),
        scratch_shapes=[pltpu.VMEM((TM, D), BF16)],
        compiler_params=_cp(("arbitrary", "arbitrary")),
        name="mem_proj",
    )(x, g, w)


def _s5_prep_kernel(lr_ref, li_ref, ls_ref, btr_ref, bti_ref, cr_ref, ci_ref, d_ref,
                    ar_ref, ai_ref, a16r_ref, a16i_ref, bbr_ref, bbi_ref, e_ref, ft_ref, bd_ref):
    lr = lr_ref[0]
    li = li_ref[0]
    dt = jnp.exp(ls_ref[0])

    def power(k):
        mag = jnp.exp(lr * dt * float(k))
        ang = li * dt * float(k)
        return mag * jnp.cos(ang), mag * jnp.sin(ang)

    pw = [power(k) for k in range(TC + 1)]
    ar, ai = pw[1]
    den = lr * lr + li * li
    nr = ar - 1.0
    z_re = (nr * lr + ai * li) / den
    z_im = (ai * lr - nr * li) / den
    mask = (lax.broadcasted_iota(jnp.int32, (WU, WS), 0) // HG
            == lax.broadcasted_iota(jnp.int32, (WU, WS), 1) // P)
    btr = btr_ref[0]
    bti = bti_ref[0]
    bbr = jnp.where(mask, z_re * btr - z_im * bti, 0.0)
    bbi = jnp.where(mask, z_re * bti + z_im * btr, 0.0)
    cr = jnp.where(mask, cr_ref[0], 0.0)
    ci = jnp.where(mask, ci_ref[0], 0.0)

    ar_ref[0] = ar
    ai_ref[0] = ai
    a16r_ref[0] = pw[TC][0]
    a16i_ref[0] = pw[TC][1]
    bbr_ref[0] = bbr
    bbi_ref[0] = bbi

    def cmul(xr, xi, k):
        pr, pi = pw[k]
        return xr * pr - xi * pi, xr * pi + xi * pr

    diag = (lax.broadcasted_iota(jnp.int32, (WU, WU), 0) == lax.broadcasted_iota(jnp.int32, (WU, WU), 1))
    for s in range(TC):
        er, ei = cmul(bbr, bbi, TC - 1 - s)
        e_ref[0, s * WU:(s + 1) * WU, 0:WS] = er.astype(BF16)
        e_ref[0, s * WU:(s + 1) * WU, WS:2 * WS] = ei.astype(BF16)
        fr, fi = cmul(cr, ci, s + 1)
        ft_ref[0, s * WU:(s + 1) * WU, 0:WS] = fr.astype(BF16)
        ft_ref[0, s * WU:(s + 1) * WU, WS:2 * WS] = (-fi).astype(BF16)
        rr, ri = cmul(cr, ci, s)
        kern = _dot3_nt(bbr, rr) - _dot3_nt(bbi, ri)
        if s == 0:
            kern = kern + jnp.where(diag, d_ref[0], 0.0)
        bd_ref[0, s] = kern


def _s5_prep(lam_re, lam_im, log_step, bt_re, bt_im, c_re, c_im, d):
    def spec(*shape):
        nd = len(shape)
        return pl.BlockSpec((1,) + shape, lambda i: (i,) + (0,) * nd)

    def sds(*shape, dtype=F32):
        return jax.ShapeDtypeStruct((NGB,) + shape, dtype)

    return pl.pallas_call(
        _s5_prep_kernel,
        grid=(NGB,),
        in_specs=[spec(1, WS)] * 3 + [spec(WU, WS)] * 4 + [spec(WU, 1)],
        out_specs=[spec(1, WS)] * 4 + [spec(WU, WS)] * 2 + [spec(TC * WU, 2 * WS)] * 2 + [spec(TC, WU, WU)],
        out_shape=[sds(1, WS)] * 4 + [sds(WU, WS)] * 2 + [sds(TC * WU, 2 * WS, dtype=BF16)] * 2
        + [sds(TC, WU, WU)],
        compiler_params=_cp(("arbitrary",)),
        name="s5_prep",
    )(lam_re, lam_im, log_step, bt_re, bt_im, c_re, c_im, d)


def _s5_chunk_kernel(u_ref, bd_ref, e_ref, ft_ref, ar_ref, ai_ref, y_ref, fin_ref,
                     w_scr, lhs_scr, s_scr, xs_scr):
    rows = S5B * NCH

    @pl.when(pl.program_id(1) == 0)
    def _():
        w_scr[...] = jnp.zeros_like(w_scr)
        bd = [bd_ref[0, k].astype(BF16) for k in range(TC)]
        for s in range(TC):
            for t in range(s, TC):
                w_scr[s * WU:(s + 1) * WU, t * WU:(t + 1) * WU] = bd[t - s]

    for s in range(TC):
        lhs_scr[:, s * WU:(s + 1) * WU] = u_ref[pl.ds(s, rows, stride=TC), :].astype(BF16)
    lhs = lhs_scr[...]
    s_loc = _dot(lhs, e_ref[0])
    nl = WS // 128
    for k in range(2 * nl):
        s_scr[k] = s_loc[:, k * 128:(k + 1) * 128]
    ar = [ar_ref[0, :, k * 128:(k + 1) * 128] for k in range(nl)]
    ai = [ai_ref[0, :, k * 128:(k + 1) * 128] for k in range(nl)]
    xr = [jnp.zeros((S5B, 128), F32)] * nl
    xi = [jnp.zeros((S5B, 128), F32)] * nl
    for c in range(NCH):
        chunk_rows = pl.ds(c, S5B, stride=NCH)
        for k in range(nl):
            xs_scr[k, chunk_rows, :] = xr[k]
            xs_scr[nl + k, chunk_rows, :] = xi[k]
            sr = s_scr[k, chunk_rows, :]
            si = s_scr[nl + k, chunk_rows, :]
            xr[k], xi[k] = ar[k] * xr[k] - ai[k] * xi[k] + sr, ar[k] * xi[k] + ai[k] * xr[k] + si
    for k in range(nl):
        fin_ref[0, 0, :, k * 128:(k + 1) * 128] = xr[k]
        fin_ref[0, 0, :, WS + k * 128:WS + (k + 1) * 128] = xi[k]
    xs = jnp.concatenate([xs_scr[k] for k in range(2 * nl)], axis=1)
    y = _dot(lhs, w_scr[...]) + _dot_nt(xs.astype(BF16), ft_ref[0])
    for t in range(TC):
        y_ref[pl.ds(t, rows, stride=TC), :] = y[:, t * WU:(t + 1) * WU]


def _s5_chunk(z_f, bd, e, ft, a16r, a16i):
    rows = S5B * NCH
    nh = BATCH // S5B
    return pl.pallas_call(
        _s5_chunk_kernel,
        grid=(NGB, nh),
        in_specs=[
            pl.BlockSpec((S5B * SEQ, WU), lambda j, b: (b, j)),
            pl.BlockSpec((1, TC, WU, WU), lambda j, b: (j, 0, 0, 0)),
            pl.BlockSpec((1, TC * WU, 2 * WS), lambda j, b: (j, 0, 0)),
            pl.BlockSpec((1, TC * WU, 2 * WS), lambda j, b: (j, 0, 0)),
            pl.BlockSpec((1, 1, WS), lambda j, b: (j, 0, 0)),
            pl.BlockSpec((1, 1, WS), lambda j, b: (j, 0, 0)),
        ],
        out_specs=[
            pl.BlockSpec((S5B * SEQ, WU), lambda j, b: (b, j)),
            pl.BlockSpec((1, 1, S5B, 2 * WS), lambda j, b: (j, b, 0, 0)),
        ],
        out_shape=[jax.ShapeDtypeStruct((NP, W), F32), jax.ShapeDtypeStruct((NGB, nh, S5B, 2 * WS), F32)],
        scratch_shapes=[pltpu.VMEM((TC * WU, TC * WU), BF16), pltpu.VMEM((rows, TC * WU), BF16),
                        pltpu.VMEM((2 * WS // 128, rows, 128), F32), pltpu.VMEM((2 * WS // 128, rows, 128), F32)],
        compiler_params=_cp(("arbitrary", "arbitrary")),
        name="s5_chunk",
    )(z_f, bd, e, ft, a16r, a16i)


def _s5_step_kernel(u_ref, sr_ref, si_ref, ar_ref, ai_ref, bbr_ref, bbi_ref, cr_ref, ci_ref, d_ref,
                    y_ref, xr_ref, xi_ref):
    mask = (lax.broadcasted_iota(jnp.int32, (WU, WS), 0) // HG
            == lax.broadcasted_iota(jnp.int32, (WU, WS), 1) // P)
    for j in range(NGB):
        ul = slice(j * WU, (j + 1) * WU)
        sl = slice(j * WS, (j + 1) * WS)
        us = u_ref[:, ul]
        ar = ar_ref[j]
        ai = ai_ref[j]
        s_re = sr_ref[:, sl]
        s_im = si_ref[:, sl]
        x_re = ar * s_re - ai * s_im + _dot3(us, bbr_ref[j])
        x_im = ar * s_im + ai * s_re + _dot3(us, bbi_ref[j])
        xr_ref[:, sl] = x_re
        xi_ref[:, sl] = x_im
        cre = jnp.where(mask, cr_ref[j], 0.0)
        cim = jnp.where(mask, ci_ref[j], 0.0)
        y_ref[:, ul] = _dot3_nt(x_re, cre) - _dot3_nt(x_im, cim) + d_ref[:, ul] * us


def _s5_step(z_f, s_re, s_im, a_re, a_im, bb_re, bb_im, c_re, c_im, d_row):
    full = lambda *shape: pl.BlockSpec(shape, lambda i: (0,) * len(shape))
    return pl.pallas_call(
        _s5_step_kernel,
        grid=(1,),
        in_specs=[pl.BlockSpec((NS, W), lambda i: (0, 0)),
                  full(NS, G * P), full(NS, G * P), full(NGB, 1, WS), full(NGB, 1, WS),
                  full(NGB, WU, WS), full(NGB, WU, WS), full(NGB, WU, WS), full(NGB, WU, WS),
                  full(1, W)],
        out_specs=[full(NS, W), full(NS, G * P), full(NS, G * P)],
        out_shape=[jax.ShapeDtypeStruct((NS, W), F32), jax.ShapeDtypeStruct((NS, G * P), F32),
                   jax.ShapeDtypeStruct((NS, G * P), F32)],
        compiler_params=_cp(("arbitrary",)),
        name="s5_step",
    )(z_f, s_re, s_im, a_re, a_im, bb_re, bb_im, c_re, c_im, d_row)


def _mlstm_chunk_kernel(q_ref, k_ref, v_ref, o_ref, zif_ref, zt0_ref, zt1_ref, zt2_ref, zt3_ref,
                        brow_ref, bcol_ref, gh_ref, h_ref, c_ref, n_ref, m_ref):
    @pl.when(pl.program_id(0) == 0)
    def _():
        c_ref[...] = jnp.zeros_like(c_ref)
        n_ref[...] = jnp.zeros_like(n_ref)
        m_ref[...] = jnp.zeros_like(m_ref)

    rr = lax.broadcasted_iota(jnp.int32, (CL, CL), 0)
    cc = lax.broadcasted_iota(jnp.int32, (CL, CL), 1)
    causal = cc <= rr
    tril = jnp.where(causal, 1.0, 0.0).astype(BF16)
    triu = jnp.where(rr <= cc, 1.0, 0.0).astype(BF16)

    for b, zt_ref in enumerate((zt0_ref, zt1_ref, zt2_ref, zt3_ref)):
        zi = zif_ref[b] + brow_ref[...]
        zt = zt_ref[...] + bcol_ref[...]
        lfc = _split3(_log_sigmoid(zi))
        bcum_col = _dot(tril, lfc[0]) + (_dot(tril, lfc[1]) + _dot(tril, lfc[2]))
        lfr = _split3(_log_sigmoid(zt))
        bcum_row = _dot(lfr[0], triu) + (_dot(lfr[1], triu) + _dot(lfr[2], triu))

        for hh in range(NH):
            hs = slice(hh * DH, (hh + 1) * DH)
            bc = bcum_col[:, NH + hh:NH + hh + 1]
            ic = zi[:, hh:hh + 1]
            br = bcum_row[NH + hh:NH + hh + 1, :]
            ir = zt[hh:hh + 1, :]
            m_prev = m_ref[b, hh:hh + 1, 0:1]
            g_inter = bc + m_prev
            dlog = jnp.where(causal, (bc - br) + ir, -jnp.inf)
            m_t = jnp.maximum(g_inter, jnp.max(dlog, axis=-1, keepdims=True))
            w_inter = jnp.exp(g_inter - m_t)
            w_intra = jnp.exp(dlog - m_t)
            qb = q_ref[b, :, hs]
            kb = k_ref[b, :, hs]
            vb = v_ref[b, :, hs]
            qf = qb.astype(F32)
            kf = kb.astype(F32)
            s = _dot_nt(qb, kb) * (w_intra * (DH ** -0.5))
            c_prev = c_ref[b, hh]
            n_prev = n_ref[b, hh:hh + 1, :]
            num = _dot(s.astype(BF16), vb) + w_inter * _dot(qb, c_prev.astype(BF16))
            nq = jnp.sum(s, axis=-1, keepdims=True) + w_inter * jnp.sum(qf * n_prev, axis=-1, keepdims=True)
            h = num / jnp.maximum(jnp.abs(nq), jnp.exp(-m_t))
            m_last = m_t[CL - 1:CL, :]
            w_last = jnp.exp((bc[CL - 1:CL, :] - bc) + ic - m_last) * (DH ** -0.5)
            wi_last = w_inter[CL - 1:CL, :]
            kw = kf * w_last
            c_ref[b, hh] = wi_last * c_prev + _dot_tn(kw.astype(BF16), vb)
            n_ref[b, hh:hh + 1, :] = wi_last * n_prev + jnp.sum(kw, axis=0, keepdims=True)
            m_ref[b, hh:hh + 1, :] = jnp.broadcast_to(m_last, (1, 128))
            hn = _rms(h, gh_ref[:, hs])
            h_ref[b, :, hs] = (hn * _sigmoid(o_ref[b, :, hs])).astype(BF16)


def _mlstm_chunk(z_f, z_b, z_if, z_t, brow, bcol, gh):
    blk = lambda col: pl.BlockSpec((BATCH, CL, W), lambda c: (0, c, col))
    const = lambda *shape: pl.BlockSpec(shape, lambda c: (0,) * len(shape))
    zt_specs = [pl.BlockSpec((8, CL), functools.partial(lambda c, b: (0, b * NCL + c), b=b)) for b in range(BATCH)]
    return pl.pallas_call(
        _mlstm_chunk_kernel,
        grid=(NCL,),
        in_specs=[blk(0), blk(1), blk(2), blk(1), pl.BlockSpec((BATCH, CL, 128), lambda c: (0, c, 0))]
        + zt_specs + [const(1, 128), const(8, 1), const(1, W)],
        out_specs=[pl.BlockSpec((BATCH, CL, W), lambda c: (0, c, 0)),
                   const(BATCH, NH, DH, DH), const(BATCH, NH, DH), const(BATCH, NH, 128)],
        out_shape=[jax.ShapeDtypeStruct((BATCH, SEQ, W), BF16),
                   jax.ShapeDtypeStruct((BATCH, NH, DH, DH), F32),
                   jax.ShapeDtypeStruct((BATCH, NH, DH), F32),
                   jax.ShapeDtypeStruct((BATCH, NH, 128), F32)],
        compiler_params=_cp(("arbitrary",)),
        name="mlstm_chunk",
    )(z_b, z_b, z_b, z_f, z_if, z_t, z_t, z_t, z_t, brow, bcol, gh)


MSR = 64


def _mlstm_step_kernel(q_ref, k_ref, v_ref, o_ref, zif_ref, brow_ref, gh_ref, c0_ref, n0_ref, m0_ref,
                       h_ref, c_ref, n_ref, m_ref):
    zi = zif_ref[...] + brow_ref[...]
    ig = zi[:, 0:NH]
    g_inter = _log_sigmoid(zi[:, NH:2 * NH]) + m0_ref[...]
    m_t = jnp.maximum(g_inter, ig)
    w_inter = jnp.exp(g_inter - m_t)
    w_intra = jnp.exp(ig - m_t) * (DH ** -0.5)
    floor = jnp.exp(-m_t)
    m_ref[...] = m_t

    def heads_on_rows(x):
        return jnp.pad(x, ((0, 128 - SB), (0, 128 - NH))).T

    w_inter_t = heads_on_rows(w_inter)
    w_intra_t = heads_on_rows(w_intra)
    floor_t = heads_on_rows(floor)
    pad = jnp.zeros((128 - SB, DH), F32)
    q_t = [jnp.concatenate([q_ref[:, hh * DH:(hh + 1) * DH], pad], axis=0).T for hh in range(NH)]
    k_t = [jnp.concatenate([k_ref[:, hh * DH:(hh + 1) * DH], pad], axis=0).T for hh in range(NH)]

    def per_head_rows(ref, s):
        return jnp.concatenate([ref[s:s + 1, hh * DH:(hh + 1) * DH] for hh in range(NH)], axis=0)

    for s in range(SB):
        q4 = per_head_rows(q_ref, s)
        k4 = per_head_rows(k_ref, s)
        v4 = per_head_rows(v_ref, s)
        qc_rows = []
        for hh in range(NH):
            wi = w_inter[s:s + 1, hh:hh + 1]
            vw = v4[hh:hh + 1, :] * w_intra[s:s + 1, hh:hh + 1]
            acc = jnp.zeros((MSR, DH), F32)
            for r0 in range(0, DH, MSR):
                c_blk = c0_ref[s, hh, r0:r0 + MSR, :]
                acc = acc + q_t[hh][r0:r0 + MSR, s:s + 1] * c_blk
                c_ref[s, hh, r0:r0 + MSR, :] = wi * c_blk + k_t[hh][r0:r0 + MSR, s:s + 1] * vw
            qc_rows.append(jnp.sum(acc, axis=0, keepdims=True))
        q_c = jnp.concatenate(qc_rows, axis=0)
        wi_c = w_inter_t[0:NH, s:s + 1]
        wa_c = w_intra_t[0:NH, s:s + 1]
        n_prev = n0_ref[s]
        sv = jnp.sum(q4 * k4, axis=-1, keepdims=True) * wa_c
        num = sv * v4 + wi_c * q_c
        nq = sv + wi_c * jnp.sum(q4 * n_prev, axis=-1, keepdims=True)
        h = num / jnp.maximum(jnp.abs(nq), floor_t[0:NH, s:s + 1])
        n_ref[s] = wi_c * n_prev + wa_c * k4
        out = _rms(h, gh_ref[...]) * _sigmoid(per_head_rows(o_ref, s))
        for hh in range(NH):
            h_ref[s:s + 1, hh * DH:(hh + 1) * DH] = out[hh:hh + 1, :]


def _mlstm_step(zs, z_f, z_if, brow, gh, c0, n0, m0):
    blk = lambda col: pl.BlockSpec((SB, W), lambda i: (i, col))
    return pl.pallas_call(
        _mlstm_step_kernel,
        grid=(NS // SB,),
        in_specs=[blk(0), blk(1), blk(2), blk(1),
                  pl.BlockSpec((SB, 128), lambda i: (i, 0)),
                  pl.BlockSpec((1, 128), lambda i: (0, 0)),
                  pl.BlockSpec((NH, DH), lambda i: (0, 0)),
                  pl.BlockSpec((SB, NH, DH, DH), lambda i: (i, 0, 0, 0)),
                  pl.BlockSpec((SB, NH, DH), lambda i: (i, 0, 0)),
                  pl.BlockSpec((SB, NH), lambda i: (i, 0))],
        out_specs=[pl.BlockSpec((SB, W), lambda i: (i, 0)),
                   pl.BlockSpec((SB, NH, DH, DH), lambda i: (i, 0, 0, 0)),
                   pl.BlockSpec((SB, NH, DH), lambda i: (i, 0, 0)),
                   pl.BlockSpec((SB, NH), lambda i: (i, 0))],
        out_shape=[jax.ShapeDtypeStruct((NS, W), F32),
                   jax.ShapeDtypeStruct((NS, NH, DH, DH), F32),
                   jax.ShapeDtypeStruct((NS, NH, DH), F32),
                   jax.ShapeDtypeStruct((NS, NH), F32)],
        compiler_params=_cp(("arbitrary",)),
        name="mlstm_step",
    )(zs, zs, zs, z_f, z_if, brow, gh, c0, n0, m0)


def _softmax_rows(s):
    e = jnp.exp(s - jnp.max(s, axis=-1, keepdims=True))
    return e / jnp.sum(e, axis=-1, keepdims=True)


def _xattn_prompt_kernel(q_ref, k_ref, v_ref, o_ref):
    for hh in range(NH):
        hs = slice(hh * DH, (hh + 1) * DH)
        s = _dot_nt(q_ref[:, hs], k_ref[:, hs].astype(BF16)) * (DH ** -0.5)
        p = _softmax_rows(s)
        o_ref[:, hs] = _dot(p.astype(BF16), v_ref[:, hs].astype(BF16)).astype(BF16)


def _xattn_prompt(z_b, kv):
    nt = SEQ // TM
    return pl.pallas_call(
        _xattn_prompt_kernel,
        grid=(BATCH, nt),
        in_specs=[pl.BlockSpec((TM, W), lambda b, t: (b * nt + t, 3)),
                  pl.BlockSpec((N_MEM, W), lambda b, t: (b, 0)),
                  pl.BlockSpec((N_MEM, W), lambda b, t: (b, 1))],
        out_specs=pl.BlockSpec((TM, W), lambda b, t: (b * nt + t, 0)),
        out_shape=jax.ShapeDtypeStruct((NP, W), BF16),
        compiler_params=_cp(("arbitrary", "arbitrary")),
        name="xattn_prompt",
    )(z_b, kv, kv)


XS = 4


XMC = 64


def _xattn_step_kernel(q_ref, k_ref, v_ref, o_ref):
    def part(t):
        for s in range(XS):
            r = t * XS + s
            q4 = jnp.concatenate([q_ref[r:r + 1, hh * DH:(hh + 1) * DH] for hh in range(NH)], axis=0)
            sc = jnp.concatenate(
                [jnp.sum(k_ref[s, m0:m0 + XMC] * q4[None], axis=-1, keepdims=True) for m0 in range(0, N_MEM, XMC)],
                axis=0) * (DH ** -0.5)
            e = jnp.exp(sc - jnp.max(sc, axis=0, keepdims=True))
            p = e / jnp.sum(e, axis=0, keepdims=True)
            acc = jnp.zeros((NH, DH), F32)
            for m0 in range(0, N_MEM, XMC):
                acc = acc + jnp.sum(p[m0:m0 + XMC] * v_ref[s, m0:m0 + XMC], axis=0)
            for hh in range(NH):
                o_ref[r:r + 1, hh * DH:(hh + 1) * DH] = acc[hh:hh + 1, :]

    for t in range(SB // XS):
        pl.when(pl.program_id(1) == t)(functools.partial(part, t))


def _xattn_step(zs, mem_k, mem_v):
    nt = SB // XS
    return pl.pallas_call(
        _xattn_step_kernel,
        grid=(NS // SB, nt),
        in_specs=[pl.BlockSpec((SB, W), lambda i, t: (i, 3)),
                  pl.BlockSpec((XS, N_MEM, NH, DH), lambda i, t: (i * nt + t, 0, 0, 0)),
                  pl.BlockSpec((XS, N_MEM, NH, DH), lambda i, t: (i * nt + t, 0, 0, 0))],
        out_specs=pl.BlockSpec((SB, W), lambda i, t: (i, 0)),
        out_shape=jax.ShapeDtypeStruct((NS, W), F32),
        compiler_params=_cp(("arbitrary", "arbitrary")),
        name="xattn_step",
    )(zs, mem_k, mem_v)


TMX = 512


def _mix_kernel(x_ref, g_ref, yp_ref, ys_ref, mp_ref, ms_ref, ap_ref, as_ref, wglu_ref,
                wg0_ref, wg1_ref, wg2_ref, wb0_ref, wb1_ref, wb2_ref, o_ref, h_scr, s5_scr):
    def tile(rows, y_ref, ml_ref, xa_ref):
        rs = slice(0, rows)

        @pl.when(pl.program_id(1) == 0)
        def _():
            h_scr[rs, :] = _rms(x_ref[rs, :], g_ref[...]).astype(BF16)
            y = _gelu_tanh(y_ref[...])
            s5_scr[rs, :] = (y * _sigmoid(_dot(y.astype(BF16), wglu_ref[...]))).astype(BF16)

        h = h_scr[rs, :]
        merged = (_sigmoid(_dot(h, wg0_ref[...])) * _dot(s5_scr[rs, :], wb0_ref[...])
                  + _sigmoid(_dot(h, wg1_ref[...])) * _dot(ml_ref[...].astype(BF16), wb1_ref[...])
                  + _sigmoid(_dot(h, wg2_ref[...])) * _dot(xa_ref[...].astype(BF16), wb2_ref[...]))
        o_ref[rs, :] = merged.astype(BF16)
        if rows < TM:
            o_ref[rows:, :] = jnp.zeros((TM - rows, TMX), BF16)

    pl.when(pl.program_id(0) < NPT)(functools.partial(tile, TM, yp_ref, mp_ref, ap_ref))
    pl.when(pl.program_id(0) == NPT)(functools.partial(tile, NS, ys_ref, ms_ref, as_ref))


def _mix(x1, g, yp, ys, mp, ms, ap, as_, wglu, wg, wb):
    prow = pl.BlockSpec((TM, W), lambda i, j: (jnp.minimum(i, NPT - 1), 0))
    srow = pl.BlockSpec((NS, W), lambda i, j: (0, 0))
    nx = D // TMX
    wgs = [pl.BlockSpec((D, TMX), functools.partial(lambda i, j, b: (0, b * nx + j), b=b)) for b in range(3)]
    wbs = pl.BlockSpec((W, TMX), lambda i, j: (0, j))
    return pl.pallas_call(
        _mix_kernel,
        grid=(NT, D // TMX),
        in_specs=[pl.BlockSpec((TM, D), lambda i, j: (i, 0)), pl.BlockSpec((1, D), lambda i, j: (0, 0)),
                  prow, srow, prow, srow, prow, srow,
                  pl.BlockSpec((W, W), lambda i, j: (0, 0)),
                  *wgs, wbs, wbs, wbs],
        out_specs=pl.BlockSpec((TM, TMX), lambda i, j: (i, j)),
        out_shape=jax.ShapeDtypeStruct((MROWS, D), BF16),
        scratch_shapes=[pltpu.VMEM((TM, D), BF16), pltpu.VMEM((TM, W), BF16)],
        compiler_params=_cp(("arbitrary", "arbitrary")),
        name="mix",
    )(x1, g, yp, ys, mp, ms, ap, as_, wglu, wg, wg, wg, *wb)


def _outproj_kernel(x_ref, m_ref, w_ref, o_ref):
    def tile(rows):
        rs = slice(0, rows)
        o_ref[rs, :] = x_ref[rs, :] + _dot(m_ref[rs, :], w_ref[...])
        if rows < TM:
            o_ref[rows:, :] = jnp.zeros((TM - rows, D), F32)

    pl.when(pl.program_id(0) < NPT)(functools.partial(tile, TM))
    pl.when(pl.program_id(0) == NPT)(functools.partial(tile, NS))


def _outproj(x1, merged, w_out):
    return pl.pallas_call(
        _outproj_kernel,
        grid=(NT,),
        in_specs=[pl.BlockSpec((TM, D), lambda i: (i, 0)), pl.BlockSpec((TM, D), lambda i: (i, 0)),
                  pl.BlockSpec((D, D), lambda i: (0, 0))],
        out_specs=pl.BlockSpec((TM, D), lambda i: (i, 0)),
        out_shape=jax.ShapeDtypeStruct((MROWS, D), F32),
        compiler_params=_cp(("arbitrary",)),
        name="out_proj",
    )(x1, merged, w_out)


def kernel(x_prompt, x_sample, mem_prompt, cache_mem_k, cache_mem_v, state_s5_re, state_s5_im, state_mlstm_C,
           state_mlstm_n, state_mlstm_m, g_ffn1, w1_gate, w1_up, w1_down, g_mix, w_in, s5_lambda_re,
           s5_lambda_im, s5_log_step, s5_b_re, s5_b_im, s5_c_re, s5_c_im, s5_d, w_s5_glu, b_igate, b_fgate,
           g_mlstm_head, g_mem, w_mem_k, w_mem_v, w_br_s5, w_br_ml, w_br_xa, w_out, g_ffn2, w2_gate, w2_up,
           w2_down, g_final):
    bf = lambda a: a.astype(BF16)

    w_in_t = w_in[0].T
    x1, w2g, w2u, w2d, w_head, w_bg = _ffn(
        (x_prompt.reshape(NP, D), x_sample.reshape(NS, D)), g_ffn1[0].reshape(1, D),
        bf(w1_gate[0]), bf(w1_up[0]), bf(w1_down[0]),
        side_jobs=(_cast_job(w2_gate, 16), _cast_job(w2_up, 16), _cast_job(w2_down, 32),
                   _transpose_job(w_in_t, 0, HEAD), _transpose_job(w_in_t, MIX0, 3 * D)))

    w_ift = bf(w_in_t[GATE0:GATE0 + 2 * NH])
    w_br = [bf(w_br_s5[0]), bf(w_br_ml[0]), bf(w_br_xa[0])]
    brow = jnp.pad(jnp.concatenate([b_igate[0], b_fgate[0]]), (0, 128 - 2 * NH)).reshape(1, 128)
    bcol = jnp.concatenate([b_igate[0], b_fgate[0]]).reshape(2 * NH, 1)
    gh = g_mlstm_head[0].reshape(1, W)

    z_f, z_b, z_if, z_t, zs_f, zs_b, zs_if = _in_proj(x1, g_mix[0].reshape(1, D), w_head, w_ift)
    zs = zs_b[:NS].astype(F32)
    per_seq = lambda a: a.reshape(BATCH, SEQ, a.shape[1])

    blk = lambda a: jnp.tile(a.reshape(NGB, WU, P), (1, 1, GB))
    row = lambda a: a.reshape(NGB, 1, WS)
    (a_re, a_im, a16_re, a16_im, bb_re, bb_im, e_op, ft_op, bd_op) = _s5_prep(
        row(s5_lambda_re[0]), row(s5_lambda_im[0]), row(jnp.repeat(s5_log_step[0], P)),
        blk(s5_b_re[0].transpose(0, 2, 1)), blk(s5_b_im[0].transpose(0, 2, 1)),
        blk(s5_c_re[0]), blk(s5_c_im[0]), s5_d[0].reshape(NGB, WU, 1))
    y_s5_p, fin = _s5_chunk(z_f, bd_op, e_op, ft_op, a16_re, a16_im)
    fin = fin.reshape(NGB, BATCH, 2, GB, P).transpose(2, 1, 0, 3, 4).reshape(2, 1, BATCH, G, P)
    y_s5_s, s5_re_s, s5_im_s = _s5_step(
        zs_f, state_s5_re[0].reshape(NS, G * P), state_s5_im[0].reshape(NS, G * P),
        a_re, a_im, bb_re, bb_im, blk(s5_c_re[0]), blk(s5_c_im[0]), s5_d[0].reshape(1, W))

    ml_p, c_p, n_p, m_p = _mlstm_chunk(per_seq(z_f), per_seq(z_b), per_seq(z_if), z_t, brow, bcol, gh)
    ml_s, c_s, n_s, m_s = _mlstm_step(zs, zs_f, zs_if, brow, gh.reshape(NH, DH), state_mlstm_C[0],
                                      state_mlstm_n[0], state_mlstm_m[0])

    w_kv = bf(jnp.concatenate([w_mem_k[0], w_mem_v[0]], axis=1))
    kv = _mem_proj(mem_prompt.reshape(BATCH * N_MEM, D), g_mem[0].reshape(1, D), w_kv)
    xa_p = _xattn_prompt(z_b, kv)
    xa_s = _xattn_step(zs, cache_mem_k[0], cache_mem_v[0])

    merged = _mix(x1, g_mix[0].reshape(1, D), y_s5_p, y_s5_s, ml_p.reshape(NP, W), ml_s, xa_p, xa_s,
                  bf(w_s5_glu[0]), w_bg, w_br)
    x2 = _outproj(x1, merged, bf(w_out[0]))
    y_p, y_s = _ffn((x2,), g_ffn2[0].reshape(1, D), w2g, w2u, w2d, g_final.reshape(1, D))

    return (y_p.reshape(BATCH, SEQ, D), y_s.reshape(NS, 1, D),
            kv[:, :W].reshape(1, BATCH, N_MEM, NH, DH), kv[:, W:].reshape(1, BATCH, N_MEM, NH, DH),
            fin[0], fin[1], c_p[None], n_p[None], m_p[:, :, 0][None],
            s5_re_s.reshape(1, NS, G, P), s5_im_s.reshape(1, NS, G, P), c_s[None], n_s[None], m_s[None])
```

```python
import functools
from typing import Callable, NamedTuple

import jax
import jax.numpy as jnp
from jax import lax
from jax.experimental import pallas as pl
from jax.experimental.pallas import tpu as pltpu

F32 = jnp.float32
BF16 = jnp.bfloat16

D = 2048
BATCH = 4
SEQ = 2048
NS = 128
NP = BATCH * SEQ
TM = 512
NPT = NP // TM
NT = NPT + 1
MROWS = NT * TM
N_MEM = 256
FF = 5504
TF_HOST = 512
TF_PLAIN = 768
G = 64
P = 64
HG = 16
TC = 16
GB = 8
NGB = G // GB
WU = GB * HG
WS = GB * P
NCH = SEQ // TC
S5B = 2
W = 1024
NH = 4
DH = 256
CL = 256
NCL = SEQ // CL
SB = 8
EPS = 1e-6
VMEM_LIMIT = 56 * 1024 * 1024


def _cp(sem, vmem=VMEM_LIMIT):
    return pltpu.CompilerParams(dimension_semantics=sem, vmem_limit_bytes=vmem)


def _dot(a, b):
    return jnp.dot(a, b, preferred_element_type=F32)


def _dot_nt(a, b):
    return lax.dot_general(a, b, (((1,), (1,)), ((), ())), preferred_element_type=F32)


def _dot_tn(a, b):
    return lax.dot_general(a, b, (((0,), (0,)), ((), ())), preferred_element_type=F32)


def _hi_lo(x):
    hi = x.astype(BF16)
    lo = (x - hi.astype(F32)).astype(BF16)
    return hi, lo


def _split3(x):
    hi = x.astype(BF16)
    r1 = x - hi.astype(F32)
    mid = r1.astype(BF16)
    lo = (r1 - mid.astype(F32)).astype(BF16)
    return hi, mid, lo


def _dot3(a, b):
    ah, al = _hi_lo(a)
    bh, bl = _hi_lo(b)
    return _dot(ah, bh) + (_dot(ah, bl) + _dot(al, bh))


def _dot3_nt(a, b):
    ah, al = _hi_lo(a)
    bh, bl = _hi_lo(b)
    return _dot_nt(ah, bh) + (_dot_nt(ah, bl) + _dot_nt(al, bh))


def _rms(x, g):
    r = lax.rsqrt(jnp.mean(x * x, axis=-1, keepdims=True) + EPS)
    return (x * r) * g


def _sigmoid(x):
    return 1.0 / (1.0 + jnp.exp(-x))


def _log_sigmoid(x):
    return jnp.minimum(x, 0.0) - jnp.log1p(jnp.exp(-jnp.abs(x)))


def _gelu_tanh(x):
    return x * (0.5 * (1.0 + jnp.tanh(0.7978845608028654 * (x + 0.044715 * (x * x * x)))))


def _ffn_kernel(*refs, two_src, final_norm, side_jobs, tf):
    nj = pl.cdiv(FF, tf)
    tf_last = FF - (nj - 1) * tf
    assert tf_last % 128 == 0
    refs = list(refs)
    if two_src:
        xp_ref, xs_ref = refs[:2]
        refs = refs[2:]
    else:
        xp_ref = xs_ref = refs[0]
        refs = refs[1:]
    g_ref, wg_ref, wu_ref, wd_ref = refs[:4]
    refs = refs[4:]
    n_in = sum(n for n, _, _ in side_jobs)
    n_out = sum(n for _, n, _ in side_jobs)
    if final_norm:
        gf_ref = refs[0]
        refs = refs[1:]
    side_in = refs[:n_in]
    refs = refs[n_in:]
    if final_norm:
        op_ref, os_ref = refs[:2]
        refs = refs[2:]
    else:
        o_ref = refs[0]
        refs = refs[1:]
    side_out = refs[:n_out]
    h_scr, acc_scr = refs[n_out:]
    j = pl.program_id(1)

    def side_work():
        a = b = 0
        for ni, no, fn in side_jobs:
            fn(side_in[a:a + ni], side_out[b:b + no])
            a += ni
            b += no

    def tile(rows, x_ref):
        rs = slice(0, rows)

        @pl.when(j == 0)
        def _():
            h_scr[rs, :] = _rms(x_ref[rs, :], g_ref[...]).astype(BF16)
            acc_scr[rs, :] = jnp.zeros((rows, D), F32)

        def accumulate(width):
            side_work()
            h = h_scr[rs, :]
            gt = _dot(h, wg_ref[:, 0:width])
            hid = (gt * _sigmoid(gt)) * _dot(h, wu_ref[:, 0:width])
            acc_scr[rs, :] += _dot(hid.astype(BF16), wd_ref[0:width, :])

        pl.when(j < nj - 1)(functools.partial(accumulate, tf))

        @pl.when(j == nj - 1)
        def _():
            accumulate(tf_last)
            y = x_ref[rs, :] + 0.5 * acc_scr[rs, :]
            if final_norm:
                y = _rms(y, gf_ref[...])
                (op_ref if rows == TM else os_ref)[...] = y
            else:
                o_ref[rs, :] = y
                if rows < TM:
                    o_ref[rows:, :] = jnp.zeros((TM - rows, D), F32)

    pl.when(pl.program_id(0) < NPT)(functools.partial(tile, TM, xp_ref))
    pl.when(pl.program_id(0) == NPT)(functools.partial(tile, NS, xs_ref))


class _SideJob(NamedTuple):
    blocks: int
    args: list
    in_specs: list
    out_shape: list
    out_specs: list
    fn: Callable


def _step_block(nj, blocks, first=0):
    return lambda i, j: first + jnp.minimum(i * nj + j, blocks - 1)


def _cast_job(a, rb, nj):
    rows_a, cols_a = a.shape[-2:]
    blk = _step_block(nj, rows_a // rb)

    def fn(ins, outs):
        outs[0][...] = ins[0][...].astype(BF16)

    return _SideJob(rows_a // rb, [a], [pl.BlockSpec((None, rb, cols_a), lambda i, j: (0, blk(i, j), 0))],
                    [jax.ShapeDtypeStruct((rows_a, cols_a), BF16)], [pl.BlockSpec((rb, cols_a), lambda i, j: (blk(i, j), 0))],
                    fn)


def _transpose_job(wt, row0, cols, nj):
    k = wt.shape[1]
    shift = row0 % 128
    assert shift in (0, 8) and cols % 128 == 0
    nblk = cols // 128
    blk = _step_block(nj, nblk, row0 // 128)
    last = pl.cdiv(wt.shape[0], 128) - 1
    specs = [pl.BlockSpec((128, k), lambda i, j: (blk(i, j), 0))]
    if shift:
        specs.append(pl.BlockSpec((128, k), lambda i, j: (jnp.minimum(blk(i, j) + 1, last), 0)))

    def fn(ins, outs):
        win = ins[0][...]
        if shift:
            win = jnp.concatenate([win[shift:, :], ins[1][0:shift, :]], axis=0)
        outs[0][...] = win.T.astype(BF16)

    out_blk = _step_block(nj, nblk)
    return _SideJob(nblk, [wt] * len(specs), specs, [jax.ShapeDtypeStruct((k, cols), BF16)],
                    [pl.BlockSpec((k, 128), lambda i, j: (0, out_blk(i, j)))], fn)


def _ffn(xs, g, wg, wu, wd, tf, g_final=None, side_jobs=()):
    nj = pl.cdiv(FF, tf)
    two_src = len(xs) == 2
    final_norm = g_final is not None
    row = pl.BlockSpec((TM, D), lambda i, j: (i, 0))
    prow = pl.BlockSpec((TM, D), lambda i, j: (jnp.minimum(i, NPT - 1), 0))
    srow = pl.BlockSpec((NS, D), lambda i, j: (0, 0))
    vec = pl.BlockSpec((1, D), lambda i, j: (0, 0))
    in_specs = ([prow, srow] if two_src else [row]) + [
        vec,
        pl.BlockSpec((D, tf), lambda i, j: (0, j)),
        pl.BlockSpec((D, tf), lambda i, j: (0, j)),
        pl.BlockSpec((tf, D), lambda i, j: (j, 0)),
    ]
    args = list(xs) + [g, wg, wu, wd]
    if final_norm:
        in_specs.append(vec)
        args.append(g_final)
        out_shape = [jax.ShapeDtypeStruct((NP, D), F32), jax.ShapeDtypeStruct((NS, D), F32)]
        out_specs = [prow, srow]
    else:
        out_shape = [jax.ShapeDtypeStruct((MROWS, D), F32)]
        out_specs = [row]
    jobs = []
    for make_job in side_jobs:
        job = make_job(nj)
        assert job.blocks <= NPT * nj, "side jobs must fit under the full-size row tiles"
        in_specs += job.in_specs
        args += job.args
        out_specs += job.out_specs
        out_shape += job.out_shape
        jobs.append((len(job.args), len(job.out_shape), job.fn))
    return pl.pallas_call(
        functools.partial(_ffn_kernel, two_src=two_src, final_norm=final_norm, side_jobs=tuple(jobs), tf=tf),
        grid=(NT, nj),
        in_specs=in_specs,
        out_specs=out_specs,
        out_shape=out_shape,
        scratch_shapes=[pltpu.VMEM((TM, D), BF16), pltpu.VMEM((TM, D), F32)],
        compiler_params=_cp(("arbitrary", "arbitrary")),
        name="ffn_final" if final_norm else "ffn",
    )(*args)


NF32 = 2
TMI = 256
NPI = NP // TMI
GATE0 = 5 * W
HEAD = GATE0 + W + 128
MIX0 = GATE0 + 2 * NH + W


def _in_proj_kernel(x_ref, g_ref, w_ref, wift_ref,
                    ofp_ref, obp_ref, zifp_ref, ztp_ref, ofs_ref, obs_ref, zifs_ref):
    def emit(of_ref, ob_ref, zif_ref, zt_ref):
        h = _rms(x_ref[...], g_ref[...]).astype(BF16)
        if zt_ref is not None:
            zt_ref[...] = _dot_nt(wift_ref[...], h)
        of_ref[:, 0:W] = _dot(h, w_ref[:, 0:W])
        for j in range(1, 4):
            ob_ref[:, (j - 1) * W:j * W] = _dot(h, w_ref[:, j * W:(j + 1) * W]).astype(BF16)
        of_ref[:, W:2 * W] = _dot(h, w_ref[:, 4 * W:5 * W])
        tail = _dot(h, w_ref[:, 5 * W:HEAD])
        zif_ref[...] = tail[:, 0:128]
        ob_ref[:, 3 * W:4 * W] = tail[:, 2 * NH:2 * NH + W].astype(BF16)

    pl.when(pl.program_id(0) < NPI)(functools.partial(emit, ofp_ref, obp_ref, zifp_ref, ztp_ref))
    pl.when(pl.program_id(0) == NPI)(functools.partial(emit, ofs_ref, obs_ref, zifs_ref, None))


def _in_proj(x, g, w, w_ift):
    once = pl.Buffered(1)
    pblk = lambda cols: pl.BlockSpec((TMI, cols), lambda i: (jnp.minimum(i, NPI - 1), 0))
    sblk = lambda cols: pl.BlockSpec((TMI, cols), lambda i: (0, 0))
    widths = (NF32 * W, 4 * W, 128)
    dtypes = (F32, BF16, F32)
    return pl.pallas_call(
        _in_proj_kernel,
        grid=(NPI + 1,),
        in_specs=[
            pl.BlockSpec((TMI, D), lambda i: (i, 0)),
            pl.BlockSpec((1, D), lambda i: (0, 0)),
            pl.BlockSpec((D, HEAD), lambda i: (0, 0), pipeline_mode=once),
            pl.BlockSpec((8, D), lambda i: (0, 0), pipeline_mode=once),
        ],
        out_specs=[pblk(c) for c in widths]
        + [pl.BlockSpec((8, TMI), lambda i: (0, jnp.minimum(i, NPI - 1)))]
        + [sblk(c) for c in widths],
        out_shape=[jax.ShapeDtypeStruct((NP, c), t) for c, t in zip(widths, dtypes)]
        + [jax.ShapeDtypeStruct((8, NP), F32)]
        + [jax.ShapeDtypeStruct((TMI, c), t) for c, t in zip(widths, dtypes)],
        compiler_params=_cp(("arbitrary",)),
        name="in_proj",
    )(x, g, w, w_ift)


def _mem_proj_kernel(x_ref, g_ref, wk_ref, wv_ref, k_ref, v_ref):
    h = _rms(x_ref[...], g_ref[...]).astype(BF16)
    k_ref[...] = _dot(h, wk_ref[...])
    v_ref[...] = _dot(h, wv_ref[...])


def _mem_proj(x, g, wk, wv):
    m = x.shape[0]
    wspec = pl.BlockSpec((D, W), lambda i: (0, 0))
    ospec = pl.BlockSpec((TM, W), lambda i: (i, 0))
    return pl.pallas_call(
        _mem_proj_kernel,
        grid=(m // TM,),
        in_specs=[pl.BlockSpec((TM, D), lambda i: (i, 0)), pl.BlockSpec((1, D), lambda i: (0, 0)), wspec, wspec],
        out_specs=[ospec, ospec],
        out_shape=[jax.ShapeDtypeStruct((m, W), F32)] * 2,
        compiler_params=_cp(("arbitrary",)),
        name="mem_proj",
    )(x, g, wk, wv)


def _s5_prep_kernel(lr_ref, li_ref, ls_ref, btr_ref, bti_ref, cr_ref, ci_ref, d_ref,
                    ar_ref, ai_ref, a16r_ref, a16i_ref, bbr_ref, bbi_ref, e_ref, ft_ref, bd_ref):
    lr = lr_ref[0]
    li = li_ref[0]
    dt = jnp.exp(ls_ref[0])

    def power(k):
        mag = jnp.exp(lr * dt * float(k))
        ang = li * dt * float(k)
        return mag * jnp.cos(ang), mag * jnp.sin(ang)

    pw = [power(k) for k in range(TC + 1)]
    ar, ai = pw[1]
    den = lr * lr + li * li
    nr = ar - 1.0
    z_re = (nr * lr + ai * li) / den
    z_im = (ai * lr - nr * li) / den
    mask = (lax.broadcasted_iota(jnp.int32, (WU, WS), 0) // HG
            == lax.broadcasted_iota(jnp.int32, (WU, WS), 1) // P)
    btr = btr_ref[0]
    bti = bti_ref[0]
    bbr = jnp.where(mask, z_re * btr - z_im * bti, 0.0)
    bbi = jnp.where(mask, z_re * bti + z_im * btr, 0.0)
    cr = jnp.where(mask, cr_ref[0], 0.0)
    ci = jnp.where(mask, ci_ref[0], 0.0)

    ar_ref[0] = ar
    ai_ref[0] = ai
    a16r_ref[0] = pw[TC][0]
    a16i_ref[0] = pw[TC][1]
    bbr_ref[0] = bbr
    bbi_ref[0] = bbi

    def cmul(xr, xi, k):
        pr, pi = pw[k]
        return xr * pr - xi * pi, xr * pi + xi * pr

    diag = (lax.broadcasted_iota(jnp.int32, (WU, WU), 0) == lax.broadcasted_iota(jnp.int32, (WU, WU), 1))
    for s in range(TC):
        er, ei = cmul(bbr, bbi, TC - 1 - s)
        e_ref[0, s * WU:(s + 1) * WU, 0:WS] = er.astype(BF16)
        e_ref[0, s * WU:(s + 1) * WU, WS:2 * WS] = ei.astype(BF16)
        fr, fi = cmul(cr, ci, s + 1)
        ft_ref[0, s * WU:(s + 1) * WU, 0:WS] = fr.astype(BF16)
        ft_ref[0, s * WU:(s + 1) * WU, WS:2 * WS] = (-fi).astype(BF16)
        rr, ri = cmul(cr, ci, s)
        kern = _dot3_nt(bbr, rr) - _dot3_nt(bbi, ri)
        if s == 0:
            kern = kern + jnp.where(diag, d_ref[0], 0.0)
        bd_ref[0, s] = kern


def _s5_prep(lam_re, lam_im, log_step, bt_re, bt_im, c_re, c_im, d):
    def spec(*shape):
        nd = len(shape)
        return pl.BlockSpec((1,) + shape, lambda i: (i,) + (0,) * nd)

    def sds(*shape, dtype=F32):
        return jax.ShapeDtypeStruct((NGB,) + shape, dtype)

    return pl.pallas_call(
        _s5_prep_kernel,
        grid=(NGB,),
        in_specs=[spec(1, WS)] * 3 + [spec(WU, WS)] * 4 + [spec(WU, 1)],
        out_specs=[spec(1, WS)] * 4 + [spec(WU, WS)] * 2 + [spec(TC * WU, 2 * WS)] * 2 + [spec(TC, WU, WU)],
        out_shape=[sds(1, WS)] * 4 + [sds(WU, WS)] * 2 + [sds(TC * WU, 2 * WS, dtype=BF16)] * 2
        + [sds(TC, WU, WU)],
        compiler_params=_cp(("arbitrary",)),
        name="s5_prep",
    )(lam_re, lam_im, log_step, bt_re, bt_im, c_re, c_im, d)


def _s5_chunk_kernel(u_ref, bd_ref, e_ref, ft_ref, ar_ref, ai_ref, y_ref, fin_ref,
                     w_scr, lhs_scr, s_scr, xs_scr):
    rows = S5B * NCH

    @pl.when(pl.program_id(1) == 0)
    def _():
        w_scr[...] = jnp.zeros_like(w_scr)
        bd = [bd_ref[0, k].astype(BF16) for k in range(TC)]
        for s in range(TC):
            for t in range(s, TC):
                w_scr[s * WU:(s + 1) * WU, t * WU:(t + 1) * WU] = bd[t - s]

    for s in range(TC):
        lhs_scr[:, s * WU:(s + 1) * WU] = u_ref[pl.ds(s, rows, stride=TC), :].astype(BF16)
    lhs = lhs_scr[...]
    s_loc = _dot(lhs, e_ref[0])
    nl = WS // 128
    for k in range(2 * nl):
        s_scr[k] = s_loc[:, k * 128:(k + 1) * 128]
    ar = [ar_ref[0, :, k * 128:(k + 1) * 128] for k in range(nl)]
    ai = [ai_ref[0, :, k * 128:(k + 1) * 128] for k in range(nl)]
    xr = [jnp.zeros((S5B, 128), F32)] * nl
    xi = [jnp.zeros((S5B, 128), F32)] * nl
    for c in range(NCH):
        chunk_rows = pl.ds(c, S5B, stride=NCH)
        for k in range(nl):
            xs_scr[k, chunk_rows, :] = xr[k]
            xs_scr[nl + k, chunk_rows, :] = xi[k]
            sr = s_scr[k, chunk_rows, :]
            si = s_scr[nl + k, chunk_rows, :]
            xr[k], xi[k] = ar[k] * xr[k] - ai[k] * xi[k] + sr, ar[k] * xi[k] + ai[k] * xr[k] + si
    for k in range(nl):
        fin_ref[0, 0, :, k * 128:(k + 1) * 128] = xr[k]
        fin_ref[0, 0, :, WS + k * 128:WS + (k + 1) * 128] = xi[k]
    xs = jnp.concatenate([xs_scr[k] for k in range(2 * nl)], axis=1)
    y = _dot(lhs, w_scr[...]) + _dot_nt(xs.astype(BF16), ft_ref[0])
    for t in range(TC):
        y_ref[pl.ds(t, rows, stride=TC), :] = y[:, t * WU:(t + 1) * WU]


def _s5_chunk(z_f, bd, e, ft, a16r, a16i):
    rows = S5B * NCH
    nh = BATCH // S5B
    return pl.pallas_call(
        _s5_chunk_kernel,
        grid=(NGB, nh),
        in_specs=[
            pl.BlockSpec((S5B * SEQ, WU), lambda j, b: (b, j)),
            pl.BlockSpec((1, TC, WU, WU), lambda j, b: (j, 0, 0, 0)),
            pl.BlockSpec((1, TC * WU, 2 * WS), lambda j, b: (j, 0, 0)),
            pl.BlockSpec((1, TC * WU, 2 * WS), lambda j, b: (j, 0, 0)),
            pl.BlockSpec((1, 1, WS), lambda j, b: (j, 0, 0)),
            pl.BlockSpec((1, 1, WS), lambda j, b: (j, 0, 0)),
        ],
        out_specs=[
            pl.BlockSpec((S5B * SEQ, WU), lambda j, b: (b, j)),
            pl.BlockSpec((1, 1, S5B, 2 * WS), lambda j, b: (j, b, 0, 0)),
        ],
        out_shape=[jax.ShapeDtypeStruct((NP, W), F32), jax.ShapeDtypeStruct((NGB, nh, S5B, 2 * WS), F32)],
        scratch_shapes=[pltpu.VMEM((TC * WU, TC * WU), BF16), pltpu.VMEM((rows, TC * WU), BF16),
                        pltpu.VMEM((2 * WS // 128, rows, 128), F32), pltpu.VMEM((2 * WS // 128, rows, 128), F32)],
        compiler_params=_cp(("arbitrary", "arbitrary")),
        name="s5_chunk",
    )(z_f, bd, e, ft, a16r, a16i)


def _s5_step_kernel(u_ref, sr_ref, si_ref, ar_ref, ai_ref, bbr_ref, bbi_ref, cr_ref, ci_ref, d_ref,
                    y_ref, xr_ref, xi_ref):
    mask = (lax.broadcasted_iota(jnp.int32, (WU, WS), 0) // HG
            == lax.broadcasted_iota(jnp.int32, (WU, WS), 1) // P)
    for j in range(NGB):
        ul = slice(j * WU, (j + 1) * WU)
        sl = slice(j * WS, (j + 1) * WS)
        us = u_ref[:, ul]
        ar = ar_ref[j]
        ai = ai_ref[j]
        s_re = sr_ref[:, sl]
        s_im = si_ref[:, sl]
        x_re = ar * s_re - ai * s_im + _dot3(us, bbr_ref[j])
        x_im = ar * s_im + ai * s_re + _dot3(us, bbi_ref[j])
        xr_ref[:, sl] = x_re
        xi_ref[:, sl] = x_im
        cre = jnp.where(mask, cr_ref[j], 0.0)
        cim = jnp.where(mask, ci_ref[j], 0.0)
        y_ref[:, ul] = _dot3_nt(x_re, cre) - _dot3_nt(x_im, cim) + d_ref[:, ul] * us


def _s5_step(z_f, s_re, s_im, a_re, a_im, bb_re, bb_im, c_re, c_im, d_row):
    full = lambda *shape: pl.BlockSpec(shape, lambda i: (0,) * len(shape))
    return pl.pallas_call(
        _s5_step_kernel,
        grid=(1,),
        in_specs=[pl.BlockSpec((NS, W), lambda i: (0, 0)),
                  full(NS, G * P), full(NS, G * P), full(NGB, 1, WS), full(NGB, 1, WS),
                  full(NGB, WU, WS), full(NGB, WU, WS), full(NGB, WU, WS), full(NGB, WU, WS),
                  full(1, W)],
        out_specs=[full(NS, W), full(NS, G * P), full(NS, G * P)],
        out_shape=[jax.ShapeDtypeStruct((NS, W), F32), jax.ShapeDtypeStruct((NS, G * P), F32),
                   jax.ShapeDtypeStruct((NS, G * P), F32)],
        compiler_params=_cp(("arbitrary",)),
        name="s5_step",
    )(z_f, s_re, s_im, a_re, a_im, bb_re, bb_im, c_re, c_im, d_row)


def _mlstm_chunk_kernel(q_ref, k_ref, v_ref, o_ref, zif_ref, zt0_ref, zt1_ref, zt2_ref, zt3_ref,
                        brow_ref, bcol_ref, gh_ref, h_ref, c_ref, n_ref, m_ref):
    @pl.when(pl.program_id(0) == 0)
    def _():
        c_ref[...] = jnp.zeros_like(c_ref)
        n_ref[...] = jnp.zeros_like(n_ref)
        m_ref[...] = jnp.zeros_like(m_ref)

    rr = lax.broadcasted_iota(jnp.int32, (CL, CL), 0)
    cc = lax.broadcasted_iota(jnp.int32, (CL, CL), 1)
    causal = cc <= rr
    tril = jnp.where(causal, 1.0, 0.0).astype(BF16)
    triu = jnp.where(rr <= cc, 1.0, 0.0).astype(BF16)

    for b, zt_ref in enumerate((zt0_ref, zt1_ref, zt2_ref, zt3_ref)):
        zi = zif_ref[b] + brow_ref[...]
        zt = zt_ref[...] + bcol_ref[...]
        lfc = _split3(_log_sigmoid(zi))
        bcum_col = _dot(tril, lfc[0]) + (_dot(tril, lfc[1]) + _dot(tril, lfc[2]))
        lfr = _split3(_log_sigmoid(zt))
        bcum_row = _dot(lfr[0], triu) + (_dot(lfr[1], triu) + _dot(lfr[2], triu))

        for hh in range(NH):
            hs = slice(hh * DH, (hh + 1) * DH)
            bc = bcum_col[:, NH + hh:NH + hh + 1]
            ic = zi[:, hh:hh + 1]
            br = bcum_row[NH + hh:NH + hh + 1, :]
            ir = zt[hh:hh + 1, :]
            m_prev = m_ref[b, hh:hh + 1, 0:1]
            g_inter = bc + m_prev
            dlog = jnp.where(causal, (bc - br) + ir, -jnp.inf)
            m_t = jnp.maximum(g_inter, jnp.max(dlog, axis=-1, keepdims=True))
            w_inter = jnp.exp(g_inter - m_t)
            w_intra = jnp.exp(dlog - m_t)
            qb = q_ref[b, :, hs]
            kb = k_ref[b, :, hs]
            vb = v_ref[b, :, hs]
            qf = qb.astype(F32)
            kf = kb.astype(F32)
            s = _dot_nt(qb, kb) * (w_intra * (DH ** -0.5))
            c_prev = c_ref[b, hh]
            n_prev = n_ref[b, hh:hh + 1, :]
            num = _dot(s.astype(BF16), vb) + w_inter * _dot(qb, c_prev.astype(BF16))
            nq = jnp.sum(s, axis=-1, keepdims=True) + w_inter * jnp.sum(qf * n_prev, axis=-1, keepdims=True)
            h = num / jnp.maximum(jnp.abs(nq), jnp.exp(-m_t))
            m_last = m_t[CL - 1:CL, :]
            w_last = jnp.exp((bc[CL - 1:CL, :] - bc) + ic - m_last) * (DH ** -0.5)
            wi_last = w_inter[CL - 1:CL, :]
            kw = kf * w_last
            c_ref[b, hh] = wi_last * c_prev + _dot_tn(kw.astype(BF16), vb)
            n_ref[b, hh:hh + 1, :] = wi_last * n_prev + jnp.sum(kw, axis=0, keepdims=True)
            m_ref[b, hh:hh + 1, :] = jnp.broadcast_to(m_last, (1, 128))
            hn = _rms(h, gh_ref[:, hs])
            h_ref[b, :, hs] = (hn * _sigmoid(o_ref[b, :, hs])).astype(BF16)


def _mlstm_chunk(z_f, z_b, z_if, z_t, brow, bcol, gh):
    blk = lambda col: pl.BlockSpec((BATCH, CL, W), lambda c: (0, c, col))
    const = lambda *shape: pl.BlockSpec(shape, lambda c: (0,) * len(shape))
    zt_specs = [pl.BlockSpec((8, CL), functools.partial(lambda c, b: (0, b * NCL + c), b=b)) for b in range(BATCH)]
    return pl.pallas_call(
        _mlstm_chunk_kernel,
        grid=(NCL,),
        in_specs=[blk(0), blk(1), blk(2), blk(1), pl.BlockSpec((BATCH, CL, 128), lambda c: (0, c, 0))]
        + zt_specs + [const(1, 128), const(8, 1), const(1, W)],
        out_specs=[pl.BlockSpec((BATCH, CL, W), lambda c: (0, c, 0)),
                   const(BATCH, NH, DH, DH), const(BATCH, NH, DH), const(BATCH, NH, 128)],
        out_shape=[jax.ShapeDtypeStruct((BATCH, SEQ, W), BF16),
                   jax.ShapeDtypeStruct((BATCH, NH, DH, DH), F32),
                   jax.ShapeDtypeStruct((BATCH, NH, DH), F32),
                   jax.ShapeDtypeStruct((BATCH, NH, 128), F32)],
        compiler_params=_cp(("arbitrary",)),
        name="mlstm_chunk",
    )(z_b, z_b, z_b, z_f, z_if, z_t, z_t, z_t, z_t, brow, bcol, gh)


MSR = 64


def _mlstm_step_kernel(q_ref, k_ref, v_ref, o_ref, zif_ref, brow_ref, gh_ref, c0_ref, n0_ref, m0_ref,
                       h_ref, c_ref, n_ref, m_ref):
    zi = zif_ref[...] + brow_ref[...]
    ig = zi[:, 0:NH]
    g_inter = _log_sigmoid(zi[:, NH:2 * NH]) + m0_ref[...]
    m_t = jnp.maximum(g_inter, ig)
    w_inter = jnp.exp(g_inter - m_t)
    w_intra = jnp.exp(ig - m_t) * (DH ** -0.5)
    floor = jnp.exp(-m_t)
    m_ref[...] = m_t

    def heads_on_rows(x):
        return jnp.pad(x, ((0, 128 - SB), (0, 128 - NH))).T

    w_inter_t = heads_on_rows(w_inter)
    w_intra_t = heads_on_rows(w_intra)
    floor_t = heads_on_rows(floor)
    pad = jnp.zeros((128 - SB, DH), F32)
    q_t = [jnp.concatenate([q_ref[:, hh * DH:(hh + 1) * DH], pad], axis=0).T for hh in range(NH)]
    k_t = [jnp.concatenate([k_ref[:, hh * DH:(hh + 1) * DH], pad], axis=0).T for hh in range(NH)]

    def per_head_rows(ref, s):
        return jnp.concatenate([ref[s:s + 1, hh * DH:(hh + 1) * DH] for hh in range(NH)], axis=0)

    for s in range(SB):
        q4 = per_head_rows(q_ref, s)
        k4 = per_head_rows(k_ref, s)
        v4 = per_head_rows(v_ref, s)
        qc_rows = []
        for hh in range(NH):
            wi = w_inter[s:s + 1, hh:hh + 1]
            vw = v4[hh:hh + 1, :] * w_intra[s:s + 1, hh:hh + 1]
            acc = jnp.zeros((MSR, DH), F32)
            for r0 in range(0, DH, MSR):
                c_blk = c0_ref[s, hh, r0:r0 + MSR, :]
                acc = acc + q_t[hh][r0:r0 + MSR, s:s + 1] * c_blk
                c_ref[s, hh, r0:r0 + MSR, :] = wi * c_blk + k_t[hh][r0:r0 + MSR, s:s + 1] * vw
            qc_rows.append(jnp.sum(acc, axis=0, keepdims=True))
        q_c = jnp.concatenate(qc_rows, axis=0)
        wi_c = w_inter_t[0:NH, s:s + 1]
        wa_c = w_intra_t[0:NH, s:s + 1]
        n_prev = n0_ref[s]
        sv = jnp.sum(q4 * k4, axis=-1, keepdims=True) * wa_c
        num = sv * v4 + wi_c * q_c
        nq = sv + wi_c * jnp.sum(q4 * n_prev, axis=-1, keepdims=True)
        h = num / jnp.maximum(jnp.abs(nq), floor_t[0:NH, s:s + 1])
        n_ref[s] = wi_c * n_prev + wa_c * k4
        out = _rms(h, gh_ref[...]) * _sigmoid(per_head_rows(o_ref, s))
        for hh in range(NH):
            h_ref[s:s + 1, hh * DH:(hh + 1) * DH] = out[hh:hh + 1, :]


def _mlstm_step(zs, z_f, z_if, brow, gh, c0, n0, m0):
    blk = lambda col: pl.BlockSpec((SB, W), lambda i: (i, col))
    return pl.pallas_call(
        _mlstm_step_kernel,
        grid=(NS // SB,),
        in_specs=[blk(0), blk(1), blk(2), blk(1),
                  pl.BlockSpec((SB, 128), lambda i: (i, 0)),
                  pl.BlockSpec((1, 128), lambda i: (0, 0)),
                  pl.BlockSpec((NH, DH), lambda i: (0, 0)),
                  pl.BlockSpec((SB, NH, DH, DH), lambda i: (i, 0, 0, 0)),
                  pl.BlockSpec((SB, NH, DH), lambda i: (i, 0, 0)),
                  pl.BlockSpec((SB, NH), lambda i: (i, 0))],
        out_specs=[pl.BlockSpec((SB, W), lambda i: (i, 0)),
                   pl.BlockSpec((SB, NH, DH, DH), lambda i: (i, 0, 0, 0)),
                   pl.BlockSpec((SB, NH, DH), lambda i: (i, 0, 0)),
                   pl.BlockSpec((SB, NH), lambda i: (i, 0))],
        out_shape=[jax.ShapeDtypeStruct((NS, W), F32),
                   jax.ShapeDtypeStruct((NS, NH, DH, DH), F32),
                   jax.ShapeDtypeStruct((NS, NH, DH), F32),
                   jax.ShapeDtypeStruct((NS, NH), F32)],
        compiler_params=_cp(("arbitrary",)),
        name="mlstm_step",
    )(zs, zs, zs, z_f, z_if, brow, gh, c0, n0, m0)


def _softmax_rows(s):
    e = jnp.exp(s - jnp.max(s, axis=-1, keepdims=True))
    return e / jnp.sum(e, axis=-1, keepdims=True)


def _xattn_prompt_kernel(q_ref, k_ref, v_ref, o_ref):
    for hh in range(NH):
        hs = slice(hh * DH, (hh + 1) * DH)
        s = _dot_nt(q_ref[:, hs], k_ref[:, hs].astype(BF16)) * (DH ** -0.5)
        p = _softmax_rows(s)
        o_ref[:, hs] = _dot(p.astype(BF16), v_ref[:, hs].astype(BF16)).astype(BF16)


def _xattn_prompt(z_b, mem_k, mem_v):
    nt = SEQ // TM
    return pl.pallas_call(
        _xattn_prompt_kernel,
        grid=(BATCH, nt),
        in_specs=[pl.BlockSpec((TM, W), lambda b, t: (b * nt + t, 3)),
                  pl.BlockSpec((N_MEM, W), lambda b, t: (b, 0)),
                  pl.BlockSpec((N_MEM, W), lambda b, t: (b, 0))],
        out_specs=pl.BlockSpec((TM, W), lambda b, t: (b * nt + t, 0)),
        out_shape=jax.ShapeDtypeStruct((NP, W), BF16),
        compiler_params=_cp(("arbitrary", "arbitrary")),
        name="xattn_prompt",
    )(z_b, mem_k, mem_v)


XS = 4


XMC = 64


def _xattn_step_kernel(q_ref, k_ref, v_ref, o_ref):
    def part(t):
        for s in range(XS):
            r = t * XS + s
            q4 = jnp.concatenate([q_ref[r:r + 1, hh * DH:(hh + 1) * DH] for hh in range(NH)], axis=0)
            sc = jnp.concatenate(
                [jnp.sum(k_ref[s, m0:m0 + XMC] * q4[None], axis=-1, keepdims=True) for m0 in range(0, N_MEM, XMC)],
                axis=0) * (DH ** -0.5)
            e = jnp.exp(sc - jnp.max(sc, axis=0, keepdims=True))
            p = e / jnp.sum(e, axis=0, keepdims=True)
            acc = jnp.zeros((NH, DH), F32)
            for m0 in range(0, N_MEM, XMC):
                acc = acc + jnp.sum(p[m0:m0 + XMC] * v_ref[s, m0:m0 + XMC], axis=0)
            for hh in range(NH):
                o_ref[r:r + 1, hh * DH:(hh + 1) * DH] = acc[hh:hh + 1, :]

    for t in range(SB // XS):
        pl.when(pl.program_id(1) == t)(functools.partial(part, t))


def _xattn_step(zs, mem_k, mem_v):
    nt = SB // XS
    return pl.pallas_call(
        _xattn_step_kernel,
        grid=(NS // SB, nt),
        in_specs=[pl.BlockSpec((SB, W), lambda i, t: (i, 3)),
                  pl.BlockSpec((XS, N_MEM, NH, DH), lambda i, t: (i * nt + t, 0, 0, 0)),
                  pl.BlockSpec((XS, N_MEM, NH, DH), lambda i, t: (i * nt + t, 0, 0, 0))],
        out_specs=pl.BlockSpec((SB, W), lambda i, t: (i, 0)),
        out_shape=jax.ShapeDtypeStruct((NS, W), F32),
        compiler_params=_cp(("arbitrary", "arbitrary")),
        name="xattn_step",
    )(zs, mem_k, mem_v)


TMX = 512


def _mix_kernel(x_ref, g_ref, yp_ref, ys_ref, mp_ref, ms_ref, ap_ref, as_ref, wglu_ref,
                wg0_ref, wg1_ref, wg2_ref, wb0_ref, wb1_ref, wb2_ref, o_ref, h_scr, s5_scr):
    def tile(rows, y_ref, ml_ref, xa_ref):
        rs = slice(0, rows)

        @pl.when(pl.program_id(1) == 0)
        def _():
            h_scr[rs, :] = _rms(x_ref[rs, :], g_ref[...]).astype(BF16)
            y = _gelu_tanh(y_ref[...])
            s5_scr[rs, :] = (y * _sigmoid(_dot(y.astype(BF16), wglu_ref[...]))).astype(BF16)

        h = h_scr[rs, :]
        merged = (_sigmoid(_dot(h, wg0_ref[...])) * _dot(s5_scr[rs, :], wb0_ref[...])
                  + _sigmoid(_dot(h, wg1_ref[...])) * _dot(ml_ref[...].astype(BF16), wb1_ref[...])
                  + _sigmoid(_dot(h, wg2_ref[...])) * _dot(xa_ref[...].astype(BF16), wb2_ref[...]))
        o_ref[rs, :] = merged.astype(BF16)
        if rows < TM:
            o_ref[rows:, :] = jnp.zeros((TM - rows, TMX), BF16)

    pl.when(pl.program_id(0) < NPT)(functools.partial(tile, TM, yp_ref, mp_ref, ap_ref))
    pl.when(pl.program_id(0) == NPT)(functools.partial(tile, NS, ys_ref, ms_ref, as_ref))


def _mix(x1, g, yp, ys, mp, ms, ap, as_, wglu, wg, wb):
    prow = pl.BlockSpec((TM, W), lambda i, j: (jnp.minimum(i, NPT - 1), 0))
    srow = pl.BlockSpec((NS, W), lambda i, j: (0, 0))
    nx = D // TMX
    wgs = [pl.BlockSpec((D, TMX), functools.partial(lambda i, j, b: (0, b * nx + j), b=b)) for b in range(3)]
    wbs = pl.BlockSpec((W, TMX), lambda i, j: (0, j))
    return pl.pallas_call(
        _mix_kernel,
        grid=(NT, D // TMX),
        in_specs=[pl.BlockSpec((TM, D), lambda i, j: (i, 0)), pl.BlockSpec((1, D), lambda i, j: (0, 0)),
                  prow, srow, prow, srow, prow, srow,
                  pl.BlockSpec((W, W), lambda i, j: (0, 0)),
                  *wgs, wbs, wbs, wbs],
        out_specs=pl.BlockSpec((TM, TMX), lambda i, j: (i, j)),
        out_shape=jax.ShapeDtypeStruct((MROWS, D), BF16),
        scratch_shapes=[pltpu.VMEM((TM, D), BF16), pltpu.VMEM((TM, W), BF16)],
        compiler_params=_cp(("arbitrary", "arbitrary")),
        name="mix",
    )(x1, g, yp, ys, mp, ms, ap, as_, wglu, wg, wg, wg, *wb)


def _outproj_kernel(x_ref, m_ref, w_ref, o_ref):
    def tile(rows):
        rs = slice(0, rows)
        o_ref[rs, :] = x_ref[rs, :] + _dot(m_ref[rs, :], w_ref[...])
        if rows < TM:
            o_ref[rows:, :] = jnp.zeros((TM - rows, D), F32)

    pl.when(pl.program_id(0) < NPT)(functools.partial(tile, TM))
    pl.when(pl.program_id(0) == NPT)(functools.partial(tile, NS))


def _outproj(x1, merged, w_out):
    return pl.pallas_call(
        _outproj_kernel,
        grid=(NT,),
        in_specs=[pl.BlockSpec((TM, D), lambda i: (i, 0)), pl.BlockSpec((TM, D), lambda i: (i, 0)),
                  pl.BlockSpec((D, D), lambda i: (0, 0))],
        out_specs=pl.BlockSpec((TM, D), lambda i: (i, 0)),
        out_shape=jax.ShapeDtypeStruct((MROWS, D), F32),
        compiler_params=_cp(("arbitrary",)),
        name="out_proj",
    )(x1, merged, w_out)


def kernel(x_prompt, x_sample, mem_prompt, cache_mem_k, cache_mem_v, state_s5_re, state_s5_im, state_mlstm_C,
           state_mlstm_n, state_mlstm_m, g_ffn1, w1_gate, w1_up, w1_down, g_mix, w_in, s5_lambda_re,
           s5_lambda_im, s5_log_step, s5_b_re, s5_b_im, s5_c_re, s5_c_im, s5_d, w_s5_glu, b_igate, b_fgate,
           g_mlstm_head, g_mem, w_mem_k, w_mem_v, w_br_s5, w_br_ml, w_br_xa, w_out, g_ffn2, w2_gate, w2_up,
           w2_down, g_final):
    bf = lambda a: a.astype(BF16)

    w_in_t = w_in[0].T
    job = functools.partial
    (x1, w2g, w2u, w2d, w_head, w_bg, wb_s5, wb_ml, wb_xa, w_o, w_glu, w_mk, w_mv) = _ffn(
        (x_prompt.reshape(NP, D), x_sample.reshape(NS, D)), g_ffn1[0].reshape(1, D),
        bf(w1_gate[0]), bf(w1_up[0]), bf(w1_down[0]), TF_HOST,
        side_jobs=(job(_cast_job, w2_gate, 16), job(_cast_job, w2_up, 16), job(_cast_job, w2_down, 32),
                   job(_transpose_job, w_in_t, 0, HEAD), job(_transpose_job, w_in_t, MIX0, 3 * D),
                   job(_cast_job, w_br_s5, 16), job(_cast_job, w_br_ml, 16), job(_cast_job, w_br_xa, 16),
                   job(_cast_job, w_out, 16), job(_cast_job, w_s5_glu, 16),
                   job(_cast_job, w_mem_k, 16), job(_cast_job, w_mem_v, 16)))

    w_ift = bf(w_in_t[GATE0:GATE0 + 2 * NH])
    w_br = [wb_s5, wb_ml, wb_xa]
    brow = jnp.pad(jnp.concatenate([b_igate[0], b_fgate[0]]), (0, 128 - 2 * NH)).reshape(1, 128)
    bcol = jnp.concatenate([b_igate[0], b_fgate[0]]).reshape(2 * NH, 1)
    gh = g_mlstm_head[0].reshape(1, W)

    z_f, z_b, z_if, z_t, zs_f, zs_b, zs_if = _in_proj(x1, g_mix[0].reshape(1, D), w_head, w_ift)
    zs = zs_b[:NS].astype(F32)
    per_seq = lambda a: a.reshape(BATCH, SEQ, a.shape[1])

    blk = lambda a: jnp.tile(a.reshape(NGB, WU, P), (1, 1, GB))
    row = lambda a: a.reshape(NGB, 1, WS)
    (a_re, a_im, a16_re, a16_im, bb_re, bb_im, e_op, ft_op, bd_op) = _s5_prep(
        row(s5_lambda_re[0]), row(s5_lambda_im[0]), row(jnp.repeat(s5_log_step[0], P)),
        blk(s5_b_re[0].transpose(0, 2, 1)), blk(s5_b_im[0].transpose(0, 2, 1)),
        blk(s5_c_re[0]), blk(s5_c_im[0]), s5_d[0].reshape(NGB, WU, 1))
    y_s5_p, fin = _s5_chunk(z_f, bd_op, e_op, ft_op, a16_re, a16_im)
    fin = fin.reshape(NGB, BATCH, 2, GB, P).transpose(2, 1, 0, 3, 4).reshape(2, 1, BATCH, G, P)
    y_s5_s, s5_re_s, s5_im_s = _s5_step(
        zs_f, state_s5_re[0].reshape(NS, G * P), state_s5_im[0].reshape(NS, G * P),
        a_re, a_im, bb_re, bb_im, blk(s5_c_re[0]), blk(s5_c_im[0]), s5_d[0].reshape(1, W))

    ml_p, c_p, n_p, m_p = _mlstm_chunk(per_seq(z_f), per_seq(z_b), per_seq(z_if), z_t, brow, bcol, gh)
    ml_s, c_s, n_s, m_s = _mlstm_step(zs, zs_f, zs_if, brow, gh.reshape(NH, DH), state_mlstm_C[0],
                                      state_mlstm_n[0], state_mlstm_m[0])

    mem_k, mem_v = _mem_proj(mem_prompt.reshape(BATCH * N_MEM, D), g_mem[0].reshape(1, D), w_mk, w_mv)
    xa_p = _xattn_prompt(z_b, mem_k, mem_v)
    xa_s = _xattn_step(zs, cache_mem_k[0], cache_mem_v[0])

    merged = _mix(x1, g_mix[0].reshape(1, D), y_s5_p, y_s5_s, ml_p.reshape(NP, W), ml_s, xa_p, xa_s,
                  w_glu, w_bg, w_br)
    x2 = _outproj(x1, merged, w_o)
    y_p, y_s = _ffn((x2,), g_ffn2[0].reshape(1, D), w2g, w2u, w2d, TF_PLAIN, g_final.reshape(1, D))

    return (y_p.reshape(BATCH, SEQ, D), y_s.reshape(NS, 1, D),
            mem_k.reshape(1, BATCH, N_MEM, NH, DH), mem_v.reshape(1, BATCH, N_MEM, NH, DH),
            fin[0], fin[1], c_p[None], n_p[None], m_p[:, :, 0][None],
            s5_re_s.reshape(1, NS, G, P), s5_im_s.reshape(1, NS, G, P), c_s[None], n_s[None], m_s[None])
```

```python
import functools
from typing import Callable, NamedTuple

import jax
import jax.numpy as jnp
from jax import lax
from jax.experimental import pallas as pl
from jax.experimental.pallas import tpu as pltpu

F32 = jnp.float32
BF16 = jnp.bfloat16

D = 2048
BATCH = 4
SEQ = 2048
NS = 128
NP = BATCH * SEQ
TM = 512
NPT = NP // TM
NT = NPT + 1
MROWS = NT * TM
N_MEM = 256
FF = 5504
TF = 512
G = 64
P = 64
HG = 16
TC = 16
GB = 8
NGB = G // GB
WU = GB * HG
WS = GB * P
NCH = SEQ // TC
S5B = 2
W = 1024
NH = 4
DH = 256
CL = 256
NCL = SEQ // CL
SB = 8
EPS = 1e-6
VMEM_LIMIT = 56 * 1024 * 1024


def _cp(sem, vmem=VMEM_LIMIT):
    return pltpu.CompilerParams(dimension_semantics=sem, vmem_limit_bytes=vmem)


def _dot(a, b):
    return jnp.dot(a, b, preferred_element_type=F32)


def _dot_nt(a, b):
    return lax.dot_general(a, b, (((1,), (1,)), ((), ())), preferred_element_type=F32)


def _dot_tn(a, b):
    return lax.dot_general(a, b, (((0,), (0,)), ((), ())), preferred_element_type=F32)


def _hi_lo(x):
    hi = x.astype(BF16)
    lo = (x - hi.astype(F32)).astype(BF16)
    return hi, lo


def _split3(x):
    hi = x.astype(BF16)
    r1 = x - hi.astype(F32)
    mid = r1.astype(BF16)
    lo = (r1 - mid.astype(F32)).astype(BF16)
    return hi, mid, lo


def _dot3(a, b):
    ah, al = _hi_lo(a)
    bh, bl = _hi_lo(b)
    return _dot(ah, bh) + (_dot(ah, bl) + _dot(al, bh))


def _dot3_nt(a, b):
    ah, al = _hi_lo(a)
    bh, bl = _hi_lo(b)
    return _dot_nt(ah, bh) + (_dot_nt(ah, bl) + _dot_nt(al, bh))


def _rms(x, g):
    r = lax.rsqrt(jnp.mean(x * x, axis=-1, keepdims=True) + EPS)
    return (x * r) * g


def _sigmoid(x):
    return 1.0 / (1.0 + jnp.exp(-x))


def _log_sigmoid(x):
    return jnp.minimum(x, 0.0) - jnp.log1p(jnp.exp(-jnp.abs(x)))


def _gelu_tanh(x):
    return x * (0.5 * (1.0 + jnp.tanh(0.7978845608028654 * (x + 0.044715 * (x * x * x)))))


def _ffn_kernel(*refs, two_src, final_norm, side_jobs, tf):
    nj = pl.cdiv(FF, tf)
    tf_last = FF - (nj - 1) * tf
    assert tf_last % 128 == 0
    refs = list(refs)
    if two_src:
        xp_ref, xs_ref = refs[:2]
        refs = refs[2:]
    else:
        xp_ref = xs_ref = refs[0]
        refs = refs[1:]
    g_ref, wg_ref, wu_ref, wd_ref = refs[:4]
    refs = refs[4:]
    n_in = sum(n for n, _, _ in side_jobs)
    n_out = sum(n for _, n, _ in side_jobs)
    if final_norm:
        gf_ref = refs[0]
        refs = refs[1:]
    side_in = refs[:n_in]
    refs = refs[n_in:]
    if final_norm:
        op_ref, os_ref = refs[:2]
        refs = refs[2:]
    else:
        o_ref = refs[0]
        refs = refs[1:]
    side_out = refs[:n_out]
    h_scr, acc_scr = refs[n_out:]
    j = pl.program_id(1)

    def side_work():
        a = b = 0
        for ni, no, fn in side_jobs:
            fn(side_in[a:a + ni], side_out[b:b + no])
            a += ni
            b += no

    def tile(rows, x_ref):
        rs = slice(0, rows)

        @pl.when(j == 0)
        def _():
            h_scr[rs, :] = _rms(x_ref[rs, :], g_ref[...]).astype(BF16)
            acc_scr[rs, :] = jnp.zeros((rows, D), F32)

        def accumulate(width):
            side_work()
            h = h_scr[rs, :]
            gt = _dot(h, wg_ref[:, 0:width])
            hid = (gt * _sigmoid(gt)) * _dot(h, wu_ref[:, 0:width])
            acc_scr[rs, :] += _dot(hid.astype(BF16), wd_ref[0:width, :])

        pl.when(j < nj - 1)(functools.partial(accumulate, tf))

        @pl.when(j == nj - 1)
        def _():
            accumulate(tf_last)
            y = x_ref[rs, :] + 0.5 * acc_scr[rs, :]
            if final_norm:
                y = _rms(y, gf_ref[...])
                (op_ref if rows == TM else os_ref)[...] = y
            else:
                o_ref[rs, :] = y
                if rows < TM:
                    o_ref[rows:, :] = jnp.zeros((TM - rows, D), F32)

    pl.when(pl.program_id(0) < NPT)(functools.partial(tile, TM, xp_ref))
    pl.when(pl.program_id(0) == NPT)(functools.partial(tile, NS, xs_ref))


class _SideJob(NamedTuple):
    blocks: int
    args: list
    in_specs: list
    out_shape: list
    out_specs: list
    fn: Callable


def _step_block(nj, blocks, first=0):
    return lambda i, j: first + jnp.minimum(i * nj + j, blocks - 1)


def _cast_job(a, rb, nj):
    rows_a, cols_a = a.shape[-2:]
    blk = _step_block(nj, rows_a // rb)

    def fn(ins, outs):
        outs[0][...] = ins[0][...].astype(BF16)

    return _SideJob(rows_a // rb, [a], [pl.BlockSpec((None, rb, cols_a), lambda i, j: (0, blk(i, j), 0))],
                    [jax.ShapeDtypeStruct((rows_a, cols_a), BF16)], [pl.BlockSpec((rb, cols_a), lambda i, j: (blk(i, j), 0))],
                    fn)


def _transpose_job(wt, row0, cols, nj):
    k = wt.shape[1]
    ksplit = 2
    kb = k // ksplit
    shift = row0 % 128
    assert shift in (0, 8) and cols % 128 == 0
    nblk = cols // 128 * ksplit
    unit = _step_block(nj, nblk)
    first = row0 // 128
    last = pl.cdiv(wt.shape[0], 128) - 1
    specs = [pl.BlockSpec((128, kb), lambda i, j: (first + unit(i, j) // ksplit, unit(i, j) % ksplit))]
    if shift:
        specs.append(pl.BlockSpec(
            (128, kb), lambda i, j: (jnp.minimum(first + unit(i, j) // ksplit + 1, last), unit(i, j) % ksplit)))

    def fn(ins, outs):
        win = ins[0][...]
        if shift:
            win = jnp.concatenate([win[shift:, :], ins[1][0:shift, :]], axis=0)
        outs[0][...] = win.T.astype(BF16)

    return _SideJob(nblk, [wt] * len(specs), specs, [jax.ShapeDtypeStruct((k, cols), BF16)],
                    [pl.BlockSpec((kb, 128), lambda i, j: (unit(i, j) % ksplit, unit(i, j) // ksplit))], fn)


def _ffn(xs, g, wg, wu, wd, tf, g_final=None, side_jobs=()):
    nj = pl.cdiv(FF, tf)
    two_src = len(xs) == 2
    final_norm = g_final is not None
    row = pl.BlockSpec((TM, D), lambda i, j: (i, 0))
    prow = pl.BlockSpec((TM, D), lambda i, j: (jnp.minimum(i, NPT - 1), 0))
    srow = pl.BlockSpec((NS, D), lambda i, j: (0, 0))
    vec = pl.BlockSpec((1, D), lambda i, j: (0, 0))
    in_specs = ([prow, srow] if two_src else [row]) + [
        vec,
        pl.BlockSpec((D, tf), lambda i, j: (0, j)),
        pl.BlockSpec((D, tf), lambda i, j: (0, j)),
        pl.BlockSpec((tf, D), lambda i, j: (j, 0)),
    ]
    args = list(xs) + [g, wg, wu, wd]
    if final_norm:
        in_specs.append(vec)
        args.append(g_final)
        out_shape = [jax.ShapeDtypeStruct((NP, D), F32), jax.ShapeDtypeStruct((NS, D), F32)]
        out_specs = [prow, srow]
    else:
        out_shape = [jax.ShapeDtypeStruct((MROWS, D), F32)]
        out_specs = [row]
    jobs = []
    for make_job in side_jobs:
        job = make_job(nj)
        assert job.blocks <= NPT * nj, "side jobs must fit under the full-size row tiles"
        in_specs += job.in_specs
        args += job.args
        out_specs += job.out_specs
        out_shape += job.out_shape
        jobs.append((len(job.args), len(job.out_shape), job.fn))
    return pl.pallas_call(
        functools.partial(_ffn_kernel, two_src=two_src, final_norm=final_norm, side_jobs=tuple(jobs), tf=tf),
        grid=(NT, nj),
        in_specs=in_specs,
        out_specs=out_specs,
        out_shape=out_shape,
        scratch_shapes=[pltpu.VMEM((TM, D), BF16), pltpu.VMEM((TM, D), F32)],
        compiler_params=_cp(("arbitrary", "arbitrary")),
        name="ffn_final" if final_norm else "ffn",
    )(*args)


NF32 = 2
TMI = 256
NPI = NP // TMI
GATE0 = 5 * W
HEAD = GATE0 + W + 128
MIX0 = GATE0 + 2 * NH + W


def _in_proj_kernel(x_ref, g_ref, w_ref, wift_ref,
                    ofp_ref, obp_ref, zifp_ref, ztp_ref, ofs_ref, obs_ref, zifs_ref):
    def emit(of_ref, ob_ref, zif_ref, zt_ref):
        h = _rms(x_ref[...], g_ref[...]).astype(BF16)
        if zt_ref is not None:
            zt_ref[...] = _dot_nt(wift_ref[...], h)
        of_ref[:, 0:W] = _dot(h, w_ref[:, 0:W])
        for j in range(1, 4):
            ob_ref[:, (j - 1) * W:j * W] = _dot(h, w_ref[:, j * W:(j + 1) * W]).astype(BF16)
        of_ref[:, W:2 * W] = _dot(h, w_ref[:, 4 * W:5 * W])
        tail = _dot(h, w_ref[:, 5 * W:HEAD])
        zif_ref[...] = tail[:, 0:128]
        ob_ref[:, 3 * W:4 * W] = tail[:, 2 * NH:2 * NH + W].astype(BF16)

    pl.when(pl.program_id(0) < NPI)(functools.partial(emit, ofp_ref, obp_ref, zifp_ref, ztp_ref))
    pl.when(pl.program_id(0) == NPI)(functools.partial(emit, ofs_ref, obs_ref, zifs_ref, None))


def _in_proj(x, g, w, w_ift):
    once = pl.Buffered(1)
    pblk = lambda cols: pl.BlockSpec((TMI, cols), lambda i: (jnp.minimum(i, NPI - 1), 0))
    sblk = lambda cols: pl.BlockSpec((TMI, cols), lambda i: (0, 0))
    widths = (NF32 * W, 4 * W, 128)
    dtypes = (F32, BF16, F32)
    return pl.pallas_call(
        _in_proj_kernel,
        grid=(NPI + 1,),
        in_specs=[
            pl.BlockSpec((TMI, D), lambda i: (i, 0)),
            pl.BlockSpec((1, D), lambda i: (0, 0)),
            pl.BlockSpec((D, HEAD), lambda i: (0, 0), pipeline_mode=once),
            pl.BlockSpec((8, D), lambda i: (0, 0), pipeline_mode=once),
        ],
        out_specs=[pblk(c) for c in widths]
        + [pl.BlockSpec((8, TMI), lambda i: (0, jnp.minimum(i, NPI - 1)))]
        + [sblk(c) for c in widths],
        out_shape=[jax.ShapeDtypeStruct((NP, c), t) for c, t in zip(widths, dtypes)]
        + [jax.ShapeDtypeStruct((8, NP), F32)]
        + [jax.ShapeDtypeStruct((TMI, c), t) for c, t in zip(widths, dtypes)],
        compiler_params=_cp(("arbitrary",)),
        name="in_proj",
    )(x, g, w, w_ift)


def _mem_proj_kernel(x_ref, g_ref, wk_ref, wv_ref, k_ref, v_ref):
    h = _rms(x_ref[...], g_ref[...]).astype(BF16)
    k_ref[...] = _dot(h, wk_ref[...])
    v_ref[...] = _dot(h, wv_ref[...])


def _mem_proj(x, g, wk, wv):
    m = x.shape[0]
    wspec = pl.BlockSpec((D, W), lambda i: (0, 0))
    ospec = pl.BlockSpec((TM, W), lambda i: (i, 0))
    return pl.pallas_call(
        _mem_proj_kernel,
        grid=(m // TM,),
        in_specs=[pl.BlockSpec((TM, D), lambda i: (i, 0)), pl.BlockSpec((1, D), lambda i: (0, 0)), wspec, wspec],
        out_specs=[ospec, ospec],
        out_shape=[jax.ShapeDtypeStruct((m, W), F32)] * 2,
        compiler_params=_cp(("arbitrary",)),
        name="mem_proj",
    )(x, g, wk, wv)


def _s5_prep_kernel(lr_ref, li_ref, ls_ref, btr_ref, bti_ref, cr_ref, ci_ref, d_ref,
                    ar_ref, ai_ref, a16r_ref, a16i_ref, bbr_ref, bbi_ref, e_ref, ft_ref, bd_ref):
    lr = lr_ref[0]
    li = li_ref[0]
    dt = jnp.exp(ls_ref[0])

    def power(k):
        mag = jnp.exp(lr * dt * float(k))
        ang = li * dt * float(k)
        return mag * jnp.cos(ang), mag * jnp.sin(ang)

    pw = [power(k) for k in range(TC + 1)]
    ar, ai = pw[1]
    den = lr * lr + li * li
    nr = ar - 1.0
    z_re = (nr * lr + ai * li) / den
    z_im = (ai * lr - nr * li) / den
    mask = (lax.broadcasted_iota(jnp.int32, (WU, WS), 0) // HG
            == lax.broadcasted_iota(jnp.int32, (WU, WS), 1) // P)
    btr = btr_ref[0]
    bti = bti_ref[0]
    bbr = jnp.where(mask, z_re * btr - z_im * bti, 0.0)
    bbi = jnp.where(mask, z_re * bti + z_im * btr, 0.0)
    cr = jnp.where(mask, cr_ref[0], 0.0)
    ci = jnp.where(mask, ci_ref[0], 0.0)

    ar_ref[0] = ar
    ai_ref[0] = ai
    a16r_ref[0] = pw[TC][0]
    a16i_ref[0] = pw[TC][1]
    bbr_ref[0] = bbr
    bbi_ref[0] = bbi

    def cmul(xr, xi, k):
        pr, pi = pw[k]
        return xr * pr - xi * pi, xr * pi + xi * pr

    diag = (lax.broadcasted_iota(jnp.int32, (WU, WU), 0) == lax.broadcasted_iota(jnp.int32, (WU, WU), 1))
    for s in range(TC):
        er, ei = cmul(bbr, bbi, TC - 1 - s)
        e_ref[0, s * WU:(s + 1) * WU, 0:WS] = er.astype(BF16)
        e_ref[0, s * WU:(s + 1) * WU, WS:2 * WS] = ei.astype(BF16)
        fr, fi = cmul(cr, ci, s + 1)
        ft_ref[0, s * WU:(s + 1) * WU, 0:WS] = fr.astype(BF16)
        ft_ref[0, s * WU:(s + 1) * WU, WS:2 * WS] = (-fi).astype(BF16)
        rr, ri = cmul(cr, ci, s)
        kern = _dot3_nt(bbr, rr) - _dot3_nt(bbi, ri)
        if s == 0:
            kern = kern + jnp.where(diag, d_ref[0], 0.0)
        bd_ref[0, s] = kern


def _s5_prep(lam_re, lam_im, log_step, bt_re, bt_im, c_re, c_im, d):
    def spec(*shape):
        nd = len(shape)
        return pl.BlockSpec((1,) + shape, lambda i: (i,) + (0,) * nd)

    def sds(*shape, dtype=F32):
        return jax.ShapeDtypeStruct((NGB,) + shape, dtype)

    return pl.pallas_call(
        _s5_prep_kernel,
        grid=(NGB,),
        in_specs=[spec(1, WS)] * 3 + [spec(WU, WS)] * 4 + [spec(WU, 1)],
        out_specs=[spec(1, WS)] * 4 + [spec(WU, WS)] * 2 + [spec(TC * WU, 2 * WS)] * 2 + [spec(TC, WU, WU)],
        out_shape=[sds(1, WS)] * 4 + [sds(WU, WS)] * 2 + [sds(TC * WU, 2 * WS, dtype=BF16)] * 2
        + [sds(TC, WU, WU)],
        compiler_params=_cp(("arbitrary",)),
        name="s5_prep",
    )(lam_re, lam_im, log_step, bt_re, bt_im, c_re, c_im, d)


def _s5_chunk_kernel(u_ref, bd_ref, e_ref, ft_ref, ar_ref, ai_ref, y_ref, fin_ref,
                     w_scr, lhs_scr, s_scr, xs_scr):
    rows = S5B * NCH

    @pl.when(pl.program_id(1) == 0)
    def _():
        w_scr[...] = jnp.zeros_like(w_scr)
        bd = [bd_ref[0, k].astype(BF16) for k in range(TC)]
        for s in range(TC):
            for t in range(s, TC):
                w_scr[s * WU:(s + 1) * WU, t * WU:(t + 1) * WU] = bd[t - s]

    for s in range(TC):
        lhs_scr[:, s * WU:(s + 1) * WU] = u_ref[pl.ds(s, rows, stride=TC), :].astype(BF16)
    lhs = lhs_scr[...]
    s_loc = _dot(lhs, e_ref[0])
    nl = WS // 128
    for k in range(2 * nl):
        s_scr[k] = s_loc[:, k * 128:(k + 1) * 128]
    ar = [ar_ref[0, :, k * 128:(k + 1) * 128] for k in range(nl)]
    ai = [ai_ref[0, :, k * 128:(k + 1) * 128] for k in range(nl)]
    xr = [jnp.zeros((S5B, 128), F32)] * nl
    xi = [jnp.zeros((S5B, 128), F32)] * nl
    for c in range(NCH):
        chunk_rows = pl.ds(c, S5B, stride=NCH)
        for k in range(nl):
            xs_scr[k, chunk_rows, :] = xr[k]
            xs_scr[nl + k, chunk_rows, :] = xi[k]
            sr = s_scr[k, chunk_rows, :]
            si = s_scr[nl + k, chunk_rows, :]
            xr[k], xi[k] = ar[k] * xr[k] - ai[k] * xi[k] + sr, ar[k] * xi[k] + ai[k] * xr[k] + si
    for k in range(nl):
        fin_ref[0, 0, :, k * 128:(k + 1) * 128] = xr[k]
        fin_ref[0, 0, :, WS + k * 128:WS + (k + 1) * 128] = xi[k]
    xs = jnp.concatenate([xs_scr[k] for k in range(2 * nl)], axis=1)
    y = _dot(lhs, w_scr[...]) + _dot_nt(xs.astype(BF16), ft_ref[0])
    for t in range(TC):
        y_ref[pl.ds(t, rows, stride=TC), :] = y[:, t * WU:(t + 1) * WU]


def _s5_chunk(z_f, bd, e, ft, a16r, a16i):
    rows = S5B * NCH
    nh = BATCH // S5B
    return pl.pallas_call(
        _s5_chunk_kernel,
        grid=(NGB, nh),
        in_specs=[
            pl.BlockSpec((S5B * SEQ, WU), lambda j, b: (b, j)),
            pl.BlockSpec((1, TC, WU, WU), lambda j, b: (j, 0, 0, 0)),
            pl.BlockSpec((1, TC * WU, 2 * WS), lambda j, b: (j, 0, 0)),
            pl.BlockSpec((1, TC * WU, 2 * WS), lambda j, b: (j, 0, 0)),
            pl.BlockSpec((1, 1, WS), lambda j, b: (j, 0, 0)),
            pl.BlockSpec((1, 1, WS), lambda j, b: (j, 0, 0)),
        ],
        out_specs=[
            pl.BlockSpec((S5B * SEQ, WU), lambda j, b: (b, j)),
            pl.BlockSpec((1, 1, S5B, 2 * WS), lambda j, b: (j, b, 0, 0)),
        ],
        out_shape=[jax.ShapeDtypeStruct((NP, W), F32), jax.ShapeDtypeStruct((NGB, nh, S5B, 2 * WS), F32)],
        scratch_shapes=[pltpu.VMEM((TC * WU, TC * WU), BF16), pltpu.VMEM((rows, TC * WU), BF16),
                        pltpu.VMEM((2 * WS // 128, rows, 128), F32), pltpu.VMEM((2 * WS // 128, rows, 128), F32)],
        compiler_params=_cp(("arbitrary", "arbitrary")),
        name="s5_chunk",
    )(z_f, bd, e, ft, a16r, a16i)


def _s5_step_kernel(u_ref, sr_ref, si_ref, ar_ref, ai_ref, bbr_ref, bbi_ref, cr_ref, ci_ref, d_ref,
                    y_ref, xr_ref, xi_ref):
    mask = (lax.broadcasted_iota(jnp.int32, (WU, WS), 0) // HG
            == lax.broadcasted_iota(jnp.int32, (WU, WS), 1) // P)
    for j in range(NGB):
        ul = slice(j * WU, (j + 1) * WU)
        sl = slice(j * WS, (j + 1) * WS)
        us = u_ref[:, ul]
        ar = ar_ref[j]
        ai = ai_ref[j]
        s_re = sr_ref[:, sl]
        s_im = si_ref[:, sl]
        x_re = ar * s_re - ai * s_im + _dot3(us, bbr_ref[j])
        x_im = ar * s_im + ai * s_re + _dot3(us, bbi_ref[j])
        xr_ref[:, sl] = x_re
        xi_ref[:, sl] = x_im
        cre = jnp.where(mask, cr_ref[j], 0.0)
        cim = jnp.where(mask, ci_ref[j], 0.0)
        y_ref[:, ul] = _dot3_nt(x_re, cre) - _dot3_nt(x_im, cim) + d_ref[:, ul] * us


def _s5_step(z_f, s_re, s_im, a_re, a_im, bb_re, bb_im, c_re, c_im, d_row):
    full = lambda *shape: pl.BlockSpec(shape, lambda i: (0,) * len(shape))
    return pl.pallas_call(
        _s5_step_kernel,
        grid=(1,),
        in_specs=[pl.BlockSpec((NS, W), lambda i: (0, 0)),
                  full(NS, G * P), full(NS, G * P), full(NGB, 1, WS), full(NGB, 1, WS),
                  full(NGB, WU, WS), full(NGB, WU, WS), full(NGB, WU, WS), full(NGB, WU, WS),
                  full(1, W)],
        out_specs=[full(NS, W), full(NS, G * P), full(NS, G * P)],
        out_shape=[jax.ShapeDtypeStruct((NS, W), F32), jax.ShapeDtypeStruct((NS, G * P), F32),
                   jax.ShapeDtypeStruct((NS, G * P), F32)],
        compiler_params=_cp(("arbitrary",)),
        name="s5_step",
    )(z_f, s_re, s_im, a_re, a_im, bb_re, bb_im, c_re, c_im, d_row)


def _mlstm_chunk_kernel(q_ref, k_ref, v_ref, o_ref, zif_ref, zt0_ref, zt1_ref, zt2_ref, zt3_ref,
                        brow_ref, bcol_ref, gh_ref, h_ref, c_ref, n_ref, m_ref):
    @pl.when(pl.program_id(0) == 0)
    def _():
        c_ref[...] = jnp.zeros_like(c_ref)
        n_ref[...] = jnp.zeros_like(n_ref)
        m_ref[...] = jnp.zeros_like(m_ref)

    rr = lax.broadcasted_iota(jnp.int32, (CL, CL), 0)
    cc = lax.broadcasted_iota(jnp.int32, (CL, CL), 1)
    causal = cc <= rr
    tril = jnp.where(causal, 1.0, 0.0).astype(BF16)
    triu = jnp.where(rr <= cc, 1.0, 0.0).astype(BF16)

    for b, zt_ref in enumerate((zt0_ref, zt1_ref, zt2_ref, zt3_ref)):
        zi = zif_ref[b] + brow_ref[...]
        zt = zt_ref[...] + bcol_ref[...]
        lfc = _split3(_log_sigmoid(zi))
        bcum_col = _dot(tril, lfc[0]) + (_dot(tril, lfc[1]) + _dot(tril, lfc[2]))
        lfr = _split3(_log_sigmoid(zt))
        bcum_row = _dot(lfr[0], triu) + (_dot(lfr[1], triu) + _dot(lfr[2], triu))

        for hh in range(NH):
            hs = slice(hh * DH, (hh + 1) * DH)
            bc = bcum_col[:, NH + hh:NH + hh + 1]
            ic = zi[:, hh:hh + 1]
            br = bcum_row[NH + hh:NH + hh + 1, :]
            ir = zt[hh:hh + 1, :]
            m_prev = m_ref[b, hh:hh + 1, 0:1]
            g_inter = bc + m_prev
            dlog = jnp.where(causal, (bc - br) + ir, -jnp.inf)
            m_t = jnp.maximum(g_inter, jnp.max(dlog, axis=-1, keepdims=True))
            w_inter = jnp.exp(g_inter - m_t)
            w_intra = jnp.exp(dlog - m_t)
            qb = q_ref[b, :, hs]
            kb = k_ref[b, :, hs]
            vb = v_ref[b, :, hs]
            qf = qb.astype(F32)
            kf = kb.astype(F32)
            s = _dot_nt(qb, kb) * (w_intra * (DH ** -0.5))
            c_prev = c_ref[b, hh]
            n_prev = n_ref[b, hh:hh + 1, :]
            num = _dot(s.astype(BF16), vb) + w_inter * _dot(qb, c_prev.astype(BF16))
            nq = jnp.sum(s, axis=-1, keepdims=True) + w_inter * jnp.sum(qf * n_prev, axis=-1, keepdims=True)
            h = num / jnp.maximum(jnp.abs(nq), jnp.exp(-m_t))
            m_last = m_t[CL - 1:CL, :]
            w_last = jnp.exp((bc[CL - 1:CL, :] - bc) + ic - m_last) * (DH ** -0.5)
            wi_last = w_inter[CL - 1:CL, :]
            kw = kf * w_last
            c_ref[b, hh] = wi_last * c_prev + _dot_tn(kw.astype(BF16), vb)
            n_ref[b, hh:hh + 1, :] = wi_last * n_prev + jnp.sum(kw, axis=0, keepdims=True)
            m_ref[b, hh:hh + 1, :] = jnp.broadcast_to(m_last, (1, 128))
            hn = _rms(h, gh_ref[:, hs])
            h_ref[b, :, hs] = (hn * _sigmoid(o_ref[b, :, hs])).astype(BF16)


def _mlstm_chunk(z_f, z_b, z_if, z_t, brow, bcol, gh):
    blk = lambda col: pl.BlockSpec((BATCH, CL, W), lambda c: (0, c, col))
    const = lambda *shape: pl.BlockSpec(shape, lambda c: (0,) * len(shape))
    zt_specs = [pl.BlockSpec((8, CL), functools.partial(lambda c, b: (0, b * NCL + c), b=b)) for b in range(BATCH)]
    return pl.pallas_call(
        _mlstm_chunk_kernel,
        grid=(NCL,),
        in_specs=[blk(0), blk(1), blk(2), blk(1), pl.BlockSpec((BATCH, CL, 128), lambda c: (0, c, 0))]
        + zt_specs + [const(1, 128), const(8, 1), const(1, W)],
        out_specs=[pl.BlockSpec((BATCH, CL, W), lambda c: (0, c, 0)),
                   const(BATCH, NH, DH, DH), const(BATCH, NH, DH), const(BATCH, NH, 128)],
        out_shape=[jax.ShapeDtypeStruct((BATCH, SEQ, W), BF16),
                   jax.ShapeDtypeStruct((BATCH, NH, DH, DH), F32),
                   jax.ShapeDtypeStruct((BATCH, NH, DH), F32),
                   jax.ShapeDtypeStruct((BATCH, NH, 128), F32)],
        compiler_params=_cp(("arbitrary",)),
        name="mlstm_chunk",
    )(z_b, z_b, z_b, z_f, z_if, z_t, z_t, z_t, z_t, brow, bcol, gh)


MSR = 64


def _mlstm_step_kernel(q_ref, k_ref, v_ref, o_ref, zif_ref, brow_ref, gh_ref, c0_ref, n0_ref, m0_ref,
                       h_ref, c_ref, n_ref, m_ref):
    zi = zif_ref[...] + brow_ref[...]
    ig = zi[:, 0:NH]
    g_inter = _log_sigmoid(zi[:, NH:2 * NH]) + m0_ref[...]
    m_t = jnp.maximum(g_inter, ig)
    w_inter = jnp.exp(g_inter - m_t)
    w_intra = jnp.exp(ig - m_t) * (DH ** -0.5)
    floor = jnp.exp(-m_t)
    m_ref[...] = m_t

    def heads_on_rows(x):
        return jnp.pad(x, ((0, 128 - SB), (0, 128 - NH))).T

    w_inter_t = heads_on_rows(w_inter)
    w_intra_t = heads_on_rows(w_intra)
    floor_t = heads_on_rows(floor)
    pad = jnp.zeros((128 - SB, DH), F32)
    q_t = [jnp.concatenate([q_ref[:, hh * DH:(hh + 1) * DH], pad], axis=0).T for hh in range(NH)]
    k_t = [jnp.concatenate([k_ref[:, hh * DH:(hh + 1) * DH], pad], axis=0).T for hh in range(NH)]

    def per_head_rows(ref, s):
        return jnp.concatenate([ref[s:s + 1, hh * DH:(hh + 1) * DH] for hh in range(NH)], axis=0)

    for s in range(SB):
        q4 = per_head_rows(q_ref, s)
        k4 = per_head_rows(k_ref, s)
        v4 = per_head_rows(v_ref, s)
        qc_rows = []
        for hh in range(NH):
            wi = w_inter[s:s + 1, hh:hh + 1]
            vw = v4[hh:hh + 1, :] * w_intra[s:s + 1, hh:hh + 1]
            acc = jnp.zeros((MSR, DH), F32)
            for r0 in range(0, DH, MSR):
                c_blk = c0_ref[s, hh, r0:r0 + MSR, :]
                acc = acc + q_t[hh][r0:r0 + MSR, s:s + 1] * c_blk
                c_ref[s, hh, r0:r0 + MSR, :] = wi * c_blk + k_t[hh][r0:r0 + MSR, s:s + 1] * vw
            qc_rows.append(jnp.sum(acc, axis=0, keepdims=True))
        q_c = jnp.concatenate(qc_rows, axis=0)
        wi_c = w_inter_t[0:NH, s:s + 1]
        wa_c = w_intra_t[0:NH, s:s + 1]
        n_prev = n0_ref[s]
        sv = jnp.sum(q4 * k4, axis=-1, keepdims=True) * wa_c
        num = sv * v4 + wi_c * q_c
        nq = sv + wi_c * jnp.sum(q4 * n_prev, axis=-1, keepdims=True)
        h = num / jnp.maximum(jnp.abs(nq), floor_t[0:NH, s:s + 1])
        n_ref[s] = wi_c * n_prev + wa_c * k4
        out = _rms(h, gh_ref[...]) * _sigmoid(per_head_rows(o_ref, s))
        for hh in range(NH):
            h_ref[s:s + 1, hh * DH:(hh + 1) * DH] = out[hh:hh + 1, :]


def _mlstm_step(zs, z_f, z_if, brow, gh, c0, n0, m0):
    blk = lambda col: pl.BlockSpec((SB, W), lambda i: (i, col))
    return pl.pallas_call(
        _mlstm_step_kernel,
        grid=(NS // SB,),
        in_specs=[blk(0), blk(1), blk(2), blk(1),
                  pl.BlockSpec((SB, 128), lambda i: (i, 0)),
                  pl.BlockSpec((1, 128), lambda i: (0, 0)),
                  pl.BlockSpec((NH, DH), lambda i: (0, 0)),
                  pl.BlockSpec((SB, NH, DH, DH), lambda i: (i, 0, 0, 0)),
                  pl.BlockSpec((SB, NH, DH), lambda i: (i, 0, 0)),
                  pl.BlockSpec((SB, NH), lambda i: (i, 0))],
        out_specs=[pl.BlockSpec((SB, W), lambda i: (i, 0)),
                   pl.BlockSpec((SB, NH, DH, DH), lambda i: (i, 0, 0, 0)),
                   pl.BlockSpec((SB, NH, DH), lambda i: (i, 0, 0)),
                   pl.BlockSpec((SB, NH), lambda i: (i, 0))],
        out_shape=[jax.ShapeDtypeStruct((NS, W), F32),
                   jax.ShapeDtypeStruct((NS, NH, DH, DH), F32),
                   jax.ShapeDtypeStruct((NS, NH, DH), F32),
                   jax.ShapeDtypeStruct((NS, NH), F32)],
        compiler_params=_cp(("arbitrary",)),
        name="mlstm_step",
    )(zs, zs, zs, z_f, z_if, brow, gh, c0, n0, m0)


def _softmax_rows(s):
    e = jnp.exp(s - jnp.max(s, axis=-1, keepdims=True))
    return e / jnp.sum(e, axis=-1, keepdims=True)


def _xattn_prompt_kernel(q_ref, k_ref, v_ref, o_ref):
    for hh in range(NH):
        hs = slice(hh * DH, (hh + 1) * DH)
        s = _dot_nt(q_ref[:, hs], k_ref[:, hs].astype(BF16)) * (DH ** -0.5)
        p = _softmax_rows(s)
        o_ref[:, hs] = _dot(p.astype(BF16), v_ref[:, hs].astype(BF16)).astype(BF16)


def _xattn_prompt(z_b, mem_k, mem_v):
    nt = SEQ // TM
    return pl.pallas_call(
        _xattn_prompt_kernel,
        grid=(BATCH, nt),
        in_specs=[pl.BlockSpec((TM, W), lambda b, t: (b * nt + t, 3)),
                  pl.BlockSpec((N_MEM, W), lambda b, t: (b, 0)),
                  pl.BlockSpec((N_MEM, W), lambda b, t: (b, 0))],
        out_specs=pl.BlockSpec((TM, W), lambda b, t: (b * nt + t, 0)),
        out_shape=jax.ShapeDtypeStruct((NP, W), BF16),
        compiler_params=_cp(("arbitrary", "arbitrary")),
        name="xattn_prompt",
    )(z_b, mem_k, mem_v)


XS = 4


XMC = 64


def _xattn_step_kernel(q_ref, k_ref, v_ref, o_ref):
    def part(t):
        for s in range(XS):
            r = t * XS + s
            q4 = jnp.concatenate([q_ref[r:r + 1, hh * DH:(hh + 1) * DH] for hh in range(NH)], axis=0)
            sc = jnp.concatenate(
                [jnp.sum(k_ref[s, m0:m0 + XMC] * q4[None], axis=-1, keepdims=True) for m0 in range(0, N_MEM, XMC)],
                axis=0) * (DH ** -0.5)
            e = jnp.exp(sc - jnp.max(sc, axis=0, keepdims=True))
            p = e / jnp.sum(e, axis=0, keepdims=True)
            acc = jnp.zeros((NH, DH), F32)
            for m0 in range(0, N_MEM, XMC):
                acc = acc + jnp.sum(p[m0:m0 + XMC] * v_ref[s, m0:m0 + XMC], axis=0)
            for hh in range(NH):
                o_ref[r:r + 1, hh * DH:(hh + 1) * DH] = acc[hh:hh + 1, :]

    for t in range(SB // XS):
        pl.when(pl.program_id(1) == t)(functools.partial(part, t))


def _xattn_step(zs, mem_k, mem_v):
    nt = SB // XS
    return pl.pallas_call(
        _xattn_step_kernel,
        grid=(NS // SB, nt),
        in_specs=[pl.BlockSpec((SB, W), lambda i, t: (i, 3)),
                  pl.BlockSpec((XS, N_MEM, NH, DH), lambda i, t: (i * nt + t, 0, 0, 0)),
                  pl.BlockSpec((XS, N_MEM, NH, DH), lambda i, t: (i * nt + t, 0, 0, 0))],
        out_specs=pl.BlockSpec((SB, W), lambda i, t: (i, 0)),
        out_shape=jax.ShapeDtypeStruct((NS, W), F32),
        compiler_params=_cp(("arbitrary", "arbitrary")),
        name="xattn_step",
    )(zs, mem_k, mem_v)


TMX = 512


def _mix_kernel(x_ref, g_ref, yp_ref, ys_ref, mp_ref, ms_ref, ap_ref, as_ref, wglu_ref,
                wg0_ref, wg1_ref, wg2_ref, wb0_ref, wb1_ref, wb2_ref, o_ref, h_scr, s5_scr):
    def tile(rows, y_ref, ml_ref, xa_ref):
        rs = slice(0, rows)

        @pl.when(pl.program_id(1) == 0)
        def _():
            h_scr[rs, :] = _rms(x_ref[rs, :], g_ref[...]).astype(BF16)
            y = _gelu_tanh(y_ref[...])
            s5_scr[rs, :] = (y * _sigmoid(_dot(y.astype(BF16), wglu_ref[...]))).astype(BF16)

        h = h_scr[rs, :]
        merged = (_sigmoid(_dot(h, wg0_ref[...])) * _dot(s5_scr[rs, :], wb0_ref[...])
                  + _sigmoid(_dot(h, wg1_ref[...])) * _dot(ml_ref[...].astype(BF16), wb1_ref[...])
                  + _sigmoid(_dot(h, wg2_ref[...])) * _dot(xa_ref[...].astype(BF16), wb2_ref[...]))
        o_ref[rs, :] = merged.astype(BF16)
        if rows < TM:
            o_ref[rows:, :] = jnp.zeros((TM - rows, TMX), BF16)

    pl.when(pl.program_id(0) < NPT)(functools.partial(tile, TM, yp_ref, mp_ref, ap_ref))
    pl.when(pl.program_id(0) == NPT)(functools.partial(tile, NS, ys_ref, ms_ref, as_ref))


def _mix(x1, g, yp, ys, mp, ms, ap, as_, wglu, wg, wb):
    prow = pl.BlockSpec((TM, W), lambda i, j: (jnp.minimum(i, NPT - 1), 0))
    srow = pl.BlockSpec((NS, W), lambda i, j: (0, 0))
    nx = D // TMX
    wgs = [pl.BlockSpec((D, TMX), functools.partial(lambda i, j, b: (0, b * nx + j), b=b)) for b in range(3)]
    wbs = pl.BlockSpec((W, TMX), lambda i, j: (0, j))
    return pl.pallas_call(
        _mix_kernel,
        grid=(NT, D // TMX),
        in_specs=[pl.BlockSpec((TM, D), lambda i, j: (i, 0)), pl.BlockSpec((1, D), lambda i, j: (0, 0)),
                  prow, srow, prow, srow, prow, srow,
                  pl.BlockSpec((W, W), lambda i, j: (0, 0)),
                  *wgs, wbs, wbs, wbs],
        out_specs=pl.BlockSpec((TM, TMX), lambda i, j: (i, j)),
        out_shape=jax.ShapeDtypeStruct((MROWS, D), BF16),
        scratch_shapes=[pltpu.VMEM((TM, D), BF16), pltpu.VMEM((TM, W), BF16)],
        compiler_params=_cp(("arbitrary", "arbitrary")),
        name="mix",
    )(x1, g, yp, ys, mp, ms, ap, as_, wglu, wg, wg, wg, *wb)


def _outproj_kernel(x_ref, m_ref, w_ref, o_ref):
    def tile(rows):
        rs = slice(0, rows)
        o_ref[rs, :] = x_ref[rs, :] + _dot(m_ref[rs, :], w_ref[...])
        if rows < TM:
            o_ref[rows:, :] = jnp.zeros((TM - rows, D), F32)

    pl.when(pl.program_id(0) < NPT)(functools.partial(tile, TM))
    pl.when(pl.program_id(0) == NPT)(functools.partial(tile, NS))


def _outproj(x1, merged, w_out):
    return pl.pallas_call(
        _outproj_kernel,
        grid=(NT,),
        in_specs=[pl.BlockSpec((TM, D), lambda i: (i, 0)), pl.BlockSpec((TM, D), lambda i: (i, 0)),
                  pl.BlockSpec((D, D), lambda i: (0, 0))],
        out_specs=pl.BlockSpec((TM, D), lambda i: (i, 0)),
        out_shape=jax.ShapeDtypeStruct((MROWS, D), F32),
        compiler_params=_cp(("arbitrary",)),
        name="out_proj",
    )(x1, merged, w_out)


def kernel(x_prompt, x_sample, mem_prompt, cache_mem_k, cache_mem_v, state_s5_re, state_s5_im, state_mlstm_C,
           state_mlstm_n, state_mlstm_m, g_ffn1, w1_gate, w1_up, w1_down, g_mix, w_in, s5_lambda_re,
           s5_lambda_im, s5_log_step, s5_b_re, s5_b_im, s5_c_re, s5_c_im, s5_d, w_s5_glu, b_igate, b_fgate,
           g_mlstm_head, g_mem, w_mem_k, w_mem_v, w_br_s5, w_br_ml, w_br_xa, w_out, g_ffn2, w2_gate, w2_up,
           w2_down, g_final):
    bf = lambda a: a.astype(BF16)

    w_in_t = w_in[0].T
    job = functools.partial
    (x1, w2g, w2u, w2d, w_head, w_bg, wb_s5, wb_ml, wb_xa, w_o, w_glu, w_mk, w_mv) = _ffn(
        (x_prompt.reshape(NP, D), x_sample.reshape(NS, D)), g_ffn1[0].reshape(1, D),
        bf(w1_gate[0]), bf(w1_up[0]), bf(w1_down[0]), TF,
        side_jobs=(job(_cast_job, w2_gate, 16), job(_cast_job, w2_up, 16), job(_cast_job, w2_down, 32),
                   job(_transpose_job, w_in_t, 0, HEAD), job(_transpose_job, w_in_t, MIX0, 3 * D),
                   job(_cast_job, w_br_s5, 16), job(_cast_job, w_br_ml, 16), job(_cast_job, w_br_xa, 16),
                   job(_cast_job, w_out, 16), job(_cast_job, w_s5_glu, 16),
                   job(_cast_job, w_mem_k, 16), job(_cast_job, w_mem_v, 16)))

    w_ift = bf(w_in_t[GATE0:GATE0 + 2 * NH])
    w_br = [wb_s5, wb_ml, wb_xa]
    brow = jnp.pad(jnp.concatenate([b_igate[0], b_fgate[0]]), (0, 128 - 2 * NH)).reshape(1, 128)
    bcol = jnp.concatenate([b_igate[0], b_fgate[0]]).reshape(2 * NH, 1)
    gh = g_mlstm_head[0].reshape(1, W)

    z_f, z_b, z_if, z_t, zs_f, zs_b, zs_if = _in_proj(x1, g_mix[0].reshape(1, D), w_head, w_ift)
    zs = zs_b[:NS].astype(F32)
    per_seq = lambda a: a.reshape(BATCH, SEQ, a.shape[1])

    blk = lambda a: jnp.tile(a.reshape(NGB, WU, P), (1, 1, GB))
    row = lambda a: a.reshape(NGB, 1, WS)
    (a_re, a_im, a16_re, a16_im, bb_re, bb_im, e_op, ft_op, bd_op) = _s5_prep(
        row(s5_lambda_re[0]), row(s5_lambda_im[0]), row(jnp.repeat(s5_log_step[0], P)),
        blk(s5_b_re[0].transpose(0, 2, 1)), blk(s5_b_im[0].transpose(0, 2, 1)),
        blk(s5_c_re[0]), blk(s5_c_im[0]), s5_d[0].reshape(NGB, WU, 1))
    y_s5_p, fin = _s5_chunk(z_f, bd_op, e_op, ft_op, a16_re, a16_im)
    fin = fin.reshape(NGB, BATCH, 2, GB, P).transpose(2, 1, 0, 3, 4).reshape(2, 1, BATCH, G, P)
    y_s5_s, s5_re_s, s5_im_s = _s5_step(
        zs_f, state_s5_re[0].reshape(NS, G * P), state_s5_im[0].reshape(NS, G * P),
        a_re, a_im, bb_re, bb_im, blk(s5_c_re[0]), blk(s5_c_im[0]), s5_d[0].reshape(1, W))

    ml_p, c_p, n_p, m_p = _mlstm_chunk(per_seq(z_f), per_seq(z_b), per_seq(z_if), z_t, brow, bcol, gh)
    ml_s, c_s, n_s, m_s = _mlstm_step(zs, zs_f, zs_if, brow, gh.reshape(NH, DH), state_mlstm_C[0],
                                      state_mlstm_n[0], state_mlstm_m[0])

    mem_k, mem_v = _mem_proj(mem_prompt.reshape(BATCH * N_MEM, D), g_mem[0].reshape(1, D), w_mk, w_mv)
    xa_p = _xattn_prompt(z_b, mem_k, mem_v)
    xa_s = _xattn_step(zs, cache_mem_k[0], cache_mem_v[0])

    merged = _mix(x1, g_mix[0].reshape(1, D), y_s5_p, y_s5_s, ml_p.reshape(NP, W), ml_s, xa_p, xa_s,
                  w_glu, w_bg, w_br)
    x2 = _outproj(x1, merged, w_o)
    y_p, y_s = _ffn((x2,), g_ffn2[0].reshape(1, D), w2g, w2u, w2d, TF, g_final.reshape(1, D))

    return (y_p.reshape(BATCH, SEQ, D), y_s.reshape(NS, 1, D),
            mem_k.reshape(1, BATCH, N_MEM, NH, DH), mem_v.reshape(1, BATCH, N_MEM, NH, DH),
            fin[0], fin[1], c_p[None], n_p[None], m_p[:, :, 0][None],
            s5_re_s.reshape(1, NS, G, P), s5_im_s.reshape(1, NS, G, P), c_s[None], n_s[None], m_s[None])
```

```python
import functools
from typing import Callable, NamedTuple

import jax
import jax.numpy as jnp
from jax import lax
from jax.experimental import pallas as pl
from jax.experimental.pallas import tpu as pltpu

F32 = jnp.float32
BF16 = jnp.bfloat16

D = 2048
BATCH = 4
SEQ = 2048
NS = 128
NP = BATCH * SEQ
TM = 512
NPT = NP // TM
NT = NPT + 1
MROWS = NT * TM
N_MEM = 256
FF = 5504
TF = 512
G = 64
P = 64
HG = 16
TC = 16
GB = 8
NGB = G // GB
WU = GB * HG
WS = GB * P
NCH = SEQ // TC
S5B = 2
W = 1024
NH = 4
DH = 256
CL = 256
NCL = SEQ // CL
SB = 8
EPS = 1e-6
VMEM_LIMIT = 56 * 1024 * 1024


def _cp(sem, vmem=VMEM_LIMIT):
    return pltpu.CompilerParams(dimension_semantics=sem, vmem_limit_bytes=vmem)


def _dot(a, b):
    return jnp.dot(a, b, preferred_element_type=F32)


def _dot_nt(a, b):
    return lax.dot_general(a, b, (((1,), (1,)), ((), ())), preferred_element_type=F32)


def _dot_tn(a, b):
    return lax.dot_general(a, b, (((0,), (0,)), ((), ())), preferred_element_type=F32)


def _hi_lo(x):
    hi = x.astype(BF16)
    lo = (x - hi.astype(F32)).astype(BF16)
    return hi, lo


def _split3(x):
    hi = x.astype(BF16)
    r1 = x - hi.astype(F32)
    mid = r1.astype(BF16)
    lo = (r1 - mid.astype(F32)).astype(BF16)
    return hi, mid, lo


def _dot3(a, b):
    ah, al = _hi_lo(a)
    bh, bl = _hi_lo(b)
    return _dot(ah, bh) + (_dot(ah, bl) + _dot(al, bh))


def _dot3_nt(a, b):
    ah, al = _hi_lo(a)
    bh, bl = _hi_lo(b)
    return _dot_nt(ah, bh) + (_dot_nt(ah, bl) + _dot_nt(al, bh))


def _rms(x, g):
    r = lax.rsqrt(jnp.mean(x * x, axis=-1, keepdims=True) + EPS)
    return (x * r) * g


def _sigmoid(x):
    return 1.0 / (1.0 + jnp.exp(-x))


def _log_sigmoid(x):
    return jnp.minimum(x, 0.0) - jnp.log1p(jnp.exp(-jnp.abs(x)))


def _gelu_tanh(x):
    return x * (0.5 * (1.0 + jnp.tanh(0.7978845608028654 * (x + 0.044715 * (x * x * x)))))


def _ffn_kernel(*refs, two_src, final_norm, side_jobs, tf):
    nj = pl.cdiv(FF, tf)
    tf_last = FF - (nj - 1) * tf
    assert tf_last % 128 == 0
    refs = list(refs)
    if two_src:
        xp_ref, xs_ref = refs[:2]
        refs = refs[2:]
    else:
        xp_ref = xs_ref = refs[0]
        refs = refs[1:]
    g_ref, wg_ref, wu_ref, wd_ref = refs[:4]
    refs = refs[4:]
    n_in = sum(n for n, _, _ in side_jobs)
    n_out = sum(n for _, n, _ in side_jobs)
    if final_norm:
        gf_ref = refs[0]
        refs = refs[1:]
    side_in = refs[:n_in]
    refs = refs[n_in:]
    if final_norm:
        op_ref, os_ref = refs[:2]
        refs = refs[2:]
    else:
        o_ref = refs[0]
        refs = refs[1:]
    side_out = refs[:n_out]
    h_scr, acc_scr = refs[n_out:]
    j = pl.program_id(1)

    def side_work():
        a = b = 0
        for ni, no, fn in side_jobs:
            fn(side_in[a:a + ni], side_out[b:b + no])
            a += ni
            b += no

    def tile(rows, x_ref):
        rs = slice(0, rows)

        @pl.when(j == 0)
        def _():
            h_scr[rs, :] = _rms(x_ref[rs, :], g_ref[...]).astype(BF16)
            acc_scr[rs, :] = jnp.zeros((rows, D), F32)

        def accumulate(width):
            side_work()
            h = h_scr[rs, :]
            gt = _dot(h, wg_ref[:, 0:width])
            hid = (gt * _sigmoid(gt)) * _dot(h, wu_ref[:, 0:width])
            acc_scr[rs, :] += _dot(hid.astype(BF16), wd_ref[0:width, :])

        pl.when(j < nj - 1)(functools.partial(accumulate, tf))

        @pl.when(j == nj - 1)
        def _():
            accumulate(tf_last)
            y = x_ref[rs, :] + 0.5 * acc_scr[rs, :]
            if final_norm:
                y = _rms(y, gf_ref[...])
                (op_ref if rows == TM else os_ref)[...] = y
            else:
                o_ref[rs, :] = y
                if rows < TM:
                    o_ref[rows:, :] = jnp.zeros((TM - rows, D), F32)

    pl.when(pl.program_id(0) < NPT)(functools.partial(tile, TM, xp_ref))
    pl.when(pl.program_id(0) == NPT)(functools.partial(tile, NS, xs_ref))


class _SideJob(NamedTuple):
    blocks: int
    args: list
    in_specs: list
    out_shape: list
    out_specs: list
    fn: Callable


def _step_block(nj, blocks, first=0):
    return lambda i, j: first + jnp.minimum(i * nj + j, blocks - 1)


def _cast_job(a, rb, nj):
    rows_a, cols_a = a.shape[-2:]
    blk = _step_block(nj, rows_a // rb)

    def fn(ins, outs):
        outs[0][...] = ins[0][...].astype(BF16)

    return _SideJob(rows_a // rb, [a], [pl.BlockSpec((None, rb, cols_a), lambda i, j: (0, blk(i, j), 0))],
                    [jax.ShapeDtypeStruct((rows_a, cols_a), BF16)], [pl.BlockSpec((rb, cols_a), lambda i, j: (blk(i, j), 0))],
                    fn)


def _transpose_job(wt, row0, cols, nj):
    k = wt.shape[1]
    ksplit = 2
    kb = k // ksplit
    shift = row0 % 128
    assert shift in (0, 8) and cols % 128 == 0
    nblk = cols // 128 * ksplit
    unit = _step_block(nj, nblk)
    first = row0 // 128
    last = pl.cdiv(wt.shape[0], 128) - 1
    specs = [pl.BlockSpec((128, kb), lambda i, j: (first + unit(i, j) // ksplit, unit(i, j) % ksplit))]
    if shift:
        specs.append(pl.BlockSpec(
            (128, kb), lambda i, j: (jnp.minimum(first + unit(i, j) // ksplit + 1, last), unit(i, j) % ksplit)))

    def fn(ins, outs):
        win = ins[0][...]
        if shift:
            win = jnp.concatenate([win[shift:, :], ins[1][0:shift, :]], axis=0)
        outs[0][...] = win.T.astype(BF16)

    return _SideJob(nblk, [wt] * len(specs), specs, [jax.ShapeDtypeStruct((k, cols), BF16)],
                    [pl.BlockSpec((kb, 128), lambda i, j: (unit(i, j) % ksplit, unit(i, j) // ksplit))], fn)


def _ffn(xs, g, wg, wu, wd, tf, g_final=None, side_jobs=()):
    nj = pl.cdiv(FF, tf)
    two_src = len(xs) == 2
    final_norm = g_final is not None
    row = pl.BlockSpec((TM, D), lambda i, j: (i, 0))
    prow = pl.BlockSpec((TM, D), lambda i, j: (jnp.minimum(i, NPT - 1), 0))
    srow = pl.BlockSpec((NS, D), lambda i, j: (0, 0))
    vec = pl.BlockSpec((1, D), lambda i, j: (0, 0))
    in_specs = ([prow, srow] if two_src else [row]) + [
        vec,
        pl.BlockSpec((D, tf), lambda i, j: (0, j)),
        pl.BlockSpec((D, tf), lambda i, j: (0, j)),
        pl.BlockSpec((tf, D), lambda i, j: (j, 0)),
    ]
    args = list(xs) + [g, wg, wu, wd]
    if final_norm:
        in_specs.append(vec)
        args.append(g_final)
        out_shape = [jax.ShapeDtypeStruct((NP, D), F32), jax.ShapeDtypeStruct((NS, D), F32)]
        out_specs = [prow, srow]
    else:
        out_shape = [jax.ShapeDtypeStruct((MROWS, D), F32)]
        out_specs = [row]
    jobs = []
    for make_job in side_jobs:
        job = make_job(nj)
        assert job.blocks <= NPT * nj, "side jobs must fit under the full-size row tiles"
        in_specs += job.in_specs
        args += job.args
        out_specs += job.out_specs
        out_shape += job.out_shape
        jobs.append((len(job.args), len(job.out_shape), job.fn))
    return pl.pallas_call(
        functools.partial(_ffn_kernel, two_src=two_src, final_norm=final_norm, side_jobs=tuple(jobs), tf=tf),
        grid=(NT, nj),
        in_specs=in_specs,
        out_specs=out_specs,
        out_shape=out_shape,
        scratch_shapes=[pltpu.VMEM((TM, D), BF16), pltpu.VMEM((TM, D), F32)],
        compiler_params=_cp(("arbitrary", "arbitrary")),
        name="ffn_final" if final_norm else "ffn",
    )(*args)


NF32 = 2
TMI = 256
NPI = NP // TMI
GATE0 = 5 * W
HEAD = GATE0 + W + 128
MIX0 = GATE0 + 2 * NH + W


def _in_proj_kernel(x_ref, g_ref, w_ref, wift_ref,
                    ofp_ref, obp_ref, zifp_ref, ztp_ref, ofs_ref, obs_ref, zifs_ref):
    def emit(of_ref, ob_ref, zif_ref, zt_ref):
        h = _rms(x_ref[...], g_ref[...]).astype(BF16)
        if zt_ref is not None:
            zt_ref[...] = _dot_nt(wift_ref[...], h)
        of_ref[:, 0:W] = _dot(h, w_ref[:, 0:W])
        for j in range(1, 4):
            ob_ref[:, (j - 1) * W:j * W] = _dot(h, w_ref[:, j * W:(j + 1) * W]).astype(BF16)
        of_ref[:, W:2 * W] = _dot(h, w_ref[:, 4 * W:5 * W])
        tail = _dot(h, w_ref[:, 5 * W:HEAD])
        zif_ref[...] = tail[:, 0:128]
        ob_ref[:, 3 * W:4 * W] = tail[:, 2 * NH:2 * NH + W].astype(BF16)

    pl.when(pl.program_id(0) < NPI)(functools.partial(emit, ofp_ref, obp_ref, zifp_ref, ztp_ref))
    pl.when(pl.program_id(0) == NPI)(functools.partial(emit, ofs_ref, obs_ref, zifs_ref, None))


def _in_proj(x, g, w, w_ift):
    once = pl.Buffered(1)
    pblk = lambda cols: pl.BlockSpec((TMI, cols), lambda i: (jnp.minimum(i, NPI - 1), 0))
    sblk = lambda cols: pl.BlockSpec((TMI, cols), lambda i: (0, 0))
    widths = (NF32 * W, 4 * W, 128)
    dtypes = (F32, BF16, F32)
    return pl.pallas_call(
        _in_proj_kernel,
        grid=(NPI + 1,),
        in_specs=[
            pl.BlockSpec((TMI, D), lambda i: (i, 0)),
            pl.BlockSpec((1, D), lambda i: (0, 0)),
            pl.BlockSpec((D, HEAD), lambda i: (0, 0), pipeline_mode=once),
            pl.BlockSpec((8, D), lambda i: (0, 0), pipeline_mode=once),
        ],
        out_specs=[pblk(c) for c in widths]
        + [pl.BlockSpec((8, TMI), lambda i: (0, jnp.minimum(i, NPI - 1)))]
        + [sblk(c) for c in widths],
        out_shape=[jax.ShapeDtypeStruct((NP, c), t) for c, t in zip(widths, dtypes)]
        + [jax.ShapeDtypeStruct((8, NP), F32)]
        + [jax.ShapeDtypeStruct((TMI, c), t) for c, t in zip(widths, dtypes)],
        compiler_params=_cp(("arbitrary",)),
        name="in_proj",
    )(x, g, w, w_ift)


def _mem_proj_kernel(x_ref, g_ref, wk_ref, wv_ref, k_ref, v_ref):
    h = _rms(x_ref[...], g_ref[...]).astype(BF16)
    k_ref[...] = _dot(h, wk_ref[...])
    v_ref[...] = _dot(h, wv_ref[...])


def _mem_proj(x, g, wk, wv):
    m = x.shape[0]
    wspec = pl.BlockSpec((D, W), lambda i: (0, 0))
    ospec = pl.BlockSpec((TM, W), lambda i: (i, 0))
    return pl.pallas_call(
        _mem_proj_kernel,
        grid=(m // TM,),
        in_specs=[pl.BlockSpec((TM, D), lambda i: (i, 0)), pl.BlockSpec((1, D), lambda i: (0, 0)), wspec, wspec],
        out_specs=[ospec, ospec],
        out_shape=[jax.ShapeDtypeStruct((m, W), F32)] * 2,
        compiler_params=_cp(("arbitrary",)),
        name="mem_proj",
    )(x, g, wk, wv)


def _s5_prep_kernel(lr_ref, li_ref, ls_ref, btr_ref, bti_ref, cr_ref, ci_ref, d_ref,
                    ar_ref, ai_ref, a16r_ref, a16i_ref, bbr_ref, bbi_ref, e_ref, ft_ref, bd_ref):
    lr = lr_ref[0]
    li = li_ref[0]
    dt = jnp.exp(ls_ref[0])

    def power(k):
        mag = jnp.exp(lr * dt * float(k))
        ang = li * dt * float(k)
        return mag * jnp.cos(ang), mag * jnp.sin(ang)

    pw = [power(k) for k in range(TC + 1)]
    ar, ai = pw[1]
    den = lr * lr + li * li
    nr = ar - 1.0
    z_re = (nr * lr + ai * li) / den
    z_im = (ai * lr - nr * li) / den
    mask = (lax.broadcasted_iota(jnp.int32, (WU, WS), 0) // HG
            == lax.broadcasted_iota(jnp.int32, (WU, WS), 1) // P)
    btr = btr_ref[0]
    bti = bti_ref[0]
    bbr = jnp.where(mask, z_re * btr - z_im * bti, 0.0)
    bbi = jnp.where(mask, z_re * bti + z_im * btr, 0.0)
    cr = jnp.where(mask, cr_ref[0], 0.0)
    ci = jnp.where(mask, ci_ref[0], 0.0)

    ar_ref[0] = ar
    ai_ref[0] = ai
    a16r_ref[0] = pw[TC][0]
    a16i_ref[0] = pw[TC][1]
    bbr_ref[0] = bbr
    bbi_ref[0] = bbi

    def cmul(xr, xi, k):
        pr, pi = pw[k]
        return xr * pr - xi * pi, xr * pi + xi * pr

    diag = (lax.broadcasted_iota(jnp.int32, (WU, WU), 0) == lax.broadcasted_iota(jnp.int32, (WU, WU), 1))
    rr, ri = cr, ci
    bbr_b = bbr.astype(BF16)
    bbi_b = bbi.astype(BF16)
    for s in range(TC):
        er, ei = cmul(bbr, bbi, TC - 1 - s)
        e_ref[0, s * WU:(s + 1) * WU, 0:WS] = er.astype(BF16)
        e_ref[0, s * WU:(s + 1) * WU, WS:2 * WS] = ei.astype(BF16)
        fr, fi = cmul(cr, ci, s + 1)
        ft_ref[0, s * WU:(s + 1) * WU, 0:WS] = fr.astype(BF16)
        ft_ref[0, s * WU:(s + 1) * WU, WS:2 * WS] = (-fi).astype(BF16)
        kern = _dot_nt(bbr_b, rr.astype(BF16)) - _dot_nt(bbi_b, ri.astype(BF16))
        if s == 0:
            kern = kern + jnp.where(diag, d_ref[0], 0.0)
        bd_ref[0, s] = kern
        rr, ri = fr, fi


def _s5_prep(lam_re, lam_im, log_step, bt_re, bt_im, c_re, c_im, d):
    def spec(*shape):
        nd = len(shape)
        return pl.BlockSpec((1,) + shape, lambda i: (i,) + (0,) * nd)

    def sds(*shape, dtype=F32):
        return jax.ShapeDtypeStruct((NGB,) + shape, dtype)

    return pl.pallas_call(
        _s5_prep_kernel,
        grid=(NGB,),
        in_specs=[spec(1, WS)] * 3 + [spec(WU, WS)] * 4 + [spec(WU, 1)],
        out_specs=[spec(1, WS)] * 4 + [spec(WU, WS)] * 2 + [spec(TC * WU, 2 * WS)] * 2 + [spec(TC, WU, WU)],
        out_shape=[sds(1, WS)] * 4 + [sds(WU, WS)] * 2 + [sds(TC * WU, 2 * WS, dtype=BF16)] * 2
        + [sds(TC, WU, WU)],
        compiler_params=_cp(("arbitrary",)),
        name="s5_prep",
    )(lam_re, lam_im, log_step, bt_re, bt_im, c_re, c_im, d)


def _s5_chunk_kernel(u_ref, bd_ref, e_ref, ft_ref, ar_ref, ai_ref, y_ref, fin_ref,
                     w_scr, lhs_scr, s_scr, xs_scr):
    rows = S5B * NCH

    @pl.when(pl.program_id(1) == 0)
    def _():
        w_scr[...] = jnp.zeros_like(w_scr)
        bd = [bd_ref[0, k].astype(BF16) for k in range(TC)]
        for s in range(TC):
            for t in range(s, TC):
                w_scr[s * WU:(s + 1) * WU, t * WU:(t + 1) * WU] = bd[t - s]

    for s in range(TC):
        lhs_scr[:, s * WU:(s + 1) * WU] = u_ref[pl.ds(s, rows, stride=TC), :].astype(BF16)
    lhs = lhs_scr[...]
    s_loc = _dot(lhs, e_ref[0])
    nl = WS // 128
    for k in range(2 * nl):
        s_scr[k] = s_loc[:, k * 128:(k + 1) * 128]
    ar = [ar_ref[0, :, k * 128:(k + 1) * 128] for k in range(nl)]
    ai = [ai_ref[0, :, k * 128:(k + 1) * 128] for k in range(nl)]
    xr = [jnp.zeros((S5B, 128), F32)] * nl
    xi = [jnp.zeros((S5B, 128), F32)] * nl
    for c in range(NCH):
        chunk_rows = pl.ds(c, S5B, stride=NCH)
        for k in range(nl):
            xs_scr[k, chunk_rows, :] = xr[k]
            xs_scr[nl + k, chunk_rows, :] = xi[k]
            sr = s_scr[k, chunk_rows, :]
            si = s_scr[nl + k, chunk_rows, :]
            xr[k], xi[k] = ar[k] * xr[k] - ai[k] * xi[k] + sr, ar[k] * xi[k] + ai[k] * xr[k] + si
    for k in range(nl):
        fin_ref[0, 0, :, k * 128:(k + 1) * 128] = xr[k]
        fin_ref[0, 0, :, WS + k * 128:WS + (k + 1) * 128] = xi[k]
    xs = jnp.concatenate([xs_scr[k] for k in range(2 * nl)], axis=1)
    y_carry = _dot_nt(xs.astype(BF16), ft_ref[0])
    for t2 in range(0, TC, 2):
        cols = slice(t2 * WU, (t2 + 2) * WU)
        y = _dot(lhs[:, 0:(t2 + 2) * WU], w_scr[0:(t2 + 2) * WU, cols]) + y_carry[:, cols]
        for t in (t2, t2 + 1):
            y_ref[pl.ds(t, rows, stride=TC), :] = y[:, (t - t2) * WU:(t - t2 + 1) * WU]


def _s5_chunk(z_f, bd, e, ft, a16r, a16i):
    rows = S5B * NCH
    nh = BATCH // S5B
    return pl.pallas_call(
        _s5_chunk_kernel,
        grid=(NGB, nh),
        in_specs=[
            pl.BlockSpec((S5B * SEQ, WU), lambda j, b: (b, j)),
            pl.BlockSpec((1, TC, WU, WU), lambda j, b: (j, 0, 0, 0)),
            pl.BlockSpec((1, TC * WU, 2 * WS), lambda j, b: (j, 0, 0)),
            pl.BlockSpec((1, TC * WU, 2 * WS), lambda j, b: (j, 0, 0)),
            pl.BlockSpec((1, 1, WS), lambda j, b: (j, 0, 0)),
            pl.BlockSpec((1, 1, WS), lambda j, b: (j, 0, 0)),
        ],
        out_specs=[
            pl.BlockSpec((S5B * SEQ, WU), lambda j, b: (b, j)),
            pl.BlockSpec((1, 1, S5B, 2 * WS), lambda j, b: (j, b, 0, 0)),
        ],
        out_shape=[jax.ShapeDtypeStruct((NP, W), F32), jax.ShapeDtypeStruct((NGB, nh, S5B, 2 * WS), F32)],
        scratch_shapes=[pltpu.VMEM((TC * WU, TC * WU), BF16), pltpu.VMEM((rows, TC * WU), BF16),
                        pltpu.VMEM((2 * WS // 128, rows, 128), F32), pltpu.VMEM((2 * WS // 128, rows, 128), F32)],
        compiler_params=_cp(("arbitrary", "arbitrary")),
        name="s5_chunk",
    )(z_f, bd, e, ft, a16r, a16i)


def _s5_step_kernel(u_ref, sr_ref, si_ref, ar_ref, ai_ref, bbr_ref, bbi_ref, cr_ref, ci_ref, d_ref,
                    y_ref, xr_ref, xi_ref):
    mask = (lax.broadcasted_iota(jnp.int32, (WU, WS), 0) // HG
            == lax.broadcasted_iota(jnp.int32, (WU, WS), 1) // P)
    for j in range(NGB):
        ul = slice(j * WU, (j + 1) * WU)
        sl = slice(j * WS, (j + 1) * WS)
        us = u_ref[:, ul]
        ar = ar_ref[j]
        ai = ai_ref[j]
        s_re = sr_ref[:, sl]
        s_im = si_ref[:, sl]
        x_re = ar * s_re - ai * s_im + _dot3(us, bbr_ref[j])
        x_im = ar * s_im + ai * s_re + _dot3(us, bbi_ref[j])
        xr_ref[:, sl] = x_re
        xi_ref[:, sl] = x_im
        cre = jnp.where(mask, cr_ref[j], 0.0)
        cim = jnp.where(mask, ci_ref[j], 0.0)
        y_ref[:, ul] = _dot3_nt(x_re, cre) - _dot3_nt(x_im, cim) + d_ref[:, ul] * us


def _s5_step(z_f, s_re, s_im, a_re, a_im, bb_re, bb_im, c_re, c_im, d_row):
    full = lambda *shape: pl.BlockSpec(shape, lambda i: (0,) * len(shape))
    return pl.pallas_call(
        _s5_step_kernel,
        grid=(1,),
        in_specs=[pl.BlockSpec((NS, W), lambda i: (0, 0)),
                  full(NS, G * P), full(NS, G * P), full(NGB, 1, WS), full(NGB, 1, WS),
                  full(NGB, WU, WS), full(NGB, WU, WS), full(NGB, WU, WS), full(NGB, WU, WS),
                  full(1, W)],
        out_specs=[full(NS, W), full(NS, G * P), full(NS, G * P)],
        out_shape=[jax.ShapeDtypeStruct((NS, W), F32), jax.ShapeDtypeStruct((NS, G * P), F32),
                   jax.ShapeDtypeStruct((NS, G * P), F32)],
        compiler_params=_cp(("arbitrary",)),
        name="s5_step",
    )(z_f, s_re, s_im, a_re, a_im, bb_re, bb_im, c_re, c_im, d_row)


def _mlstm_chunk_kernel(q_ref, k_ref, v_ref, o_ref, zif_ref, zt0_ref, zt1_ref, zt2_ref, zt3_ref,
                        brow_ref, bcol_ref, gh_ref, h_ref, c_ref, n_ref, m_ref):
    @pl.when(pl.program_id(0) == 0)
    def _():
        c_ref[...] = jnp.zeros_like(c_ref)
        n_ref[...] = jnp.zeros_like(n_ref)
        m_ref[...] = jnp.zeros_like(m_ref)

    rr = lax.broadcasted_iota(jnp.int32, (CL, CL), 0)
    cc = lax.broadcasted_iota(jnp.int32, (CL, CL), 1)
    causal = cc <= rr
    tril = jnp.where(causal, 1.0, 0.0).astype(BF16)
    triu = jnp.where(rr <= cc, 1.0, 0.0).astype(BF16)

    for b, zt_ref in enumerate((zt0_ref, zt1_ref, zt2_ref, zt3_ref)):
        zi = zif_ref[b] + brow_ref[...]
        zt = zt_ref[...] + bcol_ref[...]
        lfc = _split3(_log_sigmoid(zi))
        bcum_col = _dot(tril, lfc[0]) + (_dot(tril, lfc[1]) + _dot(tril, lfc[2]))
        lfr = _split3(_log_sigmoid(zt))
        bcum_row = _dot(lfr[0], triu) + (_dot(lfr[1], triu) + _dot(lfr[2], triu))

        for hh in range(NH):
            hs = slice(hh * DH, (hh + 1) * DH)
            bc = bcum_col[:, NH + hh:NH + hh + 1]
            ic = zi[:, hh:hh + 1]
            br = bcum_row[NH + hh:NH + hh + 1, :]
            ir = zt[hh:hh + 1, :]
            m_prev = m_ref[b, hh:hh + 1, 0:1]
            g_inter = bc + m_prev
            dlog = jnp.where(causal, (bc - br) + ir, -jnp.inf)
            m_t = jnp.maximum(g_inter, jnp.max(dlog, axis=-1, keepdims=True))
            w_inter = jnp.exp(g_inter - m_t)
            w_intra = jnp.exp(dlog - m_t)
            qb = q_ref[b, :, hs]
            kb = k_ref[b, :, hs]
            vb = v_ref[b, :, hs]
            qf = qb.astype(F32)
            kf = kb.astype(F32)
            s = _dot_nt(qb, kb) * (w_intra * (DH ** -0.5))
            c_prev = c_ref[b, hh]
            n_prev = n_ref[b, hh:hh + 1, :]
            num = _dot(s.astype(BF16), vb) + w_inter * _dot(qb, c_prev.astype(BF16))
            nq = jnp.sum(s, axis=-1, keepdims=True) + w_inter * jnp.sum(qf * n_prev, axis=-1, keepdims=True)
            h = num / jnp.maximum(jnp.abs(nq), jnp.exp(-m_t))
            m_last = m_t[CL - 1:CL, :]
            w_last = jnp.exp((bc[CL - 1:CL, :] - bc) + ic - m_last) * (DH ** -0.5)
            wi_last = w_inter[CL - 1:CL, :]
            kw = kf * w_last
            c_ref[b, hh] = wi_last * c_prev + _dot_tn(kw.astype(BF16), vb)
            n_ref[b, hh:hh + 1, :] = wi_last * n_prev + jnp.sum(kw, axis=0, keepdims=True)
            m_ref[b, hh:hh + 1, :] = jnp.broadcast_to(m_last, (1, 128))
            hn = _rms(h, gh_ref[:, hs])
            h_ref[b, :, hs] = (hn * _sigmoid(o_ref[b, :, hs])).astype(BF16)


def _mlstm_chunk(z_f, z_b, z_if, z_t, brow, bcol, gh):
    blk = lambda col: pl.BlockSpec((BATCH, CL, W), lambda c: (0, c, col))
    const = lambda *shape: pl.BlockSpec(shape, lambda c: (0,) * len(shape))
    zt_specs = [pl.BlockSpec((8, CL), functools.partial(lambda c, b: (0, b * NCL + c), b=b)) for b in range(BATCH)]
    return pl.pallas_call(
        _mlstm_chunk_kernel,
        grid=(NCL,),
        in_specs=[blk(0), blk(1), blk(2), blk(1), pl.BlockSpec((BATCH, CL, 128), lambda c: (0, c, 0))]
        + zt_specs + [const(1, 128), const(8, 1), const(1, W)],
        out_specs=[pl.BlockSpec((BATCH, CL, W), lambda c: (0, c, 0)),
                   const(BATCH, NH, DH, DH), const(BATCH, NH, DH), const(BATCH, NH, 128)],
        out_shape=[jax.ShapeDtypeStruct((BATCH, SEQ, W), BF16),
                   jax.ShapeDtypeStruct((BATCH, NH, DH, DH), F32),
                   jax.ShapeDtypeStruct((BATCH, NH, DH), F32),
                   jax.ShapeDtypeStruct((BATCH, NH, 128), F32)],
        compiler_params=_cp(("arbitrary",)),
        name="mlstm_chunk",
    )(z_b, z_b, z_b, z_f, z_if, z_t, z_t, z_t, z_t, brow, bcol, gh)


MSR = 64


def _mlstm_step_kernel(q_ref, k_ref, v_ref, o_ref, zif_ref, brow_ref, gh_ref, c0_ref, n0_ref, m0_ref,
                       h_ref, c_ref, n_ref, m_ref):
    zi = zif_ref[...] + brow_ref[...]
    ig = zi[:, 0:NH]
    g_inter = _log_sigmoid(zi[:, NH:2 * NH]) + m0_ref[...]
    m_t = jnp.maximum(g_inter, ig)
    w_inter = jnp.exp(g_inter - m_t)
    w_intra = jnp.exp(ig - m_t) * (DH ** -0.5)
    floor = jnp.exp(-m_t)
    m_ref[...] = m_t

    def heads_on_rows(x):
        return jnp.pad(x, ((0, 128 - SB), (0, 128 - NH))).T

    w_inter_t = heads_on_rows(w_inter)
    w_intra_t = heads_on_rows(w_intra)
    floor_t = heads_on_rows(floor)
    pad = jnp.zeros((128 - SB, DH), F32)
    q_t = [jnp.concatenate([q_ref[:, hh * DH:(hh + 1) * DH], pad], axis=0).T for hh in range(NH)]
    k_t = [jnp.concatenate([k_ref[:, hh * DH:(hh + 1) * DH], pad], axis=0).T for hh in range(NH)]

    def per_head_rows(ref, s):
        return jnp.concatenate([ref[s:s + 1, hh * DH:(hh + 1) * DH] for hh in range(NH)], axis=0)

    for s in range(SB):
        q4 = per_head_rows(q_ref, s)
        k4 = per_head_rows(k_ref, s)
        v4 = per_head_rows(v_ref, s)
        qc_rows = []
        for hh in range(NH):
            wi = w_inter[s:s + 1, hh:hh + 1]
            vw = v4[hh:hh + 1, :] * w_intra[s:s + 1, hh:hh + 1]
            acc = jnp.zeros((MSR, DH), F32)
            for r0 in range(0, DH, MSR):
                c_blk = c0_ref[s, hh, r0:r0 + MSR, :]
                acc = acc + q_t[hh][r0:r0 + MSR, s:s + 1] * c_blk
                c_ref[s, hh, r0:r0 + MSR, :] = wi * c_blk + k_t[hh][r0:r0 + MSR, s:s + 1] * vw
            qc_rows.append(jnp.sum(acc, axis=0, keepdims=True))
        q_c = jnp.concatenate(qc_rows, axis=0)
        wi_c = w_inter_t[0:NH, s:s + 1]
        wa_c = w_intra_t[0:NH, s:s + 1]
        n_prev = n0_ref[s]
        sv = jnp.sum(q4 * k4, axis=-1, keepdims=True) * wa_c
        num = sv * v4 + wi_c * q_c
        nq = sv + wi_c * jnp.sum(q4 * n_prev, axis=-1, keepdims=True)
        h = num / jnp.maximum(jnp.abs(nq), floor_t[0:NH, s:s + 1])
        n_ref[s] = wi_c * n_prev + wa_c * k4
        out = _rms(h, gh_ref[...]) * _sigmoid(per_head_rows(o_ref, s))
        for hh in range(NH):
            h_ref[s:s + 1, hh * DH:(hh + 1) * DH] = out[hh:hh + 1, :]


def _mlstm_step(zs, z_f, z_if, brow, gh, c0, n0, m0):
    blk = lambda col: pl.BlockSpec((SB, W), lambda i: (i, col))
    return pl.pallas_call(
        _mlstm_step_kernel,
        grid=(NS // SB,),
        in_specs=[blk(0), blk(1), blk(2), blk(1),
                  pl.BlockSpec((SB, 128), lambda i: (i, 0)),
                  pl.BlockSpec((1, 128), lambda i: (0, 0)),
                  pl.BlockSpec((NH, DH), lambda i: (0, 0)),
                  pl.BlockSpec((SB, NH, DH, DH), lambda i: (i, 0, 0, 0)),
                  pl.BlockSpec((SB, NH, DH), lambda i: (i, 0, 0)),
                  pl.BlockSpec((SB, NH), lambda i: (i, 0))],
        out_specs=[pl.BlockSpec((SB, W), lambda i: (i, 0)),
                   pl.BlockSpec((SB, NH, DH, DH), lambda i: (i, 0, 0, 0)),
                   pl.BlockSpec((SB, NH, DH), lambda i: (i, 0, 0)),
                   pl.BlockSpec((SB, NH), lambda i: (i, 0))],
        out_shape=[jax.ShapeDtypeStruct((NS, W), F32),
                   jax.ShapeDtypeStruct((NS, NH, DH, DH), F32),
                   jax.ShapeDtypeStruct((NS, NH, DH), F32),
                   jax.ShapeDtypeStruct((NS, NH), F32)],
        compiler_params=_cp(("arbitrary",)),
        name="mlstm_step",
    )(zs, zs, zs, z_f, z_if, brow, gh, c0, n0, m0)


def _softmax_rows(s):
    e = jnp.exp(s - jnp.max(s, axis=-1, keepdims=True))
    return e / jnp.sum(e, axis=-1, keepdims=True)


def _xattn_prompt_kernel(q_ref, k_ref, v_ref, o_ref):
    for hh in range(NH):
        hs = slice(hh * DH, (hh + 1) * DH)
        s = _dot_nt(q_ref[:, hs], k_ref[:, hs].astype(BF16)) * (DH ** -0.5)
        p = _softmax_rows(s)
        o_ref[:, hs] = _dot(p.astype(BF16), v_ref[:, hs].astype(BF16)).astype(BF16)


def _xattn_prompt(z_b, mem_k, mem_v):
    nt = SEQ // TM
    return pl.pallas_call(
        _xattn_prompt_kernel,
        grid=(BATCH, nt),
        in_specs=[pl.BlockSpec((TM, W), lambda b, t: (b * nt + t, 3)),
                  pl.BlockSpec((N_MEM, W), lambda b, t: (b, 0)),
                  pl.BlockSpec((N_MEM, W), lambda b, t: (b, 0))],
        out_specs=pl.BlockSpec((TM, W), lambda b, t: (b * nt + t, 0)),
        out_shape=jax.ShapeDtypeStruct((NP, W), BF16),
        compiler_params=_cp(("arbitrary", "arbitrary")),
        name="xattn_prompt",
    )(z_b, mem_k, mem_v)


XS = 4


XMC = 64


def _xattn_step_kernel(q_ref, k_ref, v_ref, o_ref):
    def part(t):
        for s in range(XS):
            r = t * XS + s
            q4 = jnp.concatenate([q_ref[r:r + 1, hh * DH:(hh + 1) * DH] for hh in range(NH)], axis=0)
            sc = jnp.concatenate(
                [jnp.sum(k_ref[s, m0:m0 + XMC] * q4[None], axis=-1, keepdims=True) for m0 in range(0, N_MEM, XMC)],
                axis=0) * (DH ** -0.5)
            e = jnp.exp(sc - jnp.max(sc, axis=0, keepdims=True))
            p = e / jnp.sum(e, axis=0, keepdims=True)
            acc = jnp.zeros((NH, DH), F32)
            for m0 in range(0, N_MEM, XMC):
                acc = acc + jnp.sum(p[m0:m0 + XMC] * v_ref[s, m0:m0 + XMC], axis=0)
            for hh in range(NH):
                o_ref[r:r + 1, hh * DH:(hh + 1) * DH] = acc[hh:hh + 1, :]

    for t in range(SB // XS):
        pl.when(pl.program_id(1) == t)(functools.partial(part, t))


def _xattn_step(zs, mem_k, mem_v):
    nt = SB // XS
    return pl.pallas_call(
        _xattn_step_kernel,
        grid=(NS // SB, nt),
        in_specs=[pl.BlockSpec((SB, W), lambda i, t: (i, 3)),
                  pl.BlockSpec((XS, N_MEM, NH, DH), lambda i, t: (i * nt + t, 0, 0, 0)),
                  pl.BlockSpec((XS, N_MEM, NH, DH), lambda i, t: (i * nt + t, 0, 0, 0))],
        out_specs=pl.BlockSpec((SB, W), lambda i, t: (i, 0)),
        out_shape=jax.ShapeDtypeStruct((NS, W), F32),
        compiler_params=_cp(("arbitrary", "arbitrary")),
        name="xattn_step",
    )(zs, mem_k, mem_v)


TMX = 512


def _mix_kernel(x_ref, g_ref, yp_ref, ys_ref, mp_ref, ms_ref, ap_ref, as_ref, wglu_ref,
                wg0_ref, wg1_ref, wg2_ref, wb0_ref, wb1_ref, wb2_ref, o_ref, h_scr, s5_scr):
    def tile(rows, y_ref, ml_ref, xa_ref):
        rs = slice(0, rows)

        @pl.when(pl.program_id(1) == 0)
        def _():
            h_scr[rs, :] = _rms(x_ref[rs, :], g_ref[...]).astype(BF16)
            y = _gelu_tanh(y_ref[...])
            s5_scr[rs, :] = (y * _sigmoid(_dot(y.astype(BF16), wglu_ref[...]))).astype(BF16)

        h = h_scr[rs, :]
        merged = (_sigmoid(_dot(h, wg0_ref[...])) * _dot(s5_scr[rs, :], wb0_ref[...])
                  + _sigmoid(_dot(h, wg1_ref[...])) * _dot(ml_ref[...].astype(BF16), wb1_ref[...])
                  + _sigmoid(_dot(h, wg2_ref[...])) * _dot(xa_ref[...].astype(BF16), wb2_ref[...]))
        o_ref[rs, :] = merged.astype(BF16)
        if rows < TM:
            o_ref[rows:, :] = jnp.zeros((TM - rows, TMX), BF16)

    pl.when(pl.program_id(0) < NPT)(functools.partial(tile, TM, yp_ref, mp_ref, ap_ref))
    pl.when(pl.program_id(0) == NPT)(functools.partial(tile, NS, ys_ref, ms_ref, as_ref))


def _mix(x1, g, yp, ys, mp, ms, ap, as_, wglu, wg, wb):
    prow = pl.BlockSpec((TM, W), lambda i, j: (jnp.minimum(i, NPT - 1), 0))
    srow = pl.BlockSpec((NS, W), lambda i, j: (0, 0))
    nx = D // TMX
    wgs = [pl.BlockSpec((D, TMX), functools.partial(lambda i, j, b: (0, b * nx + j), b=b)) for b in range(3)]
    wbs = pl.BlockSpec((W, TMX), lambda i, j: (0, j))
    return pl.pallas_call(
        _mix_kernel,
        grid=(NT, D // TMX),
        in_specs=[pl.BlockSpec((TM, D), lambda i, j: (i, 0)), pl.BlockSpec((1, D), lambda i, j: (0, 0)),
                  prow, srow, prow, srow, prow, srow,
                  pl.BlockSpec((W, W), lambda i, j: (0, 0)),
                  *wgs, wbs, wbs, wbs],
        out_specs=pl.BlockSpec((TM, TMX), lambda i, j: (i, j)),
        out_shape=jax.ShapeDtypeStruct((MROWS, D), BF16),
        scratch_shapes=[pltpu.VMEM((TM, D), BF16), pltpu.VMEM((TM, W), BF16)],
        compiler_params=_cp(("arbitrary", "arbitrary")),
        name="mix",
    )(x1, g, yp, ys, mp, ms, ap, as_, wglu, wg, wg, wg, *wb)


def _outproj_kernel(x_ref, m_ref, w_ref, o_ref):
    def tile(rows):
        rs = slice(0, rows)
        o_ref[rs, :] = x_ref[rs, :] + _dot(m_ref[rs, :], w_ref[...])
        if rows < TM:
            o_ref[rows:, :] = jnp.zeros((TM - rows, D), F32)

    pl.when(pl.program_id(0) < NPT)(functools.partial(tile, TM))
    pl.when(pl.program_id(0) == NPT)(functools.partial(tile, NS))


def _outproj(x1, merged, w_out):
    return pl.pallas_call(
        _outproj_kernel,
        grid=(NT,),
        in_specs=[pl.BlockSpec((TM, D), lambda i: (i, 0)), pl.BlockSpec((TM, D), lambda i: (i, 0)),
                  pl.BlockSpec((D, D), lambda i: (0, 0))],
        out_specs=pl.BlockSpec((TM, D), lambda i: (i, 0)),
        out_shape=jax.ShapeDtypeStruct((MROWS, D), F32),
        compiler_params=_cp(("arbitrary",)),
        name="out_proj",
    )(x1, merged, w_out)


def kernel(x_prompt, x_sample, mem_prompt, cache_mem_k, cache_mem_v, state_s5_re, state_s5_im, state_mlstm_C,
           state_mlstm_n, state_mlstm_m, g_ffn1, w1_gate, w1_up, w1_down, g_mix, w_in, s5_lambda_re,
           s5_lambda_im, s5_log_step, s5_b_re, s5_b_im, s5_c_re, s5_c_im, s5_d, w_s5_glu, b_igate, b_fgate,
           g_mlstm_head, g_mem, w_mem_k, w_mem_v, w_br_s5, w_br_ml, w_br_xa, w_out, g_ffn2, w2_gate, w2_up,
           w2_down, g_final):
    bf = lambda a: a.astype(BF16)

    w_in_t = w_in[0].T
    job = functools.partial
    (x1, w2g, w2u, w2d, w_head, w_bg, wb_s5, wb_ml, wb_xa, w_o, w_glu, w_mk, w_mv) = _ffn(
        (x_prompt.reshape(NP, D), x_sample.reshape(NS, D)), g_ffn1[0].reshape(1, D),
        bf(w1_gate[0]), bf(w1_up[0]), bf(w1_down[0]), TF,
        side_jobs=(job(_cast_job, w2_gate, 16), job(_cast_job, w2_up, 16), job(_cast_job, w2_down, 32),
                   job(_transpose_job, w_in_t, 0, HEAD), job(_transpose_job, w_in_t, MIX0, 3 * D),
                   job(_cast_job, w_br_s5, 16), job(_cast_job, w_br_ml, 16), job(_cast_job, w_br_xa, 16),
                   job(_cast_job, w_out, 16), job(_cast_job, w_s5_glu, 16),
                   job(_cast_job, w_mem_k, 16), job(_cast_job, w_mem_v, 16)))

    w_ift = bf(w_in_t[GATE0:GATE0 + 2 * NH])
    w_br = [wb_s5, wb_ml, wb_xa]
    brow = jnp.pad(jnp.concatenate([b_igate[0], b_fgate[0]]), (0, 128 - 2 * NH)).reshape(1, 128)
    bcol = jnp.concatenate([b_igate[0], b_fgate[0]]).reshape(2 * NH, 1)
    gh = g_mlstm_head[0].reshape(1, W)

    z_f, z_b, z_if, z_t, zs_f, zs_b, zs_if = _in_proj(x1, g_mix[0].reshape(1, D), w_head, w_ift)
    zs = zs_b[:NS].astype(F32)
    per_seq = lambda a: a.reshape(BATCH, SEQ, a.shape[1])

    blk = lambda a: jnp.tile(a.reshape(NGB, WU, P), (1, 1, GB))
    row = lambda a: a.reshape(NGB, 1, WS)
    (a_re, a_im, a16_re, a16_im, bb_re, bb_im, e_op, ft_op, bd_op) = _s5_prep(
        row(s5_lambda_re[0]), row(s5_lambda_im[0]), row(jnp.repeat(s5_log_step[0], P)),
        blk(s5_b_re[0].transpose(0, 2, 1)), blk(s5_b_im[0].transpose(0, 2, 1)),
        blk(s5_c_re[0]), blk(s5_c_im[0]), s5_d[0].reshape(NGB, WU, 1))
    y_s5_p, fin = _s5_chunk(z_f, bd_op, e_op, ft_op, a16_re, a16_im)
    fin = fin.reshape(NGB, BATCH, 2, GB, P).transpose(2, 1, 0, 3, 4).reshape(2, 1, BATCH, G, P)
    y_s5_s, s5_re_s, s5_im_s = _s5_step(
        zs_f, state_s5_re[0].reshape(NS, G * P), state_s5_im[0].reshape(NS, G * P),
        a_re, a_im, bb_re, bb_im, blk(s5_c_re[0]), blk(s5_c_im[0]), s5_d[0].reshape(1, W))

    ml_p, c_p, n_p, m_p = _mlstm_chunk(per_seq(z_f), per_seq(z_b), per_seq(z_if), z_t, brow, bcol, gh)
    ml_s, c_s, n_s, m_s = _mlstm_step(zs, zs_f, zs_if, brow, gh.reshape(NH, DH), state_mlstm_C[0],
                                      state_mlstm_n[0], state_mlstm_m[0])

    mem_k, mem_v = _mem_proj(mem_prompt.reshape(BATCH * N_MEM, D), g_mem[0].reshape(1, D), w_mk, w_mv)
    xa_p = _xattn_prompt(z_b, mem_k, mem_v)
    xa_s = _xattn_step(zs, cache_mem_k[0], cache_mem_v[0])

    merged = _mix(x1, g_mix[0].reshape(1, D), y_s5_p, y_s5_s, ml_p.reshape(NP, W), ml_s, xa_p, xa_s,
                  w_glu, w_bg, w_br)
    x2 = _outproj(x1, merged, w_o)
    y_p, y_s = _ffn((x2,), g_ffn2[0].reshape(1, D), w2g, w2u, w2d, TF, g_final.reshape(1, D))

    return (y_p.reshape(BATCH, SEQ, D), y_s.reshape(NS, 1, D),
            mem_k.reshape(1, BATCH, N_MEM, NH, DH), mem_v.reshape(1, BATCH, N_MEM, NH, DH),
            fin[0], fin[1], c_p[None], n_p[None], m_p[:, :, 0][None],
            s5_re_s.reshape(1, NS, G, P), s5_im_s.reshape(1, NS, G, P), c_s[None], n_s[None], m_s[None])
```

```python
import functools
from typing import Callable, NamedTuple

import jax
import jax.numpy as jnp
from jax import lax
from jax.experimental import pallas as pl
from jax.experimental.pallas import tpu as pltpu

F32 = jnp.float32
BF16 = jnp.bfloat16

D = 2048
BATCH = 4
SEQ = 2048
NS = 128
NP = BATCH * SEQ
TM = 512
NPT = NP // TM
NT = NPT + 1
MROWS = NT * TM
N_MEM = 256
FF = 5504
TF = 512
G = 64
P = 64
HG = 16
TC = 16
GB = 8
NGB = G // GB
WU = GB * HG
WS = GB * P
NCH = SEQ // TC
S5B = 2
W = 1024
NH = 4
DH = 256
CL = 256
NCL = SEQ // CL
SB = 8
EPS = 1e-6
VMEM_LIMIT = 56 * 1024 * 1024


def _cp(sem, vmem=VMEM_LIMIT):
    return pltpu.CompilerParams(dimension_semantics=sem, vmem_limit_bytes=vmem)


def _dot(a, b):
    return jnp.dot(a, b, preferred_element_type=F32)


def _dot_nt(a, b):
    return lax.dot_general(a, b, (((1,), (1,)), ((), ())), preferred_element_type=F32)


def _dot_tn(a, b):
    return lax.dot_general(a, b, (((0,), (0,)), ((), ())), preferred_element_type=F32)


def _hi_lo(x):
    hi = x.astype(BF16)
    lo = (x - hi.astype(F32)).astype(BF16)
    return hi, lo


def _split3(x):
    hi = x.astype(BF16)
    r1 = x - hi.astype(F32)
    mid = r1.astype(BF16)
    lo = (r1 - mid.astype(F32)).astype(BF16)
    return hi, mid, lo


def _dot3(a, b):
    ah, al = _hi_lo(a)
    bh, bl = _hi_lo(b)
    return _dot(ah, bh) + (_dot(ah, bl) + _dot(al, bh))


def _dot3_nt(a, b):
    ah, al = _hi_lo(a)
    bh, bl = _hi_lo(b)
    return _dot_nt(ah, bh) + (_dot_nt(ah, bl) + _dot_nt(al, bh))


def _rms(x, g):
    r = lax.rsqrt(jnp.mean(x * x, axis=-1, keepdims=True) + EPS)
    return (x * r) * g


def _sigmoid(x):
    return 1.0 / (1.0 + jnp.exp(-x))


def _log_sigmoid(x):
    return jnp.minimum(x, 0.0) - jnp.log1p(jnp.exp(-jnp.abs(x)))


def _gelu_tanh(x):
    return x * (0.5 * (1.0 + jnp.tanh(0.7978845608028654 * (x + 0.044715 * (x * x * x)))))


def _ffn_kernel(*refs, two_src, final_norm, side_jobs, tf):
    nj = pl.cdiv(FF, tf)
    tf_last = FF - (nj - 1) * tf
    assert tf_last % 128 == 0
    refs = list(refs)
    if two_src:
        xp_ref, xs_ref = refs[:2]
        refs = refs[2:]
    else:
        xp_ref = xs_ref = refs[0]
        refs = refs[1:]
    g_ref, wg_ref, wu_ref, wd_ref = refs[:4]
    refs = refs[4:]
    n_in = sum(n for n, _, _ in side_jobs)
    n_out = sum(n for _, n, _ in side_jobs)
    if final_norm:
        gf_ref = refs[0]
        refs = refs[1:]
    side_in = refs[:n_in]
    refs = refs[n_in:]
    if final_norm:
        op_ref, os_ref = refs[:2]
        refs = refs[2:]
    else:
        o_ref = refs[0]
        refs = refs[1:]
    side_out = refs[:n_out]
    h_scr, acc_scr = refs[n_out:]
    j = pl.program_id(1)

    def side_work():
        a = b = 0
        for ni, no, fn in side_jobs:
            fn(side_in[a:a + ni], side_out[b:b + no])
            a += ni
            b += no

    def tile(rows, x_ref):
        rs = slice(0, rows)

        @pl.when(j == 0)
        def _():
            h_scr[rs, :] = _rms(x_ref[rs, :], g_ref[...]).astype(BF16)
            acc_scr[rs, :] = jnp.zeros((rows, D), F32)

        def accumulate(width):
            side_work()
            h = h_scr[rs, :]
            gt = _dot(h, wg_ref[:, 0:width])
            hid = (gt * _sigmoid(gt)) * _dot(h, wu_ref[:, 0:width])
            acc_scr[rs, :] += _dot(hid.astype(BF16), wd_ref[0:width, :])

        pl.when(j < nj - 1)(functools.partial(accumulate, tf))

        @pl.when(j == nj - 1)
        def _():
            accumulate(tf_last)
            y = x_ref[rs, :] + 0.5 * acc_scr[rs, :]
            if final_norm:
                y = _rms(y, gf_ref[...])
                (op_ref if rows == TM else os_ref)[...] = y
            else:
                o_ref[rs, :] = y
                if rows < TM:
                    o_ref[rows:, :] = jnp.zeros((TM - rows, D), F32)

    pl.when(pl.program_id(0) < NPT)(functools.partial(tile, TM, xp_ref))
    pl.when(pl.program_id(0) == NPT)(functools.partial(tile, NS, xs_ref))


class _SideJob(NamedTuple):
    blocks: int
    args: list
    in_specs: list
    out_shape: list
    out_specs: list
    fn: Callable


def _step_block(nj, blocks, first=0):
    return lambda i, j: first + jnp.minimum(i * nj + j, blocks - 1)


def _cast_job(a, rb, nj):
    rows_a, cols_a = a.shape[-2:]
    blk = _step_block(nj, rows_a // rb)

    def fn(ins, outs):
        outs[0][...] = ins[0][...].astype(BF16)

    return _SideJob(rows_a // rb, [a], [pl.BlockSpec((None, rb, cols_a), lambda i, j: (0, blk(i, j), 0))],
                    [jax.ShapeDtypeStruct((rows_a, cols_a), BF16)], [pl.BlockSpec((rb, cols_a), lambda i, j: (blk(i, j), 0))],
                    fn)


def _transpose_job(wt, row0, cols, nj, place=lambda c: c):
    k = wt.shape[1]
    ksplit = 2
    kb = k // ksplit
    shift = row0 % 128
    assert shift in (0, 8) and cols % 128 == 0
    nblk = cols // 128 * ksplit
    unit = _step_block(nj, nblk)
    first = row0 // 128
    last = pl.cdiv(wt.shape[0], 128) - 1
    specs = [pl.BlockSpec((128, kb), lambda i, j: (first + unit(i, j) // ksplit, unit(i, j) % ksplit))]
    if shift:
        specs.append(pl.BlockSpec(
            (128, kb), lambda i, j: (jnp.minimum(first + unit(i, j) // ksplit + 1, last), unit(i, j) % ksplit)))

    def fn(ins, outs):
        win = ins[0][...]
        if shift:
            win = jnp.concatenate([win[shift:, :], ins[1][0:shift, :]], axis=0)
        outs[0][...] = win.T.astype(BF16)

    return _SideJob(nblk, [wt] * len(specs), specs, [jax.ShapeDtypeStruct((k, cols), BF16)],
                    [pl.BlockSpec((kb, 128), lambda i, j: (unit(i, j) % ksplit, place(unit(i, j) // ksplit)))], fn)


def _ffn(xs, g, wg, wu, wd, tf, g_final=None, side_jobs=()):
    nj = pl.cdiv(FF, tf)
    two_src = len(xs) == 2
    final_norm = g_final is not None
    row = pl.BlockSpec((TM, D), lambda i, j: (i, 0))
    prow = pl.BlockSpec((TM, D), lambda i, j: (jnp.minimum(i, NPT - 1), 0))
    srow = pl.BlockSpec((NS, D), lambda i, j: (0, 0))
    vec = pl.BlockSpec((1, D), lambda i, j: (0, 0))
    in_specs = ([prow, srow] if two_src else [row]) + [
        vec,
        pl.BlockSpec((D, tf), lambda i, j: (0, j)),
        pl.BlockSpec((D, tf), lambda i, j: (0, j)),
        pl.BlockSpec((tf, D), lambda i, j: (j, 0)),
    ]
    args = list(xs) + [g, wg, wu, wd]
    if final_norm:
        in_specs.append(vec)
        args.append(g_final)
        out_shape = [jax.ShapeDtypeStruct((NP, D), F32), jax.ShapeDtypeStruct((NS, D), F32)]
        out_specs = [prow, srow]
    else:
        out_shape = [jax.ShapeDtypeStruct((MROWS, D), F32)]
        out_specs = [row]
    jobs = []
    for make_job in side_jobs:
        job = make_job(nj)
        assert job.blocks <= NPT * nj, "side jobs must fit under the full-size row tiles"
        in_specs += job.in_specs
        args += job.args
        out_specs += job.out_specs
        out_shape += job.out_shape
        jobs.append((len(job.args), len(job.out_shape), job.fn))
    return pl.pallas_call(
        functools.partial(_ffn_kernel, two_src=two_src, final_norm=final_norm, side_jobs=tuple(jobs), tf=tf),
        grid=(NT, nj),
        in_specs=in_specs,
        out_specs=out_specs,
        out_shape=out_shape,
        scratch_shapes=[pltpu.VMEM((TM, D), BF16), pltpu.VMEM((TM, D), F32)],
        compiler_params=_cp(("arbitrary", "arbitrary")),
        name="ffn_final" if final_norm else "ffn",
    )(*args)


NF32 = 2
TMI = 256
NPI = NP // TMI
GATE0 = 5 * W
HEAD = GATE0 + W + 128
MIX0 = GATE0 + 2 * NH + W


def _in_proj_kernel(x_ref, g_ref, w_ref, wift_ref,
                    ofp_ref, obp_ref, zifp_ref, ztp_ref, ofs_ref, obs_ref, zifs_ref):
    def emit(of_ref, ob_ref, zif_ref, zt_ref):
        h = _rms(x_ref[...], g_ref[...]).astype(BF16)
        if zt_ref is not None:
            zt_ref[...] = _dot_nt(wift_ref[...], h)
        of_ref[:, 0:W] = _dot(h, w_ref[:, 0:W])
        for j in range(1, 4):
            ob_ref[:, (j - 1) * W:j * W] = _dot(h, w_ref[:, j * W:(j + 1) * W]).astype(BF16)
        of_ref[:, W:2 * W] = _dot(h, w_ref[:, 4 * W:5 * W])
        tail = _dot(h, w_ref[:, 5 * W:HEAD])
        zif_ref[...] = tail[:, 0:128]
        ob_ref[:, 3 * W:4 * W] = tail[:, 2 * NH:2 * NH + W].astype(BF16)

    pl.when(pl.program_id(0) < NPI)(functools.partial(emit, ofp_ref, obp_ref, zifp_ref, ztp_ref))
    pl.when(pl.program_id(0) == NPI)(functools.partial(emit, ofs_ref, obs_ref, zifs_ref, None))


def _in_proj(x, g, w, w_ift):
    once = pl.Buffered(1)
    pblk = lambda cols: pl.BlockSpec((TMI, cols), lambda i: (jnp.minimum(i, NPI - 1), 0))
    sblk = lambda cols: pl.BlockSpec((TMI, cols), lambda i: (0, 0))
    widths = (NF32 * W, 4 * W, 128)
    dtypes = (F32, BF16, F32)
    return pl.pallas_call(
        _in_proj_kernel,
        grid=(NPI + 1,),
        in_specs=[
            pl.BlockSpec((TMI, D), lambda i: (i, 0)),
            pl.BlockSpec((1, D), lambda i: (0, 0)),
            pl.BlockSpec((D, HEAD), lambda i: (0, 0), pipeline_mode=once),
            pl.BlockSpec((8, D), lambda i: (0, 0), pipeline_mode=once),
        ],
        out_specs=[pblk(c) for c in widths]
        + [pl.BlockSpec((8, TMI), lambda i: (0, jnp.minimum(i, NPI - 1)))]
        + [sblk(c) for c in widths],
        out_shape=[jax.ShapeDtypeStruct((NP, c), t) for c, t in zip(widths, dtypes)]
        + [jax.ShapeDtypeStruct((8, NP), F32)]
        + [jax.ShapeDtypeStruct((TMI, c), t) for c, t in zip(widths, dtypes)],
        compiler_params=_cp(("arbitrary",)),
        name="in_proj",
    )(x, g, w, w_ift)


def _mem_proj_kernel(x_ref, g_ref, wk_ref, wv_ref, k_ref, v_ref):
    h = _rms(x_ref[...], g_ref[...]).astype(BF16)
    k_ref[...] = _dot(h, wk_ref[...])
    v_ref[...] = _dot(h, wv_ref[...])


def _mem_proj(x, g, wk, wv):
    m = x.shape[0]
    wspec = pl.BlockSpec((D, W), lambda i: (0, 0))
    ospec = pl.BlockSpec((TM, W), lambda i: (i, 0))
    return pl.pallas_call(
        _mem_proj_kernel,
        grid=(m // TM,),
        in_specs=[pl.BlockSpec((TM, D), lambda i: (i, 0)), pl.BlockSpec((1, D), lambda i: (0, 0)), wspec, wspec],
        out_specs=[ospec, ospec],
        out_shape=[jax.ShapeDtypeStruct((m, W), F32)] * 2,
        compiler_params=_cp(("arbitrary",)),
        name="mem_proj",
    )(x, g, wk, wv)


def _s5_prep_kernel(lr_ref, li_ref, ls_ref, btr_ref, bti_ref, cr_ref, ci_ref, d_ref,
                    ar_ref, ai_ref, a16r_ref, a16i_ref, bbr_ref, bbi_ref, e_ref, ft_ref, bd_ref):
    lr = lr_ref[0]
    li = li_ref[0]
    dt = jnp.exp(ls_ref[0])

    def power(k):
        mag = jnp.exp(lr * dt * float(k))
        ang = li * dt * float(k)
        return mag * jnp.cos(ang), mag * jnp.sin(ang)

    pw = [power(k) for k in range(TC + 1)]
    ar, ai = pw[1]
    den = lr * lr + li * li
    nr = ar - 1.0
    z_re = (nr * lr + ai * li) / den
    z_im = (ai * lr - nr * li) / den
    mask = (lax.broadcasted_iota(jnp.int32, (WU, WS), 0) // HG
            == lax.broadcasted_iota(jnp.int32, (WU, WS), 1) // P)
    btr = btr_ref[0]
    bti = bti_ref[0]
    bbr = jnp.where(mask, z_re * btr - z_im * bti, 0.0)
    bbi = jnp.where(mask, z_re * bti + z_im * btr, 0.0)
    cr = jnp.where(mask, cr_ref[0], 0.0)
    ci = jnp.where(mask, ci_ref[0], 0.0)

    ar_ref[0] = ar
    ai_ref[0] = ai
    a16r_ref[0] = pw[TC][0]
    a16i_ref[0] = pw[TC][1]
    bbr_ref[0] = bbr
    bbi_ref[0] = bbi

    def cmul(xr, xi, k):
        pr, pi = pw[k]
        return xr * pr - xi * pi, xr * pi + xi * pr

    diag = (lax.broadcasted_iota(jnp.int32, (WU, WU), 0) == lax.broadcasted_iota(jnp.int32, (WU, WU), 1))
    rr, ri = cr, ci
    bbr_b = bbr.astype(BF16)
    bbi_b = bbi.astype(BF16)
    for s in range(TC):
        er, ei = cmul(bbr, bbi, TC - 1 - s)
        e_ref[0, s * WU:(s + 1) * WU, 0:WS] = er.astype(BF16)
        e_ref[0, s * WU:(s + 1) * WU, WS:2 * WS] = ei.astype(BF16)
        fr, fi = cmul(cr, ci, s + 1)
        ft_ref[0, s * WU:(s + 1) * WU, 0:WS] = fr.astype(BF16)
        ft_ref[0, s * WU:(s + 1) * WU, WS:2 * WS] = (-fi).astype(BF16)
        kern = _dot_nt(bbr_b, rr.astype(BF16)) - _dot_nt(bbi_b, ri.astype(BF16))
        if s == 0:
            kern = kern + jnp.where(diag, d_ref[0], 0.0)
        bd_ref[0, s] = kern
        rr, ri = fr, fi


def _s5_prep(lam_re, lam_im, log_step, bt_re, bt_im, c_re, c_im, d):
    def spec(*shape):
        nd = len(shape)
        return pl.BlockSpec((1,) + shape, lambda i: (i,) + (0,) * nd)

    def sds(*shape, dtype=F32):
        return jax.ShapeDtypeStruct((NGB,) + shape, dtype)

    return pl.pallas_call(
        _s5_prep_kernel,
        grid=(NGB,),
        in_specs=[spec(1, WS)] * 3 + [spec(WU, WS)] * 4 + [spec(WU, 1)],
        out_specs=[spec(1, WS)] * 4 + [spec(WU, WS)] * 2 + [spec(TC * WU, 2 * WS)] * 2 + [spec(TC, WU, WU)],
        out_shape=[sds(1, WS)] * 4 + [sds(WU, WS)] * 2 + [sds(TC * WU, 2 * WS, dtype=BF16)] * 2
        + [sds(TC, WU, WU)],
        compiler_params=_cp(("arbitrary",)),
        name="s5_prep",
    )(lam_re, lam_im, log_step, bt_re, bt_im, c_re, c_im, d)


def _s5_chunk_kernel(u_ref, bd_ref, e_ref, ft_ref, ar_ref, ai_ref, y_ref, fin_ref,
                     w_scr, lhs_scr, s_scr, xs_scr):
    rows = S5B * NCH

    @pl.when(pl.program_id(1) == 0)
    def _():
        w_scr[...] = jnp.zeros_like(w_scr)
        bd = [bd_ref[0, k].astype(BF16) for k in range(TC)]
        for s in range(TC):
            for t in range(s, TC):
                w_scr[s * WU:(s + 1) * WU, t * WU:(t + 1) * WU] = bd[t - s]

    for s in range(TC):
        lhs_scr[:, s * WU:(s + 1) * WU] = u_ref[pl.ds(s, rows, stride=TC), :].astype(BF16)
    lhs = lhs_scr[...]
    s_loc = _dot(lhs, e_ref[0])
    nl = WS // 128
    for k in range(2 * nl):
        s_scr[k] = s_loc[:, k * 128:(k + 1) * 128]
    ar = [ar_ref[0, :, k * 128:(k + 1) * 128] for k in range(nl)]
    ai = [ai_ref[0, :, k * 128:(k + 1) * 128] for k in range(nl)]
    xr = [jnp.zeros((S5B, 128), F32)] * nl
    xi = [jnp.zeros((S5B, 128), F32)] * nl
    for c in range(NCH):
        chunk_rows = pl.ds(c, S5B, stride=NCH)
        for k in range(nl):
            xs_scr[k, chunk_rows, :] = xr[k]
            xs_scr[nl + k, chunk_rows, :] = xi[k]
            sr = s_scr[k, chunk_rows, :]
            si = s_scr[nl + k, chunk_rows, :]
            xr[k], xi[k] = ar[k] * xr[k] - ai[k] * xi[k] + sr, ar[k] * xi[k] + ai[k] * xr[k] + si
    for k in range(nl):
        fin_ref[0, 0, :, k * 128:(k + 1) * 128] = xr[k]
        fin_ref[0, 0, :, WS + k * 128:WS + (k + 1) * 128] = xi[k]
    xs = jnp.concatenate([xs_scr[k] for k in range(2 * nl)], axis=1)
    y_carry = _dot_nt(xs.astype(BF16), ft_ref[0])
    for t2 in range(0, TC, 2):
        cols = slice(t2 * WU, (t2 + 2) * WU)
        y = _dot(lhs[:, 0:(t2 + 2) * WU], w_scr[0:(t2 + 2) * WU, cols]) + y_carry[:, cols]
        for t in (t2, t2 + 1):
            y_ref[pl.ds(t, rows, stride=TC), :] = y[:, (t - t2) * WU:(t - t2 + 1) * WU]


def _s5_chunk(z_f, bd, e, ft, a16r, a16i):
    rows = S5B * NCH
    nh = BATCH // S5B
    return pl.pallas_call(
        _s5_chunk_kernel,
        grid=(NGB, nh),
        in_specs=[
            pl.BlockSpec((S5B * SEQ, WU), lambda j, b: (b, j)),
            pl.BlockSpec((1, TC, WU, WU), lambda j, b: (j, 0, 0, 0)),
            pl.BlockSpec((1, TC * WU, 2 * WS), lambda j, b: (j, 0, 0)),
            pl.BlockSpec((1, TC * WU, 2 * WS), lambda j, b: (j, 0, 0)),
            pl.BlockSpec((1, 1, WS), lambda j, b: (j, 0, 0)),
            pl.BlockSpec((1, 1, WS), lambda j, b: (j, 0, 0)),
        ],
        out_specs=[
            pl.BlockSpec((S5B * SEQ, WU), lambda j, b: (b, j)),
            pl.BlockSpec((1, 1, S5B, 2 * WS), lambda j, b: (j, b, 0, 0)),
        ],
        out_shape=[jax.ShapeDtypeStruct((NP, W), F32), jax.ShapeDtypeStruct((NGB, nh, S5B, 2 * WS), F32)],
        scratch_shapes=[pltpu.VMEM((TC * WU, TC * WU), BF16), pltpu.VMEM((rows, TC * WU), BF16),
                        pltpu.VMEM((2 * WS // 128, rows, 128), F32), pltpu.VMEM((2 * WS // 128, rows, 128), F32)],
        compiler_params=_cp(("arbitrary", "arbitrary")),
        name="s5_chunk",
    )(z_f, bd, e, ft, a16r, a16i)


def _s5_step_kernel(u_ref, sr_ref, si_ref, ar_ref, ai_ref, bbr_ref, bbi_ref, cr_ref, ci_ref, d_ref,
                    y_ref, xr_ref, xi_ref):
    mask = (lax.broadcasted_iota(jnp.int32, (WU, WS), 0) // HG
            == lax.broadcasted_iota(jnp.int32, (WU, WS), 1) // P)
    for j in range(NGB):
        ul = slice(j * WU, (j + 1) * WU)
        sl = slice(j * WS, (j + 1) * WS)
        us = u_ref[:, ul]
        ar = ar_ref[j]
        ai = ai_ref[j]
        s_re = sr_ref[:, sl]
        s_im = si_ref[:, sl]
        x_re = ar * s_re - ai * s_im + _dot3(us, bbr_ref[j])
        x_im = ar * s_im + ai * s_re + _dot3(us, bbi_ref[j])
        xr_ref[:, sl] = x_re
        xi_ref[:, sl] = x_im
        cre = jnp.where(mask, cr_ref[j], 0.0)
        cim = jnp.where(mask, ci_ref[j], 0.0)
        y_ref[:, ul] = _dot3_nt(x_re, cre) - _dot3_nt(x_im, cim) + d_ref[:, ul] * us


def _s5_step(z_f, s_re, s_im, a_re, a_im, bb_re, bb_im, c_re, c_im, d_row):
    full = lambda *shape: pl.BlockSpec(shape, lambda i: (0,) * len(shape))
    return pl.pallas_call(
        _s5_step_kernel,
        grid=(1,),
        in_specs=[pl.BlockSpec((NS, W), lambda i: (0, 0)),
                  full(NS, G * P), full(NS, G * P), full(NGB, 1, WS), full(NGB, 1, WS),
                  full(NGB, WU, WS), full(NGB, WU, WS), full(NGB, WU, WS), full(NGB, WU, WS),
                  full(1, W)],
        out_specs=[full(NS, W), full(NS, G * P), full(NS, G * P)],
        out_shape=[jax.ShapeDtypeStruct((NS, W), F32), jax.ShapeDtypeStruct((NS, G * P), F32),
                   jax.ShapeDtypeStruct((NS, G * P), F32)],
        compiler_params=_cp(("arbitrary",)),
        name="s5_step",
    )(z_f, s_re, s_im, a_re, a_im, bb_re, bb_im, c_re, c_im, d_row)


def _mlstm_chunk_kernel(q_ref, k_ref, v_ref, o_ref, zif_ref, zt0_ref, zt1_ref, zt2_ref, zt3_ref,
                        brow_ref, bcol_ref, gh_ref, h_ref, c_ref, n_ref, m_ref):
    @pl.when(pl.program_id(0) == 0)
    def _():
        c_ref[...] = jnp.zeros_like(c_ref)
        n_ref[...] = jnp.zeros_like(n_ref)
        m_ref[...] = jnp.zeros_like(m_ref)

    rr = lax.broadcasted_iota(jnp.int32, (CL, CL), 0)
    cc = lax.broadcasted_iota(jnp.int32, (CL, CL), 1)
    causal = cc <= rr
    tril = jnp.where(causal, 1.0, 0.0).astype(BF16)
    triu = jnp.where(rr <= cc, 1.0, 0.0).astype(BF16)

    for b, zt_ref in enumerate((zt0_ref, zt1_ref, zt2_ref, zt3_ref)):
        zi = zif_ref[b] + brow_ref[...]
        zt = zt_ref[...] + bcol_ref[...]
        lfc = _split3(_log_sigmoid(zi))
        bcum_col = _dot(tril, lfc[0]) + (_dot(tril, lfc[1]) + _dot(tril, lfc[2]))
        lfr = _split3(_log_sigmoid(zt))
        bcum_row = _dot(lfr[0], triu) + (_dot(lfr[1], triu) + _dot(lfr[2], triu))

        for hh in range(NH):
            hs = slice(hh * DH, (hh + 1) * DH)
            bc = bcum_col[:, NH + hh:NH + hh + 1]
            ic = zi[:, hh:hh + 1]
            br = bcum_row[NH + hh:NH + hh + 1, :]
            ir = zt[hh:hh + 1, :]
            m_prev = m_ref[b, hh:hh + 1, 0:1]
            g_inter = bc + m_prev
            dlog = jnp.where(causal, (bc - br) + ir, -jnp.inf)
            m_t = jnp.maximum(g_inter, jnp.max(dlog, axis=-1, keepdims=True))
            w_inter = jnp.exp(g_inter - m_t)
            w_intra = jnp.exp(dlog - m_t)
            qb = q_ref[b, :, hs]
            kb = k_ref[b, :, hs]
            vb = v_ref[b, :, hs]
            qf = qb.astype(F32)
            kf = kb.astype(F32)
            s = _dot_nt(qb, kb) * (w_intra * (DH ** -0.5))
            c_prev = c_ref[b, hh]
            n_prev = n_ref[b, hh:hh + 1, :]
            num = _dot(s.astype(BF16), vb) + w_inter * _dot(qb, c_prev.astype(BF16))
            nq = jnp.sum(s, axis=-1, keepdims=True) + w_inter * jnp.sum(qf * n_prev, axis=-1, keepdims=True)
            h = num / jnp.maximum(jnp.abs(nq), jnp.exp(-m_t))
            m_last = m_t[CL - 1:CL, :]
            w_last = jnp.exp((bc[CL - 1:CL, :] - bc) + ic - m_last) * (DH ** -0.5)
            wi_last = w_inter[CL - 1:CL, :]
            kw = kf * w_last
            c_ref[b, hh] = wi_last * c_prev + _dot_tn(kw.astype(BF16), vb)
            n_ref[b, hh:hh + 1, :] = wi_last * n_prev + jnp.sum(kw, axis=0, keepdims=True)
            m_ref[b, hh:hh + 1, :] = jnp.broadcast_to(m_last, (1, 128))
            hn = _rms(h, gh_ref[:, hs])
            h_ref[b, :, hs] = (hn * _sigmoid(o_ref[b, :, hs])).astype(BF16)


def _mlstm_chunk(z_f, z_b, z_if, z_t, brow, bcol, gh):
    blk = lambda col: pl.BlockSpec((BATCH, CL, W), lambda c: (0, c, col))
    const = lambda *shape: pl.BlockSpec(shape, lambda c: (0,) * len(shape))
    zt_specs = [pl.BlockSpec((8, CL), functools.partial(lambda c, b: (0, b * NCL + c), b=b)) for b in range(BATCH)]
    return pl.pallas_call(
        _mlstm_chunk_kernel,
        grid=(NCL,),
        in_specs=[blk(0), blk(1), blk(2), blk(1), pl.BlockSpec((BATCH, CL, 128), lambda c: (0, c, 0))]
        + zt_specs + [const(1, 128), const(8, 1), const(1, W)],
        out_specs=[pl.BlockSpec((BATCH, CL, W), lambda c: (0, c, 0)),
                   const(BATCH, NH, DH, DH), const(BATCH, NH, DH), const(BATCH, NH, 128)],
        out_shape=[jax.ShapeDtypeStruct((BATCH, SEQ, W), BF16),
                   jax.ShapeDtypeStruct((BATCH, NH, DH, DH), F32),
                   jax.ShapeDtypeStruct((BATCH, NH, DH), F32),
                   jax.ShapeDtypeStruct((BATCH, NH, 128), F32)],
        compiler_params=_cp(("arbitrary",)),
        name="mlstm_chunk",
    )(z_b, z_b, z_b, z_f, z_if, z_t, z_t, z_t, z_t, brow, bcol, gh)


MSR = 64


def _mlstm_step_kernel(q_ref, k_ref, v_ref, o_ref, zif_ref, brow_ref, gh_ref, c0_ref, n0_ref, m0_ref,
                       h_ref, c_ref, n_ref, m_ref):
    zi = zif_ref[...] + brow_ref[...]
    ig = zi[:, 0:NH]
    g_inter = _log_sigmoid(zi[:, NH:2 * NH]) + m0_ref[...]
    m_t = jnp.maximum(g_inter, ig)
    w_inter = jnp.exp(g_inter - m_t)
    w_intra = jnp.exp(ig - m_t) * (DH ** -0.5)
    floor = jnp.exp(-m_t)
    m_ref[...] = m_t

    def heads_on_rows(x):
        return jnp.pad(x, ((0, 128 - SB), (0, 128 - NH))).T

    w_inter_t = heads_on_rows(w_inter)
    w_intra_t = heads_on_rows(w_intra)
    floor_t = heads_on_rows(floor)
    pad = jnp.zeros((128 - SB, DH), F32)
    q_t = [jnp.concatenate([q_ref[:, hh * DH:(hh + 1) * DH], pad], axis=0).T for hh in range(NH)]
    k_t = [jnp.concatenate([k_ref[:, hh * DH:(hh + 1) * DH], pad], axis=0).T for hh in range(NH)]

    def per_head_rows(ref, s):
        return jnp.concatenate([ref[s:s + 1, hh * DH:(hh + 1) * DH] for hh in range(NH)], axis=0)

    for s in range(SB):
        q4 = per_head_rows(q_ref, s)
        k4 = per_head_rows(k_ref, s)
        v4 = per_head_rows(v_ref, s)
        qc_rows = []
        for hh in range(NH):
            wi = w_inter[s:s + 1, hh:hh + 1]
            vw = v4[hh:hh + 1, :] * w_intra[s:s + 1, hh:hh + 1]
            acc = jnp.zeros((MSR, DH), F32)
            for r0 in range(0, DH, MSR):
                c_blk = c0_ref[s, hh, r0:r0 + MSR, :]
                acc = acc + q_t[hh][r0:r0 + MSR, s:s + 1] * c_blk
                c_ref[s, hh, r0:r0 + MSR, :] = wi * c_blk + k_t[hh][r0:r0 + MSR, s:s + 1] * vw
            qc_rows.append(jnp.sum(acc, axis=0, keepdims=True))
        q_c = jnp.concatenate(qc_rows, axis=0)
        wi_c = w_inter_t[0:NH, s:s + 1]
        wa_c = w_intra_t[0:NH, s:s + 1]
        n_prev = n0_ref[s]
        sv = jnp.sum(q4 * k4, axis=-1, keepdims=True) * wa_c
        num = sv * v4 + wi_c * q_c
        nq = sv + wi_c * jnp.sum(q4 * n_prev, axis=-1, keepdims=True)
        h = num / jnp.maximum(jnp.abs(nq), floor_t[0:NH, s:s + 1])
        n_ref[s] = wi_c * n_prev + wa_c * k4
        out = _rms(h, gh_ref[...]) * _sigmoid(per_head_rows(o_ref, s))
        for hh in range(NH):
            h_ref[s:s + 1, hh * DH:(hh + 1) * DH] = out[hh:hh + 1, :]


def _mlstm_step(zs, z_f, z_if, brow, gh, c0, n0, m0):
    blk = lambda col: pl.BlockSpec((SB, W), lambda i: (i, col))
    return pl.pallas_call(
        _mlstm_step_kernel,
        grid=(NS // SB,),
        in_specs=[blk(0), blk(1), blk(2), blk(1),
                  pl.BlockSpec((SB, 128), lambda i: (i, 0)),
                  pl.BlockSpec((1, 128), lambda i: (0, 0)),
                  pl.BlockSpec((NH, DH), lambda i: (0, 0)),
                  pl.BlockSpec((SB, NH, DH, DH), lambda i: (i, 0, 0, 0)),
                  pl.BlockSpec((SB, NH, DH), lambda i: (i, 0, 0)),
                  pl.BlockSpec((SB, NH), lambda i: (i, 0))],
        out_specs=[pl.BlockSpec((SB, W), lambda i: (i, 0)),
                   pl.BlockSpec((SB, NH, DH, DH), lambda i: (i, 0, 0, 0)),
                   pl.BlockSpec((SB, NH, DH), lambda i: (i, 0, 0)),
                   pl.BlockSpec((SB, NH), lambda i: (i, 0))],
        out_shape=[jax.ShapeDtypeStruct((NS, W), F32),
                   jax.ShapeDtypeStruct((NS, NH, DH, DH), F32),
                   jax.ShapeDtypeStruct((NS, NH, DH), F32),
                   jax.ShapeDtypeStruct((NS, NH), F32)],
        compiler_params=_cp(("arbitrary",)),
        name="mlstm_step",
    )(zs, zs, zs, z_f, z_if, brow, gh, c0, n0, m0)


def _softmax_rows(s):
    e = jnp.exp(s - jnp.max(s, axis=-1, keepdims=True))
    return e / jnp.sum(e, axis=-1, keepdims=True)


def _xattn_prompt_kernel(q_ref, k_ref, v_ref, o_ref):
    for hh in range(NH):
        hs = slice(hh * DH, (hh + 1) * DH)
        s = _dot_nt(q_ref[:, hs], k_ref[:, hs].astype(BF16)) * (DH ** -0.5)
        p = _softmax_rows(s)
        o_ref[:, hs] = _dot(p.astype(BF16), v_ref[:, hs].astype(BF16)).astype(BF16)


def _xattn_prompt(z_b, mem_k, mem_v):
    nt = SEQ // TM
    return pl.pallas_call(
        _xattn_prompt_kernel,
        grid=(BATCH, nt),
        in_specs=[pl.BlockSpec((TM, W), lambda b, t: (b * nt + t, 3)),
                  pl.BlockSpec((N_MEM, W), lambda b, t: (b, 0)),
                  pl.BlockSpec((N_MEM, W), lambda b, t: (b, 0))],
        out_specs=pl.BlockSpec((TM, W), lambda b, t: (b * nt + t, 0)),
        out_shape=jax.ShapeDtypeStruct((NP, W), BF16),
        compiler_params=_cp(("arbitrary", "arbitrary")),
        name="xattn_prompt",
    )(z_b, mem_k, mem_v)


XS = 4


XMC = 64


def _xattn_step_kernel(q_ref, k_ref, v_ref, o_ref):
    def part(t):
        for s in range(XS):
            r = t * XS + s
            q4 = jnp.concatenate([q_ref[r:r + 1, hh * DH:(hh + 1) * DH] for hh in range(NH)], axis=0)
            sc = jnp.concatenate(
                [jnp.sum(k_ref[s, m0:m0 + XMC] * q4[None], axis=-1, keepdims=True) for m0 in range(0, N_MEM, XMC)],
                axis=0) * (DH ** -0.5)
            e = jnp.exp(sc - jnp.max(sc, axis=0, keepdims=True))
            p = e / jnp.sum(e, axis=0, keepdims=True)
            acc = jnp.zeros((NH, DH), F32)
            for m0 in range(0, N_MEM, XMC):
                acc = acc + jnp.sum(p[m0:m0 + XMC] * v_ref[s, m0:m0 + XMC], axis=0)
            for hh in range(NH):
                o_ref[r:r + 1, hh * DH:(hh + 1) * DH] = acc[hh:hh + 1, :]

    for t in range(SB // XS):
        pl.when(pl.program_id(1) == t)(functools.partial(part, t))


def _xattn_step(zs, mem_k, mem_v):
    nt = SB // XS
    return pl.pallas_call(
        _xattn_step_kernel,
        grid=(NS // SB, nt),
        in_specs=[pl.BlockSpec((SB, W), lambda i, t: (i, 3)),
                  pl.BlockSpec((XS, N_MEM, NH, DH), lambda i, t: (i * nt + t, 0, 0, 0)),
                  pl.BlockSpec((XS, N_MEM, NH, DH), lambda i, t: (i * nt + t, 0, 0, 0))],
        out_specs=pl.BlockSpec((SB, W), lambda i, t: (i, 0)),
        out_shape=jax.ShapeDtypeStruct((NS, W), F32),
        compiler_params=_cp(("arbitrary", "arbitrary")),
        name="xattn_step",
    )(zs, mem_k, mem_v)


TMX = 512


def _gate_tile_major(c):
    per_branch, per_tile = D // 128, TMX // 128
    b, r = c // per_branch, c % per_branch
    return (r // per_tile) * (3 * per_tile) + b * per_tile + r % per_tile


def _mix_kernel(x_ref, g_ref, yp_ref, ys_ref, mp_ref, ms_ref, ap_ref, as_ref, wglu_ref,
                wg_ref, wb0_ref, wb1_ref, wb2_ref, o_ref, h_scr, s5_scr):
    def tile(rows, y_ref, ml_ref, xa_ref):
        rs = slice(0, rows)

        @pl.when(pl.program_id(1) == 0)
        def _():
            h_scr[rs, :] = _rms(x_ref[rs, :], g_ref[...]).astype(BF16)
            y = _gelu_tanh(y_ref[...])
            s5_scr[rs, :] = (y * _sigmoid(_dot(y.astype(BF16), wglu_ref[...]))).astype(BF16)

        gates = _sigmoid(_dot(h_scr[rs, :], wg_ref[...]))
        merged = (gates[:, 0:TMX] * _dot(s5_scr[rs, :], wb0_ref[...])
                  + gates[:, TMX:2 * TMX] * _dot(ml_ref[...].astype(BF16), wb1_ref[...])
                  + gates[:, 2 * TMX:3 * TMX] * _dot(xa_ref[...].astype(BF16), wb2_ref[...]))
        o_ref[rs, :] = merged.astype(BF16)
        if rows < TM:
            o_ref[rows:, :] = jnp.zeros((TM - rows, TMX), BF16)

    pl.when(pl.program_id(0) < NPT)(functools.partial(tile, TM, yp_ref, mp_ref, ap_ref))
    pl.when(pl.program_id(0) == NPT)(functools.partial(tile, NS, ys_ref, ms_ref, as_ref))


def _mix(x1, g, yp, ys, mp, ms, ap, as_, wglu, wg, wb):
    prow = pl.BlockSpec((TM, W), lambda i, j: (jnp.minimum(i, NPT - 1), 0))
    srow = pl.BlockSpec((NS, W), lambda i, j: (0, 0))
    wgs = pl.BlockSpec((D, 3 * TMX), lambda i, j: (0, j))
    wbs = pl.BlockSpec((W, TMX), lambda i, j: (0, j))
    return pl.pallas_call(
        _mix_kernel,
        grid=(NT, D // TMX),
        in_specs=[pl.BlockSpec((TM, D), lambda i, j: (i, 0)), pl.BlockSpec((1, D), lambda i, j: (0, 0)),
                  prow, srow, prow, srow, prow, srow,
                  pl.BlockSpec((W, W), lambda i, j: (0, 0)),
                  wgs, wbs, wbs, wbs],
        out_specs=pl.BlockSpec((TM, TMX), lambda i, j: (i, j)),
        out_shape=jax.ShapeDtypeStruct((MROWS, D), BF16),
        scratch_shapes=[pltpu.VMEM((TM, D), BF16), pltpu.VMEM((TM, W), BF16)],
        compiler_params=_cp(("arbitrary", "arbitrary")),
        name="mix",
    )(x1, g, yp, ys, mp, ms, ap, as_, wglu, wg, *wb)


def _outproj_kernel(x_ref, m_ref, w_ref, o_ref):
    def tile(rows):
        rs = slice(0, rows)
        o_ref[rs, :] = x_ref[rs, :] + _dot(m_ref[rs, :], w_ref[...])
        if rows < TM:
            o_ref[rows:, :] = jnp.zeros((TM - rows, D), F32)

    pl.when(pl.program_id(0) < NPT)(functools.partial(tile, TM))
    pl.when(pl.program_id(0) == NPT)(functools.partial(tile, NS))


def _outproj(x1, merged, w_out):
    return pl.pallas_call(
        _outproj_kernel,
        grid=(NT,),
        in_specs=[pl.BlockSpec((TM, D), lambda i: (i, 0)), pl.BlockSpec((TM, D), lambda i: (i, 0)),
                  pl.BlockSpec((D, D), lambda i: (0, 0))],
        out_specs=pl.BlockSpec((TM, D), lambda i: (i, 0)),
        out_shape=jax.ShapeDtypeStruct((MROWS, D), F32),
        compiler_params=_cp(("arbitrary",)),
        name="out_proj",
    )(x1, merged, w_out)


def kernel(x_prompt, x_sample, mem_prompt, cache_mem_k, cache_mem_v, state_s5_re, state_s5_im, state_mlstm_C,
           state_mlstm_n, state_mlstm_m, g_ffn1, w1_gate, w1_up, w1_down, g_mix, w_in, s5_lambda_re,
           s5_lambda_im, s5_log_step, s5_b_re, s5_b_im, s5_c_re, s5_c_im, s5_d, w_s5_glu, b_igate, b_fgate,
           g_mlstm_head, g_mem, w_mem_k, w_mem_v, w_br_s5, w_br_ml, w_br_xa, w_out, g_ffn2, w2_gate, w2_up,
           w2_down, g_final):
    bf = lambda a: a.astype(BF16)

    w_in_t = w_in[0].T
    job = functools.partial
    (x1, w2g, w2u, w2d, w_head, w_bg, wb_s5, wb_ml, wb_xa, w_o, w_glu, w_mk, w_mv) = _ffn(
        (x_prompt.reshape(NP, D), x_sample.reshape(NS, D)), g_ffn1[0].reshape(1, D),
        bf(w1_gate[0]), bf(w1_up[0]), bf(w1_down[0]), TF,
        side_jobs=(job(_cast_job, w2_gate, 16), job(_cast_job, w2_up, 16), job(_cast_job, w2_down, 32),
                   job(_transpose_job, w_in_t, 0, HEAD), job(_transpose_job, w_in_t, MIX0, 3 * D, place=_gate_tile_major),
                   job(_cast_job, w_br_s5, 16), job(_cast_job, w_br_ml, 16), job(_cast_job, w_br_xa, 16),
                   job(_cast_job, w_out, 16), job(_cast_job, w_s5_glu, 16),
                   job(_cast_job, w_mem_k, 16), job(_cast_job, w_mem_v, 16)))

    w_ift = bf(w_in_t[GATE0:GATE0 + 2 * NH])
    w_br = [wb_s5, wb_ml, wb_xa]
    brow = jnp.pad(jnp.concatenate([b_igate[0], b_fgate[0]]), (0, 128 - 2 * NH)).reshape(1, 128)
    bcol = jnp.concatenate([b_igate[0], b_fgate[0]]).reshape(2 * NH, 1)
    gh = g_mlstm_head[0].reshape(1, W)

    z_f, z_b, z_if, z_t, zs_f, zs_b, zs_if = _in_proj(x1, g_mix[0].reshape(1, D), w_head, w_ift)
    zs = zs_b[:NS].astype(F32)
    per_seq = lambda a: a.reshape(BATCH, SEQ, a.shape[1])

    blk = lambda a: jnp.tile(a.reshape(NGB, WU, P), (1, 1, GB))
    row = lambda a: a.reshape(NGB, 1, WS)
    (a_re, a_im, a16_re, a16_im, bb_re, bb_im, e_op, ft_op, bd_op) = _s5_prep(
        row(s5_lambda_re[0]), row(s5_lambda_im[0]), row(jnp.repeat(s5_log_step[0], P)),
        blk(s5_b_re[0].transpose(0, 2, 1)), blk(s5_b_im[0].transpose(0, 2, 1)),
        blk(s5_c_re[0]), blk(s5_c_im[0]), s5_d[0].reshape(NGB, WU, 1))
    y_s5_p, fin = _s5_chunk(z_f, bd_op, e_op, ft_op, a16_re, a16_im)
    fin = fin.reshape(NGB, BATCH, 2, GB, P).transpose(2, 1, 0, 3, 4).reshape(2, 1, BATCH, G, P)
    y_s5_s, s5_re_s, s5_im_s = _s5_step(
        zs_f, state_s5_re[0].reshape(NS, G * P), state_s5_im[0].reshape(NS, G * P),
        a_re, a_im, bb_re, bb_im, blk(s5_c_re[0]), blk(s5_c_im[0]), s5_d[0].reshape(1, W))

    ml_p, c_p, n_p, m_p = _mlstm_chunk(per_seq(z_f), per_seq(z_b), per_seq(z_if), z_t, brow, bcol, gh)
    ml_s, c_s, n_s, m_s = _mlstm_step(zs, zs_f, zs_if, brow, gh.reshape(NH, DH), state_mlstm_C[0],
                                      state_mlstm_n[0], state_mlstm_m[0])

    mem_k, mem_v = _mem_proj(mem_prompt.reshape(BATCH * N_MEM, D), g_mem[0].reshape(1, D), w_mk, w_mv)
    xa_p = _xattn_prompt(z_b, mem_k, mem_v)
    xa_s = _xattn_step(zs, cache_mem_k[0], cache_mem_v[0])

    merged = _mix(x1, g_mix[0].reshape(1, D), y_s5_p, y_s5_s, ml_p.reshape(NP, W), ml_s, xa_p, xa_s,
                  w_glu, w_bg, w_br)
    x2 = _outproj(x1, merged, w_o)
    y_p, y_s = _ffn((x2,), g_ffn2[0].reshape(1, D), w2g, w2u, w2d, TF, g_final.reshape(1, D))

    return (y_p.reshape(BATCH, SEQ, D), y_s.reshape(NS, 1, D),
            mem_k.reshape(1, BATCH, N_MEM, NH, DH), mem_v.reshape(1, BATCH, N_MEM, NH, DH),
            fin[0], fin[1], c_p[None], n_p[None], m_p[:, :, 0][None],
            s5_re_s.reshape(1, NS, G, P), s5_im_s.reshape(1, NS, G, P), c_s[None], n_s[None], m_s[None])
```

```python
import functools
from typing import Callable, NamedTuple

import jax
import jax.numpy as jnp
from jax import lax
from jax.experimental import pallas as pl
from jax.experimental.pallas import tpu as pltpu

F32 = jnp.float32
BF16 = jnp.bfloat16

D = 2048
BATCH = 4
SEQ = 2048
NS = 128
NP = BATCH * SEQ
TM = 512
NPT = NP // TM
NT = NPT + 1
MROWS = NT * TM
N_MEM = 256
FF = 5504
TF = 512
G = 64
P = 64
HG = 16
TC = 16
GB = 8
NGB = G // GB
WU = GB * HG
WS = GB * P
NCH = SEQ // TC
S5B = 2
W = 1024
NH = 4
DH = 256
CL = 256
NCL = SEQ // CL
SB = 8
EPS = 1e-6
VMEM_LIMIT = 56 * 1024 * 1024


def _cp(sem, vmem=VMEM_LIMIT):
    return pltpu.CompilerParams(dimension_semantics=sem, vmem_limit_bytes=vmem)


def _dot(a, b):
    return jnp.dot(a, b, preferred_element_type=F32)


def _dot_nt(a, b):
    return lax.dot_general(a, b, (((1,), (1,)), ((), ())), preferred_element_type=F32)


def _dot_tn(a, b):
    return lax.dot_general(a, b, (((0,), (0,)), ((), ())), preferred_element_type=F32)


def _hi_lo(x):
    hi = x.astype(BF16)
    lo = (x - hi.astype(F32)).astype(BF16)
    return hi, lo


def _split3(x):
    hi = x.astype(BF16)
    r1 = x - hi.astype(F32)
    mid = r1.astype(BF16)
    lo = (r1 - mid.astype(F32)).astype(BF16)
    return hi, mid, lo


def _dot3(a, b):
    ah, al = _hi_lo(a)
    bh, bl = _hi_lo(b)
    return _dot(ah, bh) + (_dot(ah, bl) + _dot(al, bh))


def _dot3_nt(a, b):
    ah, al = _hi_lo(a)
    bh, bl = _hi_lo(b)
    return _dot_nt(ah, bh) + (_dot_nt(ah, bl) + _dot_nt(al, bh))


def _rms(x, g):
    r = lax.rsqrt(jnp.mean(x * x, axis=-1, keepdims=True) + EPS)
    return (x * r) * g


def _sigmoid(x):
    return 1.0 / (1.0 + jnp.exp(-x))


def _log_sigmoid(x):
    return jnp.minimum(x, 0.0) - jnp.log1p(jnp.exp(-jnp.abs(x)))


def _gelu_tanh(x):
    return x * (0.5 * (1.0 + jnp.tanh(0.7978845608028654 * (x + 0.044715 * (x * x * x)))))


def _ffn_kernel(*refs, two_src, final_norm, side_jobs, tf):
    nj = pl.cdiv(FF, tf)
    tf_last = FF - (nj - 1) * tf
    assert tf_last % 128 == 0
    refs = list(refs)
    if two_src:
        xp_ref, xs_ref = refs[:2]
        refs = refs[2:]
    else:
        xp_ref = xs_ref = refs[0]
        refs = refs[1:]
    g_ref, wg_ref, wu_ref, wd_ref = refs[:4]
    refs = refs[4:]
    n_in = sum(n for n, _, _ in side_jobs)
    n_out = sum(n for _, n, _ in side_jobs)
    if final_norm:
        gf_ref = refs[0]
        refs = refs[1:]
    side_in = refs[:n_in]
    refs = refs[n_in:]
    if final_norm:
        op_ref, os_ref = refs[:2]
        refs = refs[2:]
    else:
        o_ref = refs[0]
        refs = refs[1:]
    side_out = refs[:n_out]
    h_scr, acc_scr = refs[n_out:]
    j = pl.program_id(1)

    def side_work():
        a = b = 0
        for ni, no, fn in side_jobs:
            fn(side_in[a:a + ni], side_out[b:b + no])
            a += ni
            b += no

    def tile(rows, x_ref):
        rs = slice(0, rows)

        def accumulate(width, first=False):
            side_work()
            h = h_scr[rs, :]
            gt = _dot(h, wg_ref[:, 0:width])
            hid = (gt * _sigmoid(gt)) * _dot(h, wu_ref[:, 0:width])
            part = _dot(hid.astype(BF16), wd_ref[0:width, :])
            acc_scr[rs, :] = part if first else acc_scr[rs, :] + part

        @pl.when(j == 0)
        def _():
            h_scr[rs, :] = _rms(x_ref[rs, :], g_ref[...]).astype(BF16)
            accumulate(tf, first=True)

        pl.when(jnp.logical_and(j > 0, j < nj - 1))(functools.partial(accumulate, tf))

        @pl.when(j == nj - 1)
        def _():
            accumulate(tf_last)
            y = x_ref[rs, :] + 0.5 * acc_scr[rs, :]
            if final_norm:
                y = _rms(y, gf_ref[...])
                (op_ref if rows == TM else os_ref)[...] = y
            else:
                o_ref[rs, :] = y
                if rows < TM:
                    o_ref[rows:, :] = jnp.zeros((TM - rows, D), F32)

    pl.when(pl.program_id(0) < NPT)(functools.partial(tile, TM, xp_ref))
    pl.when(pl.program_id(0) == NPT)(functools.partial(tile, NS, xs_ref))


class _SideJob(NamedTuple):
    blocks: int
    args: list
    in_specs: list
    out_shape: list
    out_specs: list
    fn: Callable


def _step_block(nj, blocks, first=0):
    return lambda i, j: first + jnp.minimum(i * nj + j, blocks - 1)


def _cast_job(a, rb, nj):
    rows_a, cols_a = a.shape[-2:]
    blk = _step_block(nj, rows_a // rb)

    def fn(ins, outs):
        outs[0][...] = ins[0][...].astype(BF16)

    return _SideJob(rows_a // rb, [a], [pl.BlockSpec((None, rb, cols_a), lambda i, j: (0, blk(i, j), 0))],
                    [jax.ShapeDtypeStruct((rows_a, cols_a), BF16)], [pl.BlockSpec((rb, cols_a), lambda i, j: (blk(i, j), 0))],
                    fn)


def _transpose_job(wt, row0, cols, nj):
    k = wt.shape[1]
    ksplit = 2
    kb = k // ksplit
    shift = row0 % 128
    assert shift in (0, 8) and cols % 128 == 0
    nblk = cols // 128 * ksplit
    unit = _step_block(nj, nblk)
    first = row0 // 128
    last = pl.cdiv(wt.shape[0], 128) - 1
    specs = [pl.BlockSpec((128, kb), lambda i, j: (first + unit(i, j) // ksplit, unit(i, j) % ksplit))]
    if shift:
        specs.append(pl.BlockSpec(
            (128, kb), lambda i, j: (jnp.minimum(first + unit(i, j) // ksplit + 1, last), unit(i, j) % ksplit)))

    def fn(ins, outs):
        win = ins[0][...]
        if shift:
            win = jnp.concatenate([win[shift:, :], ins[1][0:shift, :]], axis=0)
        outs[0][...] = win.T.astype(BF16)

    return _SideJob(nblk, [wt] * len(specs), specs, [jax.ShapeDtypeStruct((k, cols), BF16)],
                    [pl.BlockSpec((kb, 128), lambda i, j: (unit(i, j) % ksplit, unit(i, j) // ksplit))], fn)


def _ffn(xs, g, wg, wu, wd, tf, g_final=None, side_jobs=()):
    nj = pl.cdiv(FF, tf)
    two_src = len(xs) == 2
    final_norm = g_final is not None
    row = pl.BlockSpec((TM, D), lambda i, j: (i, 0))
    prow = pl.BlockSpec((TM, D), lambda i, j: (jnp.minimum(i, NPT - 1), 0))
    srow = pl.BlockSpec((NS, D), lambda i, j: (0, 0))
    vec = pl.BlockSpec((1, D), lambda i, j: (0, 0))
    in_specs = ([prow, srow] if two_src else [row]) + [
        vec,
        pl.BlockSpec((D, tf), lambda i, j: (0, j)),
        pl.BlockSpec((D, tf), lambda i, j: (0, j)),
        pl.BlockSpec((tf, D), lambda i, j: (j, 0)),
    ]
    args = list(xs) + [g, wg, wu, wd]
    if final_norm:
        in_specs.append(vec)
        args.append(g_final)
        out_shape = [jax.ShapeDtypeStruct((NP, D), F32), jax.ShapeDtypeStruct((NS, D), F32)]
        out_specs = [prow, srow]
    else:
        out_shape = [jax.ShapeDtypeStruct((MROWS, D), F32)]
        out_specs = [row]
    jobs = []
    for make_job in side_jobs:
        job = make_job(nj)
        assert job.blocks <= NPT * nj, "side jobs must fit under the full-size row tiles"
        in_specs += job.in_specs
        args += job.args
        out_specs += job.out_specs
        out_shape += job.out_shape
        jobs.append((len(job.args), len(job.out_shape), job.fn))
    return pl.pallas_call(
        functools.partial(_ffn_kernel, two_src=two_src, final_norm=final_norm, side_jobs=tuple(jobs), tf=tf),
        grid=(NT, nj),
        in_specs=in_specs,
        out_specs=out_specs,
        out_shape=out_shape,
        scratch_shapes=[pltpu.VMEM((TM, D), BF16), pltpu.VMEM((TM, D), F32)],
        compiler_params=_cp(("arbitrary", "arbitrary")),
        name="ffn_final" if final_norm else "ffn",
    )(*args)


NF32 = 2
TMI = 256
NPI = NP // TMI
GATE0 = 5 * W
HEAD = GATE0 + W + 128
MIX0 = GATE0 + 2 * NH + W


def _in_proj_kernel(x_ref, g_ref, w_ref, wift_ref,
                    ofp_ref, obp_ref, zifp_ref, ztp_ref, ofs_ref, obs_ref, zifs_ref):
    def emit(of_ref, ob_ref, zif_ref, zt_ref):
        h = _rms(x_ref[...], g_ref[...]).astype(BF16)
        if zt_ref is not None:
            zt_ref[...] = _dot_nt(wift_ref[...], h)
        of_ref[:, 0:W] = _dot(h, w_ref[:, 0:W])
        for j in range(1, 4):
            ob_ref[:, (j - 1) * W:j * W] = _dot(h, w_ref[:, j * W:(j + 1) * W]).astype(BF16)
        of_ref[:, W:2 * W] = _dot(h, w_ref[:, 4 * W:5 * W])
        tail = _dot(h, w_ref[:, 5 * W:HEAD])
        zif_ref[...] = tail[:, 0:128]
        ob_ref[:, 3 * W:4 * W] = tail[:, 2 * NH:2 * NH + W].astype(BF16)

    pl.when(pl.program_id(0) < NPI)(functools.partial(emit, ofp_ref, obp_ref, zifp_ref, ztp_ref))
    pl.when(pl.program_id(0) == NPI)(functools.partial(emit, ofs_ref, obs_ref, zifs_ref, None))


def _in_proj(x, g, w, w_ift):
    once = pl.Buffered(1)
    pblk = lambda cols: pl.BlockSpec((TMI, cols), lambda i: (jnp.minimum(i, NPI - 1), 0))
    sblk = lambda cols: pl.BlockSpec((TMI, cols), lambda i: (0, 0))
    widths = (NF32 * W, 4 * W, 128)
    dtypes = (F32, BF16, F32)
    return pl.pallas_call(
        _in_proj_kernel,
        grid=(NPI + 1,),
        in_specs=[
            pl.BlockSpec((TMI, D), lambda i: (i, 0)),
            pl.BlockSpec((1, D), lambda i: (0, 0)),
            pl.BlockSpec((D, HEAD), lambda i: (0, 0), pipeline_mode=once),
            pl.BlockSpec((8, D), lambda i: (0, 0), pipeline_mode=once),
        ],
        out_specs=[pblk(c) for c in widths]
        + [pl.BlockSpec((8, TMI), lambda i: (0, jnp.minimum(i, NPI - 1)))]
        + [sblk(c) for c in widths],
        out_shape=[jax.ShapeDtypeStruct((NP, c), t) for c, t in zip(widths, dtypes)]
        + [jax.ShapeDtypeStruct((8, NP), F32)]
        + [jax.ShapeDtypeStruct((TMI, c), t) for c, t in zip(widths, dtypes)],
        compiler_params=_cp(("arbitrary",)),
        name="in_proj",
    )(x, g, w, w_ift)


def _mem_proj_kernel(x_ref, g_ref, wk_ref, wv_ref, k_ref, v_ref):
    h = _rms(x_ref[...], g_ref[...]).astype(BF16)
    k_ref[...] = _dot(h, wk_ref[...])
    v_ref[...] = _dot(h, wv_ref[...])


def _mem_proj(x, g, wk, wv):
    m = x.shape[0]
    wspec = pl.BlockSpec((D, W), lambda i: (0, 0))
    ospec = pl.BlockSpec((TM, W), lambda i: (i, 0))
    return pl.pallas_call(
        _mem_proj_kernel,
        grid=(m // TM,),
        in_specs=[pl.BlockSpec((TM, D), lambda i: (i, 0)), pl.BlockSpec((1, D), lambda i: (0, 0)), wspec, wspec],
        out_specs=[ospec, ospec],
        out_shape=[jax.ShapeDtypeStruct((m, W), F32)] * 2,
        compiler_params=_cp(("arbitrary",)),
        name="mem_proj",
    )(x, g, wk, wv)


def _s5_prep_kernel(lr_ref, li_ref, ls_ref, btr_ref, bti_ref, cr_ref, ci_ref, d_ref,
                    ar_ref, ai_ref, a16r_ref, a16i_ref, bbr_ref, bbi_ref, e_ref, ft_ref, bd_ref):
    lr = lr_ref[0]
    li = li_ref[0]
    dt = jnp.exp(ls_ref[0])

    def power(k):
        mag = jnp.exp(lr * dt * float(k))
        ang = li * dt * float(k)
        return mag * jnp.cos(ang), mag * jnp.sin(ang)

    pw = [power(k) for k in range(TC + 1)]
    ar, ai = pw[1]
    den = lr * lr + li * li
    nr = ar - 1.0
    z_re = (nr * lr + ai * li) / den
    z_im = (ai * lr - nr * li) / den
    mask = (lax.broadcasted_iota(jnp.int32, (WU, WS), 0) // HG
            == lax.broadcasted_iota(jnp.int32, (WU, WS), 1) // P)
    btr = btr_ref[0]
    bti = bti_ref[0]
    bbr = jnp.where(mask, z_re * btr - z_im * bti, 0.0)
    bbi = jnp.where(mask, z_re * bti + z_im * btr, 0.0)
    cr = jnp.where(mask, cr_ref[0], 0.0)
    ci = jnp.where(mask, ci_ref[0], 0.0)

    ar_ref[0] = ar
    ai_ref[0] = ai
    a16r_ref[0] = pw[TC][0]
    a16i_ref[0] = pw[TC][1]
    bbr_ref[0] = bbr
    bbi_ref[0] = bbi

    def cmul(xr, xi, k):
        pr, pi = pw[k]
        return xr * pr - xi * pi, xr * pi + xi * pr

    diag = (lax.broadcasted_iota(jnp.int32, (WU, WU), 0) == lax.broadcasted_iota(jnp.int32, (WU, WU), 1))
    rr, ri = cr, ci
    bbr_b = bbr.astype(BF16)
    bbi_b = bbi.astype(BF16)
    for s in range(TC):
        er, ei = cmul(bbr, bbi, TC - 1 - s)
        e_ref[0, s * WU:(s + 1) * WU, 0:WS] = er.astype(BF16)
        e_ref[0, s * WU:(s + 1) * WU, WS:2 * WS] = ei.astype(BF16)
        fr, fi = cmul(cr, ci, s + 1)
        ft_ref[0, s * WU:(s + 1) * WU, 0:WS] = fr.astype(BF16)
        ft_ref[0, s * WU:(s + 1) * WU, WS:2 * WS] = (-fi).astype(BF16)
        kern = _dot_nt(bbr_b, rr.astype(BF16)) - _dot_nt(bbi_b, ri.astype(BF16))
        if s == 0:
            kern = kern + jnp.where(diag, d_ref[0], 0.0)
        bd_ref[0, s] = kern
        rr, ri = fr, fi


def _s5_prep(lam_re, lam_im, log_step, bt_re, bt_im, c_re, c_im, d):
    def spec(*shape):
        nd = len(shape)
        return pl.BlockSpec((1,) + shape, lambda i: (i,) + (0,) * nd)

    def sds(*shape, dtype=F32):
        return jax.ShapeDtypeStruct((NGB,) + shape, dtype)

    return pl.pallas_call(
        _s5_prep_kernel,
        grid=(NGB,),
        in_specs=[spec(1, WS)] * 3 + [spec(WU, WS)] * 4 + [spec(WU, 1)],
        out_specs=[spec(1, WS)] * 4 + [spec(WU, WS)] * 2 + [spec(TC * WU, 2 * WS)] * 2 + [spec(TC, WU, WU)],
        out_shape=[sds(1, WS)] * 4 + [sds(WU, WS)] * 2 + [sds(TC * WU, 2 * WS, dtype=BF16)] * 2
        + [sds(TC, WU, WU)],
        compiler_params=_cp(("arbitrary",)),
        name="s5_prep",
    )(lam_re, lam_im, log_step, bt_re, bt_im, c_re, c_im, d)


def _s5_chunk_kernel(u_ref, bd_ref, e_ref, ft_ref, ar_ref, ai_ref, y_ref, fin_ref,
                     w_scr, lhs_scr, s_scr, xs_scr):
    rows = S5B * NCH

    @pl.when(pl.program_id(1) == 0)
    def _():
        w_scr[...] = jnp.zeros_like(w_scr)
        bd = [bd_ref[0, k].astype(BF16) for k in range(TC)]
        for s in range(TC):
            for t in range(s, TC):
                w_scr[s * WU:(s + 1) * WU, t * WU:(t + 1) * WU] = bd[t - s]

    for s in range(TC):
        lhs_scr[:, s * WU:(s + 1) * WU] = u_ref[pl.ds(s, rows, stride=TC), :].astype(BF16)
    lhs = lhs_scr[...]
    s_loc = _dot(lhs, e_ref[0])
    nl = WS // 128
    for k in range(2 * nl):
        s_scr[k] = s_loc[:, k * 128:(k + 1) * 128]
    ar = [ar_ref[0, :, k * 128:(k + 1) * 128] for k in range(nl)]
    ai = [ai_ref[0, :, k * 128:(k + 1) * 128] for k in range(nl)]
    xr = [jnp.zeros((S5B, 128), F32)] * nl
    xi = [jnp.zeros((S5B, 128), F32)] * nl
    for c in range(NCH):
        chunk_rows = pl.ds(c, S5B, stride=NCH)
        for k in range(nl):
            xs_scr[k, chunk_rows, :] = xr[k]
            xs_scr[nl + k, chunk_rows, :] = xi[k]
            sr = s_scr[k, chunk_rows, :]
            si = s_scr[nl + k, chunk_rows, :]
            xr[k], xi[k] = ar[k] * xr[k] - ai[k] * xi[k] + sr, ar[k] * xi[k] + ai[k] * xr[k] + si
    for k in range(nl):
        fin_ref[0, 0, :, k * 128:(k + 1) * 128] = xr[k]
        fin_ref[0, 0, :, WS + k * 128:WS + (k + 1) * 128] = xi[k]
    xs = jnp.concatenate([xs_scr[k] for k in range(2 * nl)], axis=1)
    y_carry = _dot_nt(xs.astype(BF16), ft_ref[0])
    for t2 in range(0, TC, 2):
        cols = slice(t2 * WU, (t2 + 2) * WU)
        y = _dot(lhs[:, 0:(t2 + 2) * WU], w_scr[0:(t2 + 2) * WU, cols]) + y_carry[:, cols]
        for t in (t2, t2 + 1):
            y_ref[pl.ds(t, rows, stride=TC), :] = y[:, (t - t2) * WU:(t - t2 + 1) * WU]


def _s5_chunk(z_f, bd, e, ft, a16r, a16i):
    rows = S5B * NCH
    nh = BATCH // S5B
    return pl.pallas_call(
        _s5_chunk_kernel,
        grid=(NGB, nh),
        in_specs=[
            pl.BlockSpec((S5B * SEQ, WU), lambda j, b: (b, j)),
            pl.BlockSpec((1, TC, WU, WU), lambda j, b: (j, 0, 0, 0)),
            pl.BlockSpec((1, TC * WU, 2 * WS), lambda j, b: (j, 0, 0)),
            pl.BlockSpec((1, TC * WU, 2 * WS), lambda j, b: (j, 0, 0)),
            pl.BlockSpec((1, 1, WS), lambda j, b: (j, 0, 0)),
            pl.BlockSpec((1, 1, WS), lambda j, b: (j, 0, 0)),
        ],
        out_specs=[
            pl.BlockSpec((S5B * SEQ, WU), lambda j, b: (b, j)),
            pl.BlockSpec((1, 1, S5B, 2 * WS), lambda j, b: (j, b, 0, 0)),
        ],
        out_shape=[jax.ShapeDtypeStruct((NP, W), F32), jax.ShapeDtypeStruct((NGB, nh, S5B, 2 * WS), F32)],
        scratch_shapes=[pltpu.VMEM((TC * WU, TC * WU), BF16), pltpu.VMEM((rows, TC * WU), BF16),
                        pltpu.VMEM((2 * WS // 128, rows, 128), F32), pltpu.VMEM((2 * WS // 128, rows, 128), F32)],
        compiler_params=_cp(("arbitrary", "arbitrary")),
        name="s5_chunk",
    )(z_f, bd, e, ft, a16r, a16i)


def _s5_step_kernel(u_ref, sr_ref, si_ref, ar_ref, ai_ref, bbr_ref, bbi_ref, cr_ref, ci_ref, d_ref,
                    y_ref, xr_ref, xi_ref):
    mask = (lax.broadcasted_iota(jnp.int32, (WU, WS), 0) // HG
            == lax.broadcasted_iota(jnp.int32, (WU, WS), 1) // P)
    for j in range(NGB):
        ul = slice(j * WU, (j + 1) * WU)
        sl = slice(j * WS, (j + 1) * WS)
        us = u_ref[:, ul]
        ar = ar_ref[j]
        ai = ai_ref[j]
        s_re = sr_ref[:, sl]
        s_im = si_ref[:, sl]
        x_re = ar * s_re - ai * s_im + _dot3(us, bbr_ref[j])
        x_im = ar * s_im + ai * s_re + _dot3(us, bbi_ref[j])
        xr_ref[:, sl] = x_re
        xi_ref[:, sl] = x_im
        cre = jnp.where(mask, cr_ref[j], 0.0)
        cim = jnp.where(mask, ci_ref[j], 0.0)
        y_ref[:, ul] = _dot3_nt(x_re, cre) - _dot3_nt(x_im, cim) + d_ref[:, ul] * us


def _s5_step(z_f, s_re, s_im, a_re, a_im, bb_re, bb_im, c_re, c_im, d_row):
    full = lambda *shape: pl.BlockSpec(shape, lambda i: (0,) * len(shape))
    return pl.pallas_call(
        _s5_step_kernel,
        grid=(1,),
        in_specs=[pl.BlockSpec((NS, W), lambda i: (0, 0)),
                  full(NS, G * P), full(NS, G * P), full(NGB, 1, WS), full(NGB, 1, WS),
                  full(NGB, WU, WS), full(NGB, WU, WS), full(NGB, WU, WS), full(NGB, WU, WS),
                  full(1, W)],
        out_specs=[full(NS, W), full(NS, G * P), full(NS, G * P)],
        out_shape=[jax.ShapeDtypeStruct((NS, W), F32), jax.ShapeDtypeStruct((NS, G * P), F32),
                   jax.ShapeDtypeStruct((NS, G * P), F32)],
        compiler_params=_cp(("arbitrary",)),
        name="s5_step",
    )(z_f, s_re, s_im, a_re, a_im, bb_re, bb_im, c_re, c_im, d_row)


def _mlstm_chunk_kernel(q_ref, k_ref, v_ref, o_ref, zif_ref, zt0_ref, zt1_ref, zt2_ref, zt3_ref,
                        brow_ref, bcol_ref, gh_ref, h_ref, c_ref, n_ref, m_ref):
    @pl.when(pl.program_id(0) == 0)
    def _():
        c_ref[...] = jnp.zeros_like(c_ref)
        n_ref[...] = jnp.zeros_like(n_ref)
        m_ref[...] = jnp.zeros_like(m_ref)

    rr = lax.broadcasted_iota(jnp.int32, (CL, CL), 0)
    cc = lax.broadcasted_iota(jnp.int32, (CL, CL), 1)
    causal = cc <= rr
    tril = jnp.where(causal, 1.0, 0.0).astype(BF16)
    triu = jnp.where(rr <= cc, 1.0, 0.0).astype(BF16)

    for b, zt_ref in enumerate((zt0_ref, zt1_ref, zt2_ref, zt3_ref)):
        zi = zif_ref[b] + brow_ref[...]
        zt = zt_ref[...] + bcol_ref[...]
        lfc = _split3(_log_sigmoid(zi))
        bcum_col = _dot(tril, lfc[0]) + (_dot(tril, lfc[1]) + _dot(tril, lfc[2]))
        lfr = _split3(_log_sigmoid(zt))
        bcum_row = _dot(lfr[0], triu) + (_dot(lfr[1], triu) + _dot(lfr[2], triu))

        for hh in range(NH):
            hs = slice(hh * DH, (hh + 1) * DH)
            bc = bcum_col[:, NH + hh:NH + hh + 1]
            ic = zi[:, hh:hh + 1]
            br = bcum_row[NH + hh:NH + hh + 1, :]
            ir = zt[hh:hh + 1, :]
            m_prev = m_ref[b, hh:hh + 1, 0:1]
            g_inter = bc + m_prev
            dlog = jnp.where(causal, (bc - br) + ir, -jnp.inf)
            m_t = jnp.maximum(g_inter, jnp.max(dlog, axis=-1, keepdims=True))
            w_inter = jnp.exp(g_inter - m_t)
            w_intra = jnp.exp(dlog - m_t)
            qb = q_ref[b, :, hs]
            kb = k_ref[b, :, hs]
            vb = v_ref[b, :, hs]
            qf = qb.astype(F32)
            kf = kb.astype(F32)
            s = _dot_nt(qb, kb) * (w_intra * (DH ** -0.5))
            c_prev = c_ref[b, hh]
            n_prev = n_ref[b, hh:hh + 1, :]
            num = _dot(s.astype(BF16), vb) + w_inter * _dot(qb, c_prev.astype(BF16))
            nq = jnp.sum(s, axis=-1, keepdims=True) + w_inter * jnp.sum(qf * n_prev, axis=-1, keepdims=True)
            h = num / jnp.maximum(jnp.abs(nq), jnp.exp(-m_t))
            m_last = m_t[CL - 1:CL, :]
            w_last = jnp.exp((bc[CL - 1:CL, :] - bc) + ic - m_last) * (DH ** -0.5)
            wi_last = w_inter[CL - 1:CL, :]
            kw = kf * w_last
            c_ref[b, hh] = wi_last * c_prev + _dot_tn(kw.astype(BF16), vb)
            n_ref[b, hh:hh + 1, :] = wi_last * n_prev + jnp.sum(kw, axis=0, keepdims=True)
            m_ref[b, hh:hh + 1, :] = jnp.broadcast_to(m_last, (1, 128))
            hn = _rms(h, gh_ref[:, hs])
            h_ref[b, :, hs] = (hn * _sigmoid(o_ref[b, :, hs])).astype(BF16)


def _mlstm_chunk(z_f, z_b, z_if, z_t, brow, bcol, gh):
    blk = lambda col: pl.BlockSpec((BATCH, CL, W), lambda c: (0, c, col))
    const = lambda *shape: pl.BlockSpec(shape, lambda c: (0,) * len(shape))
    zt_specs = [pl.BlockSpec((8, CL), functools.partial(lambda c, b: (0, b * NCL + c), b=b)) for b in range(BATCH)]
    return pl.pallas_call(
        _mlstm_chunk_kernel,
        grid=(NCL,),
        in_specs=[blk(0), blk(1), blk(2), blk(1), pl.BlockSpec((BATCH, CL, 128), lambda c: (0, c, 0))]
        + zt_specs + [const(1, 128), const(8, 1), const(1, W)],
        out_specs=[pl.BlockSpec((BATCH, CL, W), lambda c: (0, c, 0)),
                   const(BATCH, NH, DH, DH), const(BATCH, NH, DH), const(BATCH, NH, 128)],
        out_shape=[jax.ShapeDtypeStruct((BATCH, SEQ, W), BF16),
                   jax.ShapeDtypeStruct((BATCH, NH, DH, DH), F32),
                   jax.ShapeDtypeStruct((BATCH, NH, DH), F32),
                   jax.ShapeDtypeStruct((BATCH, NH, 128), F32)],
        compiler_params=_cp(("arbitrary",)),
        name="mlstm_chunk",
    )(z_b, z_b, z_b, z_f, z_if, z_t, z_t, z_t, z_t, brow, bcol, gh)


MSR = 64


def _mlstm_step_kernel(q_ref, k_ref, v_ref, o_ref, zif_ref, brow_ref, gh_ref, c0_ref, n0_ref, m0_ref,
                       h_ref, c_ref, n_ref, m_ref):
    zi = zif_ref[...] + brow_ref[...]
    ig = zi[:, 0:NH]
    g_inter = _log_sigmoid(zi[:, NH:2 * NH]) + m0_ref[...]
    m_t = jnp.maximum(g_inter, ig)
    w_inter = jnp.exp(g_inter - m_t)
    w_intra = jnp.exp(ig - m_t) * (DH ** -0.5)
    floor = jnp.exp(-m_t)
    m_ref[...] = m_t

    def heads_on_rows(x):
        return jnp.pad(x, ((0, 128 - SB), (0, 128 - NH))).T

    w_inter_t = heads_on_rows(w_inter)
    w_intra_t = heads_on_rows(w_intra)
    floor_t = heads_on_rows(floor)
    pad = jnp.zeros((128 - SB, DH), F32)
    q_t = [jnp.concatenate([q_ref[:, hh * DH:(hh + 1) * DH], pad], axis=0).T for hh in range(NH)]
    k_t = [jnp.concatenate([k_ref[:, hh * DH:(hh + 1) * DH], pad], axis=0).T for hh in range(NH)]

    def per_head_rows(ref, s):
        return jnp.concatenate([ref[s:s + 1, hh * DH:(hh + 1) * DH] for hh in range(NH)], axis=0)

    for s in range(SB):
        q4 = per_head_rows(q_ref, s)
        k4 = per_head_rows(k_ref, s)
        v4 = per_head_rows(v_ref, s)
        qc_rows = []
        for hh in range(NH):
            wi = w_inter[s:s + 1, hh:hh + 1]
            vw = v4[hh:hh + 1, :] * w_intra[s:s + 1, hh:hh + 1]
            acc = jnp.zeros((MSR, DH), F32)
            for r0 in range(0, DH, MSR):
                c_blk = c0_ref[s, hh, r0:r0 + MSR, :]
                acc = acc + q_t[hh][r0:r0 + MSR, s:s + 1] * c_blk
                c_ref[s, hh, r0:r0 + MSR, :] = wi * c_blk + k_t[hh][r0:r0 + MSR, s:s + 1] * vw
            qc_rows.append(jnp.sum(acc, axis=0, keepdims=True))
        q_c = jnp.concatenate(qc_rows, axis=0)
        wi_c = w_inter_t[0:NH, s:s + 1]
        wa_c = w_intra_t[0:NH, s:s + 1]
        n_prev = n0_ref[s]
        sv = jnp.sum(q4 * k4, axis=-1, keepdims=True) * wa_c
        num = sv * v4 + wi_c * q_c
        nq = sv + wi_c * jnp.sum(q4 * n_prev, axis=-1, keepdims=True)
        h = num / jnp.maximum(jnp.abs(nq), floor_t[0:NH, s:s + 1])
        n_ref[s] = wi_c * n_prev + wa_c * k4
        out = _rms(h, gh_ref[...]) * _sigmoid(per_head_rows(o_ref, s))
        for hh in range(NH):
            h_ref[s:s + 1, hh * DH:(hh + 1) * DH] = out[hh:hh + 1, :]


def _mlstm_step(zs, z_f, z_if, brow, gh, c0, n0, m0):
    blk = lambda col: pl.BlockSpec((SB, W), lambda i: (i, col))
    return pl.pallas_call(
        _mlstm_step_kernel,
        grid=(NS // SB,),
        in_specs=[blk(0), blk(1), blk(2), blk(1),
                  pl.BlockSpec((SB, 128), lambda i: (i, 0)),
                  pl.BlockSpec((1, 128), lambda i: (0, 0)),
                  pl.BlockSpec((NH, DH), lambda i: (0, 0)),
                  pl.BlockSpec((SB, NH, DH, DH), lambda i: (i, 0, 0, 0)),
                  pl.BlockSpec((SB, NH, DH), lambda i: (i, 0, 0)),
                  pl.BlockSpec((SB, NH), lambda i: (i, 0))],
        out_specs=[pl.BlockSpec((SB, W), lambda i: (i, 0)),
                   pl.BlockSpec((SB, NH, DH, DH), lambda i: (i, 0, 0, 0)),
                   pl.BlockSpec((SB, NH, DH), lambda i: (i, 0, 0)),
                   pl.BlockSpec((SB, NH), lambda i: (i, 0))],
        out_shape=[jax.ShapeDtypeStruct((NS, W), F32),
                   jax.ShapeDtypeStruct((NS, NH, DH, DH), F32),
                   jax.ShapeDtypeStruct((NS, NH, DH), F32),
                   jax.ShapeDtypeStruct((NS, NH), F32)],
        compiler_params=_cp(("arbitrary",)),
        name="mlstm_step",
    )(zs, zs, zs, z_f, z_if, brow, gh, c0, n0, m0)


def _softmax_rows(s):
    e = jnp.exp(s - jnp.max(s, axis=-1, keepdims=True))
    return e / jnp.sum(e, axis=-1, keepdims=True)


def _xattn_prompt_kernel(q_ref, k_ref, v_ref, o_ref):
    for hh in range(NH):
        hs = slice(hh * DH, (hh + 1) * DH)
        s = _dot_nt(q_ref[:, hs], k_ref[:, hs].astype(BF16)) * (DH ** -0.5)
        p = _softmax_rows(s)
        o_ref[:, hs] = _dot(p.astype(BF16), v_ref[:, hs].astype(BF16)).astype(BF16)


def _xattn_prompt(z_b, mem_k, mem_v):
    nt = SEQ // TM
    return pl.pallas_call(
        _xattn_prompt_kernel,
        grid=(BATCH, nt),
        in_specs=[pl.BlockSpec((TM, W), lambda b, t: (b * nt + t, 3)),
                  pl.BlockSpec((N_MEM, W), lambda b, t: (b, 0)),
                  pl.BlockSpec((N_MEM, W), lambda b, t: (b, 0))],
        out_specs=pl.BlockSpec((TM, W), lambda b, t: (b * nt + t, 0)),
        out_shape=jax.ShapeDtypeStruct((NP, W), BF16),
        compiler_params=_cp(("arbitrary", "arbitrary")),
        name="xattn_prompt",
    )(z_b, mem_k, mem_v)


XS = 4


XMC = 64


def _xattn_step_kernel(q_ref, k_ref, v_ref, o_ref):
    def part(t):
        for s in range(XS):
            r = t * XS + s
            q4 = jnp.concatenate([q_ref[r:r + 1, hh * DH:(hh + 1) * DH] for hh in range(NH)], axis=0)
            sc = jnp.concatenate(
                [jnp.sum(k_ref[s, m0:m0 + XMC] * q4[None], axis=-1, keepdims=True) for m0 in range(0, N_MEM, XMC)],
                axis=0) * (DH ** -0.5)
            e = jnp.exp(sc - jnp.max(sc, axis=0, keepdims=True))
            p = e / jnp.sum(e, axis=0, keepdims=True)
            acc = jnp.zeros((NH, DH), F32)
            for m0 in range(0, N_MEM, XMC):
                acc = acc + jnp.sum(p[m0:m0 + XMC] * v_ref[s, m0:m0 + XMC], axis=0)
            for hh in range(NH):
                o_ref[r:r + 1, hh * DH:(hh + 1) * DH] = acc[hh:hh + 1, :]

    for t in range(SB // XS):
        pl.when(pl.program_id(1) == t)(functools.partial(part, t))


def _xattn_step(zs, mem_k, mem_v):
    nt = SB // XS
    return pl.pallas_call(
        _xattn_step_kernel,
        grid=(NS // SB, nt),
        in_specs=[pl.BlockSpec((SB, W), lambda i, t: (i, 3)),
                  pl.BlockSpec((XS, N_MEM, NH, DH), lambda i, t: (i * nt + t, 0, 0, 0)),
                  pl.BlockSpec((XS, N_MEM, NH, DH), lambda i, t: (i * nt + t, 0, 0, 0))],
        out_specs=pl.BlockSpec((SB, W), lambda i, t: (i, 0)),
        out_shape=jax.ShapeDtypeStruct((NS, W), F32),
        compiler_params=_cp(("arbitrary", "arbitrary")),
        name="xattn_step",
    )(zs, mem_k, mem_v)


TMX = 512


def _mix_kernel(x_ref, g_ref, yp_ref, ys_ref, mp_ref, ms_ref, ap_ref, as_ref, wglu_ref,
                wg0_ref, wg1_ref, wg2_ref, wb0_ref, wb1_ref, wb2_ref, o_ref, h_scr, s5_scr):
    def tile(rows, y_ref, ml_ref, xa_ref):
        rs = slice(0, rows)

        @pl.when(pl.program_id(1) == 0)
        def _():
            h_scr[rs, :] = _rms(x_ref[rs, :], g_ref[...]).astype(BF16)
            y = _gelu_tanh(y_ref[...])
            s5_scr[rs, :] = (y * _sigmoid(_dot(y.astype(BF16), wglu_ref[...]))).astype(BF16)

        h = h_scr[rs, :]
        merged = (_sigmoid(_dot(h, wg0_ref[...])) * _dot(s5_scr[rs, :], wb0_ref[...])
                  + _sigmoid(_dot(h, wg1_ref[...])) * _dot(ml_ref[...].astype(BF16), wb1_ref[...])
                  + _sigmoid(_dot(h, wg2_ref[...])) * _dot(xa_ref[...].astype(BF16), wb2_ref[...]))
        o_ref[rs, :] = merged.astype(BF16)
        if rows < TM:
            o_ref[rows:, :] = jnp.zeros((TM - rows, TMX), BF16)

    pl.when(pl.program_id(0) < NPT)(functools.partial(tile, TM, yp_ref, mp_ref, ap_ref))
    pl.when(pl.program_id(0) == NPT)(functools.partial(tile, NS, ys_ref, ms_ref, as_ref))


def _mix(x1, g, yp, ys, mp, ms, ap, as_, wglu, wg, wb):
    prow = pl.BlockSpec((TM, W), lambda i, j: (jnp.minimum(i, NPT - 1), 0))
    srow = pl.BlockSpec((NS, W), lambda i, j: (0, 0))
    nx = D // TMX
    wgs = [pl.BlockSpec((D, TMX), functools.partial(lambda i, j, b: (0, b * nx + j), b=b)) for b in range(3)]
    wbs = pl.BlockSpec((W, TMX), lambda i, j: (0, j))
    return pl.pallas_call(
        _mix_kernel,
        grid=(NT, D // TMX),
        in_specs=[pl.BlockSpec((TM, D), lambda i, j: (i, 0)), pl.BlockSpec((1, D), lambda i, j: (0, 0)),
                  prow, srow, prow, srow, prow, srow,
                  pl.BlockSpec((W, W), lambda i, j: (0, 0)),
                  *wgs, wbs, wbs, wbs],
        out_specs=pl.BlockSpec((TM, TMX), lambda i, j: (i, j)),
        out_shape=jax.ShapeDtypeStruct((MROWS, D), BF16),
        scratch_shapes=[pltpu.VMEM((TM, D), BF16), pltpu.VMEM((TM, W), BF16)],
        compiler_params=_cp(("arbitrary", "arbitrary")),
        name="mix",
    )(x1, g, yp, ys, mp, ms, ap, as_, wglu, wg, wg, wg, *wb)


def _outproj_kernel(x_ref, m_ref, w_ref, o_ref):
    def tile(rows):
        rs = slice(0, rows)
        o_ref[rs, :] = x_ref[rs, :] + _dot(m_ref[rs, :], w_ref[...])
        if rows < TM:
            o_ref[rows:, :] = jnp.zeros((TM - rows, D), F32)

    pl.when(pl.program_id(0) < NPT)(functools.partial(tile, TM))
    pl.when(pl.program_id(0) == NPT)(functools.partial(tile, NS))


def _outproj(x1, merged, w_out):
    return pl.pallas_call(
        _outproj_kernel,
        grid=(NT,),
        in_specs=[pl.BlockSpec((TM, D), lambda i: (i, 0)), pl.BlockSpec((TM, D), lambda i: (i, 0)),
                  pl.BlockSpec((D, D), lambda i: (0, 0))],
        out_specs=pl.BlockSpec((TM, D), lambda i: (i, 0)),
        out_shape=jax.ShapeDtypeStruct((MROWS, D), F32),
        compiler_params=_cp(("arbitrary",)),
        name="out_proj",
    )(x1, merged, w_out)


def kernel(x_prompt, x_sample, mem_prompt, cache_mem_k, cache_mem_v, state_s5_re, state_s5_im, state_mlstm_C,
           state_mlstm_n, state_mlstm_m, g_ffn1, w1_gate, w1_up, w1_down, g_mix, w_in, s5_lambda_re,
           s5_lambda_im, s5_log_step, s5_b_re, s5_b_im, s5_c_re, s5_c_im, s5_d, w_s5_glu, b_igate, b_fgate,
           g_mlstm_head, g_mem, w_mem_k, w_mem_v, w_br_s5, w_br_ml, w_br_xa, w_out, g_ffn2, w2_gate, w2_up,
           w2_down, g_final):
    bf = lambda a: a.astype(BF16)

    w_in_t = w_in[0].T
    job = functools.partial
    (x1, w2g, w2u, w2d, w_head, w_bg, wb_s5, wb_ml, wb_xa, w_o, w_glu, w_mk, w_mv) = _ffn(
        (x_prompt.reshape(NP, D), x_sample.reshape(NS, D)), g_ffn1[0].reshape(1, D),
        bf(w1_gate[0]), bf(w1_up[0]), bf(w1_down[0]), TF,
        side_jobs=(job(_cast_job, w2_gate, 16), job(_cast_job, w2_up, 16), job(_cast_job, w2_down, 32),
                   job(_transpose_job, w_in_t, 0, HEAD), job(_transpose_job, w_in_t, MIX0, 3 * D),
                   job(_cast_job, w_br_s5, 16), job(_cast_job, w_br_ml, 16), job(_cast_job, w_br_xa, 16),
                   job(_cast_job, w_out, 16), job(_cast_job, w_s5_glu, 16),
                   job(_cast_job, w_mem_k, 16), job(_cast_job, w_mem_v, 16)))

    w_ift = bf(w_in_t[GATE0:GATE0 + 2 * NH])
    w_br = [wb_s5, wb_ml, wb_xa]
    brow = jnp.pad(jnp.concatenate([b_igate[0], b_fgate[0]]), (0, 128 - 2 * NH)).reshape(1, 128)
    bcol = jnp.concatenate([b_igate[0], b_fgate[0]]).reshape(2 * NH, 1)
    gh = g_mlstm_head[0].reshape(1, W)

    z_f, z_b, z_if, z_t, zs_f, zs_b, zs_if = _in_proj(x1, g_mix[0].reshape(1, D), w_head, w_ift)
    zs = zs_b[:NS].astype(F32)
    per_seq = lambda a: a.reshape(BATCH, SEQ, a.shape[1])

    blk = lambda a: jnp.tile(a.reshape(NGB, WU, P), (1, 1, GB))
    row = lambda a: a.reshape(NGB, 1, WS)
    (a_re, a_im, a16_re, a16_im, bb_re, bb_im, e_op, ft_op, bd_op) = _s5_prep(
        row(s5_lambda_re[0]), row(s5_lambda_im[0]), row(jnp.repeat(s5_log_step[0], P)),
        blk(s5_b_re[0].transpose(0, 2, 1)), blk(s5_b_im[0].transpose(0, 2, 1)),
        blk(s5_c_re[0]), blk(s5_c_im[0]), s5_d[0].reshape(NGB, WU, 1))
    y_s5_p, fin = _s5_chunk(z_f, bd_op, e_op, ft_op, a16_re, a16_im)
    fin = fin.reshape(NGB, BATCH, 2, GB, P).transpose(2, 1, 0, 3, 4).reshape(2, 1, BATCH, G, P)
    y_s5_s, s5_re_s, s5_im_s = _s5_step(
        zs_f, state_s5_re[0].reshape(NS, G * P), state_s5_im[0].reshape(NS, G * P),
        a_re, a_im, bb_re, bb_im, blk(s5_c_re[0]), blk(s5_c_im[0]), s5_d[0].reshape(1, W))

    ml_p, c_p, n_p, m_p = _mlstm_chunk(per_seq(z_f), per_seq(z_b), per_seq(z_if), z_t, brow, bcol, gh)
    ml_s, c_s, n_s, m_s = _mlstm_step(zs, zs_f, zs_if, brow, gh.reshape(NH, DH), state_mlstm_C[0],
                                      state_mlstm_n[0], state_mlstm_m[0])

    mem_k, mem_v = _mem_proj(mem_prompt.reshape(BATCH * N_MEM, D), g_mem[0].reshape(1, D), w_mk, w_mv)
    xa_p = _xattn_prompt(z_b, mem_k, mem_v)
    xa_s = _xattn_step(zs, cache_mem_k[0], cache_mem_v[0])

    merged = _mix(x1, g_mix[0].reshape(1, D), y_s5_p, y_s5_s, ml_p.reshape(NP, W), ml_s, xa_p, xa_s,
                  w_glu, w_bg, w_br)
    x2 = _outproj(x1, merged, w_o)
    y_p, y_s = _ffn((x2,), g_ffn2[0].reshape(1, D), w2g, w2u, w2d, TF, g_final.reshape(1, D))

    return (y_p.reshape(BATCH, SEQ, D), y_s.reshape(NS, 1, D),
            mem_k.reshape(1, BATCH, N_MEM, NH, DH), mem_v.reshape(1, BATCH, N_MEM, NH, DH),
            fin[0], fin[1], c_p[None], n_p[None], m_p[:, :, 0][None],
            s5_re_s.reshape(1, NS, G, P), s5_im_s.reshape(1, NS, G, P), c_s[None], n_s[None], m_s[None])
```

```python
import functools
from typing import Callable, NamedTuple

import jax
import jax.numpy as jnp
from jax import lax
from jax.experimental import pallas as pl
from jax.experimental.pallas import tpu as pltpu

F32 = jnp.float32
BF16 = jnp.bfloat16

D = 2048
BATCH = 4
SEQ = 2048
NS = 128
NP = BATCH * SEQ
TM = 512
NPT = NP // TM
NT = NPT + 1
MROWS = NT * TM
N_MEM = 256
FF = 5504
TF = 512
G = 64
P = 64
HG = 16
TC = 16
GB = 8
NGB = G // GB
WU = GB * HG
WS = GB * P
NCH = SEQ // TC
S5B = 2
W = 1024
NH = 4
DH = 256
CL = 256
NCL = SEQ // CL
SB = 8
EPS = 1e-6
VMEM_LIMIT = 56 * 1024 * 1024


def _cp(sem, vmem=VMEM_LIMIT):
    return pltpu.CompilerParams(dimension_semantics=sem, vmem_limit_bytes=vmem)


def _dot(a, b):
    return jnp.dot(a, b, preferred_element_type=F32)


def _dot_nt(a, b):
    return lax.dot_general(a, b, (((1,), (1,)), ((), ())), preferred_element_type=F32)


def _dot_tn(a, b):
    return lax.dot_general(a, b, (((0,), (0,)), ((), ())), preferred_element_type=F32)


def _hi_lo(x):
    hi = x.astype(BF16)
    lo = (x - hi.astype(F32)).astype(BF16)
    return hi, lo


def _split3(x):
    hi = x.astype(BF16)
    r1 = x - hi.astype(F32)
    mid = r1.astype(BF16)
    lo = (r1 - mid.astype(F32)).astype(BF16)
    return hi, mid, lo


def _dot3(a, b):
    ah, al = _hi_lo(a)
    bh, bl = _hi_lo(b)
    return _dot(ah, bh) + (_dot(ah, bl) + _dot(al, bh))


def _dot3_nt(a, b):
    ah, al = _hi_lo(a)
    bh, bl = _hi_lo(b)
    return _dot_nt(ah, bh) + (_dot_nt(ah, bl) + _dot_nt(al, bh))


def _rms(x, g):
    r = lax.rsqrt(jnp.mean(x * x, axis=-1, keepdims=True) + EPS)
    return (x * r) * g


def _sigmoid(x):
    return 1.0 / (1.0 + jnp.exp(-x))


def _log_sigmoid(x):
    return jnp.minimum(x, 0.0) - jnp.log1p(jnp.exp(-jnp.abs(x)))


def _gelu_tanh(x):
    return x * (0.5 * (1.0 + jnp.tanh(0.7978845608028654 * (x + 0.044715 * (x * x * x)))))


def _ffn_kernel(*refs, two_src, final_norm, side_jobs, tf):
    nj = pl.cdiv(FF, tf)
    tf_last = FF - (nj - 1) * tf
    assert tf_last % 128 == 0
    refs = list(refs)
    if two_src:
        xp_ref, xs_ref = refs[:2]
        refs = refs[2:]
    else:
        xp_ref = xs_ref = refs[0]
        refs = refs[1:]
    g_ref, wg_ref, wu_ref, wd_ref = refs[:4]
    refs = refs[4:]
    n_in = sum(n for n, _, _ in side_jobs)
    n_out = sum(n for _, n, _ in side_jobs)
    if final_norm:
        gf_ref = refs[0]
        refs = refs[1:]
    side_in = refs[:n_in]
    refs = refs[n_in:]
    if final_norm:
        op_ref, os_ref = refs[:2]
        refs = refs[2:]
    else:
        o_ref = refs[0]
        refs = refs[1:]
    side_out = refs[:n_out]
    h_scr, acc_scr = refs[n_out:]
    j = pl.program_id(1)

    def side_work():
        a = b = 0
        for ni, no, fn in side_jobs:
            fn(side_in[a:a + ni], side_out[b:b + no])
            a += ni
            b += no

    def tile(rows, x_ref):
        rs = slice(0, rows)

        def accumulate(width, first=False, last=False):
            side_work()
            h = h_scr[rs, :]
            gt = _dot(h, wg_ref[:, 0:width])
            hid = (gt * _sigmoid(gt)) * _dot(h, wu_ref[:, 0:width])
            part = _dot(hid.astype(BF16), wd_ref[0:width, :])
            total = part if first else acc_scr[rs, :] + part
            if last:
                return total
            acc_scr[rs, :] = total

        @pl.when(j == 0)
        def _():
            h_scr[rs, :] = _rms(x_ref[rs, :], g_ref[...]).astype(BF16)
            accumulate(tf, first=True)

        pl.when(jnp.logical_and(j > 0, j < nj - 1))(functools.partial(accumulate, tf))

        @pl.when(j == nj - 1)
        def _():
            y = x_ref[rs, :] + 0.5 * accumulate(tf_last, last=True)
            if final_norm:
                y = _rms(y, gf_ref[...])
                (op_ref if rows == TM else os_ref)[...] = y
            else:
                o_ref[rs, :] = y
                if rows < TM:
                    o_ref[rows:, :] = jnp.zeros((TM - rows, D), F32)

    pl.when(pl.program_id(0) < NPT)(functools.partial(tile, TM, xp_ref))
    pl.when(pl.program_id(0) == NPT)(functools.partial(tile, NS, xs_ref))


class _SideJob(NamedTuple):
    blocks: int
    args: list
    in_specs: list
    out_shape: list
    out_specs: list
    fn: Callable


def _step_block(nj, blocks, first=0):
    return lambda i, j: first + jnp.minimum(i * nj + j, blocks - 1)


def _cast_job(a, rb, nj):
    rows_a, cols_a = a.shape[-2:]
    blk = _step_block(nj, rows_a // rb)

    def fn(ins, outs):
        outs[0][...] = ins[0][...].astype(BF16)

    return _SideJob(rows_a // rb, [a], [pl.BlockSpec((None, rb, cols_a), lambda i, j: (0, blk(i, j), 0))],
                    [jax.ShapeDtypeStruct((rows_a, cols_a), BF16)], [pl.BlockSpec((rb, cols_a), lambda i, j: (blk(i, j), 0))],
                    fn)


def _transpose_job(wt, row0, cols, nj):
    k = wt.shape[1]
    ksplit = 2
    kb = k // ksplit
    shift = row0 % 128
    assert shift in (0, 8) and cols % 128 == 0
    nblk = cols // 128 * ksplit
    unit = _step_block(nj, nblk)
    first = row0 // 128
    last = pl.cdiv(wt.shape[0], 128) - 1
    specs = [pl.BlockSpec((128, kb), lambda i, j: (first + unit(i, j) // ksplit, unit(i, j) % ksplit))]
    if shift:
        specs.append(pl.BlockSpec(
            (128, kb), lambda i, j: (jnp.minimum(first + unit(i, j) // ksplit + 1, last), unit(i, j) % ksplit)))

    def fn(ins, outs):
        win = ins[0][...]
        if shift:
            win = jnp.concatenate([win[shift:, :], ins[1][0:shift, :]], axis=0)
        outs[0][...] = win.T.astype(BF16)

    return _SideJob(nblk, [wt] * len(specs), specs, [jax.ShapeDtypeStruct((k, cols), BF16)],
                    [pl.BlockSpec((kb, 128), lambda i, j: (unit(i, j) % ksplit, unit(i, j) // ksplit))], fn)


def _ffn(xs, g, wg, wu, wd, tf, g_final=None, side_jobs=()):
    nj = pl.cdiv(FF, tf)
    two_src = len(xs) == 2
    final_norm = g_final is not None
    row = pl.BlockSpec((TM, D), lambda i, j: (i, 0))
    prow = pl.BlockSpec((TM, D), lambda i, j: (jnp.minimum(i, NPT - 1), 0))
    srow = pl.BlockSpec((NS, D), lambda i, j: (0, 0))
    vec = pl.BlockSpec((1, D), lambda i, j: (0, 0))
    in_specs = ([prow, srow] if two_src else [row]) + [
        vec,
        pl.BlockSpec((D, tf), lambda i, j: (0, j)),
        pl.BlockSpec((D, tf), lambda i, j: (0, j)),
        pl.BlockSpec((tf, D), lambda i, j: (j, 0)),
    ]
    args = list(xs) + [g, wg, wu, wd]
    if final_norm:
        in_specs.append(vec)
        args.append(g_final)
        out_shape = [jax.ShapeDtypeStruct((NP, D), F32), jax.ShapeDtypeStruct((NS, D), F32)]
        out_specs = [prow, srow]
    else:
        out_shape = [jax.ShapeDtypeStruct((MROWS, D), F32)]
        out_specs = [row]
    jobs = []
    for make_job in side_jobs:
        job = make_job(nj)
        assert job.blocks <= NPT * nj, "side jobs must fit under the full-size row tiles"
        in_specs += job.in_specs
        args += job.args
        out_specs += job.out_specs
        out_shape += job.out_shape
        jobs.append((len(job.args), len(job.out_shape), job.fn))
    return pl.pallas_call(
        functools.partial(_ffn_kernel, two_src=two_src, final_norm=final_norm, side_jobs=tuple(jobs), tf=tf),
        grid=(NT, nj),
        in_specs=in_specs,
        out_specs=out_specs,
        out_shape=out_shape,
        scratch_shapes=[pltpu.VMEM((TM, D), BF16), pltpu.VMEM((TM, D), F32)],
        compiler_params=_cp(("arbitrary", "arbitrary")),
        name="ffn_final" if final_norm else "ffn",
    )(*args)


NF32 = 2
TMI = 256
NPI = NP // TMI
GATE0 = 5 * W
HEAD = GATE0 + W + 128
MIX0 = GATE0 + 2 * NH + W


def _in_proj_kernel(x_ref, g_ref, w_ref, wift_ref,
                    ofp_ref, obp_ref, zifp_ref, ztp_ref, ofs_ref, obs_ref, zifs_ref):
    def emit(of_ref, ob_ref, zif_ref, zt_ref):
        h = _rms(x_ref[...], g_ref[...]).astype(BF16)
        if zt_ref is not None:
            zt_ref[...] = _dot_nt(wift_ref[...], h)
        of_ref[:, 0:W] = _dot(h, w_ref[:, 0:W])
        for j in range(1, 4):
            ob_ref[:, (j - 1) * W:j * W] = _dot(h, w_ref[:, j * W:(j + 1) * W]).astype(BF16)
        of_ref[:, W:2 * W] = _dot(h, w_ref[:, 4 * W:5 * W])
        tail = _dot(h, w_ref[:, 5 * W:HEAD])
        zif_ref[...] = tail[:, 0:128]
        ob_ref[:, 3 * W:4 * W] = tail[:, 2 * NH:2 * NH + W].astype(BF16)

    pl.when(pl.program_id(0) < NPI)(functools.partial(emit, ofp_ref, obp_ref, zifp_ref, ztp_ref))
    pl.when(pl.program_id(0) == NPI)(functools.partial(emit, ofs_ref, obs_ref, zifs_ref, None))


def _in_proj(x, g, w, w_ift):
    once = pl.Buffered(1)
    pblk = lambda cols: pl.BlockSpec((TMI, cols), lambda i: (jnp.minimum(i, NPI - 1), 0))
    sblk = lambda cols: pl.BlockSpec((TMI, cols), lambda i: (0, 0))
    widths = (NF32 * W, 4 * W, 128)
    dtypes = (F32, BF16, F32)
    return pl.pallas_call(
        _in_proj_kernel,
        grid=(NPI + 1,),
        in_specs=[
            pl.BlockSpec((TMI, D), lambda i: (i, 0)),
            pl.BlockSpec((1, D), lambda i: (0, 0)),
            pl.BlockSpec((D, HEAD), lambda i: (0, 0), pipeline_mode=once),
            pl.BlockSpec((8, D), lambda i: (0, 0), pipeline_mode=once),
        ],
        out_specs=[pblk(c) for c in widths]
        + [pl.BlockSpec((8, TMI), lambda i: (0, jnp.minimum(i, NPI - 1)))]
        + [sblk(c) for c in widths],
        out_shape=[jax.ShapeDtypeStruct((NP, c), t) for c, t in zip(widths, dtypes)]
        + [jax.ShapeDtypeStruct((8, NP), F32)]
        + [jax.ShapeDtypeStruct((TMI, c), t) for c, t in zip(widths, dtypes)],
        compiler_params=_cp(("arbitrary",)),
        name="in_proj",
    )(x, g, w, w_ift)


def _mem_proj_kernel(x_ref, g_ref, wk_ref, wv_ref, k_ref, v_ref):
    h = _rms(x_ref[...], g_ref[...]).astype(BF16)
    k_ref[...] = _dot(h, wk_ref[...])
    v_ref[...] = _dot(h, wv_ref[...])


def _mem_proj(x, g, wk, wv):
    m = x.shape[0]
    wspec = pl.BlockSpec((D, W), lambda i: (0, 0))
    ospec = pl.BlockSpec((TM, W), lambda i: (i, 0))
    return pl.pallas_call(
        _mem_proj_kernel,
        grid=(m // TM,),
        in_specs=[pl.BlockSpec((TM, D), lambda i: (i, 0)), pl.BlockSpec((1, D), lambda i: (0, 0)), wspec, wspec],
        out_specs=[ospec, ospec],
        out_shape=[jax.ShapeDtypeStruct((m, W), F32)] * 2,
        compiler_params=_cp(("arbitrary",)),
        name="mem_proj",
    )(x, g, wk, wv)


def _s5_prep_kernel(lr_ref, li_ref, ls_ref, btr_ref, bti_ref, cr_ref, ci_ref, d_ref,
                    ar_ref, ai_ref, a16r_ref, a16i_ref, bbr_ref, bbi_ref, e_ref, ft_ref, bd_ref):
    lr = lr_ref[0]
    li = li_ref[0]
    dt = jnp.exp(ls_ref[0])

    def power(k):
        mag = jnp.exp(lr * dt * float(k))
        ang = li * dt * float(k)
        return mag * jnp.cos(ang), mag * jnp.sin(ang)

    pw = [power(k) for k in range(TC + 1)]
    ar, ai = pw[1]
    den = lr * lr + li * li
    nr = ar - 1.0
    z_re = (nr * lr + ai * li) / den
    z_im = (ai * lr - nr * li) / den
    mask = (lax.broadcasted_iota(jnp.int32, (WU, WS), 0) // HG
            == lax.broadcasted_iota(jnp.int32, (WU, WS), 1) // P)
    btr = btr_ref[0]
    bti = bti_ref[0]
    bbr = jnp.where(mask, z_re * btr - z_im * bti, 0.0)
    bbi = jnp.where(mask, z_re * bti + z_im * btr, 0.0)
    cr = jnp.where(mask, cr_ref[0], 0.0)
    ci = jnp.where(mask, ci_ref[0], 0.0)

    ar_ref[0] = ar
    ai_ref[0] = ai
    a16r_ref[0] = pw[TC][0]
    a16i_ref[0] = pw[TC][1]
    bbr_ref[0] = bbr
    bbi_ref[0] = bbi

    def cmul(xr, xi, k):
        pr, pi = pw[k]
        return xr * pr - xi * pi, xr * pi + xi * pr

    diag = (lax.broadcasted_iota(jnp.int32, (WU, WU), 0) == lax.broadcasted_iota(jnp.int32, (WU, WU), 1))
    rr, ri = cr, ci
    bbr_b = bbr.astype(BF16)
    bbi_b = bbi.astype(BF16)
    for s in range(TC):
        er, ei = cmul(bbr, bbi, TC - 1 - s)
        e_ref[0, s * WU:(s + 1) * WU, 0:WS] = er.astype(BF16)
        e_ref[0, s * WU:(s + 1) * WU, WS:2 * WS] = ei.astype(BF16)
        fr, fi = cmul(cr, ci, s + 1)
        ft_ref[0, s * WU:(s + 1) * WU, 0:WS] = fr.astype(BF16)
        ft_ref[0, s * WU:(s + 1) * WU, WS:2 * WS] = (-fi).astype(BF16)
        kern = _dot_nt(bbr_b, rr.astype(BF16)) - _dot_nt(bbi_b, ri.astype(BF16))
        if s == 0:
            kern = kern + jnp.where(diag, d_ref[0], 0.0)
        bd_ref[0, s] = kern
        rr, ri = fr, fi


def _s5_prep(lam_re, lam_im, log_step, bt_re, bt_im, c_re, c_im, d):
    def spec(*shape):
        nd = len(shape)
        return pl.BlockSpec((1,) + shape, lambda i: (i,) + (0,) * nd)

    def sds(*shape, dtype=F32):
        return jax.ShapeDtypeStruct((NGB,) + shape, dtype)

    return pl.pallas_call(
        _s5_prep_kernel,
        grid=(NGB,),
        in_specs=[spec(1, WS)] * 3 + [spec(WU, WS)] * 4 + [spec(WU, 1)],
        out_specs=[spec(1, WS)] * 4 + [spec(WU, WS)] * 2 + [spec(TC * WU, 2 * WS)] * 2 + [spec(TC, WU, WU)],
        out_shape=[sds(1, WS)] * 4 + [sds(WU, WS)] * 2 + [sds(TC * WU, 2 * WS, dtype=BF16)] * 2
        + [sds(TC, WU, WU)],
        compiler_params=_cp(("arbitrary",)),
        name="s5_prep",
    )(lam_re, lam_im, log_step, bt_re, bt_im, c_re, c_im, d)


def _s5_chunk_kernel(u_ref, bd_ref, e_ref, ft_ref, ar_ref, ai_ref, y_ref, fin_ref,
                     w_scr, lhs_scr, s_scr, xs_scr):
    rows = S5B * NCH

    @pl.when(pl.program_id(1) == 0)
    def _():
        w_scr[...] = jnp.zeros_like(w_scr)
        bd = [bd_ref[0, k].astype(BF16) for k in range(TC)]
        for s in range(TC):
            for t in range(s, TC):
                w_scr[s * WU:(s + 1) * WU, t * WU:(t + 1) * WU] = bd[t - s]

    for s in range(TC):
        lhs_scr[:, s * WU:(s + 1) * WU] = u_ref[pl.ds(s, rows, stride=TC), :].astype(BF16)
    lhs = lhs_scr[...]
    s_loc = _dot(lhs, e_ref[0])
    nl = WS // 128
    for k in range(2 * nl):
        s_scr[k] = s_loc[:, k * 128:(k + 1) * 128]
    ar = [ar_ref[0, :, k * 128:(k + 1) * 128] for k in range(nl)]
    ai = [ai_ref[0, :, k * 128:(k + 1) * 128] for k in range(nl)]
    xr = [jnp.zeros((S5B, 128), F32)] * nl
    xi = [jnp.zeros((S5B, 128), F32)] * nl
    for c in range(NCH):
        chunk_rows = pl.ds(c, S5B, stride=NCH)
        for k in range(nl):
            xs_scr[k, chunk_rows, :] = xr[k]
            xs_scr[nl + k, chunk_rows, :] = xi[k]
            sr = s_scr[k, chunk_rows, :]
            si = s_scr[nl + k, chunk_rows, :]
            xr[k], xi[k] = ar[k] * xr[k] - ai[k] * xi[k] + sr, ar[k] * xi[k] + ai[k] * xr[k] + si
    for k in range(nl):
        fin_ref[0, 0, :, k * 128:(k + 1) * 128] = xr[k]
        fin_ref[0, 0, :, WS + k * 128:WS + (k + 1) * 128] = xi[k]
    xs = jnp.concatenate([xs_scr[k] for k in range(2 * nl)], axis=1)
    y_carry = _dot_nt(xs.astype(BF16), ft_ref[0])
    for t2 in range(0, TC, 2):
        cols = slice(t2 * WU, (t2 + 2) * WU)
        y = _dot(lhs[:, 0:(t2 + 2) * WU], w_scr[0:(t2 + 2) * WU, cols]) + y_carry[:, cols]
        for t in (t2, t2 + 1):
            y_ref[pl.ds(t, rows, stride=TC), :] = y[:, (t - t2) * WU:(t - t2 + 1) * WU]


def _s5_chunk(z_f, bd, e, ft, a16r, a16i):
    rows = S5B * NCH
    nh = BATCH // S5B
    return pl.pallas_call(
        _s5_chunk_kernel,
        grid=(NGB, nh),
        in_specs=[
            pl.BlockSpec((S5B * SEQ, WU), lambda j, b: (b, j)),
            pl.BlockSpec((1, TC, WU, WU), lambda j, b: (j, 0, 0, 0)),
            pl.BlockSpec((1, TC * WU, 2 * WS), lambda j, b: (j, 0, 0)),
            pl.BlockSpec((1, TC * WU, 2 * WS), lambda j, b: (j, 0, 0)),
            pl.BlockSpec((1, 1, WS), lambda j, b: (j, 0, 0)),
            pl.BlockSpec((1, 1, WS), lambda j, b: (j, 0, 0)),
        ],
        out_specs=[
            pl.BlockSpec((S5B * SEQ, WU), lambda j, b: (b, j)),
            pl.BlockSpec((1, 1, S5B, 2 * WS), lambda j, b: (j, b, 0, 0)),
        ],
        out_shape=[jax.ShapeDtypeStruct((NP, W), F32), jax.ShapeDtypeStruct((NGB, nh, S5B, 2 * WS), F32)],
        scratch_shapes=[pltpu.VMEM((TC * WU, TC * WU), BF16), pltpu.VMEM((rows, TC * WU), BF16),
                        pltpu.VMEM((2 * WS // 128, rows, 128), F32), pltpu.VMEM((2 * WS // 128, rows, 128), F32)],
        compiler_params=_cp(("arbitrary", "arbitrary")),
        name="s5_chunk",
    )(z_f, bd, e, ft, a16r, a16i)


def _s5_step_kernel(u_ref, sr_ref, si_ref, ar_ref, ai_ref, bbr_ref, bbi_ref, cr_ref, ci_ref, d_ref,
                    y_ref, xr_ref, xi_ref):
    mask = (lax.broadcasted_iota(jnp.int32, (WU, WS), 0) // HG
            == lax.broadcasted_iota(jnp.int32, (WU, WS), 1) // P)
    for j in range(NGB):
        ul = slice(j * WU, (j + 1) * WU)
        sl = slice(j * WS, (j + 1) * WS)
        us = u_ref[:, ul]
        ar = ar_ref[j]
        ai = ai_ref[j]
        s_re = sr_ref[:, sl]
        s_im = si_ref[:, sl]
        x_re = ar * s_re - ai * s_im + _dot3(us, bbr_ref[j])
        x_im = ar * s_im + ai * s_re + _dot3(us, bbi_ref[j])
        xr_ref[:, sl] = x_re
        xi_ref[:, sl] = x_im
        cre = jnp.where(mask, cr_ref[j], 0.0)
        cim = jnp.where(mask, ci_ref[j], 0.0)
        y_ref[:, ul] = _dot3_nt(x_re, cre) - _dot3_nt(x_im, cim) + d_ref[:, ul] * us


def _s5_step(z_f, s_re, s_im, a_re, a_im, bb_re, bb_im, c_re, c_im, d_row):
    full = lambda *shape: pl.BlockSpec(shape, lambda i: (0,) * len(shape))
    return pl.pallas_call(
        _s5_step_kernel,
        grid=(1,),
        in_specs=[pl.BlockSpec((NS, W), lambda i: (0, 0)),
                  full(NS, G * P), full(NS, G * P), full(NGB, 1, WS), full(NGB, 1, WS),
                  full(NGB, WU, WS), full(NGB, WU, WS), full(NGB, WU, WS), full(NGB, WU, WS),
                  full(1, W)],
        out_specs=[full(NS, W), full(NS, G * P), full(NS, G * P)],
        out_shape=[jax.ShapeDtypeStruct((NS, W), F32), jax.ShapeDtypeStruct((NS, G * P), F32),
                   jax.ShapeDtypeStruct((NS, G * P), F32)],
        compiler_params=_cp(("arbitrary",)),
        name="s5_step",
    )(z_f, s_re, s_im, a_re, a_im, bb_re, bb_im, c_re, c_im, d_row)


def _mlstm_chunk_kernel(q_ref, k_ref, v_ref, o_ref, zif_ref, zt0_ref, zt1_ref, zt2_ref, zt3_ref,
                        brow_ref, bcol_ref, gh_ref, h_ref, c_ref, n_ref, m_ref):
    @pl.when(pl.program_id(0) == 0)
    def _():
        c_ref[...] = jnp.zeros_like(c_ref)
        n_ref[...] = jnp.zeros_like(n_ref)
        m_ref[...] = jnp.zeros_like(m_ref)

    rr = lax.broadcasted_iota(jnp.int32, (CL, CL), 0)
    cc = lax.broadcasted_iota(jnp.int32, (CL, CL), 1)
    causal = cc <= rr
    tril = jnp.where(causal, 1.0, 0.0).astype(BF16)
    triu = jnp.where(rr <= cc, 1.0, 0.0).astype(BF16)

    for b, zt_ref in enumerate((zt0_ref, zt1_ref, zt2_ref, zt3_ref)):
        zi = zif_ref[b] + brow_ref[...]
        zt = zt_ref[...] + bcol_ref[...]
        lfc = _split3(_log_sigmoid(zi))
        bcum_col = _dot(tril, lfc[0]) + (_dot(tril, lfc[1]) + _dot(tril, lfc[2]))
        lfr = _split3(_log_sigmoid(zt))
        bcum_row = _dot(lfr[0], triu) + (_dot(lfr[1], triu) + _dot(lfr[2], triu))

        for hh in range(NH):
            hs = slice(hh * DH, (hh + 1) * DH)
            bc = bcum_col[:, NH + hh:NH + hh + 1]
            ic = zi[:, hh:hh + 1]
            br = bcum_row[NH + hh:NH + hh + 1, :]
            ir = zt[hh:hh + 1, :]
            m_prev = m_ref[b, hh:hh + 1, 0:1]
            g_inter = bc + m_prev
            dlog = jnp.where(causal, (bc - br) + ir, -jnp.inf)
            m_t = jnp.maximum(g_inter, jnp.max(dlog, axis=-1, keepdims=True))
            w_inter = jnp.exp(g_inter - m_t)
            w_intra = jnp.exp(dlog - m_t)
            qb = q_ref[b, :, hs]
            kb = k_ref[b, :, hs]
            vb = v_ref[b, :, hs]
            qf = qb.astype(F32)
            kf = kb.astype(F32)
            s = _dot_nt(qb, kb) * (w_intra * (DH ** -0.5))
            c_prev = c_ref[b, hh]
            n_prev = n_ref[b, hh:hh + 1, :]
            num = _dot(s.astype(BF16), vb) + w_inter * _dot(qb, c_prev.astype(BF16))
            nq = jnp.sum(s, axis=-1, keepdims=True) + w_inter * jnp.sum(qf * n_prev, axis=-1, keepdims=True)
            h = num / jnp.maximum(jnp.abs(nq), jnp.exp(-m_t))
            m_last = m_t[CL - 1:CL, :]
            w_last = jnp.exp((bc[CL - 1:CL, :] - bc) + ic - m_last) * (DH ** -0.5)
            wi_last = w_inter[CL - 1:CL, :]
            kw = kf * w_last
            c_ref[b, hh] = wi_last * c_prev + _dot_tn(kw.astype(BF16), vb)
            n_ref[b, hh:hh + 1, :] = wi_last * n_prev + jnp.sum(kw, axis=0, keepdims=True)
            m_ref[b, hh:hh + 1, :] = jnp.broadcast_to(m_last, (1, 128))
            hn = _rms(h, gh_ref[:, hs])
            h_ref[b, :, hs] = (hn * _sigmoid(o_ref[b, :, hs])).astype(BF16)


def _mlstm_chunk(z_f, z_b, z_if, z_t, brow, bcol, gh):
    blk = lambda col: pl.BlockSpec((BATCH, CL, W), lambda c: (0, c, col))
    const = lambda *shape: pl.BlockSpec(shape, lambda c: (0,) * len(shape))
    zt_specs = [pl.BlockSpec((8, CL), functools.partial(lambda c, b: (0, b * NCL + c), b=b)) for b in range(BATCH)]
    return pl.pallas_call(
        _mlstm_chunk_kernel,
        grid=(NCL,),
        in_specs=[blk(0), blk(1), blk(2), blk(1), pl.BlockSpec((BATCH, CL, 128), lambda c: (0, c, 0))]
        + zt_specs + [const(1, 128), const(8, 1), const(1, W)],
        out_specs=[pl.BlockSpec((BATCH, CL, W), lambda c: (0, c, 0)),
                   const(BATCH, NH, DH, DH), const(BATCH, NH, DH), const(BATCH, NH, 128)],
        out_shape=[jax.ShapeDtypeStruct((BATCH, SEQ, W), BF16),
                   jax.ShapeDtypeStruct((BATCH, NH, DH, DH), F32),
                   jax.ShapeDtypeStruct((BATCH, NH, DH), F32),
                   jax.ShapeDtypeStruct((BATCH, NH, 128), F32)],
        compiler_params=_cp(("arbitrary",)),
        name="mlstm_chunk",
    )(z_b, z_b, z_b, z_f, z_if, z_t, z_t, z_t, z_t, brow, bcol, gh)


MSR = 64


def _mlstm_step_kernel(q_ref, k_ref, v_ref, o_ref, zif_ref, brow_ref, gh_ref, c0_ref, n0_ref, m0_ref,
                       h_ref, c_ref, n_ref, m_ref):
    zi = zif_ref[...] + brow_ref[...]
    ig = zi[:, 0:NH]
    g_inter = _log_sigmoid(zi[:, NH:2 * NH]) + m0_ref[...]
    m_t = jnp.maximum(g_inter, ig)
    w_inter = jnp.exp(g_inter - m_t)
    w_intra = jnp.exp(ig - m_t) * (DH ** -0.5)
    floor = jnp.exp(-m_t)
    m_ref[...] = m_t

    def heads_on_rows(x):
        return jnp.pad(x, ((0, 128 - SB), (0, 128 - NH))).T

    w_inter_t = heads_on_rows(w_inter)
    w_intra_t = heads_on_rows(w_intra)
    floor_t = heads_on_rows(floor)
    pad = jnp.zeros((128 - SB, DH), F32)
    q_t = [jnp.concatenate([q_ref[:, hh * DH:(hh + 1) * DH], pad], axis=0).T for hh in range(NH)]
    k_t = [jnp.concatenate([k_ref[:, hh * DH:(hh + 1) * DH], pad], axis=0).T for hh in range(NH)]

    def per_head_rows(ref, s):
        return jnp.concatenate([ref[s:s + 1, hh * DH:(hh + 1) * DH] for hh in range(NH)], axis=0)

    for s in range(SB):
        q4 = per_head_rows(q_ref, s)
        k4 = per_head_rows(k_ref, s)
        v4 = per_head_rows(v_ref, s)
        qc_rows = []
        for hh in range(NH):
            wi = w_inter[s:s + 1, hh:hh + 1]
            vw = v4[hh:hh + 1, :] * w_intra[s:s + 1, hh:hh + 1]
            acc = jnp.zeros((MSR, DH), F32)
            for r0 in range(0, DH, MSR):
                c_blk = c0_ref[s, hh, r0:r0 + MSR, :]
                acc = acc + q_t[hh][r0:r0 + MSR, s:s + 1] * c_blk
                c_ref[s, hh, r0:r0 + MSR, :] = wi * c_blk + k_t[hh][r0:r0 + MSR, s:s + 1] * vw
            qc_rows.append(jnp.sum(acc, axis=0, keepdims=True))
        q_c = jnp.concatenate(qc_rows, axis=0)
        wi_c = w_inter_t[0:NH, s:s + 1]
        wa_c = w_intra_t[0:NH, s:s + 1]
        n_prev = n0_ref[s]
        sv = jnp.sum(q4 * k4, axis=-1, keepdims=True) * wa_c
        num = sv * v4 + wi_c * q_c
        nq = sv + wi_c * jnp.sum(q4 * n_prev, axis=-1, keepdims=True)
        h = num / jnp.maximum(jnp.abs(nq), floor_t[0:NH, s:s + 1])
        n_ref[s] = wi_c * n_prev + wa_c * k4
        out = _rms(h, gh_ref[...]) * _sigmoid(per_head_rows(o_ref, s))
        for hh in range(NH):
            h_ref[s:s + 1, hh * DH:(hh + 1) * DH] = out[hh:hh + 1, :]


def _mlstm_step(zs, z_f, z_if, brow, gh, c0, n0, m0):
    blk = lambda col: pl.BlockSpec((SB, W), lambda i: (i, col))
    return pl.pallas_call(
        _mlstm_step_kernel,
        grid=(NS // SB,),
        in_specs=[blk(0), blk(1), blk(2), blk(1),
                  pl.BlockSpec((SB, 128), lambda i: (i, 0)),
                  pl.BlockSpec((1, 128), lambda i: (0, 0)),
                  pl.BlockSpec((NH, DH), lambda i: (0, 0)),
                  pl.BlockSpec((SB, NH, DH, DH), lambda i: (i, 0, 0, 0)),
                  pl.BlockSpec((SB, NH, DH), lambda i: (i, 0, 0)),
                  pl.BlockSpec((SB, NH), lambda i: (i, 0))],
        out_specs=[pl.BlockSpec((SB, W), lambda i: (i, 0)),
                   pl.BlockSpec((SB, NH, DH, DH), lambda i: (i, 0, 0, 0)),
                   pl.BlockSpec((SB, NH, DH), lambda i: (i, 0, 0)),
                   pl.BlockSpec((SB, NH), lambda i: (i, 0))],
        out_shape=[jax.ShapeDtypeStruct((NS, W), F32),
                   jax.ShapeDtypeStruct((NS, NH, DH, DH), F32),
                   jax.ShapeDtypeStruct((NS, NH, DH), F32),
                   jax.ShapeDtypeStruct((NS, NH), F32)],
        compiler_params=_cp(("arbitrary",)),
        name="mlstm_step",
    )(zs, zs, zs, z_f, z_if, brow, gh, c0, n0, m0)


def _softmax_rows(s):
    e = jnp.exp(s - jnp.max(s, axis=-1, keepdims=True))
    return e / jnp.sum(e, axis=-1, keepdims=True)


def _xattn_prompt_kernel(q_ref, k_ref, v_ref, o_ref):
    for hh in range(NH):
        hs = slice(hh * DH, (hh + 1) * DH)
        s = _dot_nt(q_ref[:, hs], k_ref[:, hs].astype(BF16)) * (DH ** -0.5)
        p = _softmax_rows(s)
        o_ref[:, hs] = _dot(p.astype(BF16), v_ref[:, hs].astype(BF16)).astype(BF16)


def _xattn_prompt(z_b, mem_k, mem_v):
    nt = SEQ // TM
    return pl.pallas_call(
        _xattn_prompt_kernel,
        grid=(BATCH, nt),
        in_specs=[pl.BlockSpec((TM, W), lambda b, t: (b * nt + t, 3)),
                  pl.BlockSpec((N_MEM, W), lambda b, t: (b, 0)),
                  pl.BlockSpec((N_MEM, W), lambda b, t: (b, 0))],
        out_specs=pl.BlockSpec((TM, W), lambda b, t: (b * nt + t, 0)),
        out_shape=jax.ShapeDtypeStruct((NP, W), BF16),
        compiler_params=_cp(("arbitrary", "arbitrary")),
        name="xattn_prompt",
    )(z_b, mem_k, mem_v)


XS = 4


XMC = 64


def _xattn_step_kernel(q_ref, k_ref, v_ref, o_ref):
    def part(t):
        for s in range(XS):
            r = t * XS + s
            q4 = jnp.concatenate([q_ref[r:r + 1, hh * DH:(hh + 1) * DH] for hh in range(NH)], axis=0)
            sc = jnp.concatenate(
                [jnp.sum(k_ref[s, m0:m0 + XMC] * q4[None], axis=-1, keepdims=True) for m0 in range(0, N_MEM, XMC)],
                axis=0) * (DH ** -0.5)
            e = jnp.exp(sc - jnp.max(sc, axis=0, keepdims=True))
            p = e / jnp.sum(e, axis=0, keepdims=True)
            acc = jnp.zeros((NH, DH), F32)
            for m0 in range(0, N_MEM, XMC):
                acc = acc + jnp.sum(p[m0:m0 + XMC] * v_ref[s, m0:m0 + XMC], axis=0)
            for hh in range(NH):
                o_ref[r:r + 1, hh * DH:(hh + 1) * DH] = acc[hh:hh + 1, :]

    for t in range(SB // XS):
        pl.when(pl.program_id(1) == t)(functools.partial(part, t))


def _xattn_step(zs, mem_k, mem_v):
    nt = SB // XS
    return pl.pallas_call(
        _xattn_step_kernel,
        grid=(NS // SB, nt),
        in_specs=[pl.BlockSpec((SB, W), lambda i, t: (i, 3)),
                  pl.BlockSpec((XS, N_MEM, NH, DH), lambda i, t: (i * nt + t, 0, 0, 0)),
                  pl.BlockSpec((XS, N_MEM, NH, DH), lambda i, t: (i * nt + t, 0, 0, 0))],
        out_specs=pl.BlockSpec((SB, W), lambda i, t: (i, 0)),
        out_shape=jax.ShapeDtypeStruct((NS, W), F32),
        compiler_params=_cp(("arbitrary", "arbitrary")),
        name="xattn_step",
    )(zs, mem_k, mem_v)


TMX = 512


def _mix_kernel(x_ref, g_ref, yp_ref, ys_ref, mp_ref, ms_ref, ap_ref, as_ref, wglu_ref,
                wg0_ref, wg1_ref, wg2_ref, wb0_ref, wb1_ref, wb2_ref, o_ref, h_scr, s5_scr):
    def tile(rows, y_ref, ml_ref, xa_ref):
        rs = slice(0, rows)

        @pl.when(pl.program_id(1) == 0)
        def _():
            h_scr[rs, :] = _rms(x_ref[rs, :], g_ref[...]).astype(BF16)
            y = _gelu_tanh(y_ref[...])
            s5_scr[rs, :] = (y * _sigmoid(_dot(y.astype(BF16), wglu_ref[...]))).astype(BF16)

        h = h_scr[rs, :]
        merged = (_sigmoid(_dot(h, wg0_ref[...])) * _dot(s5_scr[rs, :], wb0_ref[...])
                  + _sigmoid(_dot(h, wg1_ref[...])) * _dot(ml_ref[...].astype(BF16), wb1_ref[...])
                  + _sigmoid(_dot(h, wg2_ref[...])) * _dot(xa_ref[...].astype(BF16), wb2_ref[...]))
        o_ref[rs, :] = merged.astype(BF16)
        if rows < TM:
            o_ref[rows:, :] = jnp.zeros((TM - rows, TMX), BF16)

    pl.when(pl.program_id(0) < NPT)(functools.partial(tile, TM, yp_ref, mp_ref, ap_ref))
    pl.when(pl.program_id(0) == NPT)(functools.partial(tile, NS, ys_ref, ms_ref, as_ref))


def _mix(x1, g, yp, ys, mp, ms, ap, as_, wglu, wg, wb):
    prow = pl.BlockSpec((TM, W), lambda i, j: (jnp.minimum(i, NPT - 1), 0))
    srow = pl.BlockSpec((NS, W), lambda i, j: (0, 0))
    nx = D // TMX
    wgs = [pl.BlockSpec((D, TMX), functools.partial(lambda i, j, b: (0, b * nx + j), b=b)) for b in range(3)]
    wbs = pl.BlockSpec((W, TMX), lambda i, j: (0, j))
    return pl.pallas_call(
        _mix_kernel,
        grid=(NT, D // TMX),
        in_specs=[pl.BlockSpec((TM, D), lambda i, j: (i, 0)), pl.BlockSpec((1, D), lambda i, j: (0, 0)),
                  prow, srow, prow, srow, prow, srow,
                  pl.BlockSpec((W, W), lambda i, j: (0, 0)),
                  *wgs, wbs, wbs, wbs],
        out_specs=pl.BlockSpec((TM, TMX), lambda i, j: (i, j)),
        out_shape=jax.ShapeDtypeStruct((MROWS, D), BF16),
        scratch_shapes=[pltpu.VMEM((TM, D), BF16), pltpu.VMEM((TM, W), BF16)],
        compiler_params=_cp(("arbitrary", "arbitrary")),
        name="mix",
    )(x1, g, yp, ys, mp, ms, ap, as_, wglu, wg, wg, wg, *wb)


def _outproj_kernel(x_ref, m_ref, w_ref, o_ref):
    def tile(rows):
        rs = slice(0, rows)
        o_ref[rs, :] = x_ref[rs, :] + _dot(m_ref[rs, :], w_ref[...])
        if rows < TM:
            o_ref[rows:, :] = jnp.zeros((TM - rows, D), F32)

    pl.when(pl.program_id(0) < NPT)(functools.partial(tile, TM))
    pl.when(pl.program_id(0) == NPT)(functools.partial(tile, NS))


def _outproj(x1, merged, w_out):
    return pl.pallas_call(
        _outproj_kernel,
        grid=(NT,),
        in_specs=[pl.BlockSpec((TM, D), lambda i: (i, 0)), pl.BlockSpec((TM, D), lambda i: (i, 0)),
                  pl.BlockSpec((D, D), lambda i: (0, 0))],
        out_specs=pl.BlockSpec((TM, D), lambda i: (i, 0)),
        out_shape=jax.ShapeDtypeStruct((MROWS, D), F32),
        compiler_params=_cp(("arbitrary",)),
        name="out_proj",
    )(x1, merged, w_out)


def kernel(x_prompt, x_sample, mem_prompt, cache_mem_k, cache_mem_v, state_s5_re, state_s5_im, state_mlstm_C,
           state_mlstm_n, state_mlstm_m, g_ffn1, w1_gate, w1_up, w1_down, g_mix, w_in, s5_lambda_re,
           s5_lambda_im, s5_log_step, s5_b_re, s5_b_im, s5_c_re, s5_c_im, s5_d, w_s5_glu, b_igate, b_fgate,
           g_mlstm_head, g_mem, w_mem_k, w_mem_v, w_br_s5, w_br_ml, w_br_xa, w_out, g_ffn2, w2_gate, w2_up,
           w2_down, g_final):
    bf = lambda a: a.astype(BF16)

    w_in_t = w_in[0].T
    job = functools.partial
    (x1, w2g, w2u, w2d, w_head, w_bg, wb_s5, wb_ml, wb_xa, w_o, w_glu, w_mk, w_mv) = _ffn(
        (x_prompt.reshape(NP, D), x_sample.reshape(NS, D)), g_ffn1[0].reshape(1, D),
        bf(w1_gate[0]), bf(w1_up[0]), bf(w1_down[0]), TF,
        side_jobs=(job(_cast_job, w2_gate, 16), job(_cast_job, w2_up, 16), job(_cast_job, w2_down, 32),
                   job(_transpose_job, w_in_t, 0, HEAD), job(_transpose_job, w_in_t, MIX0, 3 * D),
                   job(_cast_job, w_br_s5, 16), job(_cast_job, w_br_ml, 16), job(_cast_job, w_br_xa, 16),
                   job(_cast_job, w_out, 16), job(_cast_job, w_s5_glu, 16),
                   job(_cast_job, w_mem_k, 16), job(_cast_job, w_mem_v, 16)))

    w_ift = bf(w_in_t[GATE0:GATE0 + 2 * NH])
    w_br = [wb_s5, wb_ml, wb_xa]
    brow = jnp.pad(jnp.concatenate([b_igate[0], b_fgate[0]]), (0, 128 - 2 * NH)).reshape(1, 128)
    bcol = jnp.concatenate([b_igate[0], b_fgate[0]]).reshape(2 * NH, 1)
    gh = g_mlstm_head[0].reshape(1, W)

    z_f, z_b, z_if, z_t, zs_f, zs_b, zs_if = _in_proj(x1, g_mix[0].reshape(1, D), w_head, w_ift)
    zs = zs_b[:NS].astype(F32)
    per_seq = lambda a: a.reshape(BATCH, SEQ, a.shape[1])

    blk = lambda a: jnp.tile(a.reshape(NGB, WU, P), (1, 1, GB))
    row = lambda a: a.reshape(NGB, 1, WS)
    (a_re, a_im, a16_re, a16_im, bb_re, bb_im, e_op, ft_op, bd_op) = _s5_prep(
        row(s5_lambda_re[0]), row(s5_lambda_im[0]), row(jnp.repeat(s5_log_step[0], P)),
        blk(s5_b_re[0].transpose(0, 2, 1)), blk(s5_b_im[0].transpose(0, 2, 1)),
        blk(s5_c_re[0]), blk(s5_c_im[0]), s5_d[0].reshape(NGB, WU, 1))
    y_s5_p, fin = _s5_chunk(z_f, bd_op, e_op, ft_op, a16_re, a16_im)
    fin = fin.reshape(NGB, BATCH, 2, GB, P).transpose(2, 1, 0, 3, 4).reshape(2, 1, BATCH, G, P)
    y_s5_s, s5_re_s, s5_im_s = _s5_step(
        zs_f, state_s5_re[0].reshape(NS, G * P), state_s5_im[0].reshape(NS, G * P),
        a_re, a_im, bb_re, bb_im, blk(s5_c_re[0]), blk(s5_c_im[0]), s5_d[0].reshape(1, W))

    ml_p, c_p, n_p, m_p = _mlstm_chunk(per_seq(z_f), per_seq(z_b), per_seq(z_if), z_t, brow, bcol, gh)
    ml_s, c_s, n_s, m_s = _mlstm_step(zs, zs_f, zs_if, brow, gh.reshape(NH, DH), state_mlstm_C[0],
                                      state_mlstm_n[0], state_mlstm_m[0])

    mem_k, mem_v = _mem_proj(mem_prompt.reshape(BATCH * N_MEM, D), g_mem[0].reshape(1, D), w_mk, w_mv)
    xa_p = _xattn_prompt(z_b, mem_k, mem_v)
    xa_s = _xattn_step(zs, cache_mem_k[0], cache_mem_v[0])

    merged = _mix(x1, g_mix[0].reshape(1, D), y_s5_p, y_s5_s, ml_p.reshape(NP, W), ml_s, xa_p, xa_s,
                  w_glu, w_bg, w_br)
    x2 = _outproj(x1, merged, w_o)
    y_p, y_s = _ffn((x2,), g_ffn2[0].reshape(1, D), w2g, w2u, w2d, TF, g_final.reshape(1, D))

    return (y_p.reshape(BATCH, SEQ, D), y_s.reshape(NS, 1, D),
            mem_k.reshape(1, BATCH, N_MEM, NH, DH), mem_v.reshape(1, BATCH, N_MEM, NH, DH),
            fin[0], fin[1], c_p[None], n_p[None], m_p[:, :, 0][None],
            s5_re_s.reshape(1, NS, G, P), s5_im_s.reshape(1, NS, G, P), c_s[None], n_s[None], m_s[None])
```
